```python
import jax, jax.numpy as jnp
from jax import lax
import numpy as np

D_MODEL = 2048
BATCH = 32
SEQ = 256
DEPTH = 2
DEC_BATCH = 2
DEC_SEQ = 4096
PAST_LEN = 512

GRID_W = 64
HEAD_DIM = 128
D_ATTN = D_MODEL // 2
N_Q_HEADS = D_ATTN // HEAD_DIM
N_KV_HEADS = max(1, N_Q_HEADS // 4)
Q_PER_KV = N_Q_HEADS // N_KV_HEADS
WINDOW = 128
ATTN_BLOCK = 128
ATTN_SCALE = HEAD_DIM ** -0.5
ROPE_BASE = 10000.0
MASK_VALUE = -1e30
D_REC = D_MODEL // 4
REC_DK = 128
REC_DV = 128
N_REC_HEADS = D_REC // REC_DV
D_REC_K = N_REC_HEADS * REC_DK
REC_CHUNK = 32
CONV_CH = D_MODEL // 4
CONV_WIDTH = 31
D_MIX = D_ATTN + D_REC + CONV_CH
D_FF = 256 * ((8 * D_MODEL // 3 + 255) // 256)
N_MOD = 9
EPS = 1e-6
GATE_FLOOR = 1e-30
IN_SIZES = (D_ATTN, N_KV_HEADS * HEAD_DIM, N_KV_HEADS * HEAD_DIM, D_REC_K, D_REC_K, D_REC_K, D_REC, D_REC, CONV_CH, CONV_CH)
IN_COLS = sum(IN_SIZES)
IN_SPLITS = tuple(int(s) for s in np.cumsum(IN_SIZES)[:-1])

kernel_name = 'hybrid_diffusion_parallel_heads_step'

F32 = jnp.float32


def rms_norm(x, g):
    xf = x.astype(F32)
    y = xf * lax.rsqrt(jnp.mean(xf * xf, axis=-1, keepdims=True) + EPS)
    return (y * g.astype(F32)).astype(x.dtype)


def layer_norm(x, g, b):
    xf = x.astype(F32)
    mu = jnp.mean(xf, axis=-1, keepdims=True)
    var = jnp.mean(jnp.square(xf - mu), axis=-1, keepdims=True)
    y = (xf - mu) * lax.rsqrt(var + EPS)
    return (y * g.astype(F32) + b.astype(F32)).astype(x.dtype)


def swiglu(h, w_in, w_out):
    a, b = jnp.split(h @ w_in, 2, axis=-1)
    return (jax.nn.silu(a) * b) @ w_out


def axial_rope(x):
    n = x.shape[-2]
    rows = n // GRID_W
    row = jnp.repeat(jnp.arange(rows), GRID_W)
    col = jnp.arange(rows * GRID_W) % GRID_W
    quarter = HEAD_DIM // 4
    half = HEAD_DIM // 2
    inv_freq = ROPE_BASE ** (-jnp.arange(quarter, dtype=F32) / quarter)
    xf = x.astype(F32)

    def rotate(xh, pos):
        ang = pos.astype(F32)[:, None] * inv_freq
        cos, sin = jnp.cos(ang), jnp.sin(ang)
        x1, x2 = xh[..., :quarter], xh[..., quarter:]
        return jnp.concatenate([x1 * cos - x2 * sin, x1 * sin + x2 * cos], axis=-1)

    out = jnp.concatenate([rotate(xf[..., :half], row), rotate(xf[..., half:], col)], axis=-1)
    return out.astype(x.dtype)


def sink_attend(s, v, sink):
    sk = sink.astype(F32)[None, :, :, None, None]
    m = jnp.maximum(jnp.max(s, axis=-1, keepdims=True), sk)
    p = jnp.exp(s - m)
    den = jnp.sum(p, axis=-1, keepdims=True) + jnp.exp(sk - m)
    return jnp.einsum('bkgqs,bksd->bkgqd', p, v) / den


def context_attention(q, k, v, sink):
    B, KV, G, T, HD = q.shape
    nb = T // ATTN_BLOCK
    kf, vf = k.astype(F32), v.astype(F32)
    qb = q.astype(F32).reshape(B, KV, G, nb, ATTN_BLOCK, HD).transpose(3, 0, 1, 2, 4, 5)

    def block(qi):
        s = jnp.einsum('bkgqd,bksd->bkgqs', qi, kf) * ATTN_SCALE
        return sink_attend(s, vf, sink)

    o = lax.map(block, qb)
    return o.transpose(1, 2, 3, 0, 4, 5).reshape(B, KV, G, T, HD)


def latent_attention(q, k, v, k_ctx, v_ctx, sink):
    B, KV, G, N, HD = q.shape
    nb = N // ATTN_BLOCK
    span = 3 * ATTN_BLOCK
    pad = ((0, 0), (0, 0), (ATTN_BLOCK, ATTN_BLOCK), (0, 0))
    kp = jnp.pad(k.astype(F32), pad)
    vp = jnp.pad(v.astype(F32), pad)
    kc, vc = k_ctx.astype(F32), v_ctx.astype(F32)
    qb = q.astype(F32).reshape(B, KV, G, nb, ATTN_BLOCK, HD).transpose(3, 0, 1, 2, 4, 5)

    def block(args):
        i, qi = args
        start = i * ATTN_BLOCK
        ki = lax.dynamic_slice_in_dim(kp, start, span, axis=2)
        vi = lax.dynamic_slice_in_dim(vp, start, span, axis=2)
        qpos = start + jnp.arange(ATTN_BLOCK)
        kpos = start - ATTN_BLOCK + jnp.arange(span)
        valid = (jnp.abs(qpos[:, None] - kpos[None, :]) <= WINDOW) & (kpos >= 0) & (kpos < N)
        s_loc = jnp.where(valid, jnp.einsum('bkgqd,bksd->bkgqs', qi, ki) * ATTN_SCALE, MASK_VALUE)
        s_ctx = jnp.einsum('bkgqd,bksd->bkgqs', qi, kc) * ATTN_SCALE
        return sink_attend(jnp.concatenate([s_loc, s_ctx], axis=-1), jnp.concatenate([vi, vc], axis=2), sink)

    o = lax.map(block, (jnp.arange(nb), qb))
    return o.transpose(1, 2, 3, 0, 4, 5).reshape(B, KV, G, N, HD)


def forget_gate(z, lb):
    z = z.astype(F32)
    lb = lb.astype(F32)
    f = lb + (1.0 - lb) * jax.nn.sigmoid(z)
    log_f = jnp.log(jnp.maximum(f, GATE_FLOOR))
    k = (1.0 - lb) * jax.nn.sigmoid(-z)
    return k, log_f


def hgrn_scan(q, k, v, log_f, s0):
    B, H, T, DK = q.shape
    nc = T // REC_CHUNK
    causal = jnp.tril(jnp.ones((REC_CHUNK, REC_CHUNK), dtype=bool))[:, :, None]

    def to_chunks(a):
        return a.reshape(B, H, nc, REC_CHUNK, a.shape[-1]).transpose(2, 0, 1, 3, 4)

    def step(S, inp):
        qc, kc, vc, lc = inp
        cum = jnp.cumsum(lc, axis=-2)
        o_inter = jnp.einsum('bhtd,bhde->bhte', qc * jnp.exp(cum), S)
        diff = cum[:, :, :, None, :] - cum[:, :, None, :, :]
        decay = jnp.where(causal, jnp.exp(jnp.where(causal, diff, 0.0)), 0.0)
        a = jnp.einsum('bhtd,bhsd,bhtsd->bhts', qc, kc, decay)
        o = o_inter + jnp.einsum('bhts,bhse->bhte', a, vc)
        last = cum[:, :, -1:, :]
        S_new = jnp.exp(last[:, :, 0, :])[..., None] * S + jnp.einsum('bhsd,bhse->bhde', kc * jnp.exp(last - cum), vc)
        return S_new, o

    S, o = lax.scan(step, s0, (to_chunks(q), to_chunks(k), to_chunks(v), to_chunks(log_f)))
    return o.transpose(1, 2, 0, 3, 4).reshape(B, H, T, v.shape[-1]), S


def rec_bidir(q, k_f, k_b, lf_f, lf_b, v, s0_f, s0_b):
    o_f, s_f = hgrn_scan(q, k_f, v, lf_f, s0_f)
    rev = lambda a: jnp.flip(a, axis=2)
    o_b, s_b = hgrn_scan(rev(q), rev(k_b), rev(v), rev(lf_b), s0_b)
    return o_f + rev(o_b), s_f, s_b


def conv_module(a, b, w, bias, ln_g, ln_b):
    h = a * jax.nn.sigmoid(b)
    half = CONV_WIDTH // 2
    h = lax.conv_general_dilated(h, w[:, None, :].astype(h.dtype), window_strides=(1,), padding=[(half, half)],
                                 dimension_numbers=('NWC', 'WIO', 'NWC'), feature_group_count=CONV_CH)
    h = layer_norm(h + bias, ln_g, ln_b)
    return jax.nn.silu(h)


def mixer(h, P, l, lb_l, ctx):
    B, T, _ = h.shape
    dt = h.dtype
    zq, zk, zv, rq, rf_f, rf_b, ri, rg, ca, cb = jnp.split(h @ P['w_in'][l], IN_SPLITS, axis=-1)
    q = rms_norm(zq.reshape(B, T, N_KV_HEADS, Q_PER_KV, HEAD_DIM), P['q_norm_g'][l]).transpose(0, 2, 3, 1, 4)
    k = rms_norm(zk.reshape(B, T, N_KV_HEADS, HEAD_DIM), P['k_norm_g'][l]).transpose(0, 2, 1, 3)
    v = zv.reshape(B, T, N_KV_HEADS, HEAD_DIM).transpose(0, 2, 1, 3)
    sink = P['attn_sink'][l].reshape(N_KV_HEADS, Q_PER_KV)
    heads = lambda a, d: a.reshape(B, T, N_REC_HEADS, d).transpose(0, 2, 1, 3).astype(F32)
    lb = lb_l.reshape(2, N_REC_HEADS, 1, REC_DK)
    qr = heads(rq, REC_DK)
    vr = heads(ri, REC_DV)
    kf, lf_f = forget_gate(heads(rf_f, REC_DK), lb[0])
    kb, lf_b = forget_gate(heads(rf_b, REC_DK), lb[1])
    if ctx is None:
        attn = context_attention(q, k, v, sink)
        zero = jnp.zeros((B, N_REC_HEADS, REC_DK, REC_DV), F32)
        rec, s_f, s_b = rec_bidir(qr, kf, kb, lf_f, lf_b, vr, zero, zero)
        new_ctx = (k, v, jnp.stack([s_f, s_b], axis=1).astype(dt))
    else:
        k_ctx, v_ctx, s_ctx = ctx
        attn = latent_attention(axial_rope(q), axial_rope(k), v, k_ctx, v_ctx, sink)
        rec, _, _ = rec_bidir(qr, kf, kb, lf_f, lf_b, vr, s_ctx[:, 0].astype(F32), s_ctx[:, 1].astype(F32))
        new_ctx = None
    attn = attn.transpose(0, 3, 1, 2, 4).reshape(B, T, D_ATTN).astype(dt)
    rec = rms_norm(rec.transpose(0, 2, 1, 3), P['rec_norm_g'][l]).reshape(B, T, D_REC).astype(dt) * jax.nn.silu(rg)
    conv = conv_module(ca, cb, P['conv_w'][l], P['conv_b'][l], P['conv_ln_g'][l], P['conv_ln_b'][l])
    out = jnp.concatenate([attn, rec, conv], axis=-1) @ P['w_out'][l]
    return out, new_ctx


def trunk_layer(x, cond, P, l, lb_l, ctx):
    mod = (jax.nn.silu(cond) @ P['w_ada'][l] + P['b_ada'][l])[:, None, :]
    sh1, sc1, g1, sh2, sc2, g2, sh3, sc3, g3 = jnp.split(mod, N_MOD, axis=-1)
    h = rms_norm(x, P['norm_g'][l, 0]) * (1 + sc1) + sh1
    x = x + 0.5 * g1 * swiglu(h, P['w_ffn_in'][l, 0], P['w_ffn_out'][l, 0])
    h = rms_norm(x, P['norm_g'][l, 1]) * (1 + sc2) + sh2
    mix, new_ctx = mixer(h, P, l, lb_l, ctx)
    x = x + g2 * mix
    h = rms_norm(x, P['norm_g'][l, 2]) * (1 + sc3) + sh3
    x = x + 0.5 * g3 * swiglu(h, P['w_ffn_in'][l, 1], P['w_ffn_out'][l, 1])
    return x, new_ctx


def setup_inputs(seed: int = 0) -> dict:
    key = jax.random.key(seed)
    ks = jax.random.split(key, 24)
    nrm = lambda k, shape, s: jax.random.normal(k, shape, F32) * s
    return {
        'x_prompt': nrm(ks[0], (BATCH, SEQ, D_MODEL), 1.0),
        'x_sample': nrm(ks[1], (DEC_BATCH, DEC_SEQ, D_MODEL), 1.0),
        'cache_k': nrm(ks[2], (DEC_BATCH, DEPTH, N_KV_HEADS, PAST_LEN, HEAD_DIM), 1.0),
        'cache_v': nrm(ks[3], (DEC_BATCH, DEPTH, N_KV_HEADS, PAST_LEN, HEAD_DIM), 1.0),
        'state_rec': nrm(ks[4], (DEC_BATCH, DEPTH, 2, N_REC_HEADS, REC_DK, REC_DV), 0.5),
        'c': nrm(ks[5], (DEC_BATCH, D_MODEL), 1.0),
        'c_ctx': nrm(ks[6], (D_MODEL,), 1.0),
        'w_ada': nrm(ks[7], (DEPTH, D_MODEL, N_MOD * D_MODEL), 0.5 * D_MODEL ** -0.5),
        'b_ada': nrm(ks[8], (DEPTH, N_MOD * D_MODEL), 0.02),
        'norm_g': 1.0 + nrm(ks[9], (DEPTH, 3, D_MODEL), 0.05),
        'w_ffn_in': nrm(ks[10], (DEPTH, 2, D_MODEL, 2 * D_FF), D_MODEL ** -0.5),
        'w_ffn_out': nrm(ks[11], (DEPTH, 2, D_FF, D_MODEL), D_FF ** -0.5),
        'w_in': nrm(ks[12], (DEPTH, D_MODEL, IN_COLS), D_MODEL ** -0.5),
        'w_out': nrm(ks[13], (DEPTH, D_MIX, D_MODEL), D_MIX ** -0.5),
        'q_norm_g': 1.0 + nrm(ks[14], (DEPTH, HEAD_DIM), 0.05),
        'k_norm_g': 1.0 + nrm(ks[15], (DEPTH, HEAD_DIM), 0.05),
        'attn_sink': nrm(ks[16], (DEPTH, N_Q_HEADS), 0.5),
        'rec_lb_logits': nrm(ks[17], (DEPTH, 2, D_REC_K), 0.5),
        'rec_norm_g': 1.0 + nrm(ks[18], (DEPTH, N_REC_HEADS, REC_DV), 0.05),
        'conv_w': nrm(ks[19], (DEPTH, CONV_WIDTH, CONV_CH), CONV_WIDTH ** -0.5),
        'conv_b': nrm(ks[20], (DEPTH, CONV_CH), 0.02),
        'conv_ln_g': 1.0 + nrm(ks[21], (DEPTH, CONV_CH), 0.05),
        'conv_ln_b': nrm(ks[22], (DEPTH, CONV_CH), 0.02),
    }


def reference(x_prompt, x_sample, cache_k, cache_v, state_rec, c, c_ctx, w_ada, b_ada, norm_g, w_ffn_in, w_ffn_out,
              w_in, w_out, q_norm_g, k_norm_g, attn_sink, rec_lb_logits, rec_norm_g, conv_w, conv_b, conv_ln_g, conv_ln_b):
    P = {'w_ada': w_ada, 'b_ada': b_ada, 'norm_g': norm_g, 'w_ffn_in': w_ffn_in, 'w_ffn_out': w_ffn_out,
         'w_in': w_in, 'w_out': w_out, 'q_norm_g': q_norm_g, 'k_norm_g': k_norm_g, 'attn_sink': attn_sink,
         'rec_norm_g': rec_norm_g, 'conv_w': conv_w, 'conv_b': conv_b, 'conv_ln_g': conv_ln_g, 'conv_ln_b': conv_ln_b}
    p_lb = jax.nn.softmax(rec_lb_logits.astype(F32), axis=0)
    lb_all = jnp.cumsum(p_lb, axis=0) - p_lb[:1]

    h = x_prompt
    ks, vs, ss = [], [], []
    for l in range(DEPTH):
        h, (k_l, v_l, s_l) = trunk_layer(h, c_ctx[None, :], P, l, lb_all[l], None)
        ks.append(k_l)
        vs.append(v_l)
        ss.append(s_l)
    y_prompt = h
    new_cache_k = jnp.stack(ks, axis=1)
    new_cache_v = jnp.stack(vs, axis=1)
    new_state_rec = jnp.stack(ss, axis=1)

    h = x_sample
    for l in range(DEPTH):
        h, _ = trunk_layer(h, c, P, l, lb_all[l], (cache_k[:, l], cache_v[:, l], state_rec[:, l]))
    y_sample = h
    return (y_prompt, y_sample, new_cache_k, new_cache_v, new_state_rec)
```

```python
import functools

import jax
import jax.numpy as jnp
import numpy as np
from jax import lax
from jax.experimental import pallas as pl
from jax.experimental.pallas import tpu as pltpu

F32 = jnp.float32
BF16 = jnp.bfloat16

D_MODEL = 2048
DEPTH = 2
GRID_W = 64
HEAD_DIM = 128
D_ATTN = D_MODEL // 2
N_Q_HEADS = D_ATTN // HEAD_DIM
N_KV_HEADS = 2
Q_PER_KV = N_Q_HEADS // N_KV_HEADS
WINDOW = 128
ATTN_BLOCK = 128
ATTN_SCALE = HEAD_DIM ** -0.5
ROPE_BASE = 10000.0
MASK_VALUE = -1e30
D_REC = D_MODEL // 4
REC_DK = 128
REC_DV = 128
N_REC_HEADS = D_REC // REC_DV
REC_CHUNK = 32
CONV_CH = D_MODEL // 4
CONV_WIDTH = 31
D_FF = 5632
N_MOD = 9
EPS = 1e-6
GATE_FLOOR = 1e-30
IN_COLS = 5120

COL_Q = 0
COL_K = 8
COL_V = 10
COL_RQ = 12
COL_RF_F = 16
COL_RF_B = 20
COL_RI = 24
COL_RG = 28
COL_CA = 32
COL_CB = 36

SUBLANES = 8
VMEM_LIMIT = 56 * 1024 * 1024

TM = 512
TF = 512
TN_IN = 512
TN_ADA = 1024
CONV_ROWS = 256
CONV_HALO = 16
CONV_SUB = 32
LAT_Q_TILE = 1024


def _params(*sem):
    return pltpu.CompilerParams(dimension_semantics=sem, vmem_limit_bytes=VMEM_LIMIT)


def _dot(a, b):
    return jnp.dot(a, b, preferred_element_type=F32)


def _dot_nt(a, b):
    return lax.dot_general(a, b, (((1,), (1,)), ((), ())), preferred_element_type=F32)


def _dot_tn(a, b):
    return lax.dot_general(a, b, (((0,), (0,)), ((), ())), preferred_element_type=F32)


def _rms(x, g):
    return x * lax.rsqrt(jnp.mean(x * x, axis=-1, keepdims=True) + EPS) * g


def _sigmoid(x):
    return 1.0 / (1.0 + jnp.exp(-x))


def _ada_kernel(c_ref, w_ref, b_ref, o_ref):
    c = c_ref[...]
    s = (c * _sigmoid(c)).astype(BF16)
    o_ref[...] = _dot(s, w_ref[...].astype(BF16)) + b_ref[...]


def _modulation(cond8, w_ada, b_ada):
    ncol = N_MOD * D_MODEL
    return pl.pallas_call(
        _ada_kernel,
        grid=(DEPTH, ncol // TN_ADA),
        in_specs=[
            pl.BlockSpec((SUBLANES, D_MODEL), lambda l, j: (0, 0)),
            pl.BlockSpec((None, D_MODEL, TN_ADA), lambda l, j: (l, 0, j)),
            pl.BlockSpec((None, 1, TN_ADA), lambda l, j: (l, 0, j)),
        ],
        out_specs=pl.BlockSpec((None, SUBLANES, TN_ADA), lambda l, j: (l, 0, j)),
        out_shape=jax.ShapeDtypeStruct((DEPTH, SUBLANES, ncol), F32),
        compiler_params=_params("parallel", "parallel"),
    )(cond8, w_ada, b_ada.reshape(DEPTH, 1, ncol))


def _mod_spec(chunk, rows_per_cond):
    return pl.BlockSpec((None, None, 1, D_MODEL), lambda i, j: ((i * TM) // rows_per_cond, chunk, 0, 0))


def _ffn_kernel(x_ref, sh_ref, sc_ref, g_ref, ng_ref, wa_ref, wb_ref, wo_ref, o_ref, h_scr, acc_scr):
    j = pl.program_id(1)

    @pl.when(j == 0)
    def _():
        h = _rms(x_ref[...], ng_ref[...]) * (1.0 + sc_ref[...]) + sh_ref[...]
        h_scr[...] = h.astype(BF16)

    h = h_scr[...]
    a = _dot(h, wa_ref[...])
    b = _dot(h, wb_ref[...])
    act = (a * _sigmoid(a) * b).astype(BF16)
    contrib = _dot(act, wo_ref[...])

    @pl.when(j == 0)
    def _():
        acc_scr[...] = contrib

    @pl.when(j > 0)
    def _():
        acc_scr[...] += contrib

    @pl.when(j == pl.num_programs(1) - 1)
    def _():
        o_ref[...] = x_ref[...] + 0.5 * g_ref[...] * acc_scr[...]


def _ffn(x, mod, first_chunk, norm_g, w_in, w_out, rows_per_cond):
    n = x.shape[0]
    nf = D_FF // TF
    return pl.pallas_call(
        _ffn_kernel,
        grid=(n // TM, nf),
        in_specs=[
            pl.BlockSpec((TM, D_MODEL), lambda i, j: (i, 0)),
            _mod_spec(first_chunk, rows_per_cond),
            _mod_spec(first_chunk + 1, rows_per_cond),
            _mod_spec(first_chunk + 2, rows_per_cond),
            pl.BlockSpec((1, D_MODEL), lambda i, j: (0, 0)),
            pl.BlockSpec((D_MODEL, TF), lambda i, j: (0, j)),
            pl.BlockSpec((D_MODEL, TF), lambda i, j: (0, j + nf)),
            pl.BlockSpec((TF, D_MODEL), lambda i, j: (j, 0)),
        ],
        out_specs=pl.BlockSpec((TM, D_MODEL), lambda i, j: (i, 0)),
        out_shape=jax.ShapeDtypeStruct((n, D_MODEL), F32),
        scratch_shapes=[pltpu.VMEM((TM, D_MODEL), BF16), pltpu.VMEM((TM, D_MODEL), F32)],
        compiler_params=_params("parallel", "arbitrary"),
    )(x, mod, mod, mod, norm_g.reshape(1, D_MODEL), w_in, w_in, w_out)


def _inproj_kernel(x_ref, sh_ref, sc_ref, ng_ref, w_ref, o_ref, h_scr):
    @pl.when(pl.program_id(1) == 0)
    def _():
        h = _rms(x_ref[...], ng_ref[...]) * (1.0 + sc_ref[...]) + sh_ref[...]
        h_scr[...] = h.astype(BF16)

    o_ref[...] = _dot(h_scr[...], w_ref[...])


def _inproj(x, mod, norm_g, w_in, rows_per_cond):
    n = x.shape[0]
    return pl.pallas_call(
        _inproj_kernel,
        grid=(n // TM, IN_COLS // TN_IN),
        in_specs=[
            pl.BlockSpec((TM, D_MODEL), lambda i, j: (i, 0)),
            _mod_spec(3, rows_per_cond),
            _mod_spec(4, rows_per_cond),
            pl.BlockSpec((1, D_MODEL), lambda i, j: (0, 0)),
            pl.BlockSpec((D_MODEL, TN_IN), lambda i, j: (0, j)),
        ],
        out_specs=pl.BlockSpec((TM, TN_IN), lambda i, j: (i, j)),
        out_shape=jax.ShapeDtypeStruct((n, IN_COLS), F32),
        scratch_shapes=[pltpu.VMEM((TM, D_MODEL), BF16)],
        compiler_params=_params("parallel", "arbitrary"),
    )(x, mod, mod, norm_g.reshape(1, D_MODEL), w_in)


def _softmax_sink(scores, values, sink):
    m = sink
    for s in scores:
        m = jnp.maximum(jnp.max(s, axis=-1, keepdims=True), m)
    den = jnp.exp(sink - m)
    acc = None
    for s, v in zip(scores, values):
        p = jnp.exp(s - m)
        den = den + jnp.sum(p, axis=-1, keepdims=True)
        pv = _dot(p.astype(BF16), v)
        acc = pv if acc is None else acc + pv
    return acc / den


def _ctx_attn_kernel(sink_ref, q_ref, k_ref, v_ref, qg_ref, kg_ref, o_ref, kc_ref, vc_ref):
    kv = pl.program_id(1)
    kn = _rms(k_ref[...], kg_ref[...])
    v = v_ref[...]
    kc_ref[...] = kn
    vc_ref[...] = v
    kb = kn.astype(BF16)
    vb = v.astype(BF16)
    for g in range(Q_PER_KV):
        cols = slice(g * HEAD_DIM, (g + 1) * HEAD_DIM)
        qn = _rms(q_ref[:, cols], qg_ref[...]).astype(BF16)
        s = _dot_nt(qn, kb) * ATTN_SCALE
        o = _softmax_sink([s], [vb], sink_ref[kv, g])
        o_ref[:, cols] = o.astype(BF16)


def _ctx_attention(z, sink, q_g, k_g, batch, seq):
    qw = Q_PER_KV * HEAD_DIM
    cache_shape = jax.ShapeDtypeStruct((batch, N_KV_HEADS, seq, HEAD_DIM), F32)
    cache_spec = pl.BlockSpec((None, None, seq, HEAD_DIM), lambda b, kv: (b, kv, 0, 0))
    return pl.pallas_call(
        _ctx_attn_kernel,
        grid=(batch, N_KV_HEADS),
        in_specs=[
            pl.BlockSpec(memory_space=pltpu.SMEM),
            pl.BlockSpec((seq, qw), lambda b, kv: (b, kv)),
            pl.BlockSpec((seq, HEAD_DIM), lambda b, kv: (b, COL_K + kv)),
            pl.BlockSpec((seq, HEAD_DIM), lambda b, kv: (b, COL_V + kv)),
            pl.BlockSpec((1, HEAD_DIM), lambda b, kv: (0, 0)),
            pl.BlockSpec((1, HEAD_DIM), lambda b, kv: (0, 0)),
        ],
        out_specs=[pl.BlockSpec((seq, qw), lambda b, kv: (b, kv)), cache_spec, cache_spec],
        out_shape=[jax.ShapeDtypeStruct((batch * seq, D_ATTN), BF16), cache_shape, cache_shape],
        compiler_params=_params("parallel", "parallel"),
    )(sink, z, z, z, q_g.reshape(1, HEAD_DIM), k_g.reshape(1, HEAD_DIM))


def _rope(x, cos, sin_a, sin_b):
    quarter = HEAD_DIM // 4
    up = pltpu.roll(x, HEAD_DIM - quarter, 1)
    down = pltpu.roll(x, quarter, 1)
    return x * cos + up * sin_a + down * sin_b


def _lat_attn_kernel(sink_ref, q_ref, k_ref, v_ref, kc_ref, vc_ref, qg_ref, kg_ref,
                     cos_ref, sa_ref, sb_ref, cosq_ref, saq_ref, sbq_ref, o_ref, k_scr, v_scr, kc_scr, vc_scr):
    kv = pl.program_id(1)
    qt = pl.program_id(2)
    seq = k_ref.shape[0]
    span = 3 * ATTN_BLOCK

    @pl.when(qt == 0)
    def _():
        kn = _rms(k_ref[...], kg_ref[...])
        k_scr[...] = _rope(kn, cos_ref[...], sa_ref[...], sb_ref[...]).astype(BF16)
        v_scr[...] = v_ref[...].astype(BF16)
        kc_scr[...] = kc_ref[...].astype(BF16)
        vc_scr[...] = vc_ref[...].astype(BF16)

    kcb = kc_scr[...]
    vcb = vc_scr[...]
    for blk in range(LAT_Q_TILE // ATTN_BLOCK):
        rows = slice(blk * ATTN_BLOCK, (blk + 1) * ATTN_BLOCK)
        q0 = (qt * (LAT_Q_TILE // ATTN_BLOCK) + blk) * ATTN_BLOCK
        start = pl.multiple_of(jnp.clip(q0 - ATTN_BLOCK, 0, seq - span), ATTN_BLOCK)
        kl = k_scr[pl.ds(start, span), :]
        vl = v_scr[pl.ds(start, span), :]
        qpos = q0 + lax.broadcasted_iota(jnp.int32, (ATTN_BLOCK, span), 0)
        kpos = start + lax.broadcasted_iota(jnp.int32, (ATTN_BLOCK, span), 1)
        valid = jnp.abs(qpos - kpos) <= WINDOW
        cos, sa, sb = cosq_ref[rows, :], saq_ref[rows, :], sbq_ref[rows, :]
        for g in range(Q_PER_KV):
            cols = slice(g * HEAD_DIM, (g + 1) * HEAD_DIM)
            qn = _rope(_rms(q_ref[rows, cols], qg_ref[...]), cos, sa, sb).astype(BF16)
            s_loc = jnp.where(valid, _dot_nt(qn, kl) * ATTN_SCALE, MASK_VALUE)
            s_ctx = _dot_nt(qn, kcb) * ATTN_SCALE
            o = _softmax_sink([s_loc, s_ctx], [vl, vcb], sink_ref[kv, g])
            o_ref[rows, cols] = o.astype(BF16)


def _rope_tables(seq):
    quarter = HEAD_DIM // 4
    pos = np.arange(seq)
    inv_freq = ROPE_BASE ** (-np.arange(quarter, dtype=np.float32) / quarter)
    inv_freq = jnp.asarray(inv_freq, F32)
    zero = jnp.zeros((seq, quarter), F32)

    def trig(p):
        ang = jnp.asarray(p, F32)[:, None] * inv_freq
        return jnp.cos(ang), jnp.sin(ang)

    cr, sr = trig(pos // GRID_W)
    cc, sc = trig(pos % GRID_W)
    cos = jnp.concatenate([cr, cr, cc, cc], axis=-1)
    sin_a = jnp.concatenate([-sr, zero, -sc, zero], axis=-1)
    sin_b = jnp.concatenate([zero, sr, zero, sc], axis=-1)
    return cos, sin_a, sin_b


def _lat_attention(z, sink, q_g, k_g, k_ctx, v_ctx, tables, batch, seq):
    qw = Q_PER_KV * HEAD_DIM
    nqt = seq // LAT_Q_TILE
    past = k_ctx.shape[2]
    cos, sin_a, sin_b = tables
    full_tab = pl.BlockSpec((seq, HEAD_DIM), lambda b, kv, qt: (0, 0))
    tile_tab = pl.BlockSpec((LAT_Q_TILE, HEAD_DIM), lambda b, kv, qt: (qt, 0))
    ctx_spec = pl.BlockSpec((None, None, past, HEAD_DIM), lambda b, kv, qt: (b, kv, 0, 0))
    gain = pl.BlockSpec((1, HEAD_DIM), lambda b, kv, qt: (0, 0))
    return pl.pallas_call(
        _lat_attn_kernel,
        grid=(batch, N_KV_HEADS, nqt),
        in_specs=[
            pl.BlockSpec(memory_space=pltpu.SMEM),
            pl.BlockSpec((LAT_Q_TILE, qw), lambda b, kv, qt: (b * nqt + qt, kv)),
            pl.BlockSpec((seq, HEAD_DIM), lambda b, kv, qt: (b, COL_K + kv)),
            pl.BlockSpec((seq, HEAD_DIM), lambda b, kv, qt: (b, COL_V + kv)),
            ctx_spec, ctx_spec, gain, gain,
            full_tab, full_tab, full_tab, tile_tab, tile_tab, tile_tab,
        ],
        out_specs=pl.BlockSpec((LAT_Q_TILE, qw), lambda b, kv, qt: (b * nqt + qt, kv)),
        out_shape=jax.ShapeDtypeStruct((batch * seq, D_ATTN), BF16),
        scratch_shapes=[pltpu.VMEM((seq, HEAD_DIM), BF16), pltpu.VMEM((seq, HEAD_DIM), BF16),
                        pltpu.VMEM((past, HEAD_DIM), BF16), pltpu.VMEM((past, HEAD_DIM), BF16)],
        compiler_params=_params("parallel", "parallel", "arbitrary"),
    )(sink, z, z, z, k_ctx, v_ctx, q_g.reshape(1, HEAD_DIM), k_g.reshape(1, HEAD_DIM),
      cos, sin_a, sin_b, cos, sin_a, sin_b)


def _split3(x):
    hi = x.astype(BF16)
    r1 = x - hi.astype(F32)
    mid = r1.astype(BF16)
    lo = (r1 - mid.astype(F32)).astype(BF16)
    return hi, mid, lo


def _rec_chunk(forward, q, zf, v, lb, st_ref):
    c = REC_CHUNK
    nsub = c // SUBLANES
    t = jnp.exp(-jnp.abs(zf))
    r = 1.0 / (1.0 + t)
    tr = t * r
    nonneg = zf >= 0
    sig_pos = jnp.where(nonneg, r, tr)
    sig_neg = jnp.where(nonneg, tr, r)
    one_m_lb = 1.0 - lb
    f = lb + one_m_lb * sig_pos
    log_f = jnp.log(jnp.maximum(f, GATE_FLOOR))
    k = one_m_lb * sig_neg

    ri = lax.broadcasted_iota(jnp.int32, (c, c), 0)
    ci = lax.broadcasted_iota(jnp.int32, (c, c), 1)
    tri = jnp.where((ci <= ri) if forward else (ci >= ri), 1.0, 0.0).astype(BF16)
    hi, mid, lo = _split3(log_f)
    cum = _dot(tri, hi) + _dot(tri, mid) + _dot(tri, lo)
    last = cum[c - 1:c, :] if forward else cum[0:1, :]

    sub_row = lax.broadcasted_iota(jnp.int32, (SUBLANES, REC_DK), 0)
    pieces = []
    plan = []
    for s in range(c):
        sb, sl = divmod(s, SUBLANES)
        cum_s = cum[s:s + 1, :]
        k_s = k[s:s + 1, :]
        for tb in (range(sb, nsub) if forward else range(0, sb + 1)):
            rows = slice(tb * SUBLANES, (tb + 1) * SUBLANES)
            diff = cum[rows, :] - cum_s
            if tb == sb:
                keep = (sub_row >= sl) if forward else (sub_row <= sl)
                p = jnp.where(keep, jnp.exp(jnp.minimum(diff, 0.0)) * q[rows, :] * k_s, 0.0)
            else:
                p = jnp.exp(diff) * q[rows, :] * k_s
            pieces.append(p.astype(BF16))
            plan.append((s, tb))
    stacked = jnp.concatenate(pieces, axis=0)
    lane_sums = _dot(stacked, jnp.ones((REC_DK, REC_DV), BF16))
    intra = [None] * nsub
    for idx, (s, tb) in enumerate(plan):
        term = lane_sums[idx * SUBLANES:(idx + 1) * SUBLANES, :] * v[s:s + 1, :]
        intra[tb] = term if intra[tb] is None else intra[tb] + term
    o_intra = jnp.concatenate(intra, axis=0)

    st = st_ref[...]
    o_inter = _dot_nt((q * jnp.exp(cum)).astype(BF16), st.astype(BF16))
    k_dec = (k * jnp.exp(last - cum)).astype(BF16)
    st_ref[...] = st * jnp.exp(last) + _dot_tn(v.astype(BF16), k_dec)
    return o_inter + o_intra


def _rec_kernel(layer, has_init, *refs):
    if has_init:
        (q_ref, ff_ref, fb_ref, v_ref, g_ref, lg_ref, ng_ref, s0_ref, o_ref, o_scr, st_scr) = refs
        s_out_ref = None
    else:
        (q_ref, ff_ref, fb_ref, v_ref, g_ref, lg_ref, ng_ref, o_ref, s_out_ref, o_scr, st_scr) = refs
    seq = q_ref.shape[0]
    nchunk = seq // REC_CHUNK

    logits = lg_ref[...]
    e = jnp.exp(logits - jnp.max(logits, axis=0, keepdims=True))
    p = e / jnp.sum(e, axis=0, keepdims=True)
    lb = jnp.zeros_like(p[0])
    for i in range(1, layer + 1):
        lb = lb + p[i]

    def init_state(d):
        if has_init:
            st_scr[...] = s0_ref[d].T
        else:
            st_scr[...] = jnp.zeros_like(st_scr)

    def rows_of(ci):
        return pl.ds(pl.multiple_of(ci * REC_CHUNK, REC_CHUNK), REC_CHUNK)

    init_state(0)

    def fwd_body(ci, carry):
        rows = rows_of(ci)
        o_scr[rows, :] = _rec_chunk(True, q_ref[rows, :], ff_ref[rows, :], v_ref[rows, :], lb[0], st_scr)
        return carry

    lax.fori_loop(0, nchunk, fwd_body, 0)
    if not has_init:
        s_out_ref[0] = st_scr[...].T

    init_state(1)

    def bwd_body(it, carry):
        rows = rows_of(nchunk - 1 - it)
        o = o_scr[rows, :] + _rec_chunk(False, q_ref[rows, :], fb_ref[rows, :], v_ref[rows, :], lb[1], st_scr)
        gate = g_ref[rows, :]
        o_ref[rows, :] = (_rms(o, ng_ref[...]) * (gate * _sigmoid(gate))).astype(BF16)
        return carry

    lax.fori_loop(0, nchunk, bwd_body, 0)
    if not has_init:
        s_out_ref[1] = st_scr[...].T


def _recurrence(z, layer, lb_logits, norm_g, s0, batch, seq):
    has_init = s0 is not None
    col = lambda base: pl.BlockSpec((seq, REC_DK), lambda b, h: (b, base + h))
    state_spec = pl.BlockSpec((None, 2, None, REC_DK, REC_DV), lambda b, h: (b, 0, h, 0, 0))
    in_specs = [col(COL_RQ), col(COL_RF_F), col(COL_RF_B), col(COL_RI), col(COL_RG),
                pl.BlockSpec((DEPTH, 2, None, 1, REC_DK), lambda b, h: (0, 0, h, 0, 0)),
                pl.BlockSpec((None, 1, REC_DV), lambda b, h: (h, 0, 0))]
    args = [z, z, z, z, z, lb_logits, norm_g]
    out_specs = [pl.BlockSpec((seq, REC_DV), lambda b, h: (b, h))]
    out_shape = [jax.ShapeDtypeStruct((batch * seq, D_REC), BF16)]
    if has_init:
        in_specs.append(state_spec)
        args.append(s0)
    else:
        out_specs.append(state_spec)
        out_shape.append(jax.ShapeDtypeStruct((batch, 2, N_REC_HEADS, REC_DK, REC_DV), F32))
    return pl.pallas_call(
        functools.partial(_rec_kernel, layer, has_init),
        grid=(batch, N_REC_HEADS),
        in_specs=in_specs,
        out_specs=out_specs,
        out_shape=out_shape,
        scratch_shapes=[pltpu.VMEM((seq, REC_DV), F32), pltpu.VMEM((REC_DV, REC_DK), F32)],
        compiler_params=_params("parallel", "parallel"),
    )(*args)


def _conv_kernel(tiles_per_seq, a_ref, b_ref, at_ref, bt_ref, ab_ref, bb_ref, w_ref, bias_ref, lg_ref, lb_ref,
                 o_ref, h_scr):
    i = pl.program_id(0)
    tile_in_seq = i % tiles_per_seq
    has_top = jnp.where(tile_in_seq > 0, 1.0, 0.0)
    has_bot = jnp.where(tile_in_seq < tiles_per_seq - 1, 1.0, 0.0)
    rows = a_ref.shape[0]
    half = CONV_WIDTH // 2

    def glu(a, b):
        return a * _sigmoid(b)

    h_scr[0:CONV_HALO, :] = glu(at_ref[...], bt_ref[...]) * has_top
    h_scr[CONV_HALO:CONV_HALO + rows, :] = glu(a_ref[...], b_ref[...])
    h_scr[CONV_HALO + rows:, :] = glu(ab_ref[...], bb_ref[...]) * has_bot

    for r0 in range(0, rows, CONV_SUB):
        acc = None
        for j in range(CONV_WIDTH):
            lo = r0 + CONV_HALO - half + j
            term = h_scr[lo:lo + CONV_SUB, :] * w_ref[j:j + 1, :]
            acc = term if acc is None else acc + term
        y = acc + bias_ref[...]
        mu = jnp.mean(y, axis=-1, keepdims=True)
        yc = y - mu
        var = jnp.mean(yc * yc, axis=-1, keepdims=True)
        yn = yc * lax.rsqrt(var + EPS) * lg_ref[...] + lb_ref[...]
        o_ref[r0:r0 + CONV_SUB, :] = (yn * _sigmoid(yn)).astype(BF16)


def _conv(z, w, bias, ln_g, ln_b, seq):
    n = z.shape[0]
    tiles_per_seq = seq // CONV_ROWS
    ntiles = n // CONV_ROWS
    cw = CONV_CH // HEAD_DIM
    halo_per_tile = CONV_ROWS // CONV_HALO
    nhalo = n // CONV_HALO
    ca, cb = COL_CA // cw, COL_CB // cw
    mid = lambda c: pl.BlockSpec((CONV_ROWS, CONV_CH), lambda i: (i, c))
    top = lambda c: pl.BlockSpec((CONV_HALO, CONV_CH), lambda i: (jnp.maximum(i * halo_per_tile - 1, 0), c))
    bot = lambda c: pl.BlockSpec((CONV_HALO, CONV_CH), lambda i: (jnp.minimum((i + 1) * halo_per_tile, nhalo - 1), c))
    vec = pl.BlockSpec((1, CONV_CH), lambda i: (0, 0))
    return pl.pallas_call(
        functools.partial(_conv_kernel, tiles_per_seq),
        grid=(ntiles,),
        in_specs=[mid(ca), mid(cb), top(ca), top(cb), bot(ca), bot(cb),
                  pl.BlockSpec((CONV_WIDTH, CONV_CH), lambda i: (0, 0)), vec, vec, vec],
        out_specs=pl.BlockSpec((CONV_ROWS, CONV_CH), lambda i: (i, 0)),
        out_shape=jax.ShapeDtypeStruct((n, CONV_CH), BF16),
        scratch_shapes=[pltpu.VMEM((CONV_ROWS + 2 * CONV_HALO, CONV_CH), F32)],
        compiler_params=_params("parallel"),
    )(z, z, z, z, z, z, w, bias.reshape(1, CONV_CH), ln_g.reshape(1, CONV_CH), ln_b.reshape(1, CONV_CH))


def _outproj_kernel(x_ref, g_ref, a_ref, r_ref, c_ref, wa_ref, wr_ref, wc_ref, o_ref):
    mix = _dot(a_ref[...], wa_ref[...]) + _dot(r_ref[...], wr_ref[...]) + _dot(c_ref[...], wc_ref[...])
    o_ref[...] = x_ref[...] + g_ref[...] * mix


def _outproj(x, mod, attn, rec, conv, w_out, rows_per_cond):
    n = x.shape[0]
    tn = D_MODEL // 2
    row = lambda width: pl.BlockSpec((TM, width), lambda i, j: (i, 0))
    return pl.pallas_call(
        _outproj_kernel,
        grid=(n // TM, D_MODEL // tn),
        in_specs=[
            pl.BlockSpec((TM, tn), lambda i, j: (i, j)),
            pl.BlockSpec((None, None, 1, tn), lambda i, j: ((i * TM) // rows_per_cond, 5, 0, j)),
            row(D_ATTN), row(D_REC), row(CONV_CH),
            pl.BlockSpec((D_ATTN, tn), lambda i, j: (0, j)),
            pl.BlockSpec((D_REC, tn), lambda i, j: (D_ATTN // D_REC, j)),
            pl.BlockSpec((CONV_CH, tn), lambda i, j: ((D_ATTN + D_REC) // CONV_CH, j)),
        ],
        out_specs=pl.BlockSpec((TM, tn), lambda i, j: (i, j)),
        out_shape=jax.ShapeDtypeStruct((n, D_MODEL), F32),
        compiler_params=_params("parallel", "parallel"),
    )(x, mod, attn, rec, conv, w_out, w_out, w_out)


def _trunk_layer(x, mod, l, P, mixers, rows_per_cond):
    x = _ffn(x, mod, 0, P['norm_g'][l, 0], P['w_ffn_in'][l, 0], P['w_ffn_out'][l, 0], rows_per_cond)
    z = _inproj(x, mod, P['norm_g'][l, 1], P['w_in'][l], rows_per_cond)
    attn, rec, conv, extras = mixers(z)
    x = _outproj(x, mod, attn, rec, conv, P['w_out'][l], rows_per_cond)
    x = _ffn(x, mod, 6, P['norm_g'][l, 2], P['w_ffn_in'][l, 1], P['w_ffn_out'][l, 1], rows_per_cond)
    return x, extras


def kernel(x_prompt, x_sample, cache_k, cache_v, state_rec, c, c_ctx, w_ada, b_ada, norm_g, w_ffn_in, w_ffn_out,
           w_in, w_out, q_norm_g, k_norm_g, attn_sink, rec_lb_logits, rec_norm_g, conv_w, conv_b, conv_ln_g, conv_ln_b):
    batch, seq, _ = x_prompt.shape
    dec_batch, dec_seq, _ = x_sample.shape
    assert seq % TM == 0 or TM % seq == 0
    assert dec_seq % TM == 0 and dec_seq % LAT_Q_TILE == 0 and seq % CONV_ROWS == 0 and dec_seq % CONV_ROWS == 0

    P = {'norm_g': norm_g, 'w_ffn_in': w_ffn_in.astype(BF16), 'w_ffn_out': w_ffn_out.astype(BF16),
         'w_in': w_in.astype(BF16), 'w_out': w_out.astype(BF16)}

    lb_logits = rec_lb_logits.reshape(DEPTH, 2, N_REC_HEADS, 1, REC_DK)

    cond8 = jnp.zeros((SUBLANES, D_MODEL), F32).at[0].set(c_ctx).at[1:1 + dec_batch].set(c)
    mod = _modulation(cond8, w_ada, b_ada).reshape(DEPTH, SUBLANES, N_MOD, 1, D_MODEL)
    tables = _rope_tables(dec_seq)
    sinks = attn_sink.reshape(DEPTH, N_KV_HEADS, Q_PER_KV)
    rec_g = rec_norm_g.reshape(DEPTH, N_REC_HEADS, 1, REC_DV)

    def shared_mixers(z, l, seq_len):
        return _conv(z, conv_w[l], conv_b[l], conv_ln_g[l], conv_ln_b[l], seq_len)

    h = x_prompt.reshape(batch * seq, D_MODEL)
    ks, vs, ss = [], [], []
    for l in range(DEPTH):
        def ctx_mixers(z, l=l):
            attn, k_l, v_l = _ctx_attention(z, sinks[l], q_norm_g[l], k_norm_g[l], batch, seq)
            rec, s_l = _recurrence(z, l, lb_logits, rec_g[l], None, batch, seq)
            return attn, rec, shared_mixers(z, l, seq), (k_l, v_l, s_l)

        h, (k_l, v_l, s_l) = _trunk_layer(h, mod[l, 0:1], l, P, ctx_mixers, batch * seq)
        ks.append(k_l)
        vs.append(v_l)
        ss.append(s_l)
    y_prompt = h.reshape(batch, seq, D_MODEL)

    h = x_sample.reshape(dec_batch * dec_seq, D_MODEL)
    for l in range(DEPTH):
        def lat_mixers(z, l=l):
            attn = _lat_attention(z, sinks[l], q_norm_g[l], k_norm_g[l], cache_k[:, l], cache_v[:, l], tables,
                                  dec_batch, dec_seq)
            rec = _recurrence(z, l, lb_logits, rec_g[l], state_rec[:, l], dec_batch, dec_seq)[0]
            return attn, rec, shared_mixers(z, l, dec_seq), None

        h, _ = _trunk_layer(h, mod[l, 1:1 + dec_batch], l, P, lat_mixers, dec_seq)
    y_sample = h.reshape(dec_batch, dec_seq, D_MODEL)

    return (y_prompt, y_sample, jnp.stack(ks, axis=1), jnp.stack(vs, axis=1), jnp.stack(ss, axis=1))
```

```python
import functools

import jax
import jax.numpy as jnp
import numpy as np
from jax import lax
from jax.experimental import pallas as pl
from jax.experimental.pallas import tpu as pltpu

F32 = jnp.float32
BF16 = jnp.bfloat16

D_MODEL = 2048
DEPTH = 2
GRID_W = 64
HEAD_DIM = 128
D_ATTN = D_MODEL // 2
N_Q_HEADS = D_ATTN // HEAD_DIM
N_KV_HEADS = 2
Q_PER_KV = N_Q_HEADS // N_KV_HEADS
WINDOW = 128
ATTN_BLOCK = 128
ATTN_SCALE = HEAD_DIM ** -0.5
ROPE_BASE = 10000.0
MASK_VALUE = -1e30
D_REC = D_MODEL // 4
REC_DK = 128
REC_DV = 128
N_REC_HEADS = D_REC // REC_DV
REC_CHUNK = 32
CONV_CH = D_MODEL // 4
CONV_WIDTH = 31
D_FF = 5632
N_MOD = 9
EPS = 1e-6
GATE_FLOOR = 1e-30
IN_COLS = 5120

COL_Q = 0
COL_K = 8
COL_V = 10
COL_RQ = 12
COL_RF_F = 16
COL_RF_B = 20
COL_RI = 24
COL_RG = 28
COL_CA = 32
COL_CB = 36

SUBLANES = 8
VMEM_LIMIT = 56 * 1024 * 1024

TM = 512
TF = 512
TN_FFN_OUT = 512
TN_IN = 512
TN_ADA = 1024
CONV_ROWS = 256
CONV_HALO = 16
CONV_SUB = 32
LAT_Q_TILE = 1024
REC_LOCAL_UNROLL = 4
REC_STATE_UNROLL = 4


def _params(*sem):
    return pltpu.CompilerParams(dimension_semantics=sem, vmem_limit_bytes=VMEM_LIMIT)


def _dot(a, b):
    return jnp.dot(a, b, preferred_element_type=F32)


def _dot_nt(a, b):
    return lax.dot_general(a, b, (((1,), (1,)), ((), ())), preferred_element_type=F32)


def _dot_tn(a, b):
    return lax.dot_general(a, b, (((0,), (0,)), ((), ())), preferred_element_type=F32)


def _rms(x, g):
    return x * lax.rsqrt(jnp.mean(x * x, axis=-1, keepdims=True) + EPS) * g


def _sigmoid(x):
    return 1.0 / (1.0 + jnp.exp(-x))


def _ada_kernel(c_ref, w_ref, b_ref, o_ref):
    c = c_ref[...]
    s = (c * _sigmoid(c)).astype(BF16)
    o_ref[...] = _dot(s, w_ref[...].astype(BF16)) + b_ref[...]


def _modulation(cond8, w_ada, b_ada):
    ncol = N_MOD * D_MODEL
    return pl.pallas_call(
        _ada_kernel,
        grid=(DEPTH, ncol // TN_ADA),
        in_specs=[
            pl.BlockSpec((SUBLANES, D_MODEL), lambda l, j: (0, 0)),
            pl.BlockSpec((None, D_MODEL, TN_ADA), lambda l, j: (l, 0, j)),
            pl.BlockSpec((None, 1, TN_ADA), lambda l, j: (l, 0, j)),
        ],
        out_specs=pl.BlockSpec((None, SUBLANES, TN_ADA), lambda l, j: (l, 0, j)),
        out_shape=jax.ShapeDtypeStruct((DEPTH, SUBLANES, ncol), F32),
        compiler_params=_params("parallel", "parallel"),
        name="modulation",
    )(cond8, w_ada, b_ada.reshape(DEPTH, 1, ncol))


def _mod_spec(chunk, rows_per_cond):
    return pl.BlockSpec((None, None, 1, D_MODEL), lambda i, j: ((i * TM) // rows_per_cond, chunk, 0, 0))


def _ffn_kernel(nf, x_ref, xres_ref, sh_ref, sc_ref, g_ref, ng_ref, wa_ref, wb_ref, wo_ref, o_ref, h_scr, act_scr):
    j = pl.program_id(1)

    @pl.when(j == 0)
    def _():
        h = _rms(x_ref[...], ng_ref[...]) * (1.0 + sc_ref[...]) + sh_ref[...]
        h_scr[...] = h.astype(BF16)

    @pl.when(j < nf)
    def _():
        h = h_scr[...]
        a = _dot(h, wa_ref[...])
        b = _dot(h, wb_ref[...])
        act_scr[:, pl.ds(pl.multiple_of(j * TF, TF), TF)] = (a * _sigmoid(a) * b).astype(BF16)

    @pl.when(j >= nf)
    def _():
        o_ref[...] = xres_ref[...] + 0.5 * g_ref[...] * _dot(act_scr[...], wo_ref[...])


def _ffn(x, mod, first_chunk, norm_g, w_in, w_out, layer, which, rows_per_cond):
    n = x.shape[0]
    nf = D_FF // TF
    nout = D_MODEL // TN_FFN_OUT
    hid = lambda j: jnp.minimum(j, nf - 1)
    out = lambda j: jnp.maximum(j - nf, 0)
    return pl.pallas_call(
        functools.partial(_ffn_kernel, nf),
        grid=(n // TM, nf + nout),
        in_specs=[
            pl.BlockSpec((TM, D_MODEL), lambda i, j: (i, 0)),
            pl.BlockSpec((TM, TN_FFN_OUT), lambda i, j: (i, out(j))),
            _mod_spec(first_chunk, rows_per_cond),
            _mod_spec(first_chunk + 1, rows_per_cond),
            pl.BlockSpec((None, None, 1, TN_FFN_OUT),
                         lambda i, j: ((i * TM) // rows_per_cond, first_chunk + 2, 0, out(j))),
            pl.BlockSpec((1, D_MODEL), lambda i, j: (0, 0)),
            pl.BlockSpec((None, None, D_MODEL, TF), lambda i, j: (layer, which, 0, hid(j))),
            pl.BlockSpec((None, None, D_MODEL, TF), lambda i, j: (layer, which, 0, hid(j) + nf)),
            pl.BlockSpec((None, None, D_FF, TN_FFN_OUT), lambda i, j: (layer, which, 0, out(j))),
        ],
        out_specs=pl.BlockSpec((TM, TN_FFN_OUT), lambda i, j: (i, out(j))),
        out_shape=jax.ShapeDtypeStruct((n, D_MODEL), F32),
        scratch_shapes=[pltpu.VMEM((TM, D_MODEL), BF16), pltpu.VMEM((TM, D_FF), BF16)],
        compiler_params=_params("parallel", "arbitrary"),
        name="ffn",
    )(x, x, mod, mod, mod, norm_g.reshape(1, D_MODEL), w_in, w_in, w_out)


def _inproj_kernel(x_ref, sh_ref, sc_ref, ng_ref, w_ref, o_ref, h_scr):
    @pl.when(pl.program_id(1) == 0)
    def _():
        h = _rms(x_ref[...], ng_ref[...]) * (1.0 + sc_ref[...]) + sh_ref[...]
        h_scr[...] = h.astype(BF16)

    o_ref[...] = _dot(h_scr[...], w_ref[...])


def _inproj(x, mod, norm_g, w_in, layer, rows_per_cond):
    n = x.shape[0]
    return pl.pallas_call(
        _inproj_kernel,
        grid=(n // TM, IN_COLS // TN_IN),
        in_specs=[
            pl.BlockSpec((TM, D_MODEL), lambda i, j: (i, 0)),
            _mod_spec(3, rows_per_cond),
            _mod_spec(4, rows_per_cond),
            pl.BlockSpec((1, D_MODEL), lambda i, j: (0, 0)),
            pl.BlockSpec((None, D_MODEL, TN_IN), lambda i, j: (layer, 0, j)),
        ],
        out_specs=pl.BlockSpec((TM, TN_IN), lambda i, j: (i, j)),
        out_shape=jax.ShapeDtypeStruct((n, IN_COLS), F32),
        scratch_shapes=[pltpu.VMEM((TM, D_MODEL), BF16)],
        compiler_params=_params("parallel", "arbitrary"),
        name="inproj",
    )(x, mod, mod, norm_g.reshape(1, D_MODEL), w_in)


def _softmax_sink(scores, values, sink):
    m = sink
    for s in scores:
        m = jnp.maximum(jnp.max(s, axis=-1, keepdims=True), m)
    den = jnp.exp(sink - m)
    acc = None
    for s, v in zip(scores, values):
        p = jnp.exp(s - m)
        den = den + jnp.sum(p, axis=-1, keepdims=True)
        pv = _dot(p.astype(BF16), v)
        acc = pv if acc is None else acc + pv
    return acc / den


def _ctx_attn_kernel(sink_ref, q_ref, k_ref, v_ref, qg_ref, kg_ref, o_ref, kc_ref, vc_ref):
    kv = pl.program_id(1)
    kn = _rms(k_ref[...], kg_ref[...])
    v = v_ref[...]
    kc_ref[...] = kn
    vc_ref[...] = v
    kb = kn.astype(BF16)
    vb = v.astype(BF16)
    for g in range(Q_PER_KV):
        cols = slice(g * HEAD_DIM, (g + 1) * HEAD_DIM)
        qn = _rms(q_ref[:, cols], qg_ref[...]).astype(BF16)
        s = _dot_nt(qn, kb) * ATTN_SCALE
        o = _softmax_sink([s], [vb], sink_ref[kv, g])
        o_ref[:, cols] = o.astype(BF16)


def _ctx_attention(z, sink, q_g, k_g, batch, seq):
    qw = Q_PER_KV * HEAD_DIM
    cache_shape = jax.ShapeDtypeStruct((batch, N_KV_HEADS, seq, HEAD_DIM), F32)
    cache_spec = pl.BlockSpec((None, None, seq, HEAD_DIM), lambda b, kv: (b, kv, 0, 0))
    return pl.pallas_call(
        _ctx_attn_kernel,
        grid=(batch, N_KV_HEADS),
        in_specs=[
            pl.BlockSpec(memory_space=pltpu.SMEM),
            pl.BlockSpec((seq, qw), lambda b, kv: (b, kv)),
            pl.BlockSpec((seq, HEAD_DIM), lambda b, kv: (b, COL_K + kv)),
            pl.BlockSpec((seq, HEAD_DIM), lambda b, kv: (b, COL_V + kv)),
            pl.BlockSpec((1, HEAD_DIM), lambda b, kv: (0, 0)),
            pl.BlockSpec((1, HEAD_DIM), lambda b, kv: (0, 0)),
        ],
        out_specs=[pl.BlockSpec((seq, qw), lambda b, kv: (b, kv)), cache_spec, cache_spec],
        out_shape=[jax.ShapeDtypeStruct((batch * seq, D_ATTN), BF16), cache_shape, cache_shape],
        compiler_params=_params("parallel", "parallel"),
        name="ctx_attention",
    )(sink, z, z, z, q_g.reshape(1, HEAD_DIM), k_g.reshape(1, HEAD_DIM))


def _rope(x, cos, sin_a, sin_b):
    quarter = HEAD_DIM // 4
    up = pltpu.roll(x, HEAD_DIM - quarter, 1)
    down = pltpu.roll(x, quarter, 1)
    return x * cos + up * sin_a + down * sin_b


def _lat_attn_kernel(sink_ref, q_ref, k_ref, v_ref, kc_ref, vc_ref, qg_ref, kg_ref,
                     cos_ref, sa_ref, sb_ref, cosq_ref, saq_ref, sbq_ref, o_ref, k_scr, v_scr, kc_scr, vc_scr):
    kv = pl.program_id(1)
    qt = pl.program_id(2)
    seq = k_ref.shape[0]
    span = 3 * ATTN_BLOCK

    @pl.when(qt == 0)
    def _():
        kn = _rms(k_ref[...], kg_ref[...])
        k_scr[...] = _rope(kn, cos_ref[...], sa_ref[...], sb_ref[...]).astype(BF16)
        v_scr[...] = v_ref[...].astype(BF16)
        kc_scr[...] = kc_ref[...].astype(BF16)
        vc_scr[...] = vc_ref[...].astype(BF16)

    kcb = kc_scr[...]
    vcb = vc_scr[...]
    for blk in range(LAT_Q_TILE // ATTN_BLOCK):
        rows = slice(blk * ATTN_BLOCK, (blk + 1) * ATTN_BLOCK)
        q0 = (qt * (LAT_Q_TILE // ATTN_BLOCK) + blk) * ATTN_BLOCK
        start = pl.multiple_of(jnp.clip(q0 - ATTN_BLOCK, 0, seq - span), ATTN_BLOCK)
        kl = k_scr[pl.ds(start, span), :]
        vl = v_scr[pl.ds(start, span), :]
        qpos = q0 + lax.broadcasted_iota(jnp.int32, (ATTN_BLOCK, span), 0)
        kpos = start + lax.broadcasted_iota(jnp.int32, (ATTN_BLOCK, span), 1)
        valid = jnp.abs(qpos - kpos) <= WINDOW
        cos, sa, sb = cosq_ref[rows, :], saq_ref[rows, :], sbq_ref[rows, :]
        for g in range(Q_PER_KV):
            cols = slice(g * HEAD_DIM, (g + 1) * HEAD_DIM)
            qn = _rope(_rms(q_ref[rows, cols], qg_ref[...]), cos, sa, sb).astype(BF16)
            s_loc = jnp.where(valid, _dot_nt(qn, kl) * ATTN_SCALE, MASK_VALUE)
            s_ctx = _dot_nt(qn, kcb) * ATTN_SCALE
            o = _softmax_sink([s_loc, s_ctx], [vl, vcb], sink_ref[kv, g])
            o_ref[rows, cols] = o.astype(BF16)


def _rope_tables(seq):
    quarter = HEAD_DIM // 4
    pos = np.arange(seq)
    inv_freq = ROPE_BASE ** (-np.arange(quarter, dtype=np.float32) / quarter)
    inv_freq = jnp.asarray(inv_freq, F32)
    zero = jnp.zeros((seq, quarter), F32)

    def trig(p):
        ang = jnp.asarray(p, F32)[:, None] * inv_freq
        return jnp.cos(ang), jnp.sin(ang)

    cr, sr = trig(pos // GRID_W)
    cc, sc = trig(pos % GRID_W)
    cos = jnp.concatenate([cr, cr, cc, cc], axis=-1)
    sin_a = jnp.concatenate([-sr, zero, -sc, zero], axis=-1)
    sin_b = jnp.concatenate([zero, sr, zero, sc], axis=-1)
    return cos, sin_a, sin_b


def _lat_attention(z, sink, q_g, k_g, k_ctx, v_ctx, tables, batch, seq):
    qw = Q_PER_KV * HEAD_DIM
    nqt = seq // LAT_Q_TILE
    past = k_ctx.shape[2]
    cos, sin_a, sin_b = tables
    full_tab = pl.BlockSpec((seq, HEAD_DIM), lambda b, kv, qt: (0, 0))
    tile_tab = pl.BlockSpec((LAT_Q_TILE, HEAD_DIM), lambda b, kv, qt: (qt, 0))
    ctx_spec = pl.BlockSpec((None, None, past, HEAD_DIM), lambda b, kv, qt: (b, kv, 0, 0))
    gain = pl.BlockSpec((1, HEAD_DIM), lambda b, kv, qt: (0, 0))
    return pl.pallas_call(
        _lat_attn_kernel,
        grid=(batch, N_KV_HEADS, nqt),
        in_specs=[
            pl.BlockSpec(memory_space=pltpu.SMEM),
            pl.BlockSpec((LAT_Q_TILE, qw), lambda b, kv, qt: (b * nqt + qt, kv)),
            pl.BlockSpec((seq, HEAD_DIM), lambda b, kv, qt: (b, COL_K + kv)),
            pl.BlockSpec((seq, HEAD_DIM), lambda b, kv, qt: (b, COL_V + kv)),
            ctx_spec, ctx_spec, gain, gain,
            full_tab, full_tab, full_tab, tile_tab, tile_tab, tile_tab,
        ],
        out_specs=pl.BlockSpec((LAT_Q_TILE, qw), lambda b, kv, qt: (b * nqt + qt, kv)),
        out_shape=jax.ShapeDtypeStruct((batch * seq, D_ATTN), BF16),
        scratch_shapes=[pltpu.VMEM((seq, HEAD_DIM), BF16), pltpu.VMEM((seq, HEAD_DIM), BF16),
                        pltpu.VMEM((past, HEAD_DIM), BF16), pltpu.VMEM((past, HEAD_DIM), BF16)],
        compiler_params=_params("parallel", "parallel", "arbitrary"),
        name="lat_attention",
    )(sink, z, z, z, k_ctx, v_ctx, q_g.reshape(1, HEAD_DIM), k_g.reshape(1, HEAD_DIM),
      cos, sin_a, sin_b, cos, sin_a, sin_b)


def _split3(x):
    hi = x.astype(BF16)
    r1 = x - hi.astype(F32)
    mid = r1.astype(BF16)
    lo = (r1 - mid.astype(F32)).astype(BF16)
    return hi, mid, lo


def _rec_gates(forward, zf, lb):
    c = REC_CHUNK
    t = jnp.exp(-jnp.abs(zf))
    r = 1.0 / (1.0 + t)
    tr = t * r
    nonneg = zf >= 0
    sig_pos = jnp.where(nonneg, r, tr)
    sig_neg = jnp.where(nonneg, tr, r)
    one_m_lb = 1.0 - lb
    log_f = jnp.log(jnp.maximum(lb + one_m_lb * sig_pos, GATE_FLOOR))
    k = one_m_lb * sig_neg
    ri = lax.broadcasted_iota(jnp.int32, (c, c), 0)
    ci = lax.broadcasted_iota(jnp.int32, (c, c), 1)
    tri = jnp.where((ci <= ri) if forward else (ci >= ri), 1.0, 0.0).astype(BF16)
    hi, mid, lo = _split3(log_f)
    cum = _dot(tri, hi) + _dot(tri, mid) + _dot(tri, lo)
    return k, cum


def _rec_same_block(forward, q, k, cum):
    sub_row = lax.broadcasted_iota(jnp.int32, (SUBLANES, REC_DK), 0)
    pieces = []
    for b in range(REC_CHUNK // SUBLANES):
        blk = slice(b * SUBLANES, (b + 1) * SUBLANES)
        cum_b, q_b = cum[blk, :], q[blk, :]
        for sl in range(SUBLANES):
            s = b * SUBLANES + sl
            keep = (sub_row >= sl) if forward else (sub_row <= sl)
            decay = jnp.exp(jnp.minimum(cum_b - cum[s:s + 1, :], 0.0))
            pieces.append(jnp.where(keep, decay * q_b * k[s:s + 1, :], 0.0).astype(BF16))
    return _dot(jnp.concatenate(pieces, axis=0), jnp.ones((REC_DK, REC_DV), BF16))


def _rec_cross_block(forward, q, k, v, cum):
    c = REC_CHUNK
    q_parts, k_parts, v_parts, segments = [], [], [], []
    for tb in range(c // SUBLANES):
        blk = slice(tb * SUBLANES, (tb + 1) * SUBLANES)
        src, ref_row = (slice(0, blk.start), blk.start - 1) if forward else (slice(blk.stop, c), blk.stop)
        if src.stop == src.start:
            q_parts.append(jnp.zeros((SUBLANES, REC_DK), F32))
            continue
        ref = cum[ref_row:ref_row + 1, :]
        q_parts.append(q[blk, :] * jnp.exp(cum[blk, :] - ref))
        k_parts.append(k[src, :] * jnp.exp(ref - cum[src, :]))
        v_parts.append(v[src, :])
        segments.append((tb, src.stop - src.start))
    a = _dot_nt(jnp.concatenate(q_parts, axis=0).astype(BF16), jnp.concatenate(k_parts, axis=0).astype(BF16))
    return a, segments, jnp.concatenate(v_parts, axis=0).astype(BF16)


def _rec_cross_apply(a, segments, v_all):
    c, ncol = a.shape
    row_blk = lax.broadcasted_iota(jnp.int32, (c, ncol), 0) // SUBLANES
    col = lax.broadcasted_iota(jnp.int32, (c, ncol), 1)
    col_blk = jnp.full((c, ncol), -1, jnp.int32)
    start = 0
    for tb, width in segments:
        col_blk = jnp.where((col >= start) & (col < start + width), tb, col_blk)
        start += width
    return _dot(jnp.where(row_blk == col_blk, a, 0.0).astype(BF16), v_all)


def _rec_same_apply(lane_sums, v):
    o_blocks = []
    for b in range(REC_CHUNK // SUBLANES):
        acc = None
        for sl in range(SUBLANES):
            s = b * SUBLANES + sl
            term = lane_sums[s * SUBLANES:(s + 1) * SUBLANES, :] * v[s:s + 1, :]
            acc = term if acc is None else acc + term
        o_blocks.append(acc)
    return jnp.concatenate(o_blocks, axis=0)


def _rec_kernel(layer, has_init, *refs):
    if has_init:
        (q_ref, ff_ref, fb_ref, v_ref, g_ref, lg_ref, ng_ref, s0_ref, o_ref,
         o_scr, qd_scr, kd_scr, gl_scr, st_scr) = refs
        s_out_ref = None
    else:
        (q_ref, ff_ref, fb_ref, v_ref, g_ref, lg_ref, ng_ref, o_ref, s_out_ref,
         o_scr, qd_scr, kd_scr, gl_scr, st_scr) = refs
    seq = q_ref.shape[0]
    nchunk = seq // REC_CHUNK
    c = REC_CHUNK

    logits = lg_ref[...]
    e = jnp.exp(logits - jnp.max(logits, axis=0, keepdims=True))
    p = e / jnp.sum(e, axis=0, keepdims=True)
    lb = jnp.zeros_like(p[0])
    for i in range(1, layer + 1):
        lb = lb + p[i]

    def rows_of(ci):
        return pl.ds(pl.multiple_of(ci * c, c), c)

    z_refs = (ff_ref, fb_ref)

    def local_body(it, carry):
        chunks = [it * REC_LOCAL_UNROLL + u for u in range(REC_LOCAL_UNROLL)]
        qs = [q_ref[rows_of(ci), :] for ci in chunks]
        vs = [v_ref[rows_of(ci), :] for ci in chunks]
        items = [(u, d) for u in range(REC_LOCAL_UNROLL) for d in range(2)]
        gates = [_rec_gates(d == 0, z_refs[d][rows_of(chunks[u]), :], lb[d]) for u, d in items]
        same = [_rec_same_block(d == 0, qs[u], k, cum) for (u, d), (k, cum) in zip(items, gates)]
        cross = [_rec_cross_block(d == 0, qs[u], k, vs[u], cum) for (u, d), (k, cum) in zip(items, gates)]
        totals = [None] * REC_LOCAL_UNROLL
        for (u, d), (k, cum), lane_sums, (a, segments, v_all) in zip(items, gates, same, cross):
            rows = rows_of(chunks[u])
            last = cum[c - 1:c, :] if d == 0 else cum[0:1, :]
            qd_scr[d, rows, :] = (qs[u] * jnp.exp(cum)).astype(BF16)
            kd_scr[d, rows, :] = (k * jnp.exp(last - cum)).astype(BF16)
            gl_scr[d, chunks[u]] = jnp.broadcast_to(jnp.exp(last), (SUBLANES, REC_DK))
            o_local = _rec_same_apply(lane_sums, vs[u]) + _rec_cross_apply(a, segments, v_all)
            totals[u] = o_local if totals[u] is None else totals[u] + o_local
        for u, ci in enumerate(chunks):
            o_scr[rows_of(ci), :] = totals[u]
        return carry

    lax.fori_loop(0, nchunk // REC_LOCAL_UNROLL, local_body, 0)

    for d in range(2):
        if has_init:
            st_scr[d] = s0_ref[d].T
        else:
            st_scr[d] = jnp.zeros((REC_DV, REC_DK), F32)

    def state_body(it, carry):
        steps = [it * REC_STATE_UNROLL + u for u in range(REC_STATE_UNROLL)]
        order = [(d, step if d == 0 else nchunk - 1 - step) for step in steps for d in range(2)]
        updates = [_dot_tn(v_ref[rows_of(ci), :].astype(BF16), kd_scr[d, rows_of(ci), :]) for d, ci in order]
        for (d, ci), update in zip(order, updates):
            rows = rows_of(ci)
            st = st_scr[d]
            o_scr[rows, :] += _dot_nt(qd_scr[d, rows, :], st.astype(BF16))
            decayed = (st.reshape(REC_DV // SUBLANES, SUBLANES, REC_DK) * gl_scr[d, ci]).reshape(REC_DV, REC_DK)
            st_scr[d] = decayed + update
        return carry

    lax.fori_loop(0, nchunk // REC_STATE_UNROLL, state_body, 0)
    if not has_init:
        s_out_ref[0] = st_scr[0].T
        s_out_ref[1] = st_scr[1].T

    def out_body(ci, carry):
        rows = rows_of(ci)
        gate = g_ref[rows, :]
        o_ref[rows, :] = (_rms(o_scr[rows, :], ng_ref[...]) * (gate * _sigmoid(gate))).astype(BF16)
        return carry

    lax.fori_loop(0, nchunk, out_body, 0, unroll=4)


def _recurrence(z, layer, lb_logits, norm_g, s0, batch, seq):
    has_init = s0 is not None
    col = lambda base: pl.BlockSpec((seq, REC_DK), lambda b, h: (b, base + h))
    state_spec = pl.BlockSpec((None, 2, None, REC_DK, REC_DV), lambda b, h: (b, 0, h, 0, 0))
    in_specs = [col(COL_RQ), col(COL_RF_F), col(COL_RF_B), col(COL_RI), col(COL_RG),
                pl.BlockSpec((DEPTH, 2, None, 1, REC_DK), lambda b, h: (0, 0, h, 0, 0)),
                pl.BlockSpec((None, 1, REC_DV), lambda b, h: (h, 0, 0))]
    args = [z, z, z, z, z, lb_logits, norm_g]
    out_specs = [pl.BlockSpec((seq, REC_DV), lambda b, h: (b, h))]
    out_shape = [jax.ShapeDtypeStruct((batch * seq, D_REC), BF16)]
    if has_init:
        in_specs.append(state_spec)
        args.append(s0)
    else:
        out_specs.append(state_spec)
        out_shape.append(jax.ShapeDtypeStruct((batch, 2, N_REC_HEADS, REC_DK, REC_DV), F32))
    return pl.pallas_call(
        functools.partial(_rec_kernel, layer, has_init),
        grid=(batch, N_REC_HEADS),
        in_specs=in_specs,
        out_specs=out_specs,
        out_shape=out_shape,
        scratch_shapes=[pltpu.VMEM((seq, REC_DV), F32),
                        pltpu.VMEM((2, seq, REC_DK), BF16),
                        pltpu.VMEM((2, seq, REC_DK), BF16),
                        pltpu.VMEM((2, seq // REC_CHUNK, SUBLANES, REC_DK), F32),
                        pltpu.VMEM((2, REC_DV, REC_DK), F32)],
        compiler_params=_params("parallel", "parallel"),
        name="hgrn2",
    )(*args)


def _conv_kernel(tiles_per_seq, a_ref, b_ref, at_ref, bt_ref, ab_ref, bb_ref, w_ref, bias_ref, lg_ref, lb_ref,
                 o_ref, h_scr):
    i = pl.program_id(0)
    tile_in_seq = i % tiles_per_seq
    has_top = jnp.where(tile_in_seq > 0, 1.0, 0.0)
    has_bot = jnp.where(tile_in_seq < tiles_per_seq - 1, 1.0, 0.0)
    rows = a_ref.shape[0]
    half = CONV_WIDTH // 2

    def glu(a, b):
        return a * _sigmoid(b)

    h_scr[0:CONV_HALO, :] = glu(at_ref[...], bt_ref[...]) * has_top
    h_scr[CONV_HALO:CONV_HALO + rows, :] = glu(a_ref[...], b_ref[...])
    h_scr[CONV_HALO + rows:, :] = glu(ab_ref[...], bb_ref[...]) * has_bot

    for r0 in range(0, rows, CONV_SUB):
        acc = None
        for j in range(CONV_WIDTH):
            lo = r0 + CONV_HALO - half + j
            term = h_scr[lo:lo + CONV_SUB, :] * w_ref[j:j + 1, :]
            acc = term if acc is None else acc + term
        y = acc + bias_ref[...]
        mu = jnp.mean(y, axis=-1, keepdims=True)
        yc = y - mu
        var = jnp.mean(yc * yc, axis=-1, keepdims=True)
        yn = yc * lax.rsqrt(var + EPS) * lg_ref[...] + lb_ref[...]
        o_ref[r0:r0 + CONV_SUB, :] = (yn * _sigmoid(yn)).astype(BF16)


def _conv(z, w, bias, ln_g, ln_b, seq):
    n = z.shape[0]
    tiles_per_seq = seq // CONV_ROWS
    ntiles = n // CONV_ROWS
    cw = CONV_CH // HEAD_DIM
    halo_per_tile = CONV_ROWS // CONV_HALO
    nhalo = n // CONV_HALO
    ca, cb = COL_CA // cw, COL_CB // cw
    mid = lambda c: pl.BlockSpec((CONV_ROWS, CONV_CH), lambda i: (i, c))
    top = lambda c: pl.BlockSpec((CONV_HALO, CONV_CH), lambda i: (jnp.maximum(i * halo_per_tile - 1, 0), c))
    bot = lambda c: pl.BlockSpec((CONV_HALO, CONV_CH), lambda i: (jnp.minimum((i + 1) * halo_per_tile, nhalo - 1), c))
    vec = pl.BlockSpec((1, CONV_CH), lambda i: (0, 0))
    return pl.pallas_call(
        functools.partial(_conv_kernel, tiles_per_seq),
        grid=(ntiles,),
        in_specs=[mid(ca), mid(cb), top(ca), top(cb), bot(ca), bot(cb),
                  pl.BlockSpec((CONV_WIDTH, CONV_CH), lambda i: (0, 0)), vec, vec, vec],
        out_specs=pl.BlockSpec((CONV_ROWS, CONV_CH), lambda i: (i, 0)),
        out_shape=jax.ShapeDtypeStruct((n, CONV_CH), BF16),
        scratch_shapes=[pltpu.VMEM((CONV_ROWS + 2 * CONV_HALO, CONV_CH), F32)],
        compiler_params=_params("parallel"),
        name="conv",
    )(z, z, z, z, z, z, w, bias.reshape(1, CONV_CH), ln_g.reshape(1, CONV_CH), ln_b.reshape(1, CONV_CH))


def _outproj_kernel(x_ref, g_ref, a_ref, r_ref, c_ref, wa_ref, wr_ref, wc_ref, o_ref):
    mix = _dot(a_ref[...], wa_ref[...]) + _dot(r_ref[...], wr_ref[...]) + _dot(c_ref[...], wc_ref[...])
    o_ref[...] = x_ref[...] + g_ref[...] * mix


def _outproj(x, mod, attn, rec, conv, w_out, layer, rows_per_cond):
    n = x.shape[0]
    tn = D_MODEL // 2
    row = lambda width: pl.BlockSpec((TM, width), lambda i, j: (i, 0))
    return pl.pallas_call(
        _outproj_kernel,
        grid=(n // TM, D_MODEL // tn),
        in_specs=[
            pl.BlockSpec((TM, tn), lambda i, j: (i, j)),
            pl.BlockSpec((None, None, 1, tn), lambda i, j: ((i * TM) // rows_per_cond, 5, 0, j)),
            row(D_ATTN), row(D_REC), row(CONV_CH),
            pl.BlockSpec((None, D_ATTN, tn), lambda i, j: (layer, 0, j)),
            pl.BlockSpec((None, D_REC, tn), lambda i, j: (layer, D_ATTN // D_REC, j)),
            pl.BlockSpec((None, CONV_CH, tn), lambda i, j: (layer, (D_ATTN + D_REC) // CONV_CH, j)),
        ],
        out_specs=pl.BlockSpec((TM, tn), lambda i, j: (i, j)),
        out_shape=jax.ShapeDtypeStruct((n, D_MODEL), F32),
        compiler_params=_params("parallel", "parallel"),
        name="outproj",
    )(x, mod, attn, rec, conv, w_out, w_out, w_out)


def _trunk_layer(x, mod, l, P, mixers, rows_per_cond):
    x = _ffn(x, mod, 0, P['norm_g'][l, 0], P['w_ffn_in'], P['w_ffn_out'], l, 0, rows_per_cond)
    z = _inproj(x, mod, P['norm_g'][l, 1], P['w_in'], l, rows_per_cond)
    attn, rec, conv, extras = mixers(z)
    x = _outproj(x, mod, attn, rec, conv, P['w_out'], l, rows_per_cond)
    x = _ffn(x, mod, 6, P['norm_g'][l, 2], P['w_ffn_in'], P['w_ffn_out'], l, 1, rows_per_cond)
    return x, extras


def kernel(x_prompt, x_sample, cache_k, cache_v, state_rec, c, c_ctx, w_ada, b_ada, norm_g, w_ffn_in, w_ffn_out,
           w_in, w_out, q_norm_g, k_norm_g, attn_sink, rec_lb_logits, rec_norm_g, conv_w, conv_b, conv_ln_g, conv_ln_b):
    batch, seq, _ = x_prompt.shape
    dec_batch, dec_seq, _ = x_sample.shape
    assert seq % TM == 0 or TM % seq == 0
    assert dec_seq % TM == 0 and dec_seq % LAT_Q_TILE == 0 and seq % CONV_ROWS == 0 and dec_seq % CONV_ROWS == 0

    P = {'norm_g': norm_g, 'w_ffn_in': w_ffn_in.astype(BF16), 'w_ffn_out': w_ffn_out.astype(BF16),
         'w_in': w_in.astype(BF16), 'w_out': w_out.astype(BF16)}

    lb_logits = rec_lb_logits.reshape(DEPTH, 2, N_REC_HEADS, 1, REC_DK)

    cond8 = jnp.zeros((SUBLANES, D_MODEL), F32).at[0].set(c_ctx).at[1:1 + dec_batch].set(c)
    mod = _modulation(cond8, w_ada, b_ada).reshape(DEPTH, SUBLANES, N_MOD, 1, D_MODEL)
    tables = _rope_tables(dec_seq)
    sinks = attn_sink.reshape(DEPTH, N_KV_HEADS, Q_PER_KV)
    rec_g = rec_norm_g.reshape(DEPTH, N_REC_HEADS, 1, REC_DV)

    def shared_mixers(z, l, seq_len):
        return _conv(z, conv_w[l], conv_b[l], conv_ln_g[l], conv_ln_b[l], seq_len)

    h = x_prompt.reshape(batch * seq, D_MODEL)
    ks, vs, ss = [], [], []
    for l in range(DEPTH):
        def ctx_mixers(z, l=l):
            attn, k_l, v_l = _ctx_attention(z, sinks[l], q_norm_g[l], k_norm_g[l], batch, seq)
            rec, s_l = _recurrence(z, l, lb_logits, rec_g[l], None, batch, seq)
            return attn, rec, shared_mixers(z, l, seq), (k_l, v_l, s_l)

        h, (k_l, v_l, s_l) = _trunk_layer(h, mod[l, 0:1], l, P, ctx_mixers, batch * seq)
        ks.append(k_l)
        vs.append(v_l)
        ss.append(s_l)
    y_prompt = h.reshape(batch, seq, D_MODEL)

    h = x_sample.reshape(dec_batch * dec_seq, D_MODEL)
    for l in range(DEPTH):
        def lat_mixers(z, l=l):
            attn = _lat_attention(z, sinks[l], q_norm_g[l], k_norm_g[l], cache_k[:, l], cache_v[:, l], tables,
                                  dec_batch, dec_seq)
            rec = _recurrence(z, l, lb_logits, rec_g[l], state_rec[:, l], dec_batch, dec_seq)[0]
            return attn, rec, shared_mixers(z, l, dec_seq), None

        h, _ = _trunk_layer(h, mod[l, 1:1 + dec_batch], l, P, lat_mixers, dec_seq)
    y_sample = h.reshape(dec_batch, dec_seq, D_MODEL)

    return (y_prompt, y_sample, jnp.stack(ks, axis=1), jnp.stack(vs, axis=1), jnp.stack(ss, axis=1))
```

```python
import functools

import jax
import jax.numpy as jnp
import numpy as np
from jax import lax
from jax.experimental import pallas as pl
from jax.experimental.pallas import tpu as pltpu

F32 = jnp.float32
BF16 = jnp.bfloat16

D_MODEL = 2048
DEPTH = 2
GRID_W = 64
HEAD_DIM = 128
D_ATTN = D_MODEL // 2
N_Q_HEADS = D_ATTN // HEAD_DIM
N_KV_HEADS = 2
Q_PER_KV = N_Q_HEADS // N_KV_HEADS
WINDOW = 128
ATTN_BLOCK = 128
ATTN_SCALE = HEAD_DIM ** -0.5
ROPE_BASE = 10000.0
MASK_VALUE = -1e30
D_REC = D_MODEL // 4
REC_DK = 128
REC_DV = 128
N_REC_HEADS = D_REC // REC_DV
REC_CHUNK = 32
CONV_CH = D_MODEL // 4
CONV_WIDTH = 31
D_FF = 5632
N_MOD = 9
EPS = 1e-6
GATE_FLOOR = 1e-30
IN_COLS = 5120

COL_Q = 0
COL_K = 8
COL_V = 10
COL_RQ = 12
COL_RF_F = 16
COL_RF_B = 20
COL_RI = 24
COL_RG = 28
COL_CA = 32
COL_CB = 36

SUBLANES = 8
VMEM_LIMIT = 56 * 1024 * 1024

TM = 512
TF = 512
TN_FFN_OUT = 512
TN_IN = 512
TN_ADA = 1024
CONV_ROWS = 256
CONV_HALO = 16
CONV_SUB = 32
LAT_Q_TILE = 1024
REC_LOCAL_UNROLL = 4
REC_STATE_UNROLL = 4


def _params(*sem):
    return pltpu.CompilerParams(dimension_semantics=sem, vmem_limit_bytes=VMEM_LIMIT)


def _dot(a, b):
    return jnp.dot(a, b, preferred_element_type=F32)


def _dot_nt(a, b):
    return lax.dot_general(a, b, (((1,), (1,)), ((), ())), preferred_element_type=F32)


def _dot_tn(a, b):
    return lax.dot_general(a, b, (((0,), (0,)), ((), ())), preferred_element_type=F32)


def _rms(x, g):
    return x * lax.rsqrt(jnp.mean(x * x, axis=-1, keepdims=True) + EPS) * g


def _sigmoid(x):
    return 1.0 / (1.0 + jnp.exp(-x))


def _ada_kernel(c_ref, w_ref, b_ref, o_ref):
    c = c_ref[...]
    s = (c * _sigmoid(c)).astype(BF16)
    o_ref[...] = _dot(s, w_ref[...].astype(BF16)) + b_ref[...]


def _modulation(cond8, w_ada, b_ada):
    ncol = N_MOD * D_MODEL
    return pl.pallas_call(
        _ada_kernel,
        grid=(DEPTH, ncol // TN_ADA),
        in_specs=[
            pl.BlockSpec((SUBLANES, D_MODEL), lambda l, j: (0, 0)),
            pl.BlockSpec((None, D_MODEL, TN_ADA), lambda l, j: (l, 0, j)),
            pl.BlockSpec((None, 1, TN_ADA), lambda l, j: (l, 0, j)),
        ],
        out_specs=pl.BlockSpec((None, SUBLANES, TN_ADA), lambda l, j: (l, 0, j)),
        out_shape=jax.ShapeDtypeStruct((DEPTH, SUBLANES, ncol), F32),
        compiler_params=_params("parallel", "parallel"),
        name="modulation",
    )(cond8, w_ada, b_ada.reshape(DEPTH, 1, ncol))


def _mod_spec(chunk, rows_per_cond):
    return pl.BlockSpec((None, None, 1, D_MODEL), lambda i, j: ((i * TM) // rows_per_cond, chunk, 0, 0))


def _ffn_kernel(nf, x_ref, xres_ref, sh_ref, sc_ref, g_ref, ng_ref, wa_ref, wb_ref, wo_ref, o_ref, h_scr, act_scr):
    j = pl.program_id(1)

    @pl.when(j == 0)
    def _():
        h = _rms(x_ref[...], ng_ref[...]) * (1.0 + sc_ref[...]) + sh_ref[...]
        h_scr[...] = h.astype(BF16)

    @pl.when(j < nf)
    def _():
        h = h_scr[...]
        a = _dot(h, wa_ref[...])
        b = _dot(h, wb_ref[...])
        act_scr[:, pl.ds(pl.multiple_of(j * TF, TF), TF)] = (a * _sigmoid(a) * b).astype(BF16)

    @pl.when(j >= nf)
    def _():
        o_ref[...] = xres_ref[...] + 0.5 * g_ref[...] * _dot(act_scr[...], wo_ref[...])


def _ffn(x, mod, first_chunk, norm_g, w_in, w_out, layer, which, rows_per_cond):
    n = x.shape[0]
    nf = D_FF // TF
    nout = D_MODEL // TN_FFN_OUT
    hid = lambda j: jnp.minimum(j, nf - 1)
    out = lambda j: jnp.maximum(j - nf, 0)
    return pl.pallas_call(
        functools.partial(_ffn_kernel, nf),
        grid=(n // TM, nf + nout),
        in_specs=[
            pl.BlockSpec((TM, D_MODEL), lambda i, j: (i, 0)),
            pl.BlockSpec((TM, TN_FFN_OUT), lambda i, j: (i, out(j))),
            _mod_spec(first_chunk, rows_per_cond),
            _mod_spec(first_chunk + 1, rows_per_cond),
            pl.BlockSpec((None, None, 1, TN_FFN_OUT),
                         lambda i, j: ((i * TM) // rows_per_cond, first_chunk + 2, 0, out(j))),
            pl.BlockSpec((1, D_MODEL), lambda i, j: (0, 0)),
            pl.BlockSpec((None, None, None, D_MODEL, TF), lambda i, j: (layer, which, hid(j), 0, 0)),
            pl.BlockSpec((None, None, None, D_MODEL, TF), lambda i, j: (layer, which, hid(j) + nf, 0, 0)),
            pl.BlockSpec((None, None, None, D_FF, TN_FFN_OUT), lambda i, j: (layer, which, out(j), 0, 0)),
        ],
        out_specs=pl.BlockSpec((TM, TN_FFN_OUT), lambda i, j: (i, out(j))),
        out_shape=jax.ShapeDtypeStruct((n, D_MODEL), F32),
        scratch_shapes=[pltpu.VMEM((TM, D_MODEL), BF16), pltpu.VMEM((TM, D_FF), BF16)],
        compiler_params=_params("parallel", "arbitrary"),
        name="ffn",
    )(x, x, mod, mod, mod, norm_g.reshape(1, D_MODEL), w_in, w_in, w_out)


def _inproj_kernel(x_ref, sh_ref, sc_ref, ng_ref, w_ref, o_ref, h_scr):
    @pl.when(pl.program_id(1) == 0)
    def _():
        h = _rms(x_ref[...], ng_ref[...]) * (1.0 + sc_ref[...]) + sh_ref[...]
        h_scr[...] = h.astype(BF16)

    o_ref[...] = _dot(h_scr[...], w_ref[...])


def _inproj(x, mod, norm_g, w_in, layer, rows_per_cond):
    n = x.shape[0]
    return pl.pallas_call(
        _inproj_kernel,
        grid=(n // TM, IN_COLS // TN_IN),
        in_specs=[
            pl.BlockSpec((TM, D_MODEL), lambda i, j: (i, 0)),
            _mod_spec(3, rows_per_cond),
            _mod_spec(4, rows_per_cond),
            pl.BlockSpec((1, D_MODEL), lambda i, j: (0, 0)),
            pl.BlockSpec((None, None, D_MODEL, TN_IN), lambda i, j: (layer, j, 0, 0)),
        ],
        out_specs=pl.BlockSpec((TM, TN_IN), lambda i, j: (i, j)),
        out_shape=jax.ShapeDtypeStruct((n, IN_COLS), F32),
        scratch_shapes=[pltpu.VMEM((TM, D_MODEL), BF16)],
        compiler_params=_params("parallel", "arbitrary"),
        name="inproj",
    )(x, mod, mod, norm_g.reshape(1, D_MODEL), w_in)


def _softmax_sink(scores, values, sink):
    m = sink
    for s in scores:
        m = jnp.maximum(jnp.max(s, axis=-1, keepdims=True), m)
    den = jnp.exp(sink - m)
    acc = None
    for s, v in zip(scores, values):
        p = jnp.exp(s - m)
        den = den + jnp.sum(p, axis=-1, keepdims=True)
        pv = _dot(p.astype(BF16), v)
        acc = pv if acc is None else acc + pv
    return acc / den


def _ctx_attn_kernel(sink_ref, q_ref, k_ref, v_ref, qg_ref, kg_ref, o_ref, kc_ref, vc_ref):
    kv = pl.program_id(1)
    kn = _rms(k_ref[...], kg_ref[...])
    v = v_ref[...]
    kc_ref[...] = kn
    vc_ref[...] = v
    kb = kn.astype(BF16)
    vb = v.astype(BF16)
    for g in range(Q_PER_KV):
        cols = slice(g * HEAD_DIM, (g + 1) * HEAD_DIM)
        qn = _rms(q_ref[:, cols], qg_ref[...]).astype(BF16)
        s = _dot_nt(qn, kb) * ATTN_SCALE
        o = _softmax_sink([s], [vb], sink_ref[kv, g])
        o_ref[:, cols] = o.astype(BF16)


def _ctx_attention(z, sink, q_g, k_g, batch, seq):
    qw = Q_PER_KV * HEAD_DIM
    cache_shape = jax.ShapeDtypeStruct((batch, N_KV_HEADS, seq, HEAD_DIM), F32)
    cache_spec = pl.BlockSpec((None, None, seq, HEAD_DIM), lambda b, kv: (b, kv, 0, 0))
    return pl.pallas_call(
        _ctx_attn_kernel,
        grid=(batch, N_KV_HEADS),
        in_specs=[
            pl.BlockSpec(memory_space=pltpu.SMEM),
            pl.BlockSpec((seq, qw), lambda b, kv: (b, kv)),
            pl.BlockSpec((seq, HEAD_DIM), lambda b, kv: (b, COL_K + kv)),
            pl.BlockSpec((seq, HEAD_DIM), lambda b, kv: (b, COL_V + kv)),
            pl.BlockSpec((1, HEAD_DIM), lambda b, kv: (0, 0)),
            pl.BlockSpec((1, HEAD_DIM), lambda b, kv: (0, 0)),
        ],
        out_specs=[pl.BlockSpec((seq, qw), lambda b, kv: (b, kv)), cache_spec, cache_spec],
        out_shape=[jax.ShapeDtypeStruct((batch * seq, D_ATTN), BF16), cache_shape, cache_shape],
        compiler_params=_params("parallel", "parallel"),
        name="ctx_attention",
    )(sink, z, z, z, q_g.reshape(1, HEAD_DIM), k_g.reshape(1, HEAD_DIM))


def _rope(x, cos, sin_a, sin_b):
    quarter = HEAD_DIM // 4
    up = pltpu.roll(x, HEAD_DIM - quarter, 1)
    down = pltpu.roll(x, quarter, 1)
    return x * cos + up * sin_a + down * sin_b


def _lat_attn_kernel(sink_ref, q_ref, k_ref, v_ref, kc_ref, vc_ref, qg_ref, kg_ref,
                     cos_ref, sa_ref, sb_ref, cosq_ref, saq_ref, sbq_ref, o_ref, k_scr, v_scr, kc_scr, vc_scr):
    kv = pl.program_id(1)
    qt = pl.program_id(2)
    seq = k_ref.shape[0]
    span = 3 * ATTN_BLOCK

    @pl.when(qt == 0)
    def _():
        kn = _rms(k_ref[...], kg_ref[...])
        k_scr[...] = _rope(kn, cos_ref[...], sa_ref[...], sb_ref[...]).astype(BF16)
        v_scr[...] = v_ref[...].astype(BF16)
        kc_scr[...] = kc_ref[...].astype(BF16)
        vc_scr[...] = vc_ref[...].astype(BF16)

    kcb = kc_scr[...]
    vcb = vc_scr[...]
    for blk in range(LAT_Q_TILE // ATTN_BLOCK):
        rows = slice(blk * ATTN_BLOCK, (blk + 1) * ATTN_BLOCK)
        q0 = (qt * (LAT_Q_TILE // ATTN_BLOCK) + blk) * ATTN_BLOCK
        start = pl.multiple_of(jnp.clip(q0 - ATTN_BLOCK, 0, seq - span), ATTN_BLOCK)
        kl = k_scr[pl.ds(start, span), :]
        vl = v_scr[pl.ds(start, span), :]
        qpos = q0 + lax.broadcasted_iota(jnp.int32, (ATTN_BLOCK, span), 0)
        kpos = start + lax.broadcasted_iota(jnp.int32, (ATTN_BLOCK, span), 1)
        valid = jnp.abs(qpos - kpos) <= WINDOW
        cos, sa, sb = cosq_ref[rows, :], saq_ref[rows, :], sbq_ref[rows, :]
        for g in range(Q_PER_KV):
            cols = slice(g * HEAD_DIM, (g + 1) * HEAD_DIM)
            qn = _rope(_rms(q_ref[rows, cols], qg_ref[...]), cos, sa, sb).astype(BF16)
            s_loc = jnp.where(valid, _dot_nt(qn, kl) * ATTN_SCALE, MASK_VALUE)
            s_ctx = _dot_nt(qn, kcb) * ATTN_SCALE
            o = _softmax_sink([s_loc, s_ctx], [vl, vcb], sink_ref[kv, g])
            o_ref[rows, cols] = o.astype(BF16)


def _rope_tables(seq):
    quarter = HEAD_DIM // 4
    pos = np.arange(seq)
    inv_freq = ROPE_BASE ** (-np.arange(quarter, dtype=np.float32) / quarter)
    inv_freq = jnp.asarray(inv_freq, F32)
    zero = jnp.zeros((seq, quarter), F32)

    def trig(p):
        ang = jnp.asarray(p, F32)[:, None] * inv_freq
        return jnp.cos(ang), jnp.sin(ang)

    cr, sr = trig(pos // GRID_W)
    cc, sc = trig(pos % GRID_W)
    cos = jnp.concatenate([cr, cr, cc, cc], axis=-1)
    sin_a = jnp.concatenate([-sr, zero, -sc, zero], axis=-1)
    sin_b = jnp.concatenate([zero, sr, zero, sc], axis=-1)
    return cos, sin_a, sin_b


def _lat_attention(z, sink, q_g, k_g, k_ctx, v_ctx, tables, batch, seq):
    qw = Q_PER_KV * HEAD_DIM
    nqt = seq // LAT_Q_TILE
    past = k_ctx.shape[2]
    cos, sin_a, sin_b = tables
    full_tab = pl.BlockSpec((seq, HEAD_DIM), lambda b, kv, qt: (0, 0))
    tile_tab = pl.BlockSpec((LAT_Q_TILE, HEAD_DIM), lambda b, kv, qt: (qt, 0))
    ctx_spec = pl.BlockSpec((None, None, past, HEAD_DIM), lambda b, kv, qt: (b, kv, 0, 0))
    gain = pl.BlockSpec((1, HEAD_DIM), lambda b, kv, qt: (0, 0))
    return pl.pallas_call(
        _lat_attn_kernel,
        grid=(batch, N_KV_HEADS, nqt),
        in_specs=[
            pl.BlockSpec(memory_space=pltpu.SMEM),
            pl.BlockSpec((LAT_Q_TILE, qw), lambda b, kv, qt: (b * nqt + qt, kv)),
            pl.BlockSpec((seq, HEAD_DIM), lambda b, kv, qt: (b, COL_K + kv)),
            pl.BlockSpec((seq, HEAD_DIM), lambda b, kv, qt: (b, COL_V + kv)),
            ctx_spec, ctx_spec, gain, gain,
            full_tab, full_tab, full_tab, tile_tab, tile_tab, tile_tab,
        ],
        out_specs=pl.BlockSpec((LAT_Q_TILE, qw), lambda b, kv, qt: (b * nqt + qt, kv)),
        out_shape=jax.ShapeDtypeStruct((batch * seq, D_ATTN), BF16),
        scratch_shapes=[pltpu.VMEM((seq, HEAD_DIM), BF16), pltpu.VMEM((seq, HEAD_DIM), BF16),
                        pltpu.VMEM((past, HEAD_DIM), BF16), pltpu.VMEM((past, HEAD_DIM), BF16)],
        compiler_params=_params("parallel", "parallel", "arbitrary"),
        name="lat_attention",
    )(sink, z, z, z, k_ctx, v_ctx, q_g.reshape(1, HEAD_DIM), k_g.reshape(1, HEAD_DIM),
      cos, sin_a, sin_b, cos, sin_a, sin_b)


def _split3(x):
    hi = x.astype(BF16)
    r1 = x - hi.astype(F32)
    mid = r1.astype(BF16)
    lo = (r1 - mid.astype(F32)).astype(BF16)
    return hi, mid, lo


def _rec_gates(forward, zf, lb):
    c = REC_CHUNK
    t = jnp.exp(-jnp.abs(zf))
    r = 1.0 / (1.0 + t)
    tr = t * r
    nonneg = zf >= 0
    sig_pos = jnp.where(nonneg, r, tr)
    sig_neg = jnp.where(nonneg, tr, r)
    one_m_lb = 1.0 - lb
    log_f = jnp.log(jnp.maximum(lb + one_m_lb * sig_pos, GATE_FLOOR))
    k = one_m_lb * sig_neg
    ri = lax.broadcasted_iota(jnp.int32, (c, c), 0)
    ci = lax.broadcasted_iota(jnp.int32, (c, c), 1)
    tri = jnp.where((ci <= ri) if forward else (ci >= ri), 1.0, 0.0).astype(BF16)
    hi, mid, lo = _split3(log_f)
    cum = _dot(tri, hi) + _dot(tri, mid) + _dot(tri, lo)
    return k, cum


def _rec_same_block(forward, q, k, cum):
    sub_row = lax.broadcasted_iota(jnp.int32, (SUBLANES, REC_DK), 0)
    pieces = []
    for b in range(REC_CHUNK // SUBLANES):
        blk = slice(b * SUBLANES, (b + 1) * SUBLANES)
        cum_b, q_b = cum[blk, :], q[blk, :]
        for sl in range(SUBLANES):
            s = b * SUBLANES + sl
            keep = (sub_row >= sl) if forward else (sub_row <= sl)
            decay = jnp.exp(jnp.minimum(cum_b - cum[s:s + 1, :], 0.0))
            pieces.append(jnp.where(keep, decay * q_b * k[s:s + 1, :], 0.0).astype(BF16))
    return _dot(jnp.concatenate(pieces, axis=0), jnp.ones((REC_DK, REC_DV), BF16))


def _rec_cross_block(forward, q, k, v, cum):
    c = REC_CHUNK
    q_parts, k_parts, v_parts, segments = [], [], [], []
    for tb in range(c // SUBLANES):
        blk = slice(tb * SUBLANES, (tb + 1) * SUBLANES)
        src, ref_row = (slice(0, blk.start), blk.start - 1) if forward else (slice(blk.stop, c), blk.stop)
        if src.stop == src.start:
            q_parts.append(jnp.zeros((SUBLANES, REC_DK), F32))
            continue
        ref = cum[ref_row:ref_row + 1, :]
        q_parts.append(q[blk, :] * jnp.exp(cum[blk, :] - ref))
        k_parts.append(k[src, :] * jnp.exp(ref - cum[src, :]))
        v_parts.append(v[src, :])
        segments.append((tb, src.stop - src.start))
    a = _dot_nt(jnp.concatenate(q_parts, axis=0).astype(BF16), jnp.concatenate(k_parts, axis=0).astype(BF16))
    return a, segments, jnp.concatenate(v_parts, axis=0).astype(BF16)


def _rec_cross_apply(a, segments, v_all):
    c, ncol = a.shape
    row_blk = lax.broadcasted_iota(jnp.int32, (c, ncol), 0) // SUBLANES
    col = lax.broadcasted_iota(jnp.int32, (c, ncol), 1)
    col_blk = jnp.full((c, ncol), -1, jnp.int32)
    start = 0
    for tb, width in segments:
        col_blk = jnp.where((col >= start) & (col < start + width), tb, col_blk)
        start += width
    return _dot(jnp.where(row_blk == col_blk, a, 0.0).astype(BF16), v_all)


def _rec_same_apply(lane_sums, v):
    o_blocks = []
    for b in range(REC_CHUNK // SUBLANES):
        acc = None
        for sl in range(SUBLANES):
            s = b * SUBLANES + sl
            term = lane_sums[s * SUBLANES:(s + 1) * SUBLANES, :] * v[s:s + 1, :]
            acc = term if acc is None else acc + term
        o_blocks.append(acc)
    return jnp.concatenate(o_blocks, axis=0)


def _rec_kernel(layer, has_init, *refs):
    if has_init:
        (q_ref, ff_ref, fb_ref, v_ref, g_ref, lg_ref, ng_ref, s0_ref, o_ref,
         o_scr, qd_scr, kd_scr, gl_scr, st_scr) = refs
        s_out_ref = None
    else:
        (q_ref, ff_ref, fb_ref, v_ref, g_ref, lg_ref, ng_ref, o_ref, s_out_ref,
         o_scr, qd_scr, kd_scr, gl_scr, st_scr) = refs
    seq = q_ref.shape[0]
    nchunk = seq // REC_CHUNK
    c = REC_CHUNK

    logits = lg_ref[...]
    e = jnp.exp(logits - jnp.max(logits, axis=0, keepdims=True))
    p = e / jnp.sum(e, axis=0, keepdims=True)
    lb = jnp.zeros_like(p[0])
    for i in range(1, layer + 1):
        lb = lb + p[i]

    def rows_of(ci):
        return pl.ds(pl.multiple_of(ci * c, c), c)

    z_refs = (ff_ref, fb_ref)

    def local_body(it, carry):
        chunks = [it * REC_LOCAL_UNROLL + u for u in range(REC_LOCAL_UNROLL)]
        qs = [q_ref[rows_of(ci), :] for ci in chunks]
        vs = [v_ref[rows_of(ci), :] for ci in chunks]
        items = [(u, d) for u in range(REC_LOCAL_UNROLL) for d in range(2)]
        gates = [_rec_gates(d == 0, z_refs[d][rows_of(chunks[u]), :], lb[d]) for u, d in items]
        same = [_rec_same_block(d == 0, qs[u], k, cum) for (u, d), (k, cum) in zip(items, gates)]
        cross = [_rec_cross_block(d == 0, qs[u], k, vs[u], cum) for (u, d), (k, cum) in zip(items, gates)]
        totals = [None] * REC_LOCAL_UNROLL
        for (u, d), (k, cum), lane_sums, (a, segments, v_all) in zip(items, gates, same, cross):
            rows = rows_of(chunks[u])
            last = cum[c - 1:c, :] if d == 0 else cum[0:1, :]
            qd_scr[d, rows, :] = (qs[u] * jnp.exp(cum)).astype(BF16)
            kd_scr[d, rows, :] = (k * jnp.exp(last - cum)).astype(BF16)
            gl_scr[d, chunks[u]] = jnp.broadcast_to(jnp.exp(last), (SUBLANES, REC_DK))
            o_local = _rec_same_apply(lane_sums, vs[u]) + _rec_cross_apply(a, segments, v_all)
            totals[u] = o_local if totals[u] is None else totals[u] + o_local
        for u, ci in enumerate(chunks):
            o_scr[rows_of(ci), :] = totals[u]
        return carry

    lax.fori_loop(0, nchunk // REC_LOCAL_UNROLL, local_body, 0)

    for d in range(2):
        if has_init:
            st_scr[d] = s0_ref[d].T
        else:
            st_scr[d] = jnp.zeros((REC_DV, REC_DK), F32)

    def state_body(it, carry):
        steps = [it * REC_STATE_UNROLL + u for u in range(REC_STATE_UNROLL)]
        order = [(d, step if d == 0 else nchunk - 1 - step) for step in steps for d in range(2)]
        updates = [_dot_tn(v_ref[rows_of(ci), :].astype(BF16), kd_scr[d, rows_of(ci), :]) for d, ci in order]
        for (d, ci), update in zip(order, updates):
            rows = rows_of(ci)
            st = st_scr[d]
            o_scr[rows, :] += _dot_nt(qd_scr[d, rows, :], st.astype(BF16))
            decayed = (st.reshape(REC_DV // SUBLANES, SUBLANES, REC_DK) * gl_scr[d, ci]).reshape(REC_DV, REC_DK)
            st_scr[d] = decayed + update
        return carry

    lax.fori_loop(0, nchunk // REC_STATE_UNROLL, state_body, 0)
    if not has_init:
        s_out_ref[0] = st_scr[0].T
        s_out_ref[1] = st_scr[1].T

    def out_body(ci, carry):
        rows = rows_of(ci)
        gate = g_ref[rows, :]
        o_ref[rows, :] = (_rms(o_scr[rows, :], ng_ref[...]) * (gate * _sigmoid(gate))).astype(BF16)
        return carry

    lax.fori_loop(0, nchunk, out_body, 0, unroll=4)


def _recurrence(z, layer, lb_logits, norm_g, s0, batch, seq):
    has_init = s0 is not None
    col = lambda base: pl.BlockSpec((seq, REC_DK), lambda b, h: (b, base + h))
    state_spec = pl.BlockSpec((None, 2, None, REC_DK, REC_DV), lambda b, h: (b, 0, h, 0, 0))
    in_specs = [col(COL_RQ), col(COL_RF_F), col(COL_RF_B), col(COL_RI), col(COL_RG),
                pl.BlockSpec((DEPTH, 2, None, 1, REC_DK), lambda b, h: (0, 0, h, 0, 0)),
                pl.BlockSpec((None, 1, REC_DV), lambda b, h: (h, 0, 0))]
    args = [z, z, z, z, z, lb_logits, norm_g]
    out_specs = [pl.BlockSpec((seq, REC_DV), lambda b, h: (b, h))]
    out_shape = [jax.ShapeDtypeStruct((batch * seq, D_REC), BF16)]
    if has_init:
        in_specs.append(state_spec)
        args.append(s0)
    else:
        out_specs.append(state_spec)
        out_shape.append(jax.ShapeDtypeStruct((batch, 2, N_REC_HEADS, REC_DK, REC_DV), F32))
    return pl.pallas_call(
        functools.partial(_rec_kernel, layer, has_init),
        grid=(batch, N_REC_HEADS),
        in_specs=in_specs,
        out_specs=out_specs,
        out_shape=out_shape,
        scratch_shapes=[pltpu.VMEM((seq, REC_DV), F32),
                        pltpu.VMEM((2, seq, REC_DK), BF16),
                        pltpu.VMEM((2, seq, REC_DK), BF16),
                        pltpu.VMEM((2, seq // REC_CHUNK, SUBLANES, REC_DK), F32),
                        pltpu.VMEM((2, REC_DV, REC_DK), F32)],
        compiler_params=_params("parallel", "parallel"),
        name="hgrn2",
    )(*args)


def _conv_kernel(tiles_per_seq, a_ref, b_ref, at_ref, bt_ref, ab_ref, bb_ref, w_ref, bias_ref, lg_ref, lb_ref,
                 o_ref, h_scr):
    i = pl.program_id(0)
    tile_in_seq = i % tiles_per_seq
    has_top = jnp.where(tile_in_seq > 0, 1.0, 0.0)
    has_bot = jnp.where(tile_in_seq < tiles_per_seq - 1, 1.0, 0.0)
    rows = a_ref.shape[0]
    half = CONV_WIDTH // 2

    def glu(a, b):
        return a * _sigmoid(b)

    h_scr[0:CONV_HALO, :] = glu(at_ref[...], bt_ref[...]) * has_top
    h_scr[CONV_HALO:CONV_HALO + rows, :] = glu(a_ref[...], b_ref[...])
    h_scr[CONV_HALO + rows:, :] = glu(ab_ref[...], bb_ref[...]) * has_bot

    for r0 in range(0, rows, CONV_SUB):
        acc = None
        for j in range(CONV_WIDTH):
            lo = r0 + CONV_HALO - half + j
            term = h_scr[lo:lo + CONV_SUB, :] * w_ref[j:j + 1, :]
            acc = term if acc is None else acc + term
        y = acc + bias_ref[...]
        mu = jnp.mean(y, axis=-1, keepdims=True)
        yc = y - mu
        var = jnp.mean(yc * yc, axis=-1, keepdims=True)
        yn = yc * lax.rsqrt(var + EPS) * lg_ref[...] + lb_ref[...]
        o_ref[r0:r0 + CONV_SUB, :] = (yn * _sigmoid(yn)).astype(BF16)


def _conv(z, w, bias, ln_g, ln_b, seq):
    n = z.shape[0]
    tiles_per_seq = seq // CONV_ROWS
    ntiles = n // CONV_ROWS
    cw = CONV_CH // HEAD_DIM
    halo_per_tile = CONV_ROWS // CONV_HALO
    nhalo = n // CONV_HALO
    ca, cb = COL_CA // cw, COL_CB // cw
    mid = lambda c: pl.BlockSpec((CONV_ROWS, CONV_CH), lambda i: (i, c))
    top = lambda c: pl.BlockSpec((CONV_HALO, CONV_CH), lambda i: (jnp.maximum(i * halo_per_tile - 1, 0), c))
    bot = lambda c: pl.BlockSpec((CONV_HALO, CONV_CH), lambda i: (jnp.minimum((i + 1) * halo_per_tile, nhalo - 1), c))
    vec = pl.BlockSpec((1, CONV_CH), lambda i: (0, 0))
    return pl.pallas_call(
        functools.partial(_conv_kernel, tiles_per_seq),
        grid=(ntiles,),
        in_specs=[mid(ca), mid(cb), top(ca), top(cb), bot(ca), bot(cb),
                  pl.BlockSpec((CONV_WIDTH, CONV_CH), lambda i: (0, 0)), vec, vec, vec],
        out_specs=pl.BlockSpec((CONV_ROWS, CONV_CH), lambda i: (i, 0)),
        out_shape=jax.ShapeDtypeStruct((n, CONV_CH), BF16),
        scratch_shapes=[pltpu.VMEM((CONV_ROWS + 2 * CONV_HALO, CONV_CH), F32)],
        compiler_params=_params("parallel"),
        name="conv",
    )(z, z, z, z, z, z, w, bias.reshape(1, CONV_CH), ln_g.reshape(1, CONV_CH), ln_b.reshape(1, CONV_CH))


def _outproj_kernel(x_ref, g_ref, a_ref, r_ref, c_ref, wa_ref, wr_ref, wc_ref, o_ref):
    mix = _dot(a_ref[...], wa_ref[...]) + _dot(r_ref[...], wr_ref[...]) + _dot(c_ref[...], wc_ref[...])
    o_ref[...] = x_ref[...] + g_ref[...] * mix


def _outproj(x, mod, attn, rec, conv, w_out, layer, rows_per_cond):
    n = x.shape[0]
    tn = D_MODEL // 2
    row = lambda width: pl.BlockSpec((TM, width), lambda i, j: (i, 0))
    return pl.pallas_call(
        _outproj_kernel,
        grid=(n // TM, D_MODEL // tn),
        in_specs=[
            pl.BlockSpec((TM, tn), lambda i, j: (i, j)),
            pl.BlockSpec((None, None, 1, tn), lambda i, j: ((i * TM) // rows_per_cond, 5, 0, j)),
            row(D_ATTN), row(D_REC), row(CONV_CH),
            pl.BlockSpec((None, None, D_ATTN, tn), lambda i, j: (layer, j, 0, 0)),
            pl.BlockSpec((None, None, D_REC, tn), lambda i, j: (layer, j, D_ATTN // D_REC, 0)),
            pl.BlockSpec((None, None, CONV_CH, tn), lambda i, j: (layer, j, (D_ATTN + D_REC) // CONV_CH, 0)),
        ],
        out_specs=pl.BlockSpec((TM, tn), lambda i, j: (i, j)),
        out_shape=jax.ShapeDtypeStruct((n, D_MODEL), F32),
        compiler_params=_params("parallel", "parallel"),
        name="outproj",
    )(x, mod, attn, rec, conv, w_out, w_out, w_out)


def _col_tiles(w, tile):
    *lead, k, n = w.shape
    w = w.reshape(*lead, k, n // tile, tile)
    return jnp.swapaxes(w, -3, -2).astype(BF16)


def _trunk_layer(x, mod, l, P, mixers, rows_per_cond):
    x = _ffn(x, mod, 0, P['norm_g'][l, 0], P['w_ffn_in'], P['w_ffn_out'], l, 0, rows_per_cond)
    z = _inproj(x, mod, P['norm_g'][l, 1], P['w_in'], l, rows_per_cond)
    attn, rec, conv, extras = mixers(z)
    x = _outproj(x, mod, attn, rec, conv, P['w_out'], l, rows_per_cond)
    x = _ffn(x, mod, 6, P['norm_g'][l, 2], P['w_ffn_in'], P['w_ffn_out'], l, 1, rows_per_cond)
    return x, extras


def kernel(x_prompt, x_sample, cache_k, cache_v, state_rec, c, c_ctx, w_ada, b_ada, norm_g, w_ffn_in, w_ffn_out,
           w_in, w_out, q_norm_g, k_norm_g, attn_sink, rec_lb_logits, rec_norm_g, conv_w, conv_b, conv_ln_g, conv_ln_b):
    batch, seq, _ = x_prompt.shape
    dec_batch, dec_seq, _ = x_sample.shape
    assert seq % TM == 0 or TM % seq == 0
    assert dec_seq % TM == 0 and dec_seq % LAT_Q_TILE == 0 and seq % CONV_ROWS == 0 and dec_seq % CONV_ROWS == 0

    P = {'norm_g': norm_g, 'w_ffn_in': _col_tiles(w_ffn_in, TF), 'w_ffn_out': _col_tiles(w_ffn_out, TN_FFN_OUT),
         'w_in': _col_tiles(w_in, TN_IN), 'w_out': _col_tiles(w_out, D_MODEL // 2)}

    lb_logits = rec_lb_logits.reshape(DEPTH, 2, N_REC_HEADS, 1, REC_DK)

    cond8 = jnp.zeros((SUBLANES, D_MODEL), F32).at[0].set(c_ctx).at[1:1 + dec_batch].set(c)
    mod = _modulation(cond8, w_ada, b_ada).reshape(DEPTH, SUBLANES, N_MOD, 1, D_MODEL)
    tables = _rope_tables(dec_seq)
    sinks = attn_sink.reshape(DEPTH, N_KV_HEADS, Q_PER_KV)
    rec_g = rec_norm_g.reshape(DEPTH, N_REC_HEADS, 1, REC_DV)

    def shared_mixers(z, l, seq_len):
        return _conv(z, conv_w[l], conv_b[l], conv_ln_g[l], conv_ln_b[l], seq_len)

    h = x_prompt.reshape(batch * seq, D_MODEL)
    ks, vs, ss = [], [], []
    for l in range(DEPTH):
        def ctx_mixers(z, l=l):
            attn, k_l, v_l = _ctx_attention(z, sinks[l], q_norm_g[l], k_norm_g[l], batch, seq)
            rec, s_l = _recurrence(z, l, lb_logits, rec_g[l], None, batch, seq)
            return attn, rec, shared_mixers(z, l, seq), (k_l, v_l, s_l)

        h, (k_l, v_l, s_l) = _trunk_layer(h, mod[l, 0:1], l, P, ctx_mixers, batch * seq)
        ks.append(k_l)
        vs.append(v_l)
        ss.append(s_l)
    y_prompt = h.reshape(batch, seq, D_MODEL)

    h = x_sample.reshape(dec_batch * dec_seq, D_MODEL)
    for l in range(DEPTH):
        def lat_mixers(z, l=l):
            attn = _lat_attention(z, sinks[l], q_norm_g[l], k_norm_g[l], cache_k[:, l], cache_v[:, l], tables,
                                  dec_batch, dec_seq)
            rec = _recurrence(z, l, lb_logits, rec_g[l], state_rec[:, l], dec_batch, dec_seq)[0]
            return attn, rec, shared_mixers(z, l, dec_seq), None

        h, _ = _trunk_layer(h, mod[l, 1:1 + dec_batch], l, P, lat_mixers, dec_seq)
    y_sample = h.reshape(dec_batch, dec_seq, D_MODEL)

    return (y_prompt, y_sample, jnp.stack(ks, axis=1), jnp.stack(vs, axis=1), jnp.stack(ss, axis=1))
```

```python
import functools

import jax
import jax.numpy as jnp
import numpy as np
from jax import lax
from jax.experimental import pallas as pl
from jax.experimental.pallas import tpu as pltpu

F32 = jnp.float32
BF16 = jnp.bfloat16

D_MODEL = 2048
DEPTH = 2
GRID_W = 64
HEAD_DIM = 128
D_ATTN = D_MODEL // 2
N_Q_HEADS = D_ATTN // HEAD_DIM
N_KV_HEADS = 2
Q_PER_KV = N_Q_HEADS // N_KV_HEADS
WINDOW = 128
ATTN_BLOCK = 128
ATTN_SCALE = HEAD_DIM ** -0.5
ROPE_BASE = 10000.0
MASK_VALUE = -1e30
D_REC = D_MODEL // 4
REC_DK = 128
REC_DV = 128
N_REC_HEADS = D_REC // REC_DV
REC_CHUNK = 32
CONV_CH = D_MODEL // 4
CONV_WIDTH = 31
D_FF = 5632
N_MOD = 9
EPS = 1e-6
GATE_FLOOR = 1e-30
LOG2_E = 1.4426950408889634
IN_COLS = 5120

COL_Q = 0
COL_K = 8
COL_V = 10
COL_RQ = 12
COL_RF_F = 16
COL_RF_B = 20
COL_RI = 24
COL_RG = 28
COL_CA = 32
COL_CB = 36

SUBLANES = 8
VMEM_LIMIT = 56 * 1024 * 1024

TM = 1024
TM_FFN = 1024
TF = 512
TN_FFN_OUT = 512
TN_IN = 1024
TN_ADA = 1024
CONV_ROWS = 256
CONV_HALO = 16
CONV_SUB = 32
LAT_Q_TILE = 1024
REC_LOCAL_UNROLL = 4
REC_STATE_UNROLL = 4


def _params(*sem):
    return pltpu.CompilerParams(dimension_semantics=sem, vmem_limit_bytes=VMEM_LIMIT)


def _dot(a, b):
    return jnp.dot(a, b, preferred_element_type=F32)


def _dot_nt(a, b):
    return lax.dot_general(a, b, (((1,), (1,)), ((), ())), preferred_element_type=F32)


def _dot_tn(a, b):
    return lax.dot_general(a, b, (((0,), (0,)), ((), ())), preferred_element_type=F32)


def _rms(x, g):
    return x * lax.rsqrt(jnp.mean(x * x, axis=-1, keepdims=True) + EPS) * g


def _sigmoid(x):
    return 1.0 / (1.0 + jnp.exp(-x))


def _ada_kernel(c_ref, w_ref, b_ref, o_ref):
    c = c_ref[...]
    s = (c * _sigmoid(c)).astype(BF16)
    o_ref[...] = _dot(s, w_ref[...].astype(BF16)) + b_ref[...]


def _modulation(cond8, w_ada, b_ada):
    ncol = N_MOD * D_MODEL
    return pl.pallas_call(
        _ada_kernel,
        grid=(DEPTH, ncol // TN_ADA),
        in_specs=[
            pl.BlockSpec((SUBLANES, D_MODEL), lambda l, j: (0, 0)),
            pl.BlockSpec((None, D_MODEL, TN_ADA), lambda l, j: (l, 0, j)),
            pl.BlockSpec((None, 1, TN_ADA), lambda l, j: (l, 0, j)),
        ],
        out_specs=pl.BlockSpec((None, SUBLANES, TN_ADA), lambda l, j: (l, 0, j)),
        out_shape=jax.ShapeDtypeStruct((DEPTH, SUBLANES, ncol), F32),
        compiler_params=_params("parallel", "parallel"),
        name="modulation",
    )(cond8, w_ada, b_ada.reshape(DEPTH, 1, ncol))


def _mod_spec(chunk, rows_per_cond, tm=TM):
    return pl.BlockSpec((None, None, 1, D_MODEL), lambda i, j: ((i * tm) // rows_per_cond, chunk, 0, 0))


def _ffn_kernel(nf, x_ref, sh_ref, sc_ref, g_ref, ng_ref, wa_ref, wb_ref, wo_ref, o_ref, h_scr, act_scr):
    j = pl.program_id(1)

    @pl.when(j == 0)
    def _():
        h = _rms(x_ref[...], ng_ref[...]) * (1.0 + sc_ref[...]) + sh_ref[...]
        h_scr[...] = h.astype(BF16)

    @pl.when(j < nf)
    def _():
        h = h_scr[...]
        a = _dot(h, wa_ref[...])
        b = _dot(h, wb_ref[...])
        act_scr[:, pl.ds(pl.multiple_of(j * TF, TF), TF)] = (a * _sigmoid(a) * b).astype(BF16)

    @pl.when(j >= nf)
    def _():
        cols = pl.ds(pl.multiple_of((j - nf) * TN_FFN_OUT, TN_FFN_OUT), TN_FFN_OUT)
        o_ref[...] = x_ref[:, cols] + 0.5 * g_ref[...] * _dot(act_scr[...], wo_ref[...])


def _ffn(x, mod, first_chunk, norm_g, w_in, w_out, layer, which, rows_per_cond):
    n = x.shape[0]
    nf = D_FF // TF
    nout = D_MODEL // TN_FFN_OUT
    hid = lambda j: jnp.minimum(j, nf - 1)
    out = lambda j: jnp.maximum(j - nf, 0)
    return pl.pallas_call(
        functools.partial(_ffn_kernel, nf),
        grid=(n // TM_FFN, nf + nout),
        in_specs=[
            pl.BlockSpec((TM_FFN, D_MODEL), lambda i, j: (i, 0), pipeline_mode=pl.Buffered(1)),
            _mod_spec(first_chunk, rows_per_cond, TM_FFN),
            _mod_spec(first_chunk + 1, rows_per_cond, TM_FFN),
            pl.BlockSpec((None, None, 1, TN_FFN_OUT),
                         lambda i, j: ((i * TM_FFN) // rows_per_cond, first_chunk + 2, 0, out(j))),
            pl.BlockSpec((1, D_MODEL), lambda i, j: (0, 0)),
            pl.BlockSpec((None, None, D_MODEL, TF), lambda i, j: (layer, which, 0, hid(j))),
            pl.BlockSpec((None, None, D_MODEL, TF), lambda i, j: (layer, which, 0, hid(j) + nf)),
            pl.BlockSpec((None, None, D_FF, TN_FFN_OUT), lambda i, j: (layer, which, 0, out(j))),
        ],
        out_specs=pl.BlockSpec((TM_FFN, TN_FFN_OUT), lambda i, j: (i, out(j))),
        out_shape=jax.ShapeDtypeStruct((n, D_MODEL), F32),
        scratch_shapes=[pltpu.VMEM((TM_FFN, D_MODEL), BF16), pltpu.VMEM((TM_FFN, D_FF), BF16)],
        compiler_params=_params("parallel", "arbitrary"),
        name="ffn",
    )(x, mod, mod, mod, norm_g.reshape(1, D_MODEL), w_in, w_in, w_out)


def _inproj_kernel(x_ref, sh_ref, sc_ref, ng_ref, w_ref, o_ref, h_scr):
    @pl.when(pl.program_id(1) == 0)
    def _():
        h = _rms(x_ref[...], ng_ref[...]) * (1.0 + sc_ref[...]) + sh_ref[...]
        h_scr[...] = h.astype(BF16)

    o_ref[...] = _dot(h_scr[...], w_ref[...])


def _inproj(x, mod, norm_g, w_in, layer, rows_per_cond):
    n = x.shape[0]
    return pl.pallas_call(
        _inproj_kernel,
        grid=(n // TM, IN_COLS // TN_IN),
        in_specs=[
            pl.BlockSpec((TM, D_MODEL), lambda i, j: (i, 0)),
            _mod_spec(3, rows_per_cond),
            _mod_spec(4, rows_per_cond),
            pl.BlockSpec((1, D_MODEL), lambda i, j: (0, 0)),
            pl.BlockSpec((None, D_MODEL, TN_IN), lambda i, j: (layer, 0, j)),
        ],
        out_specs=pl.BlockSpec((TM, TN_IN), lambda i, j: (i, j)),
        out_shape=jax.ShapeDtypeStruct((n, IN_COLS), F32),
        scratch_shapes=[pltpu.VMEM((TM, D_MODEL), BF16)],
        compiler_params=_params("parallel", "arbitrary"),
        name="inproj",
    )(x, mod, mod, norm_g.reshape(1, D_MODEL), w_in)


def _softmax_sink(scores, values, sink):
    m = sink
    for s in scores:
        m = jnp.maximum(jnp.max(s, axis=-1, keepdims=True), m)
    den = jnp.exp(sink - m)
    acc = None
    for s, v in zip(scores, values):
        p = jnp.exp(s - m)
        den = den + jnp.sum(p, axis=-1, keepdims=True)
        pv = _dot(p.astype(BF16), v)
        acc = pv if acc is None else acc + pv
    return acc / den


def _ctx_attn_kernel(sink_ref, q_ref, k_ref, v_ref, qg_ref, kg_ref, o_ref, kc_ref, vc_ref):
    kv = pl.program_id(1)
    kn = _rms(k_ref[...], kg_ref[...])
    v = v_ref[...]
    kc_ref[...] = kn
    vc_ref[...] = v
    kb = kn.astype(BF16)
    vb = v.astype(BF16)
    for g in range(Q_PER_KV):
        cols = slice(g * HEAD_DIM, (g + 1) * HEAD_DIM)
        qn = _rms(q_ref[:, cols], qg_ref[...]).astype(BF16)
        s = _dot_nt(qn, kb) * ATTN_SCALE
        o = _softmax_sink([s], [vb], sink_ref[kv, g])
        o_ref[:, cols] = o.astype(BF16)


def _ctx_attention(z, sink, q_g, k_g, batch, seq):
    qw = Q_PER_KV * HEAD_DIM
    cache_shape = jax.ShapeDtypeStruct((batch, N_KV_HEADS, seq, HEAD_DIM), F32)
    cache_spec = pl.BlockSpec((None, None, seq, HEAD_DIM), lambda b, kv: (b, kv, 0, 0))
    return pl.pallas_call(
        _ctx_attn_kernel,
        grid=(batch, N_KV_HEADS),
        in_specs=[
            pl.BlockSpec(memory_space=pltpu.SMEM),
            pl.BlockSpec((seq, qw), lambda b, kv: (b, kv)),
            pl.BlockSpec((seq, HEAD_DIM), lambda b, kv: (b, COL_K + kv)),
            pl.BlockSpec((seq, HEAD_DIM), lambda b, kv: (b, COL_V + kv)),
            pl.BlockSpec((1, HEAD_DIM), lambda b, kv: (0, 0)),
            pl.BlockSpec((1, HEAD_DIM), lambda b, kv: (0, 0)),
        ],
        out_specs=[pl.BlockSpec((seq, qw), lambda b, kv: (b, kv)), cache_spec, cache_spec],
        out_shape=[jax.ShapeDtypeStruct((batch * seq, D_ATTN), BF16), cache_shape, cache_shape],
        compiler_params=_params("parallel", "parallel"),
        name="ctx_attention",
    )(sink, z, z, z, q_g.reshape(1, HEAD_DIM), k_g.reshape(1, HEAD_DIM))


def _rope(x, cos, sin_a, sin_b):
    quarter = HEAD_DIM // 4
    up = pltpu.roll(x, HEAD_DIM - quarter, 1)
    down = pltpu.roll(x, quarter, 1)
    return x * cos + up * sin_a + down * sin_b


def _lat_attn_kernel(sink_ref, q_ref, k_ref, v_ref, kc_ref, vc_ref, qg_ref, kg_ref,
                     cos_ref, sa_ref, sb_ref, cosq_ref, saq_ref, sbq_ref, o_ref, k_scr, v_scr, kc_scr, vc_scr):
    kv = pl.program_id(1)
    qt = pl.program_id(2)
    seq = k_ref.shape[0]
    span = 3 * ATTN_BLOCK

    @pl.when(qt == 0)
    def _():
        kn = _rms(k_ref[...], kg_ref[...])
        k_scr[...] = _rope(kn, cos_ref[...], sa_ref[...], sb_ref[...]).astype(BF16)
        v_scr[...] = v_ref[...].astype(BF16)
        kc_scr[...] = kc_ref[...].astype(BF16)
        vc_scr[...] = vc_ref[...].astype(BF16)

    kcb = kc_scr[...]
    vcb = vc_scr[...]
    for blk in range(LAT_Q_TILE // ATTN_BLOCK):
        rows = slice(blk * ATTN_BLOCK, (blk + 1) * ATTN_BLOCK)
        q0 = (qt * (LAT_Q_TILE // ATTN_BLOCK) + blk) * ATTN_BLOCK
        start = pl.multiple_of(jnp.clip(q0 - ATTN_BLOCK, 0, seq - span), ATTN_BLOCK)
        kl = k_scr[pl.ds(start, span), :]
        vl = v_scr[pl.ds(start, span), :]
        qpos = q0 + lax.broadcasted_iota(jnp.int32, (ATTN_BLOCK, span), 0)
        kpos = start + lax.broadcasted_iota(jnp.int32, (ATTN_BLOCK, span), 1)
        valid = jnp.abs(qpos - kpos) <= WINDOW
        cos, sa, sb = cosq_ref[rows, :], saq_ref[rows, :], sbq_ref[rows, :]
        for g in range(Q_PER_KV):
            cols = slice(g * HEAD_DIM, (g + 1) * HEAD_DIM)
            qn = _rope(_rms(q_ref[rows, cols], qg_ref[...]), cos, sa, sb).astype(BF16)
            s_loc = jnp.where(valid, _dot_nt(qn, kl) * ATTN_SCALE, MASK_VALUE)
            s_ctx = _dot_nt(qn, kcb) * ATTN_SCALE
            o = _softmax_sink([s_loc, s_ctx], [vl, vcb], sink_ref[kv, g])
            o_ref[rows, cols] = o.astype(BF16)


def _rope_tables(seq):
    quarter = HEAD_DIM // 4
    pos = np.arange(seq)
    inv_freq = ROPE_BASE ** (-np.arange(quarter, dtype=np.float32) / quarter)
    inv_freq = jnp.asarray(inv_freq, F32)
    zero = jnp.zeros((seq, quarter), F32)

    def trig(p):
        ang = jnp.asarray(p, F32)[:, None] * inv_freq
        return jnp.cos(ang), jnp.sin(ang)

    cr, sr = trig(pos // GRID_W)
    cc, sc = trig(pos % GRID_W)
    cos = jnp.concatenate([cr, cr, cc, cc], axis=-1)
    sin_a = jnp.concatenate([-sr, zero, -sc, zero], axis=-1)
    sin_b = jnp.concatenate([zero, sr, zero, sc], axis=-1)
    return cos, sin_a, sin_b


def _lat_attention(z, sink, q_g, k_g, k_ctx, v_ctx, tables, batch, seq):
    qw = Q_PER_KV * HEAD_DIM
    nqt = seq // LAT_Q_TILE
    past = k_ctx.shape[2]
    cos, sin_a, sin_b = tables
    full_tab = pl.BlockSpec((seq, HEAD_DIM), lambda b, kv, qt: (0, 0))
    tile_tab = pl.BlockSpec((LAT_Q_TILE, HEAD_DIM), lambda b, kv, qt: (qt, 0))
    ctx_spec = pl.BlockSpec((None, None, past, HEAD_DIM), lambda b, kv, qt: (b, kv, 0, 0))
    gain = pl.BlockSpec((1, HEAD_DIM), lambda b, kv, qt: (0, 0))
    return pl.pallas_call(
        _lat_attn_kernel,
        grid=(batch, N_KV_HEADS, nqt),
        in_specs=[
            pl.BlockSpec(memory_space=pltpu.SMEM),
            pl.BlockSpec((LAT_Q_TILE, qw), lambda b, kv, qt: (b * nqt + qt, kv)),
            pl.BlockSpec((seq, HEAD_DIM), lambda b, kv, qt: (b, COL_K + kv)),
            pl.BlockSpec((seq, HEAD_DIM), lambda b, kv, qt: (b, COL_V + kv)),
            ctx_spec, ctx_spec, gain, gain,
            full_tab, full_tab, full_tab, tile_tab, tile_tab, tile_tab,
        ],
        out_specs=pl.BlockSpec((LAT_Q_TILE, qw), lambda b, kv, qt: (b * nqt + qt, kv)),
        out_shape=jax.ShapeDtypeStruct((batch * seq, D_ATTN), BF16),
        scratch_shapes=[pltpu.VMEM((seq, HEAD_DIM), BF16), pltpu.VMEM((seq, HEAD_DIM), BF16),
                        pltpu.VMEM((past, HEAD_DIM), BF16), pltpu.VMEM((past, HEAD_DIM), BF16)],
        compiler_params=_params("parallel", "parallel", "arbitrary"),
        name="lat_attention",
    )(sink, z, z, z, k_ctx, v_ctx, q_g.reshape(1, HEAD_DIM), k_g.reshape(1, HEAD_DIM),
      cos, sin_a, sin_b, cos, sin_a, sin_b)


def _split3(x):
    hi = x.astype(BF16)
    r1 = x - hi.astype(F32)
    mid = r1.astype(BF16)
    lo = (r1 - mid.astype(F32)).astype(BF16)
    return hi, mid, lo


def _rec_gates(forward, zf, lb):
    c = REC_CHUNK
    t = jnp.exp(-jnp.abs(zf))
    r = 1.0 / (1.0 + t)
    tr = t * r
    nonneg = zf >= 0
    sig_pos = jnp.where(nonneg, r, tr)
    sig_neg = jnp.where(nonneg, tr, r)
    one_m_lb = 1.0 - lb
    log_f = jnp.log(jnp.maximum(lb + one_m_lb * sig_pos, GATE_FLOOR))
    k = one_m_lb * sig_neg
    ri = lax.broadcasted_iota(jnp.int32, (c, c), 0)
    ci = lax.broadcasted_iota(jnp.int32, (c, c), 1)
    tri = jnp.where((ci <= ri) if forward else (ci >= ri), 1.0, 0.0).astype(BF16)
    hi, mid, lo = _split3(log_f)
    cum2 = (_dot(tri, hi) + _dot(tri, mid) + _dot(tri, lo)) * LOG2_E
    return cum2 - jnp.log2(k), cum2


def _rec_same_block(forward, q, key2, cum2):
    sub_row = lax.broadcasted_iota(jnp.int32, (SUBLANES, REC_DK), 0)
    pieces = []
    for b in range(REC_CHUNK // SUBLANES):
        blk = slice(b * SUBLANES, (b + 1) * SUBLANES)
        cum_b, q_b = cum2[blk, :], q[blk, :]
        for sl in range(SUBLANES):
            s = b * SUBLANES + sl
            keep = (sub_row >= sl) if forward else (sub_row <= sl)
            pieces.append(jnp.where(keep, jnp.exp2(cum_b - key2[s:s + 1, :]) * q_b, 0.0).astype(BF16))
    return _dot(jnp.concatenate(pieces, axis=0), jnp.ones((REC_DK, REC_DV), BF16))


def _rec_cross_block(forward, q, key2, v, cum2):
    c = REC_CHUNK
    q_parts, k_parts, v_parts, segments = [], [], [], []
    for tb in range(c // SUBLANES):
        blk = slice(tb * SUBLANES, (tb + 1) * SUBLANES)
        src, ref_row = (slice(0, blk.start), blk.start - 1) if forward else (slice(blk.stop, c), blk.stop)
        if src.stop == src.start:
            q_parts.append(jnp.zeros((SUBLANES, REC_DK), F32))
            continue
        ref = cum2[ref_row:ref_row + 1, :]
        q_parts.append(q[blk, :] * jnp.exp2(cum2[blk, :] - ref))
        k_parts.append(jnp.exp2(ref - key2[src, :]))
        v_parts.append(v[src, :])
        segments.append((tb, src.stop - src.start))
    a = _dot_nt(jnp.concatenate(q_parts, axis=0).astype(BF16), jnp.concatenate(k_parts, axis=0).astype(BF16))
    return a, segments, jnp.concatenate(v_parts, axis=0).astype(BF16)


def _rec_cross_apply(a, segments, v_all):
    c, ncol = a.shape
    row_blk = lax.broadcasted_iota(jnp.int32, (c, ncol), 0) // SUBLANES
    col = lax.broadcasted_iota(jnp.int32, (c, ncol), 1)
    col_blk = jnp.full((c, ncol), -1, jnp.int32)
    start = 0
    for tb, width in segments:
        col_blk = jnp.where((col >= start) & (col < start + width), tb, col_blk)
        start += width
    return _dot(jnp.where(row_blk == col_blk, a, 0.0).astype(BF16), v_all)


def _rec_same_apply(lane_sums, v):
    o_blocks = []
    for b in range(REC_CHUNK // SUBLANES):
        acc = None
        for sl in range(SUBLANES):
            s = b * SUBLANES + sl
            term = lane_sums[s * SUBLANES:(s + 1) * SUBLANES, :] * v[s:s + 1, :]
            acc = term if acc is None else acc + term
        o_blocks.append(acc)
    return jnp.concatenate(o_blocks, axis=0)


def _rec_kernel(layer, has_init, *refs):
    if has_init:
        (q_ref, ff_ref, fb_ref, v_ref, g_ref, lg_ref, ng_ref, s0_ref, o_ref,
         o_scr, qd_scr, kd_scr, gl_scr, st_scr) = refs
        s_out_ref = None
    else:
        (q_ref, ff_ref, fb_ref, v_ref, g_ref, lg_ref, ng_ref, o_ref, s_out_ref,
         o_scr, qd_scr, kd_scr, gl_scr, st_scr) = refs
    seq = q_ref.shape[0]
    nchunk = seq // REC_CHUNK
    c = REC_CHUNK

    logits = lg_ref[...]
    e = jnp.exp(logits - jnp.max(logits, axis=0, keepdims=True))
    p = e / jnp.sum(e, axis=0, keepdims=True)
    lb = jnp.zeros_like(p[0])
    for i in range(1, layer + 1):
        lb = lb + p[i]

    def rows_of(ci):
        return pl.ds(pl.multiple_of(ci * c, c), c)

    z_refs = (ff_ref, fb_ref)

    def local_body(it, carry):
        chunks = [it * REC_LOCAL_UNROLL + u for u in range(REC_LOCAL_UNROLL)]
        qs = [q_ref[rows_of(ci), :] for ci in chunks]
        vs = [v_ref[rows_of(ci), :] for ci in chunks]
        items = [(u, d) for u in range(REC_LOCAL_UNROLL) for d in range(2)]
        gates = [_rec_gates(d == 0, z_refs[d][rows_of(chunks[u]), :], lb[d]) for u, d in items]
        same = [_rec_same_block(d == 0, qs[u], key2, cum2) for (u, d), (key2, cum2) in zip(items, gates)]
        cross = [_rec_cross_block(d == 0, qs[u], key2, vs[u], cum2) for (u, d), (key2, cum2) in zip(items, gates)]
        totals = [None] * REC_LOCAL_UNROLL
        for (u, d), (key2, cum2), lane_sums, (a, segments, v_all) in zip(items, gates, same, cross):
            rows = rows_of(chunks[u])
            last = cum2[c - 1:c, :] if d == 0 else cum2[0:1, :]
            qd_scr[d, rows, :] = (qs[u] * jnp.exp2(cum2)).astype(BF16)
            kd_scr[d, rows, :] = jnp.exp2(last - key2).astype(BF16)
            gl_scr[d, chunks[u]] = jnp.broadcast_to(jnp.exp2(last), (SUBLANES, REC_DK))
            o_local = _rec_same_apply(lane_sums, vs[u]) + _rec_cross_apply(a, segments, v_all)
            totals[u] = o_local if totals[u] is None else totals[u] + o_local
        for u, ci in enumerate(chunks):
            o_scr[rows_of(ci), :] = totals[u]
        return carry

    lax.fori_loop(0, nchunk // REC_LOCAL_UNROLL, local_body, 0)

    for d in range(2):
        if has_init:
            st_scr[d] = s0_ref[d].T
        else:
            st_scr[d] = jnp.zeros((REC_DV, REC_DK), F32)

    def state_body(it, carry):
        steps = [it * REC_STATE_UNROLL + u for u in range(REC_STATE_UNROLL)]
        order = [(d, step if d == 0 else nchunk - 1 - step) for step in steps for d in range(2)]
        updates = [_dot_tn(v_ref[rows_of(ci), :].astype(BF16), kd_scr[d, rows_of(ci), :]) for d, ci in order]
        for (d, ci), update in zip(order, updates):
            rows = rows_of(ci)
            st = st_scr[d]
            o_scr[rows, :] += _dot_nt(qd_scr[d, rows, :], st.astype(BF16))
            decayed = (st.reshape(REC_DV // SUBLANES, SUBLANES, REC_DK) * gl_scr[d, ci]).reshape(REC_DV, REC_DK)
            st_scr[d] = decayed + update
        return carry

    lax.fori_loop(0, nchunk // REC_STATE_UNROLL, state_body, 0)
    if not has_init:
        s_out_ref[0] = st_scr[0].T
        s_out_ref[1] = st_scr[1].T

    def out_body(ci, carry):
        rows = rows_of(ci)
        gate = g_ref[rows, :]
        o_ref[rows, :] = (_rms(o_scr[rows, :], ng_ref[...]) * (gate * _sigmoid(gate))).astype(BF16)
        return carry

    lax.fori_loop(0, nchunk, out_body, 0, unroll=4)


def _recurrence(z, layer, lb_logits, norm_g, s0, batch, seq):
    has_init = s0 is not None
    col = lambda base: pl.BlockSpec((seq, REC_DK), lambda b, h: (b, base + h))
    state_spec = pl.BlockSpec((None, 2, None, REC_DK, REC_DV), lambda b, h: (b, 0, h, 0, 0))
    in_specs = [col(COL_RQ), col(COL_RF_F), col(COL_RF_B), col(COL_RI), col(COL_RG),
                pl.BlockSpec((DEPTH, 2, None, 1, REC_DK), lambda b, h: (0, 0, h, 0, 0)),
                pl.BlockSpec((None, 1, REC_DV), lambda b, h: (h, 0, 0))]
    args = [z, z, z, z, z, lb_logits, norm_g]
    out_specs = [pl.BlockSpec((seq, REC_DV), lambda b, h: (b, h))]
    out_shape = [jax.ShapeDtypeStruct((batch * seq, D_REC), BF16)]
    if has_init:
        in_specs.append(state_spec)
        args.append(s0)
    else:
        out_specs.append(state_spec)
        out_shape.append(jax.ShapeDtypeStruct((batch, 2, N_REC_HEADS, REC_DK, REC_DV), F32))
    return pl.pallas_call(
        functools.partial(_rec_kernel, layer, has_init),
        grid=(batch, N_REC_HEADS),
        in_specs=in_specs,
        out_specs=out_specs,
        out_shape=out_shape,
        scratch_shapes=[pltpu.VMEM((seq, REC_DV), F32),
                        pltpu.VMEM((2, seq, REC_DK), BF16),
                        pltpu.VMEM((2, seq, REC_DK), BF16),
                        pltpu.VMEM((2, seq // REC_CHUNK, SUBLANES, REC_DK), F32),
                        pltpu.VMEM((2, REC_DV, REC_DK), F32)],
        compiler_params=_params("parallel", "parallel"),
        name="hgrn2",
    )(*args)


def _conv_kernel(tiles_per_seq, a_ref, b_ref, at_ref, bt_ref, ab_ref, bb_ref, w_ref, bias_ref, lg_ref, lb_ref,
                 o_ref, h_scr):
    i = pl.program_id(0)
    tile_in_seq = i % tiles_per_seq
    has_top = jnp.where(tile_in_seq > 0, 1.0, 0.0)
    has_bot = jnp.where(tile_in_seq < tiles_per_seq - 1, 1.0, 0.0)
    rows = a_ref.shape[0]
    half = CONV_WIDTH // 2

    def glu(a, b):
        return a * _sigmoid(b)

    h_scr[0:CONV_HALO, :] = glu(at_ref[...], bt_ref[...]) * has_top
    h_scr[CONV_HALO:CONV_HALO + rows, :] = glu(a_ref[...], b_ref[...])
    h_scr[CONV_HALO + rows:, :] = glu(ab_ref[...], bb_ref[...]) * has_bot

    for r0 in range(0, rows, CONV_SUB):
        acc = None
        for j in range(CONV_WIDTH):
            lo = r0 + CONV_HALO - half + j
            term = h_scr[lo:lo + CONV_SUB, :] * w_ref[j:j + 1, :]
            acc = term if acc is None else acc + term
        y = acc + bias_ref[...]
        mu = jnp.mean(y, axis=-1, keepdims=True)
        yc = y - mu
        var = jnp.mean(yc * yc, axis=-1, keepdims=True)
        yn = yc * lax.rsqrt(var + EPS) * lg_ref[...] + lb_ref[...]
        o_ref[r0:r0 + CONV_SUB, :] = (yn * _sigmoid(yn)).astype(BF16)


def _conv(z, w, bias, ln_g, ln_b, seq):
    n = z.shape[0]
    tiles_per_seq = seq // CONV_ROWS
    ntiles = n // CONV_ROWS
    cw = CONV_CH // HEAD_DIM
    halo_per_tile = CONV_ROWS // CONV_HALO
    nhalo = n // CONV_HALO
    ca, cb = COL_CA // cw, COL_CB // cw
    mid = lambda c: pl.BlockSpec((CONV_ROWS, CONV_CH), lambda i: (i, c))
    top = lambda c: pl.BlockSpec((CONV_HALO, CONV_CH), lambda i: (jnp.maximum(i * halo_per_tile - 1, 0), c))
    bot = lambda c: pl.BlockSpec((CONV_HALO, CONV_CH), lambda i: (jnp.minimum((i + 1) * halo_per_tile, nhalo - 1), c))
    vec = pl.BlockSpec((1, CONV_CH), lambda i: (0, 0))
    return pl.pallas_call(
        functools.partial(_conv_kernel, tiles_per_seq),
        grid=(ntiles,),
        in_specs=[mid(ca), mid(cb), top(ca), top(cb), bot(ca), bot(cb),
                  pl.BlockSpec((CONV_WIDTH, CONV_CH), lambda i: (0, 0)), vec, vec, vec],
        out_specs=pl.BlockSpec((CONV_ROWS, CONV_CH), lambda i: (i, 0)),
        out_shape=jax.ShapeDtypeStruct((n, CONV_CH), BF16),
        scratch_shapes=[pltpu.VMEM((CONV_ROWS + 2 * CONV_HALO, CONV_CH), F32)],
        compiler_params=_params("parallel"),
        name="conv",
    )(z, z, z, z, z, z, w, bias.reshape(1, CONV_CH), ln_g.reshape(1, CONV_CH), ln_b.reshape(1, CONV_CH))


def _outproj_kernel(x_ref, g_ref, a_ref, r_ref, c_ref, wa_ref, wr_ref, wc_ref, o_ref):
    mix = _dot(a_ref[...], wa_ref[...]) + _dot(r_ref[...], wr_ref[...]) + _dot(c_ref[...], wc_ref[...])
    o_ref[...] = x_ref[...] + g_ref[...] * mix


def _outproj(x, mod, attn, rec, conv, w_out, layer, rows_per_cond):
    n = x.shape[0]
    tn = D_MODEL // 2
    row = lambda width: pl.BlockSpec((TM, width), lambda i, j: (i, 0))
    return pl.pallas_call(
        _outproj_kernel,
        grid=(n // TM, D_MODEL // tn),
        in_specs=[
            pl.BlockSpec((TM, tn), lambda i, j: (i, j)),
            pl.BlockSpec((None, None, 1, tn), lambda i, j: ((i * TM) // rows_per_cond, 5, 0, j)),
            row(D_ATTN), row(D_REC), row(CONV_CH),
            pl.BlockSpec((None, D_ATTN, tn), lambda i, j: (layer, 0, j)),
            pl.BlockSpec((None, D_REC, tn), lambda i, j: (layer, D_ATTN // D_REC, j)),
            pl.BlockSpec((None, CONV_CH, tn), lambda i, j: (layer, (D_ATTN + D_REC) // CONV_CH, j)),
        ],
        out_specs=pl.BlockSpec((TM, tn), lambda i, j: (i, j)),
        out_shape=jax.ShapeDtypeStruct((n, D_MODEL), F32),
        compiler_params=_params("parallel", "parallel"),
        name="outproj",
    )(x, mod, attn, rec, conv, w_out, w_out, w_out)


def _trunk_layer(x, mod, l, P, mixers, rows_per_cond):
    x = _ffn(x, mod, 0, P['norm_g'][l, 0], P['w_ffn_in'], P['w_ffn_out'], l, 0, rows_per_cond)
    z = _inproj(x, mod, P['norm_g'][l, 1], P['w_in'], l, rows_per_cond)
    attn, rec, conv, extras = mixers(z)
    x = _outproj(x, mod, attn, rec, conv, P['w_out'], l, rows_per_cond)
    x = _ffn(x, mod, 6, P['norm_g'][l, 2], P['w_ffn_in'], P['w_ffn_out'], l, 1, rows_per_cond)
    return x, extras


def kernel(x_prompt, x_sample, cache_k, cache_v, state_rec, c, c_ctx, w_ada, b_ada, norm_g, w_ffn_in, w_ffn_out,
           w_in, w_out, q_norm_g, k_norm_g, attn_sink, rec_lb_logits, rec_norm_g, conv_w, conv_b, conv_ln_g, conv_ln_b):
    batch, seq, _ = x_prompt.shape
    dec_batch, dec_seq, _ = x_sample.shape
    assert seq % TM == 0 or TM % seq == 0
    assert dec_seq % TM == 0 and dec_seq % LAT_Q_TILE == 0 and seq % CONV_ROWS == 0 and dec_seq % CONV_ROWS == 0

    P = {'norm_g': norm_g, 'w_ffn_in': w_ffn_in.astype(BF16), 'w_ffn_out': w_ffn_out.astype(BF16),
         'w_in': w_in.astype(BF16), 'w_out': w_out.astype(BF16)}

    lb_logits = rec_lb_logits.reshape(DEPTH, 2, N_REC_HEADS, 1, REC_DK)

    cond8 = jnp.zeros((SUBLANES, D_MODEL), F32).at[0].set(c_ctx).at[1:1 + dec_batch].set(c)
    mod = _modulation(cond8, w_ada, b_ada).reshape(DEPTH, SUBLANES, N_MOD, 1, D_MODEL)
    tables = _rope_tables(dec_seq)
    sinks = attn_sink.reshape(DEPTH, N_KV_HEADS, Q_PER_KV)
    rec_g = rec_norm_g.reshape(DEPTH, N_REC_HEADS, 1, REC_DV)

    def shared_mixers(z, l, seq_len):
        return _conv(z, conv_w[l], conv_b[l], conv_ln_g[l], conv_ln_b[l], seq_len)

    h = x_prompt.reshape(batch * seq, D_MODEL)
    ks, vs, ss = [], [], []
    for l in range(DEPTH):
        def ctx_mixers(z, l=l):
            attn, k_l, v_l = _ctx_attention(z, sinks[l], q_norm_g[l], k_norm_g[l], batch, seq)
            rec, s_l = _recurrence(z, l, lb_logits, rec_g[l], None, batch, seq)
            return attn, rec, shared_mixers(z, l, seq), (k_l, v_l, s_l)

        h, (k_l, v_l, s_l) = _trunk_layer(h, mod[l, 0:1], l, P, ctx_mixers, batch * seq)
        ks.append(k_l)
        vs.append(v_l)
        ss.append(s_l)
    y_prompt = h.reshape(batch, seq, D_MODEL)

    h = x_sample.reshape(dec_batch * dec_seq, D_MODEL)
    for l in range(DEPTH):
        def lat_mixers(z, l=l):
            attn = _lat_attention(z, sinks[l], q_norm_g[l], k_norm_g[l], cache_k[:, l], cache_v[:, l], tables,
                                  dec_batch, dec_seq)
            rec = _recurrence(z, l, lb_logits, rec_g[l], state_rec[:, l], dec_batch, dec_seq)[0]
            return attn, rec, shared_mixers(z, l, dec_seq), None

        h, _ = _trunk_layer(h, mod[l, 1:1 + dec_batch], l, P, lat_mixers, dec_seq)
    y_sample = h.reshape(dec_batch, dec_seq, D_MODEL)

    return (y_prompt, y_sample, jnp.stack(ks, axis=1), jnp.stack(vs, axis=1), jnp.stack(ss, axis=1))
```

```python
import functools

import jax
import jax.numpy as jnp
import numpy as np
from jax import lax
from jax.experimental import pallas as pl
from jax.experimental.pallas import tpu as pltpu

F32 = jnp.float32
BF16 = jnp.bfloat16

D_MODEL = 2048
DEPTH = 2
GRID_W = 64
HEAD_DIM = 128
D_ATTN = D_MODEL // 2
N_Q_HEADS = D_ATTN // HEAD_DIM
N_KV_HEADS = 2
Q_PER_KV = N_Q_HEADS // N_KV_HEADS
WINDOW = 128
ATTN_BLOCK = 128
ATTN_SCALE = HEAD_DIM ** -0.5
ROPE_BASE = 10000.0
MASK_VALUE = -1e30
D_REC = D_MODEL // 4
REC_DK = 128
REC_DV = 128
N_REC_HEADS = D_REC // REC_DV
REC_CHUNK = 32
CONV_CH = D_MODEL // 4
CONV_WIDTH = 31
D_FF = 5632
N_MOD = 9
EPS = 1e-6
GATE_FLOOR = 1e-30
LOG2_E = 1.4426950408889634
IN_COLS = 5120

COL_Q = 0
COL_K = 8
COL_V = 10
COL_RQ = 12
COL_RF_F = 16
COL_RF_B = 20
COL_RI = 24
COL_RG = 28
COL_CA = 32
COL_CB = 36

SUBLANES = 8
VMEM_LIMIT = 56 * 1024 * 1024

TM = 1024
TM_FFN = 1024
TF = 512
TN_FFN_OUT = 512
TN_IN = 1024
TN_ADA = 1024
CONV_ROWS = 256
CONV_HALO = 16
CONV_SUB = 32
LAT_Q_TILE = 1024
REC_LOCAL_UNROLL = 8
REC_STATE_UNROLL = 8


def _params(*sem):
    return pltpu.CompilerParams(dimension_semantics=sem, vmem_limit_bytes=VMEM_LIMIT)


def _dot(a, b):
    return jnp.dot(a, b, preferred_element_type=F32)


def _dot_nt(a, b):
    return lax.dot_general(a, b, (((1,), (1,)), ((), ())), preferred_element_type=F32)


def _dot_tn(a, b):
    return lax.dot_general(a, b, (((0,), (0,)), ((), ())), preferred_element_type=F32)


def _rms(x, g):
    return x * lax.rsqrt(jnp.mean(x * x, axis=-1, keepdims=True) + EPS) * g


def _sigmoid(x):
    return 1.0 / (1.0 + jnp.exp(-x))


def _ada_kernel(c_ref, w_ref, b_ref, o_ref):
    c = c_ref[...]
    s = (c * _sigmoid(c)).astype(BF16)
    o_ref[...] = _dot(s, w_ref[...].astype(BF16)) + b_ref[...]


def _modulation(cond8, w_ada, b_ada):
    ncol = N_MOD * D_MODEL
    return pl.pallas_call(
        _ada_kernel,
        grid=(DEPTH, ncol // TN_ADA),
        in_specs=[
            pl.BlockSpec((SUBLANES, D_MODEL), lambda l, j: (0, 0)),
            pl.BlockSpec((None, D_MODEL, TN_ADA), lambda l, j: (l, 0, j)),
            pl.BlockSpec((None, 1, TN_ADA), lambda l, j: (l, 0, j)),
        ],
        out_specs=pl.BlockSpec((None, SUBLANES, TN_ADA), lambda l, j: (l, 0, j)),
        out_shape=jax.ShapeDtypeStruct((DEPTH, SUBLANES, ncol), F32),
        compiler_params=_params("parallel", "parallel"),
        name="modulation",
    )(cond8, w_ada, b_ada.reshape(DEPTH, 1, ncol))


def _mod_spec(chunk, rows_per_cond, tm=TM):
    return pl.BlockSpec((None, None, 1, D_MODEL), lambda i, j: ((i * tm) // rows_per_cond, chunk, 0, 0))


def _ffn_kernel(nf, x_ref, sh_ref, sc_ref, g_ref, ng_ref, wa_ref, wb_ref, wo_ref, o_ref, h_scr, act_scr):
    j = pl.program_id(1)

    @pl.when(j == 0)
    def _():
        h = _rms(x_ref[...], ng_ref[...]) * (1.0 + sc_ref[...]) + sh_ref[...]
        h_scr[...] = h.astype(BF16)

    @pl.when(j < nf)
    def _():
        h = h_scr[...]
        a = _dot(h, wa_ref[...])
        b = _dot(h, wb_ref[...])
        act_scr[:, pl.ds(pl.multiple_of(j * TF, TF), TF)] = (a * _sigmoid(a) * b).astype(BF16)

    @pl.when(j >= nf)
    def _():
        cols = pl.ds(pl.multiple_of((j - nf) * TN_FFN_OUT, TN_FFN_OUT), TN_FFN_OUT)
        o_ref[...] = x_ref[:, cols] + 0.5 * g_ref[...] * _dot(act_scr[...], wo_ref[...])


def _ffn(x, mod, first_chunk, norm_g, w_in, w_out, layer, which, rows_per_cond):
    n = x.shape[0]
    nf = D_FF // TF
    nout = D_MODEL // TN_FFN_OUT
    hid = lambda j: jnp.minimum(j, nf - 1)
    out = lambda j: jnp.maximum(j - nf, 0)
    return pl.pallas_call(
        functools.partial(_ffn_kernel, nf),
        grid=(n // TM_FFN, nf + nout),
        in_specs=[
            pl.BlockSpec((TM_FFN, D_MODEL), lambda i, j: (i, 0), pipeline_mode=pl.Buffered(1)),
            _mod_spec(first_chunk, rows_per_cond, TM_FFN),
            _mod_spec(first_chunk + 1, rows_per_cond, TM_FFN),
            pl.BlockSpec((None, None, 1, TN_FFN_OUT),
                         lambda i, j: ((i * TM_FFN) // rows_per_cond, first_chunk + 2, 0, out(j))),
            pl.BlockSpec((1, D_MODEL), lambda i, j: (0, 0)),
            pl.BlockSpec((None, None, D_MODEL, TF), lambda i, j: (layer, which, 0, hid(j))),
            pl.BlockSpec((None, None, D_MODEL, TF), lambda i, j: (layer, which, 0, hid(j) + nf)),
            pl.BlockSpec((None, None, D_FF, TN_FFN_OUT), lambda i, j: (layer, which, 0, out(j))),
        ],
        out_specs=pl.BlockSpec((TM_FFN, TN_FFN_OUT), lambda i, j: (i, out(j))),
        out_shape=jax.ShapeDtypeStruct((n, D_MODEL), F32),
        scratch_shapes=[pltpu.VMEM((TM_FFN, D_MODEL), BF16), pltpu.VMEM((TM_FFN, D_FF), BF16)],
        compiler_params=_params("parallel", "arbitrary"),
        name="ffn",
    )(x, mod, mod, mod, norm_g.reshape(1, D_MODEL), w_in, w_in, w_out)


def _inproj_kernel(x_ref, sh_ref, sc_ref, ng_ref, w_ref, o_ref, h_scr):
    @pl.when(pl.program_id(1) == 0)
    def _():
        h = _rms(x_ref[...], ng_ref[...]) * (1.0 + sc_ref[...]) + sh_ref[...]
        h_scr[...] = h.astype(BF16)

    o_ref[...] = _dot(h_scr[...], w_ref[...])


def _inproj(x, mod, norm_g, w_in, layer, rows_per_cond):
    n = x.shape[0]
    return pl.pallas_call(
        _inproj_kernel,
        grid=(n // TM, IN_COLS // TN_IN),
        in_specs=[
            pl.BlockSpec((TM, D_MODEL), lambda i, j: (i, 0)),
            _mod_spec(3, rows_per_cond),
            _mod_spec(4, rows_per_cond),
            pl.BlockSpec((1, D_MODEL), lambda i, j: (0, 0)),
            pl.BlockSpec((None, D_MODEL, TN_IN), lambda i, j: (layer, 0, j)),
        ],
        out_specs=pl.BlockSpec((TM, TN_IN), lambda i, j: (i, j)),
        out_shape=jax.ShapeDtypeStruct((n, IN_COLS), F32),
        scratch_shapes=[pltpu.VMEM((TM, D_MODEL), BF16)],
        compiler_params=_params("parallel", "arbitrary"),
        name="inproj",
    )(x, mod, mod, norm_g.reshape(1, D_MODEL), w_in)


QK_SCALE_LOG2 = ATTN_SCALE * LOG2_E


def _with_ones(v):
    return jnp.concatenate([v, jnp.ones_like(v)], axis=-1)


def _softmax_sink(scores, values, sink):
    sink2 = sink * LOG2_E
    m = sink2
    for s in scores:
        m = jnp.maximum(jnp.max(s, axis=-1, keepdims=True), m)
    acc = None
    for s, v in zip(scores, values):
        pv = _dot(jnp.exp2(s - m).astype(BF16), v)
        acc = pv if acc is None else acc + pv
    return acc[:, :HEAD_DIM] / (acc[:, HEAD_DIM:] + jnp.exp2(sink2 - m))


def _ctx_attn_kernel(sink_ref, q_ref, k_ref, v_ref, qg_ref, kg_ref, o_ref, kc_ref, vc_ref):
    kv = pl.program_id(1)
    kn = _rms(k_ref[...], kg_ref[...])
    v = v_ref[...]
    kc_ref[...] = kn
    vc_ref[...] = v
    kb = kn.astype(BF16)
    vb = _with_ones(v.astype(BF16))
    for g in range(Q_PER_KV):
        cols = slice(g * HEAD_DIM, (g + 1) * HEAD_DIM)
        qn = (_rms(q_ref[:, cols], qg_ref[...]) * QK_SCALE_LOG2).astype(BF16)
        o = _softmax_sink([_dot_nt(qn, kb)], [vb], sink_ref[kv, g])
        o_ref[:, cols] = o.astype(BF16)


def _ctx_attention(z, sink, q_g, k_g, batch, seq):
    qw = Q_PER_KV * HEAD_DIM
    cache_shape = jax.ShapeDtypeStruct((batch, N_KV_HEADS, seq, HEAD_DIM), F32)
    cache_spec = pl.BlockSpec((None, None, seq, HEAD_DIM), lambda b, kv: (b, kv, 0, 0))
    return pl.pallas_call(
        _ctx_attn_kernel,
        grid=(batch, N_KV_HEADS),
        in_specs=[
            pl.BlockSpec(memory_space=pltpu.SMEM),
            pl.BlockSpec((seq, qw), lambda b, kv: (b, kv)),
            pl.BlockSpec((seq, HEAD_DIM), lambda b, kv: (b, COL_K + kv)),
            pl.BlockSpec((seq, HEAD_DIM), lambda b, kv: (b, COL_V + kv)),
            pl.BlockSpec((1, HEAD_DIM), lambda b, kv: (0, 0)),
            pl.BlockSpec((1, HEAD_DIM), lambda b, kv: (0, 0)),
        ],
        out_specs=[pl.BlockSpec((seq, qw), lambda b, kv: (b, kv)), cache_spec, cache_spec],
        out_shape=[jax.ShapeDtypeStruct((batch * seq, D_ATTN), BF16), cache_shape, cache_shape],
        compiler_params=_params("parallel", "parallel"),
        name="ctx_attention",
    )(sink, z, z, z, q_g.reshape(1, HEAD_DIM), k_g.reshape(1, HEAD_DIM))


def _rope(x, cos, sin_a, sin_b):
    quarter = HEAD_DIM // 4
    up = pltpu.roll(x, HEAD_DIM - quarter, 1)
    down = pltpu.roll(x, quarter, 1)
    return x * cos + up * sin_a + down * sin_b


def _lat_attn_kernel(sink_ref, q_ref, k_ref, v_ref, kc_ref, vc_ref, qg_ref, kg_ref,
                     cos_ref, sa_ref, sb_ref, cosq_ref, saq_ref, sbq_ref, o_ref, k_scr, v_scr, kc_scr, vc_scr):
    kv = pl.program_id(1)
    qt = pl.program_id(2)
    seq = k_ref.shape[0]
    span = 3 * ATTN_BLOCK

    @pl.when(qt == 0)
    def _():
        kn = _rms(k_ref[...], kg_ref[...])
        k_scr[...] = _rope(kn, cos_ref[...], sa_ref[...], sb_ref[...]).astype(BF16)
        v_scr[...] = _with_ones(v_ref[...].astype(BF16))
        kc_scr[...] = kc_ref[...].astype(BF16)
        vc_scr[...] = _with_ones(vc_ref[...].astype(BF16))

    kcb = kc_scr[...]
    vcb = vc_scr[...]
    for blk in range(LAT_Q_TILE // ATTN_BLOCK):
        rows = slice(blk * ATTN_BLOCK, (blk + 1) * ATTN_BLOCK)
        q0 = (qt * (LAT_Q_TILE // ATTN_BLOCK) + blk) * ATTN_BLOCK
        start = pl.multiple_of(jnp.clip(q0 - ATTN_BLOCK, 0, seq - span), ATTN_BLOCK)
        kl = k_scr[pl.ds(start, span), :]
        vl = v_scr[pl.ds(start, span), :]
        qpos = q0 + lax.broadcasted_iota(jnp.int32, (ATTN_BLOCK, span), 0)
        kpos = start + lax.broadcasted_iota(jnp.int32, (ATTN_BLOCK, span), 1)
        valid = jnp.abs(qpos - kpos) <= WINDOW
        cos, sa, sb = cosq_ref[rows, :], saq_ref[rows, :], sbq_ref[rows, :]
        for g in range(Q_PER_KV):
            cols = slice(g * HEAD_DIM, (g + 1) * HEAD_DIM)
            qn = (_rope(_rms(q_ref[rows, cols], qg_ref[...]), cos, sa, sb) * QK_SCALE_LOG2).astype(BF16)
            s_loc = jnp.where(valid, _dot_nt(qn, kl), MASK_VALUE)
            o = _softmax_sink([s_loc, _dot_nt(qn, kcb)], [vl, vcb], sink_ref[kv, g])
            o_ref[rows, cols] = o.astype(BF16)


def _rope_tables(seq):
    quarter = HEAD_DIM // 4
    pos = np.arange(seq)
    inv_freq = ROPE_BASE ** (-np.arange(quarter, dtype=np.float32) / quarter)
    inv_freq = jnp.asarray(inv_freq, F32)
    zero = jnp.zeros((seq, quarter), F32)

    def trig(p):
        ang = jnp.asarray(p, F32)[:, None] * inv_freq
        return jnp.cos(ang), jnp.sin(ang)

    cr, sr = trig(pos // GRID_W)
    cc, sc = trig(pos % GRID_W)
    cos = jnp.concatenate([cr, cr, cc, cc], axis=-1)
    sin_a = jnp.concatenate([-sr, zero, -sc, zero], axis=-1)
    sin_b = jnp.concatenate([zero, sr, zero, sc], axis=-1)
    return cos, sin_a, sin_b


def _lat_attention(z, sink, q_g, k_g, k_ctx, v_ctx, tables, batch, seq):
    qw = Q_PER_KV * HEAD_DIM
    nqt = seq // LAT_Q_TILE
    past = k_ctx.shape[2]
    cos, sin_a, sin_b = tables
    full_tab = pl.BlockSpec((seq, HEAD_DIM), lambda b, kv, qt: (0, 0))
    tile_tab = pl.BlockSpec((LAT_Q_TILE, HEAD_DIM), lambda b, kv, qt: (qt, 0))
    ctx_spec = pl.BlockSpec((None, None, past, HEAD_DIM), lambda b, kv, qt: (b, kv, 0, 0))
    gain = pl.BlockSpec((1, HEAD_DIM), lambda b, kv, qt: (0, 0))
    return pl.pallas_call(
        _lat_attn_kernel,
        grid=(batch, N_KV_HEADS, nqt),
        in_specs=[
            pl.BlockSpec(memory_space=pltpu.SMEM),
            pl.BlockSpec((LAT_Q_TILE, qw), lambda b, kv, qt: (b * nqt + qt, kv)),
            pl.BlockSpec((seq, HEAD_DIM), lambda b, kv, qt: (b, COL_K + kv)),
            pl.BlockSpec((seq, HEAD_DIM), lambda b, kv, qt: (b, COL_V + kv)),
            ctx_spec, ctx_spec, gain, gain,
            full_tab, full_tab, full_tab, tile_tab, tile_tab, tile_tab,
        ],
        out_specs=pl.BlockSpec((LAT_Q_TILE, qw), lambda b, kv, qt: (b * nqt + qt, kv)),
        out_shape=jax.ShapeDtypeStruct((batch * seq, D_ATTN), BF16),
        scratch_shapes=[pltpu.VMEM((seq, HEAD_DIM), BF16), pltpu.VMEM((seq, 2 * HEAD_DIM), BF16),
                        pltpu.VMEM((past, HEAD_DIM), BF16), pltpu.VMEM((past, 2 * HEAD_DIM), BF16)],
        compiler_params=_params("parallel", "parallel", "arbitrary"),
        name="lat_attention",
    )(sink, z, z, z, k_ctx, v_ctx, q_g.reshape(1, HEAD_DIM), k_g.reshape(1, HEAD_DIM),
      cos, sin_a, sin_b, cos, sin_a, sin_b)


def _split3(x):
    hi = x.astype(BF16)
    r1 = x - hi.astype(F32)
    mid = r1.astype(BF16)
    lo = (r1 - mid.astype(F32)).astype(BF16)
    return hi, mid, lo


def _rec_gates(forward, zf, lb):
    c = REC_CHUNK
    t = jnp.exp(-jnp.abs(zf))
    r = 1.0 / (1.0 + t)
    tr = t * r
    nonneg = zf >= 0
    sig_pos = jnp.where(nonneg, r, tr)
    sig_neg = jnp.where(nonneg, tr, r)
    one_m_lb = 1.0 - lb
    log_f = jnp.log(jnp.maximum(lb + one_m_lb * sig_pos, GATE_FLOOR))
    k = one_m_lb * sig_neg
    ri = lax.broadcasted_iota(jnp.int32, (c, c), 0)
    ci = lax.broadcasted_iota(jnp.int32, (c, c), 1)
    tri = jnp.where((ci <= ri) if forward else (ci >= ri), 1.0, 0.0).astype(BF16)
    hi, mid, lo = _split3(log_f)
    cum2 = (_dot(tri, hi) + _dot(tri, mid) + _dot(tri, lo)) * LOG2_E
    return cum2 - jnp.log2(k), cum2


def _rec_same_block(forward, q, key2, cum2):
    sub_row = lax.broadcasted_iota(jnp.int32, (SUBLANES, REC_DK), 0)
    pieces = []
    for b in range(REC_CHUNK // SUBLANES):
        blk = slice(b * SUBLANES, (b + 1) * SUBLANES)
        cum_b, q_b = cum2[blk, :], q[blk, :]
        for sl in range(SUBLANES):
            s = b * SUBLANES + sl
            keep = (sub_row >= sl) if forward else (sub_row <= sl)
            pieces.append(jnp.where(keep, jnp.exp2(cum_b - key2[s:s + 1, :]) * q_b, 0.0).astype(BF16))
    return _dot(jnp.concatenate(pieces, axis=0), jnp.ones((REC_DK, REC_DV), BF16))


def _rec_cross_block(forward, q, key2, v, cum2):
    c = REC_CHUNK
    q_parts, k_parts, v_parts, segments = [], [], [], []
    for tb in range(c // SUBLANES):
        blk = slice(tb * SUBLANES, (tb + 1) * SUBLANES)
        src, ref_row = (slice(0, blk.start), blk.start - 1) if forward else (slice(blk.stop, c), blk.stop)
        if src.stop == src.start:
            q_parts.append(jnp.zeros((SUBLANES, REC_DK), F32))
            continue
        ref = cum2[ref_row:ref_row + 1, :]
        q_parts.append(q[blk, :] * jnp.exp2(cum2[blk, :] - ref))
        k_parts.append(jnp.exp2(ref - key2[src, :]))
        v_parts.append(v[src, :])
        segments.append((tb, src.stop - src.start))
    a = _dot_nt(jnp.concatenate(q_parts, axis=0).astype(BF16), jnp.concatenate(k_parts, axis=0).astype(BF16))
    return a, segments, jnp.concatenate(v_parts, axis=0).astype(BF16)


def _rec_cross_apply(a, segments, v_all):
    c, ncol = a.shape
    row_blk = lax.broadcasted_iota(jnp.int32, (c, ncol), 0) // SUBLANES
    col = lax.broadcasted_iota(jnp.int32, (c, ncol), 1)
    col_blk = jnp.full((c, ncol), -1, jnp.int32)
    start = 0
    for tb, width in segments:
        col_blk = jnp.where((col >= start) & (col < start + width), tb, col_blk)
        start += width
    return _dot(jnp.where(row_blk == col_blk, a, 0.0).astype(BF16), v_all)


def _rec_same_apply(lane_sums, v):
    o_blocks = []
    for b in range(REC_CHUNK // SUBLANES):
        acc = None
        for sl in range(SUBLANES):
            s = b * SUBLANES + sl
            term = lane_sums[s * SUBLANES:(s + 1) * SUBLANES, :] * v[s:s + 1, :]
            acc = term if acc is None else acc + term
        o_blocks.append(acc)
    return jnp.concatenate(o_blocks, axis=0)


def _rec_kernel(layer, has_init, *refs):
    if has_init:
        (q_ref, ff_ref, fb_ref, v_ref, g_ref, lg_ref, ng_ref, s0_ref, o_ref,
         o_scr, qd_scr, kd_scr, gl_scr, oi_scr) = refs
        s_out_ref = None
    else:
        (q_ref, ff_ref, fb_ref, v_ref, g_ref, lg_ref, ng_ref, o_ref, s_out_ref,
         o_scr, qd_scr, kd_scr, gl_scr, oi_scr) = refs
    seq = q_ref.shape[0]
    nchunk = seq // REC_CHUNK
    c = REC_CHUNK

    logits = lg_ref[...]
    e = jnp.exp(logits - jnp.max(logits, axis=0, keepdims=True))
    p = e / jnp.sum(e, axis=0, keepdims=True)
    lb = jnp.zeros_like(p[0])
    for i in range(1, layer + 1):
        lb = lb + p[i]

    def rows_of(ci):
        return pl.ds(pl.multiple_of(ci * c, c), c)

    z_refs = (ff_ref, fb_ref)

    def local_body(it, carry):
        chunks = [it * REC_LOCAL_UNROLL + u for u in range(REC_LOCAL_UNROLL)]
        qs = [q_ref[rows_of(ci), :] for ci in chunks]
        vs = [v_ref[rows_of(ci), :] for ci in chunks]
        items = [(u, d) for u in range(REC_LOCAL_UNROLL) for d in range(2)]
        gates = [_rec_gates(d == 0, z_refs[d][rows_of(chunks[u]), :], lb[d]) for u, d in items]
        same = [_rec_same_block(d == 0, qs[u], key2, cum2) for (u, d), (key2, cum2) in zip(items, gates)]
        cross = [_rec_cross_block(d == 0, qs[u], key2, vs[u], cum2) for (u, d), (key2, cum2) in zip(items, gates)]
        totals = [None] * REC_LOCAL_UNROLL
        for (u, d), (key2, cum2), lane_sums, (a, segments, v_all) in zip(items, gates, same, cross):
            rows = rows_of(chunks[u])
            last = cum2[c - 1:c, :] if d == 0 else cum2[0:1, :]
            qd_scr[d, rows, :] = (qs[u] * jnp.exp2(cum2)).astype(BF16)
            kd_scr[d, rows, :] = jnp.exp2(last - key2).astype(BF16)
            gl_scr[d, chunks[u]] = jnp.broadcast_to(jnp.exp2(last), (SUBLANES, REC_DK))
            o_local = _rec_same_apply(lane_sums, vs[u]) + _rec_cross_apply(a, segments, v_all)
            totals[u] = o_local if totals[u] is None else totals[u] + o_local
        for u, ci in enumerate(chunks):
            o_scr[rows_of(ci), :] = totals[u]
        return carry

    lax.fori_loop(0, nchunk // REC_LOCAL_UNROLL, local_body, 0)

    if has_init:
        init = (s0_ref[0].T, s0_ref[1].T)
    else:
        init = (jnp.zeros((REC_DV, REC_DK), F32),) * 2

    def state_body(it, states):
        states = list(states)
        steps = [it * REC_STATE_UNROLL + u for u in range(REC_STATE_UNROLL)]
        order = [(d, step if d == 0 else nchunk - 1 - step) for step in steps for d in range(2)]
        updates = [_dot_tn(v_ref[rows_of(ci), :].astype(BF16), kd_scr[d, rows_of(ci), :]) for d, ci in order]
        for (d, ci), update in zip(order, updates):
            st = states[d]
            oi_scr[d, rows_of(ci), :] = _dot_nt(qd_scr[d, rows_of(ci), :], st.astype(BF16))
            decayed = (st.reshape(REC_DV // SUBLANES, SUBLANES, REC_DK) * gl_scr[d, ci]).reshape(REC_DV, REC_DK)
            states[d] = decayed + update
        return tuple(states)

    final = lax.fori_loop(0, nchunk // REC_STATE_UNROLL, state_body, init)
    if not has_init:
        s_out_ref[0] = final[0].T
        s_out_ref[1] = final[1].T

    def out_body(ci, carry):
        rows = rows_of(ci)
        gate = g_ref[rows, :]
        o = o_scr[rows, :] + oi_scr[0, rows, :] + oi_scr[1, rows, :]
        o_ref[rows, :] = (_rms(o, ng_ref[...]) * (gate * _sigmoid(gate))).astype(BF16)
        return carry

    lax.fori_loop(0, nchunk, out_body, 0, unroll=4)


def _recurrence(z, layer, lb_logits, norm_g, s0, batch, seq):
    has_init = s0 is not None
    col = lambda base: pl.BlockSpec((seq, REC_DK), lambda b, h: (b, base + h))
    state_spec = pl.BlockSpec((None, 2, None, REC_DK, REC_DV), lambda b, h: (b, 0, h, 0, 0))
    in_specs = [col(COL_RQ), col(COL_RF_F), col(COL_RF_B), col(COL_RI), col(COL_RG),
                pl.BlockSpec((DEPTH, 2, None, 1, REC_DK), lambda b, h: (0, 0, h, 0, 0)),
                pl.BlockSpec((None, 1, REC_DV), lambda b, h: (h, 0, 0))]
    args = [z, z, z, z, z, lb_logits, norm_g]
    out_specs = [pl.BlockSpec((seq, REC_DV), lambda b, h: (b, h))]
    out_shape = [jax.ShapeDtypeStruct((batch * seq, D_REC), BF16)]
    if has_init:
        in_specs.append(state_spec)
        args.append(s0)
    else:
        out_specs.append(state_spec)
        out_shape.append(jax.ShapeDtypeStruct((batch, 2, N_REC_HEADS, REC_DK, REC_DV), F32))
    return pl.pallas_call(
        functools.partial(_rec_kernel, layer, has_init),
        grid=(batch, N_REC_HEADS),
        in_specs=in_specs,
        out_specs=out_specs,
        out_shape=out_shape,
        scratch_shapes=[pltpu.VMEM((seq, REC_DV), F32),
                        pltpu.VMEM((2, seq, REC_DK), BF16),
                        pltpu.VMEM((2, seq, REC_DK), BF16),
                        pltpu.VMEM((2, seq // REC_CHUNK, SUBLANES, REC_DK), F32),
                        pltpu.VMEM((2, seq, REC_DV), F32)],
        compiler_params=_params("parallel", "parallel"),
        name="hgrn2",
    )(*args)


def _conv_kernel(tiles_per_seq, a_ref, b_ref, at_ref, bt_ref, ab_ref, bb_ref, w_ref, bias_ref, lg_ref, lb_ref,
                 o_ref, h_scr, w_scr):
    i = pl.program_id(0)
    tile_in_seq = i % tiles_per_seq
    has_top = jnp.where(tile_in_seq > 0, 1.0, 0.0)
    has_bot = jnp.where(tile_in_seq < tiles_per_seq - 1, 1.0, 0.0)
    rows = a_ref.shape[0]
    half = CONV_WIDTH // 2
    shifted_rows = rows + 2 * CONV_HALO - SUBLANES

    def glu(a, b):
        return a * _sigmoid(b)

    h_scr[0, 0:CONV_HALO, :] = glu(at_ref[...], bt_ref[...]) * has_top
    h_scr[0, CONV_HALO:CONV_HALO + rows, :] = glu(a_ref[...], b_ref[...])
    h_scr[0, CONV_HALO + rows:, :] = glu(ab_ref[...], bb_ref[...]) * has_bot
    for r in range(1, SUBLANES):
        h_scr[r, 0:shifted_rows, :] = h_scr[0, r:r + shifted_rows, :]
    for j in range(CONV_WIDTH):
        w_scr[j] = jnp.broadcast_to(w_ref[j:j + 1, :], (SUBLANES, CONV_CH))

    for r0 in range(0, rows, CONV_SUB):
        acc = None
        for j in range(CONV_WIDTH):
            lo = r0 + CONV_HALO - half + j
            shift = lo % SUBLANES
            weight = w_scr[j]
            window = h_scr[shift, lo - shift:lo - shift + CONV_SUB, :]
            term = (window.reshape(CONV_SUB // SUBLANES, SUBLANES, CONV_CH) * weight).reshape(CONV_SUB, CONV_CH)
            acc = term if acc is None else acc + term
        y = acc + bias_ref[...]
        mu = jnp.mean(y, axis=-1, keepdims=True)
        yc = y - mu
        var = jnp.mean(yc * yc, axis=-1, keepdims=True)
        yn = yc * lax.rsqrt(var + EPS) * lg_ref[...] + lb_ref[...]
        o_ref[r0:r0 + CONV_SUB, :] = (yn * _sigmoid(yn)).astype(BF16)


def _conv(z, w, bias, ln_g, ln_b, seq):
    n = z.shape[0]
    tiles_per_seq = seq // CONV_ROWS
    ntiles = n // CONV_ROWS
    cw = CONV_CH // HEAD_DIM
    halo_per_tile = CONV_ROWS // CONV_HALO
    nhalo = n // CONV_HALO
    ca, cb = COL_CA // cw, COL_CB // cw
    mid = lambda c: pl.BlockSpec((CONV_ROWS, CONV_CH), lambda i: (i, c))
    top = lambda c: pl.BlockSpec((CONV_HALO, CONV_CH), lambda i: (jnp.maximum(i * halo_per_tile - 1, 0), c))
    bot = lambda c: pl.BlockSpec((CONV_HALO, CONV_CH), lambda i: (jnp.minimum((i + 1) * halo_per_tile, nhalo - 1), c))
    vec = pl.BlockSpec((1, CONV_CH), lambda i: (0, 0))
    return pl.pallas_call(
        functools.partial(_conv_kernel, tiles_per_seq),
        grid=(ntiles,),
        in_specs=[mid(ca), mid(cb), top(ca), top(cb), bot(ca), bot(cb),
                  pl.BlockSpec((CONV_WIDTH, CONV_CH), lambda i: (0, 0)), vec, vec, vec],
        out_specs=pl.BlockSpec((CONV_ROWS, CONV_CH), lambda i: (i, 0)),
        out_shape=jax.ShapeDtypeStruct((n, CONV_CH), BF16),
        scratch_shapes=[pltpu.VMEM((SUBLANES, CONV_ROWS + 2 * CONV_HALO, CONV_CH), F32),
                        pltpu.VMEM((CONV_WIDTH, SUBLANES, CONV_CH), F32)],
        compiler_params=_params("parallel"),
        name="conv",
    )(z, z, z, z, z, z, w, bias.reshape(1, CONV_CH), ln_g.reshape(1, CONV_CH), ln_b.reshape(1, CONV_CH))


def _outproj_kernel(x_ref, g_ref, a_ref, r_ref, c_ref, wa_ref, wr_ref, wc_ref, o_ref):
    mix = _dot(a_ref[...], wa_ref[...]) + _dot(r_ref[...], wr_ref[...]) + _dot(c_ref[...], wc_ref[...])
    o_ref[...] = x_ref[...] + g_ref[...] * mix


def _outproj(x, mod, attn, rec, conv, w_out, layer, rows_per_cond):
    n = x.shape[0]
    tn = D_MODEL // 2
    row = lambda width: pl.BlockSpec((TM, width), lambda i, j: (i, 0))
    return pl.pallas_call(
        _outproj_kernel,
        grid=(n // TM, D_MODEL // tn),
        in_specs=[
            pl.BlockSpec((TM, tn), lambda i, j: (i, j)),
            pl.BlockSpec((None, None, 1, tn), lambda i, j: ((i * TM) // rows_per_cond, 5, 0, j)),
            row(D_ATTN), row(D_REC), row(CONV_CH),
            pl.BlockSpec((None, D_ATTN, tn), lambda i, j: (layer, 0, j)),
            pl.BlockSpec((None, D_REC, tn), lambda i, j: (layer, D_ATTN // D_REC, j)),
            pl.BlockSpec((None, CONV_CH, tn), lambda i, j: (layer, (D_ATTN + D_REC) // CONV_CH, j)),
        ],
        out_specs=pl.BlockSpec((TM, tn), lambda i, j: (i, j)),
        out_shape=jax.ShapeDtypeStruct((n, D_MODEL), F32),
        compiler_params=_params("parallel", "parallel"),
        name="outproj",
    )(x, mod, attn, rec, conv, w_out, w_out, w_out)


def _trunk_layer(x, mod, l, P, mixers, rows_per_cond):
    x = _ffn(x, mod, 0, P['norm_g'][l, 0], P['w_ffn_in'], P['w_ffn_out'], l, 0, rows_per_cond)
    z = _inproj(x, mod, P['norm_g'][l, 1], P['w_in'], l, rows_per_cond)
    attn, rec, conv, extras = mixers(z)
    x = _outproj(x, mod, attn, rec, conv, P['w_out'], l, rows_per_cond)
    x = _ffn(x, mod, 6, P['norm_g'][l, 2], P['w_ffn_in'], P['w_ffn_out'], l, 1, rows_per_cond)
    return x, extras


def kernel(x_prompt, x_sample, cache_k, cache_v, state_rec, c, c_ctx, w_ada, b_ada, norm_g, w_ffn_in, w_ffn_out,
           w_in, w_out, q_norm_g, k_norm_g, attn_sink, rec_lb_logits, rec_norm_g, conv_w, conv_b, conv_ln_g, conv_ln_b):
    batch, seq, _ = x_prompt.shape
    dec_batch, dec_seq, _ = x_sample.shape
    assert seq % TM == 0 or TM % seq == 0
    assert dec_seq % TM == 0 and dec_seq % LAT_Q_TILE == 0 and seq % CONV_ROWS == 0 and dec_seq % CONV_ROWS == 0

    P = {'norm_g': norm_g, 'w_ffn_in': w_ffn_in.astype(BF16), 'w_ffn_out': w_ffn_out.astype(BF16),
         'w_in': w_in.astype(BF16), 'w_out': w_out.astype(BF16)}

    lb_logits = rec_lb_logits.reshape(DEPTH, 2, N_REC_HEADS, 1, REC_DK)

    cond8 = jnp.zeros((SUBLANES, D_MODEL), F32).at[0].set(c_ctx).at[1:1 + dec_batch].set(c)
    mod = _modulation(cond8, w_ada, b_ada).reshape(DEPTH, SUBLANES, N_MOD, 1, D_MODEL)
    tables = _rope_tables(dec_seq)
    sinks = attn_sink.reshape(DEPTH, N_KV_HEADS, Q_PER_KV)
    rec_g = rec_norm_g.reshape(DEPTH, N_REC_HEADS, 1, REC_DV)

    def shared_mixers(z, l, seq_len):
        return _conv(z, conv_w[l], conv_b[l], conv_ln_g[l], conv_ln_b[l], seq_len)

    h = x_prompt.reshape(batch * seq, D_MODEL)
    ks, vs, ss = [], [], []
    for l in range(DEPTH):
        def ctx_mixers(z, l=l):
            attn, k_l, v_l = _ctx_attention(z, sinks[l], q_norm_g[l], k_norm_g[l], batch, seq)
            rec, s_l = _recurrence(z, l, lb_logits, rec_g[l], None, batch, seq)
            return attn, rec, shared_mixers(z, l, seq), (k_l, v_l, s_l)

        h, (k_l, v_l, s_l) = _trunk_layer(h, mod[l, 0:1], l, P, ctx_mixers, batch * seq)
        ks.append(k_l)
        vs.append(v_l)
        ss.append(s_l)
    y_prompt = h.reshape(batch, seq, D_MODEL)

    h = x_sample.reshape(dec_batch * dec_seq, D_MODEL)
    for l in range(DEPTH):
        def lat_mixers(z, l=l):
            attn = _lat_attention(z, sinks[l], q_norm_g[l], k_norm_g[l], cache_k[:, l], cache_v[:, l], tables,
                                  dec_batch, dec_seq)
            rec = _recurrence(z, l, lb_logits, rec_g[l], state_rec[:, l], dec_batch, dec_seq)[0]
            return attn, rec, shared_mixers(z, l, dec_seq), None

        h, _ = _trunk_layer(h, mod[l, 1:1 + dec_batch], l, P, lat_mixers, dec_seq)
    y_sample = h.reshape(dec_batch, dec_seq, D_MODEL)

    return (y_prompt, y_sample, jnp.stack(ks, axis=1), jnp.stack(vs, axis=1), jnp.stack(ss, axis=1))
```

```python
import functools

import jax
import jax.numpy as jnp
import numpy as np
from jax import lax
from jax.experimental import pallas as pl
from jax.experimental.pallas import tpu as pltpu

F32 = jnp.float32
BF16 = jnp.bfloat16

D_MODEL = 2048
DEPTH = 2
GRID_W = 64
HEAD_DIM = 128
D_ATTN = D_MODEL // 2
N_Q_HEADS = D_ATTN // HEAD_DIM
N_KV_HEADS = 2
Q_PER_KV = N_Q_HEADS // N_KV_HEADS
WINDOW = 128
ATTN_BLOCK = 128
ATTN_SCALE = HEAD_DIM ** -0.5
ROPE_BASE = 10000.0
MASK_VALUE = -1e30
D_REC = D_MODEL // 4
REC_DK = 128
REC_DV = 128
N_REC_HEADS = D_REC // REC_DV
REC_CHUNK = 32
CONV_CH = D_MODEL // 4
CONV_WIDTH = 31
D_FF = 5632
N_MOD = 9
EPS = 1e-6
GATE_FLOOR = 1e-30
LOG2_E = 1.4426950408889634
IN_COLS = 5120

COL_Q = 0
COL_K = 8
COL_V = 10
COL_RQ = 12
COL_RF_F = 16
COL_RF_B = 20
COL_RI = 24
COL_RG = 28
COL_CA = 32
COL_CB = 36

SUBLANES = 8
VMEM_LIMIT = 56 * 1024 * 1024

TM = 1024
TM_FFN = 1024
TF = 512
TN_FFN_OUT = 512
TN_IN = 1024
TN_ADA = 1024
CONV_ROWS = 256
CONV_HALO = 16
CONV_SUB = 32
LAT_Q_TILE = 1024
REC_LOCAL_UNROLL = 8
REC_STATE_UNROLL = 8


def _params(*sem):
    return pltpu.CompilerParams(dimension_semantics=sem, vmem_limit_bytes=VMEM_LIMIT)


def _dot(a, b):
    return jnp.dot(a, b, preferred_element_type=F32)


def _dot_nt(a, b):
    return lax.dot_general(a, b, (((1,), (1,)), ((), ())), preferred_element_type=F32)


def _dot_tn(a, b):
    return lax.dot_general(a, b, (((0,), (0,)), ((), ())), preferred_element_type=F32)


def _rms(x, g):
    return x * lax.rsqrt(jnp.mean(x * x, axis=-1, keepdims=True) + EPS) * g


def _ada_norm(x, g, scale, shift):
    gain = g * (1.0 + scale)
    return x * lax.rsqrt(jnp.mean(x * x, axis=-1, keepdims=True) + EPS) * gain + shift


def _sigmoid(x):
    return 1.0 / (1.0 + jnp.exp(-x))


def _ada_kernel(c_ref, w_ref, b_ref, o_ref):
    c = c_ref[...]
    s = (c * _sigmoid(c)).astype(BF16)
    o_ref[...] = _dot(s, w_ref[...].astype(BF16)) + b_ref[...]


def _modulation(cond8, w_ada, b_ada):
    ncol = N_MOD * D_MODEL
    return pl.pallas_call(
        _ada_kernel,
        grid=(DEPTH, ncol // TN_ADA),
        in_specs=[
            pl.BlockSpec((SUBLANES, D_MODEL), lambda l, j: (0, 0)),
            pl.BlockSpec((None, D_MODEL, TN_ADA), lambda l, j: (l, 0, j)),
            pl.BlockSpec((None, 1, TN_ADA), lambda l, j: (l, 0, j)),
        ],
        out_specs=pl.BlockSpec((None, SUBLANES, TN_ADA), lambda l, j: (l, 0, j)),
        out_shape=jax.ShapeDtypeStruct((DEPTH, SUBLANES, ncol), F32),
        compiler_params=_params("parallel", "parallel"),
        name="modulation",
    )(cond8, w_ada, b_ada.reshape(DEPTH, 1, ncol))


def _mod_spec(chunk, rows_per_cond, tm=TM):
    return pl.BlockSpec((None, None, 1, D_MODEL), lambda i, j: ((i * tm) // rows_per_cond, chunk, 0, 0))


def _ffn_kernel(nf, x_ref, sh_ref, sc_ref, g_ref, ng_ref, wa_ref, wb_ref, wo_ref, o_ref, h_scr, act_scr):
    j = pl.program_id(1)

    @pl.when(j == 0)
    def _():
        h = _ada_norm(x_ref[...], ng_ref[...], sc_ref[...], sh_ref[...])
        h_scr[...] = h.astype(BF16)

    @pl.when(j < nf)
    def _():
        h = h_scr[...]
        a = _dot(h, wa_ref[...])
        b = _dot(h, wb_ref[...])
        act_scr[:, pl.ds(pl.multiple_of(j * TF, TF), TF)] = (a * _sigmoid(a) * b).astype(BF16)

    @pl.when(j >= nf)
    def _():
        cols = pl.ds(pl.multiple_of((j - nf) * TN_FFN_OUT, TN_FFN_OUT), TN_FFN_OUT)
        o_ref[...] = x_ref[:, cols] + 0.5 * g_ref[...] * _dot(act_scr[...], wo_ref[...])


def _ffn(x, mod, first_chunk, norm_g, w_in, w_out, layer, which, rows_per_cond):
    n = x.shape[0]
    nf = D_FF // TF
    nout = D_MODEL // TN_FFN_OUT
    hid = lambda j: jnp.minimum(j, nf - 1)
    out = lambda j: jnp.maximum(j - nf, 0)
    return pl.pallas_call(
        functools.partial(_ffn_kernel, nf),
        grid=(n // TM_FFN, nf + nout),
        in_specs=[
            pl.BlockSpec((TM_FFN, D_MODEL), lambda i, j: (i, 0), pipeline_mode=pl.Buffered(1)),
            _mod_spec(first_chunk, rows_per_cond, TM_FFN),
            _mod_spec(first_chunk + 1, rows_per_cond, TM_FFN),
            pl.BlockSpec((None, None, 1, TN_FFN_OUT),
                         lambda i, j: ((i * TM_FFN) // rows_per_cond, first_chunk + 2, 0, out(j))),
            pl.BlockSpec((1, D_MODEL), lambda i, j: (0, 0)),
            pl.BlockSpec((None, None, D_MODEL, TF), lambda i, j: (layer, which, 0, hid(j))),
            pl.BlockSpec((None, None, D_MODEL, TF), lambda i, j: (layer, which, 0, hid(j) + nf)),
            pl.BlockSpec((None, None, D_FF, TN_FFN_OUT), lambda i, j: (layer, which, 0, out(j))),
        ],
        out_specs=pl.BlockSpec((TM_FFN, TN_FFN_OUT), lambda i, j: (i, out(j))),
        out_shape=jax.ShapeDtypeStruct((n, D_MODEL), F32),
        scratch_shapes=[pltpu.VMEM((TM_FFN, D_MODEL), BF16), pltpu.VMEM((TM_FFN, D_FF), BF16)],
        compiler_params=_params("parallel", "arbitrary"),
        name="ffn",
    )(x, mod, mod, mod, norm_g.reshape(1, D_MODEL), w_in, w_in, w_out)


def _inproj_kernel(x_ref, sh_ref, sc_ref, ng_ref, w_ref, o_ref, h_scr):
    @pl.when(pl.program_id(1) == 0)
    def _():
        h = _ada_norm(x_ref[...], ng_ref[...], sc_ref[...], sh_ref[...])
        h_scr[...] = h.astype(BF16)

    o_ref[...] = _dot(h_scr[...], w_ref[...])


def _inproj(x, mod, norm_g, w_in, layer, rows_per_cond):
    n = x.shape[0]
    return pl.pallas_call(
        _inproj_kernel,
        grid=(n // TM, IN_COLS // TN_IN),
        in_specs=[
            pl.BlockSpec((TM, D_MODEL), lambda i, j: (i, 0)),
            _mod_spec(3, rows_per_cond),
            _mod_spec(4, rows_per_cond),
            pl.BlockSpec((1, D_MODEL), lambda i, j: (0, 0)),
            pl.BlockSpec((None, D_MODEL, TN_IN), lambda i, j: (layer, 0, j)),
        ],
        out_specs=pl.BlockSpec((TM, TN_IN), lambda i, j: (i, j)),
        out_shape=jax.ShapeDtypeStruct((n, IN_COLS), F32),
        scratch_shapes=[pltpu.VMEM((TM, D_MODEL), BF16)],
        compiler_params=_params("parallel", "arbitrary"),
        name="inproj",
    )(x, mod, mod, norm_g.reshape(1, D_MODEL), w_in)


QK_SCALE_LOG2 = ATTN_SCALE * LOG2_E


def _with_ones(v):
    return jnp.concatenate([v, jnp.ones_like(v)], axis=-1)


def _softmax_sink(scores, values, sink):
    sink2 = sink * LOG2_E
    m = sink2
    for s in scores:
        m = jnp.maximum(jnp.max(s, axis=-1, keepdims=True), m)
    acc = None
    for s, v in zip(scores, values):
        pv = _dot(jnp.exp2(s - m).astype(BF16), v)
        acc = pv if acc is None else acc + pv
    return acc[:, :HEAD_DIM] / (acc[:, HEAD_DIM:] + jnp.exp2(sink2 - m))


def _ctx_attn_kernel(sink_ref, q_ref, k_ref, v_ref, qg_ref, kg_ref, o_ref, kc_ref, vc_ref):
    kv = pl.program_id(1)
    kn = _rms(k_ref[...], kg_ref[...])
    v = v_ref[...]
    kc_ref[...] = kn
    vc_ref[...] = v
    kb = kn.astype(BF16)
    vb = _with_ones(v.astype(BF16))
    heads = [slice(g * HEAD_DIM, (g + 1) * HEAD_DIM) for g in range(Q_PER_KV)]
    logits = [_dot_nt((_rms(q_ref[:, cols], qg_ref[...]) * QK_SCALE_LOG2).astype(BF16), kb) for cols in heads]
    for g, cols in enumerate(heads):
        o_ref[:, cols] = _softmax_sink([logits[g]], [vb], sink_ref[kv, g]).astype(BF16)


def _ctx_attention(z, sink, q_g, k_g, batch, seq):
    qw = Q_PER_KV * HEAD_DIM
    cache_shape = jax.ShapeDtypeStruct((batch, N_KV_HEADS, seq, HEAD_DIM), F32)
    cache_spec = pl.BlockSpec((None, None, seq, HEAD_DIM), lambda b, kv: (b, kv, 0, 0))
    return pl.pallas_call(
        _ctx_attn_kernel,
        grid=(batch, N_KV_HEADS),
        in_specs=[
            pl.BlockSpec(memory_space=pltpu.SMEM),
            pl.BlockSpec((seq, qw), lambda b, kv: (b, kv)),
            pl.BlockSpec((seq, HEAD_DIM), lambda b, kv: (b, COL_K + kv)),
            pl.BlockSpec((seq, HEAD_DIM), lambda b, kv: (b, COL_V + kv)),
            pl.BlockSpec((1, HEAD_DIM), lambda b, kv: (0, 0)),
            pl.BlockSpec((1, HEAD_DIM), lambda b, kv: (0, 0)),
        ],
        out_specs=[pl.BlockSpec((seq, qw), lambda b, kv: (b, kv)), cache_spec, cache_spec],
        out_shape=[jax.ShapeDtypeStruct((batch * seq, D_ATTN), BF16), cache_shape, cache_shape],
        compiler_params=_params("parallel", "parallel"),
        name="ctx_attention",
    )(sink, z, z, z, q_g.reshape(1, HEAD_DIM), k_g.reshape(1, HEAD_DIM))


def _rope(x, cos, sin_a, sin_b):
    quarter = HEAD_DIM // 4
    up = pltpu.roll(x, HEAD_DIM - quarter, 1)
    down = pltpu.roll(x, quarter, 1)
    return x * cos + up * sin_a + down * sin_b


def _lat_attn_kernel(sink_ref, q_ref, k_ref, v_ref, kc_ref, vc_ref, qg_ref, kg_ref,
                     cos_ref, sa_ref, sb_ref, cosq_ref, saq_ref, sbq_ref, o_ref, k_scr, v_scr, kc_scr, vc_scr):
    kv = pl.program_id(1)
    qt = pl.program_id(2)
    seq = k_ref.shape[0]
    span = 3 * ATTN_BLOCK

    @pl.when(qt == 0)
    def _():
        kn = _rms(k_ref[...], kg_ref[...])
        k_scr[...] = _rope(kn, cos_ref[...], sa_ref[...], sb_ref[...]).astype(BF16)
        v_scr[...] = _with_ones(v_ref[...].astype(BF16))
        kc_scr[...] = kc_ref[...].astype(BF16)
        vc_scr[...] = _with_ones(vc_ref[...].astype(BF16))

    kcb = kc_scr[...]
    vcb = vc_scr[...]
    nblk = LAT_Q_TILE // ATTN_BLOCK
    heads = [slice(g * HEAD_DIM, (g + 1) * HEAD_DIM) for g in range(Q_PER_KV)]
    stacked = Q_PER_KV * ATTN_BLOCK

    head_of_row = lax.broadcasted_iota(jnp.int32, (stacked, 1), 0) // ATTN_BLOCK
    sink_col = jnp.zeros((stacked, 1), F32)
    for g in range(Q_PER_KV):
        sink_col = jnp.where(head_of_row == g, sink_ref[kv, g], sink_col)

    def block_rows(blk):
        return slice(blk * ATTN_BLOCK, (blk + 1) * ATTN_BLOCK)

    def logits_of(blk):
        rows = block_rows(blk)
        q0 = (qt * nblk + blk) * ATTN_BLOCK
        start = pl.multiple_of(jnp.clip(q0 - ATTN_BLOCK, 0, seq - span), ATTN_BLOCK)
        qpos = q0 + lax.broadcasted_iota(jnp.int32, (stacked, span), 0) % ATTN_BLOCK
        kpos = start + lax.broadcasted_iota(jnp.int32, (stacked, span), 1)
        valid = jnp.abs(qpos - kpos) <= WINDOW
        cos, sa, sb = cosq_ref[rows, :], saq_ref[rows, :], sbq_ref[rows, :]
        qn = jnp.concatenate([_rope(_rms(q_ref[rows, cols], qg_ref[...]), cos, sa, sb) for cols in heads], axis=0)
        qn = (qn * QK_SCALE_LOG2).astype(BF16)
        s_loc = jnp.where(valid, _dot_nt(qn, k_scr[pl.ds(start, span), :]), MASK_VALUE)
        return s_loc, _dot_nt(qn, kcb), start

    pending = logits_of(0)
    for blk in range(nblk):
        upcoming = logits_of(blk + 1) if blk + 1 < nblk else None
        s_loc, s_ctx, start = pending
        o = _softmax_sink([s_loc, s_ctx], [v_scr[pl.ds(start, span), :], vcb], sink_col)
        for g, cols in enumerate(heads):
            o_ref[block_rows(blk), cols] = o[g * ATTN_BLOCK:(g + 1) * ATTN_BLOCK, :].astype(BF16)
        pending = upcoming


def _rope_tables(seq):
    quarter = HEAD_DIM // 4
    pos = np.arange(seq)
    inv_freq = ROPE_BASE ** (-np.arange(quarter, dtype=np.float32) / quarter)
    inv_freq = jnp.asarray(inv_freq, F32)
    zero = jnp.zeros((seq, quarter), F32)

    def trig(p):
        ang = jnp.asarray(p, F32)[:, None] * inv_freq
        return jnp.cos(ang), jnp.sin(ang)

    cr, sr = trig(pos // GRID_W)
    cc, sc = trig(pos % GRID_W)
    cos = jnp.concatenate([cr, cr, cc, cc], axis=-1)
    sin_a = jnp.concatenate([-sr, zero, -sc, zero], axis=-1)
    sin_b = jnp.concatenate([zero, sr, zero, sc], axis=-1)
    return cos, sin_a, sin_b


def _lat_attention(z, sink, q_g, k_g, k_ctx, v_ctx, tables, batch, seq):
    qw = Q_PER_KV * HEAD_DIM
    nqt = seq // LAT_Q_TILE
    past = k_ctx.shape[2]
    cos, sin_a, sin_b = tables
    full_tab = pl.BlockSpec((seq, HEAD_DIM), lambda b, kv, qt: (0, 0))
    tile_tab = pl.BlockSpec((LAT_Q_TILE, HEAD_DIM), lambda b, kv, qt: (qt, 0))
    ctx_spec = pl.BlockSpec((None, None, past, HEAD_DIM), lambda b, kv, qt: (b, kv, 0, 0))
    gain = pl.BlockSpec((1, HEAD_DIM), lambda b, kv, qt: (0, 0))
    return pl.pallas_call(
        _lat_attn_kernel,
        grid=(batch, N_KV_HEADS, nqt),
        in_specs=[
            pl.BlockSpec(memory_space=pltpu.SMEM),
            pl.BlockSpec((LAT_Q_TILE, qw), lambda b, kv, qt: (b * nqt + qt, kv)),
            pl.BlockSpec((seq, HEAD_DIM), lambda b, kv, qt: (b, COL_K + kv)),
            pl.BlockSpec((seq, HEAD_DIM), lambda b, kv, qt: (b, COL_V + kv)),
            ctx_spec, ctx_spec, gain, gain,
            full_tab, full_tab, full_tab, tile_tab, tile_tab, tile_tab,
        ],
        out_specs=pl.BlockSpec((LAT_Q_TILE, qw), lambda b, kv, qt: (b * nqt + qt, kv)),
        out_shape=jax.ShapeDtypeStruct((batch * seq, D_ATTN), BF16),
        scratch_shapes=[pltpu.VMEM((seq, HEAD_DIM), BF16), pltpu.VMEM((seq, 2 * HEAD_DIM), BF16),
                        pltpu.VMEM((past, HEAD_DIM), BF16), pltpu.VMEM((past, 2 * HEAD_DIM), BF16)],
        compiler_params=_params("parallel", "parallel", "arbitrary"),
        name="lat_attention",
    )(sink, z, z, z, k_ctx, v_ctx, q_g.reshape(1, HEAD_DIM), k_g.reshape(1, HEAD_DIM),
      cos, sin_a, sin_b, cos, sin_a, sin_b)


def _split3(x):
    hi = x.astype(BF16)
    r1 = x - hi.astype(F32)
    mid = r1.astype(BF16)
    lo = (r1 - mid.astype(F32)).astype(BF16)
    return hi, mid, lo


def _rec_gates(forward, zf, lb):
    c = REC_CHUNK
    t = jnp.exp(-jnp.abs(zf))
    r = 1.0 / (1.0 + t)
    tr = t * r
    nonneg = zf >= 0
    sig_pos = jnp.where(nonneg, r, tr)
    sig_neg = jnp.where(nonneg, tr, r)
    one_m_lb = 1.0 - lb
    log_f = jnp.log(jnp.maximum(lb + one_m_lb * sig_pos, GATE_FLOOR))
    k = one_m_lb * sig_neg
    ri = lax.broadcasted_iota(jnp.int32, (c, c), 0)
    ci = lax.broadcasted_iota(jnp.int32, (c, c), 1)
    tri = jnp.where((ci <= ri) if forward else (ci >= ri), 1.0, 0.0).astype(BF16)
    hi, mid, lo = _split3(log_f)
    cum2 = (_dot(tri, hi) + _dot(tri, mid) + _dot(tri, lo)) * LOG2_E
    return cum2 - jnp.log2(k), cum2


def _rec_same_block(forward, q, key_row, cum2):
    sub_row = lax.broadcasted_iota(jnp.int32, (SUBLANES, REC_DK), 0)
    pieces = []
    for b in range(REC_CHUNK // SUBLANES):
        blk = slice(b * SUBLANES, (b + 1) * SUBLANES)
        cum_b, q_b = cum2[blk, :], q[blk, :]
        for sl in range(SUBLANES):
            s = b * SUBLANES + sl
            keep = (sub_row >= sl) if forward else (sub_row <= sl)
            pieces.append(jnp.where(keep, jnp.exp2(cum_b - key_row(s)) * q_b, 0.0))
    return _dot(jnp.concatenate(pieces, axis=0).astype(BF16), jnp.ones((REC_DK, REC_DV), BF16))


def _rec_cross_block(forward, q, key2, v, cum2):
    c = REC_CHUNK
    q_parts, k_parts, v_parts, segments = [], [], [], []
    for tb in range(c // SUBLANES):
        blk = slice(tb * SUBLANES, (tb + 1) * SUBLANES)
        src, ref_row = (slice(0, blk.start), blk.start - 1) if forward else (slice(blk.stop, c), blk.stop)
        if src.stop == src.start:
            q_parts.append(jnp.zeros((SUBLANES, REC_DK), F32))
            continue
        ref = cum2[ref_row:ref_row + 1, :]
        q_parts.append(q[blk, :] * jnp.exp2(cum2[blk, :] - ref))
        k_parts.append(jnp.exp2(ref - key2[src, :]))
        v_parts.append(v[src, :])
        segments.append((tb, src.stop - src.start))
    a = _dot_nt(jnp.concatenate(q_parts, axis=0).astype(BF16), jnp.concatenate(k_parts, axis=0).astype(BF16))
    return a, segments, jnp.concatenate(v_parts, axis=0).astype(BF16)


def _rec_cross_apply(a, segments, v_all):
    c, ncol = a.shape
    row_blk = lax.broadcasted_iota(jnp.int32, (c, ncol), 0) // SUBLANES
    col = lax.broadcasted_iota(jnp.int32, (c, ncol), 1)
    col_blk = jnp.full((c, ncol), -1, jnp.int32)
    start = 0
    for tb, width in segments:
        col_blk = jnp.where((col >= start) & (col < start + width), tb, col_blk)
        start += width
    return _dot(jnp.where(row_blk == col_blk, a, 0.0).astype(BF16), v_all)


def _rec_same_apply(lane_sums, v_row):
    o_blocks = []
    for b in range(REC_CHUNK // SUBLANES):
        acc = None
        for sl in range(SUBLANES):
            s = b * SUBLANES + sl
            term = lane_sums[s * SUBLANES:(s + 1) * SUBLANES, :] * v_row(s)
            acc = term if acc is None else acc + term
        o_blocks.append(acc)
    return jnp.concatenate(o_blocks, axis=0)


def _rec_kernel(layer, has_init, *refs):
    if has_init:
        (q_ref, ff_ref, fb_ref, v_ref, g_ref, lg_ref, ng_ref, s0_ref, o_ref,
         o_scr, qd_scr, kd_scr, gl_scr, oi_scr, key_scr) = refs
        s_out_ref = None
    else:
        (q_ref, ff_ref, fb_ref, v_ref, g_ref, lg_ref, ng_ref, o_ref, s_out_ref,
         o_scr, qd_scr, kd_scr, gl_scr, oi_scr, key_scr) = refs
    seq = q_ref.shape[0]
    nchunk = seq // REC_CHUNK
    c = REC_CHUNK

    logits = lg_ref[...]
    e = jnp.exp(logits - jnp.max(logits, axis=0, keepdims=True))
    p = e / jnp.sum(e, axis=0, keepdims=True)
    lb = jnp.zeros_like(p[0])
    for i in range(1, layer + 1):
        lb = lb + p[i]

    def rows_of(ci):
        return pl.ds(pl.multiple_of(ci * c, c), c)

    z_refs = (ff_ref, fb_ref)

    def local_body(it, carry):
        chunks = [it * REC_LOCAL_UNROLL + u for u in range(REC_LOCAL_UNROLL)]
        qs = [q_ref[rows_of(ci), :] for ci in chunks]
        vs = [v_ref[rows_of(ci), :] for ci in chunks]
        items = [(u, d) for u in range(REC_LOCAL_UNROLL) for d in range(2)]
        gates = [_rec_gates(d == 0, z_refs[d][rows_of(chunks[u]), :], lb[d]) for u, d in items]
        for idx, (key2, _) in enumerate(gates):
            key_scr[idx] = key2
        key_rows = [lambda s, idx=idx: key_scr[idx, s:s + 1, :] for idx in range(len(items))]
        v_rows = [lambda s, ci=ci: v_ref[pl.ds(ci * c + s, 1), :] for ci in chunks]
        same = [_rec_same_block(d == 0, qs[u], key_rows[idx], cum2)
                for idx, ((u, d), (_, cum2)) in enumerate(zip(items, gates))]
        cross = [_rec_cross_block(d == 0, qs[u], key2, vs[u], cum2) for (u, d), (key2, cum2) in zip(items, gates)]
        totals = [None] * REC_LOCAL_UNROLL
        for (u, d), (key2, cum2), lane_sums, (a, segments, v_all) in zip(items, gates, same, cross):
            rows = rows_of(chunks[u])
            last = cum2[c - 1:c, :] if d == 0 else cum2[0:1, :]
            qd_scr[d, rows, :] = (qs[u] * jnp.exp2(cum2)).astype(BF16)
            kd_scr[d, rows, :] = jnp.exp2(last - key2).astype(BF16)
            gl_scr[d, chunks[u]] = jnp.broadcast_to(jnp.exp2(last), (SUBLANES, REC_DK))
            o_local = _rec_same_apply(lane_sums, v_rows[u]) + _rec_cross_apply(a, segments, v_all)
            totals[u] = o_local if totals[u] is None else totals[u] + o_local
        for u, ci in enumerate(chunks):
            o_scr[rows_of(ci), :] = totals[u]
        return carry

    lax.fori_loop(0, nchunk // REC_LOCAL_UNROLL, local_body, 0)

    if has_init:
        init = (s0_ref[0].T, s0_ref[1].T)
    else:
        init = (jnp.zeros((REC_DV, REC_DK), F32),) * 2

    def state_body(it, states):
        states = list(states)
        steps = [it * REC_STATE_UNROLL + u for u in range(REC_STATE_UNROLL)]
        order = [(d, step if d == 0 else nchunk - 1 - step) for step in steps for d in range(2)]
        updates = [_dot_tn(v_ref[rows_of(ci), :].astype(BF16), kd_scr[d, rows_of(ci), :]) for d, ci in order]
        for (d, ci), update in zip(order, updates):
            st = states[d]
            oi_scr[d, rows_of(ci), :] = _dot_nt(qd_scr[d, rows_of(ci), :], st.astype(BF16))
            decayed = (st.reshape(REC_DV // SUBLANES, SUBLANES, REC_DK) * gl_scr[d, ci]).reshape(REC_DV, REC_DK)
            states[d] = decayed + update
        return tuple(states)

    final = lax.fori_loop(0, nchunk // REC_STATE_UNROLL, state_body, init)
    if not has_init:
        s_out_ref[0] = final[0].T
        s_out_ref[1] = final[1].T

    def out_body(ci, carry):
        rows = rows_of(ci)
        gate = g_ref[rows, :]
        o = o_scr[rows, :] + oi_scr[0, rows, :] + oi_scr[1, rows, :]
        o_ref[rows, :] = (_rms(o, ng_ref[...]) * (gate * _sigmoid(gate))).astype(BF16)
        return carry

    lax.fori_loop(0, nchunk, out_body, 0, unroll=4)


def _recurrence(z, layer, lb_logits, norm_g, s0, batch, seq):
    has_init = s0 is not None
    col = lambda base: pl.BlockSpec((seq, REC_DK), lambda b, h: (b, base + h))
    state_spec = pl.BlockSpec((None, 2, None, REC_DK, REC_DV), lambda b, h: (b, 0, h, 0, 0))
    in_specs = [col(COL_RQ), col(COL_RF_F), col(COL_RF_B), col(COL_RI), col(COL_RG),
                pl.BlockSpec((DEPTH, 2, None, 1, REC_DK), lambda b, h: (0, 0, h, 0, 0)),
                pl.BlockSpec((None, 1, REC_DV), lambda b, h: (h, 0, 0))]
    args = [z, z, z, z, z, lb_logits, norm_g]
    out_specs = [pl.BlockSpec((seq, REC_DV), lambda b, h: (b, h))]
    out_shape = [jax.ShapeDtypeStruct((batch * seq, D_REC), BF16)]
    if has_init:
        in_specs.append(state_spec)
        args.append(s0)
    else:
        out_specs.append(state_spec)
        out_shape.append(jax.ShapeDtypeStruct((batch, 2, N_REC_HEADS, REC_DK, REC_DV), F32))
    return pl.pallas_call(
        functools.partial(_rec_kernel, layer, has_init),
        grid=(batch, N_REC_HEADS),
        in_specs=in_specs,
        out_specs=out_specs,
        out_shape=out_shape,
        scratch_shapes=[pltpu.VMEM((seq, REC_DV), F32),
                        pltpu.VMEM((2, seq, REC_DK), BF16),
                        pltpu.VMEM((2, seq, REC_DK), BF16),
                        pltpu.VMEM((2, seq // REC_CHUNK, SUBLANES, REC_DK), F32),
                        pltpu.VMEM((2, seq, REC_DV), F32),
                        pltpu.VMEM((2 * REC_LOCAL_UNROLL, REC_CHUNK, REC_DK), F32)],
        compiler_params=_params("parallel", "parallel"),
        name="hgrn2",
    )(*args)


def _conv_kernel(tiles_per_seq, a_ref, b_ref, at_ref, bt_ref, ab_ref, bb_ref, w_ref, bias_ref, lg_ref, lb_ref,
                 o_ref, h_scr, w_scr):
    i = pl.program_id(0)
    tile_in_seq = i % tiles_per_seq
    has_top = jnp.where(tile_in_seq > 0, 1.0, 0.0)
    has_bot = jnp.where(tile_in_seq < tiles_per_seq - 1, 1.0, 0.0)
    rows = a_ref.shape[0]
    half = CONV_WIDTH // 2
    shifted_rows = rows + 2 * CONV_HALO - SUBLANES

    def glu(a, b):
        return a * _sigmoid(b)

    h_scr[0, 0:CONV_HALO, :] = glu(at_ref[...], bt_ref[...]) * has_top
    h_scr[0, CONV_HALO:CONV_HALO + rows, :] = glu(a_ref[...], b_ref[...])
    h_scr[0, CONV_HALO + rows:, :] = glu(ab_ref[...], bb_ref[...]) * has_bot
    for r in range(1, SUBLANES):
        h_scr[r, 0:shifted_rows, :] = h_scr[0, r:r + shifted_rows, :]
    for j in range(CONV_WIDTH):
        w_scr[j] = jnp.broadcast_to(w_ref[j:j + 1, :], (SUBLANES, CONV_CH))

    for r0 in range(0, rows, CONV_SUB):
        acc = None
        for j in range(CONV_WIDTH):
            lo = r0 + CONV_HALO - half + j
            shift = lo % SUBLANES
            weight = w_scr[j]
            window = h_scr[shift, lo - shift:lo - shift + CONV_SUB, :]
            term = (window.reshape(CONV_SUB // SUBLANES, SUBLANES, CONV_CH) * weight).reshape(CONV_SUB, CONV_CH)
            acc = term if acc is None else acc + term
        y = acc + bias_ref[...]
        mu = jnp.mean(y, axis=-1, keepdims=True)
        yc = y - mu
        var = jnp.mean(yc * yc, axis=-1, keepdims=True)
        yn = yc * lax.rsqrt(var + EPS) * lg_ref[...] + lb_ref[...]
        o_ref[r0:r0 + CONV_SUB, :] = (yn * _sigmoid(yn)).astype(BF16)


def _conv(z, w, bias, ln_g, ln_b, seq):
    n = z.shape[0]
    tiles_per_seq = seq // CONV_ROWS
    ntiles = n // CONV_ROWS
    cw = CONV_CH // HEAD_DIM
    halo_per_tile = CONV_ROWS // CONV_HALO
    nhalo = n // CONV_HALO
    ca, cb = COL_CA // cw, COL_CB // cw
    mid = lambda c: pl.BlockSpec((CONV_ROWS, CONV_CH), lambda i: (i, c))
    top = lambda c: pl.BlockSpec((CONV_HALO, CONV_CH), lambda i: (jnp.maximum(i * halo_per_tile - 1, 0), c))
    bot = lambda c: pl.BlockSpec((CONV_HALO, CONV_CH), lambda i: (jnp.minimum((i + 1) * halo_per_tile, nhalo - 1), c))
    vec = pl.BlockSpec((1, CONV_CH), lambda i: (0, 0))
    return pl.pallas_call(
        functools.partial(_conv_kernel, tiles_per_seq),
        grid=(ntiles,),
        in_specs=[mid(ca), mid(cb), top(ca), top(cb), bot(ca), bot(cb),
                  pl.BlockSpec((CONV_WIDTH, CONV_CH), lambda i: (0, 0)), vec, vec, vec],
        out_specs=pl.BlockSpec((CONV_ROWS, CONV_CH), lambda i: (i, 0)),
        out_shape=jax.ShapeDtypeStruct((n, CONV_CH), BF16),
        scratch_shapes=[pltpu.VMEM((SUBLANES, CONV_ROWS + 2 * CONV_HALO, CONV_CH), F32),
                        pltpu.VMEM((CONV_WIDTH, SUBLANES, CONV_CH), F32)],
        compiler_params=_params("parallel"),
        name="conv",
    )(z, z, z, z, z, z, w, bias.reshape(1, CONV_CH), ln_g.reshape(1, CONV_CH), ln_b.reshape(1, CONV_CH))


def _outproj_kernel(x_ref, g_ref, a_ref, r_ref, c_ref, wa_ref, wr_ref, wc_ref, o_ref):
    mix = _dot(a_ref[...], wa_ref[...]) + _dot(r_ref[...], wr_ref[...]) + _dot(c_ref[...], wc_ref[...])
    o_ref[...] = x_ref[...] + g_ref[...] * mix


def _outproj(x, mod, attn, rec, conv, w_out, layer, rows_per_cond):
    n = x.shape[0]
    tn = D_MODEL // 2
    row = lambda width: pl.BlockSpec((TM, width), lambda i, j: (i, 0))
    return pl.pallas_call(
        _outproj_kernel,
        grid=(n // TM, D_MODEL // tn),
        in_specs=[
            pl.BlockSpec((TM, tn), lambda i, j: (i, j)),
            pl.BlockSpec((None, None, 1, tn), lambda i, j: ((i * TM) // rows_per_cond, 5, 0, j)),
            row(D_ATTN), row(D_REC), row(CONV_CH),
            pl.BlockSpec((None, D_ATTN, tn), lambda i, j: (layer, 0, j)),
            pl.BlockSpec((None, D_REC, tn), lambda i, j: (layer, D_ATTN // D_REC, j)),
            pl.BlockSpec((None, CONV_CH, tn), lambda i, j: (layer, (D_ATTN + D_REC) // CONV_CH, j)),
        ],
        out_specs=pl.BlockSpec((TM, tn), lambda i, j: (i, j)),
        out_shape=jax.ShapeDtypeStruct((n, D_MODEL), F32),
        compiler_params=_params("parallel", "parallel"),
        name="outproj",
    )(x, mod, attn, rec, conv, w_out, w_out, w_out)


def _trunk_layer(x, mod, l, P, mixers, rows_per_cond):
    x = _ffn(x, mod, 0, P['norm_g'][l, 0], P['w_ffn_in'], P['w_ffn_out'], l, 0, rows_per_cond)
    z = _inproj(x, mod, P['norm_g'][l, 1], P['w_in'], l, rows_per_cond)
    attn, rec, conv, extras = mixers(z)
    x = _outproj(x, mod, attn, rec, conv, P['w_out'], l, rows_per_cond)
    x = _ffn(x, mod, 6, P['norm_g'][l, 2], P['w_ffn_in'], P['w_ffn_out'], l, 1, rows_per_cond)
    return x, extras


def kernel(x_prompt, x_sample, cache_k, cache_v, state_rec, c, c_ctx, w_ada, b_ada, norm_g, w_ffn_in, w_ffn_out,
           w_in, w_out, q_norm_g, k_norm_g, attn_sink, rec_lb_logits, rec_norm_g, conv_w, conv_b, conv_ln_g, conv_ln_b):
    batch, seq, _ = x_prompt.shape
    dec_batch, dec_seq, _ = x_sample.shape
    assert seq % TM == 0 or TM % seq == 0
    assert dec_seq % TM == 0 and dec_seq % LAT_Q_TILE == 0 and seq % CONV_ROWS == 0 and dec_seq % CONV_ROWS == 0

    P = {'norm_g': norm_g, 'w_ffn_in': w_ffn_in.astype(BF16), 'w_ffn_out': w_ffn_out.astype(BF16),
         'w_in': w_in.astype(BF16), 'w_out': w_out.astype(BF16)}

    lb_logits = rec_lb_logits.reshape(DEPTH, 2, N_REC_HEADS, 1, REC_DK)

    cond8 = jnp.zeros((SUBLANES, D_MODEL), F32).at[0].set(c_ctx).at[1:1 + dec_batch].set(c)
    mod = _modulation(cond8, w_ada, b_ada).reshape(DEPTH, SUBLANES, N_MOD, 1, D_MODEL)
    tables = _rope_tables(dec_seq)
    sinks = attn_sink.reshape(DEPTH, N_KV_HEADS, Q_PER_KV)
    rec_g = rec_norm_g.reshape(DEPTH, N_REC_HEADS, 1, REC_DV)

    def shared_mixers(z, l, seq_len):
        return _conv(z, conv_w[l], conv_b[l], conv_ln_g[l], conv_ln_b[l], seq_len)

    h = x_prompt.reshape(batch * seq, D_MODEL)
    ks, vs, ss = [], [], []
    for l in range(DEPTH):
        def ctx_mixers(z, l=l):
            attn, k_l, v_l = _ctx_attention(z, sinks[l], q_norm_g[l], k_norm_g[l], batch, seq)
            rec, s_l = _recurrence(z, l, lb_logits, rec_g[l], None, batch, seq)
            return attn, rec, shared_mixers(z, l, seq), (k_l, v_l, s_l)

        h, (k_l, v_l, s_l) = _trunk_layer(h, mod[l, 0:1], l, P, ctx_mixers, batch * seq)
        ks.append(k_l)
        vs.append(v_l)
        ss.append(s_l)
    y_prompt = h.reshape(batch, seq, D_MODEL)

    h = x_sample.reshape(dec_batch * dec_seq, D_MODEL)
    for l in range(DEPTH):
        def lat_mixers(z, l=l):
            attn = _lat_attention(z, sinks[l], q_norm_g[l], k_norm_g[l], cache_k[:, l], cache_v[:, l], tables,
                                  dec_batch, dec_seq)
            rec = _recurrence(z, l, lb_logits, rec_g[l], state_rec[:, l], dec_batch, dec_seq)[0]
            return attn, rec, shared_mixers(z, l, dec_seq), None

        h, _ = _trunk_layer(h, mod[l, 1:1 + dec_batch], l, P, lat_mixers, dec_seq)
    y_sample = h.reshape(dec_batch, dec_seq, D_MODEL)

    return (y_prompt, y_sample, jnp.stack(ks, axis=1), jnp.stack(vs, axis=1), jnp.stack(ss, axis=1))
```

```python
import functools

import jax
import jax.numpy as jnp
import numpy as np
from jax import lax
from jax.experimental import pallas as pl
from jax.experimental.pallas import tpu as pltpu

F32 = jnp.float32
BF16 = jnp.bfloat16

D_MODEL = 2048
DEPTH = 2
GRID_W = 64
HEAD_DIM = 128
D_ATTN = D_MODEL // 2
N_Q_HEADS = D_ATTN // HEAD_DIM
N_KV_HEADS = 2
Q_PER_KV = N_Q_HEADS // N_KV_HEADS
WINDOW = 128
ATTN_BLOCK = 128
ATTN_SCALE = HEAD_DIM ** -0.5
ROPE_BASE = 10000.0
MASK_VALUE = -1e30
D_REC = D_MODEL // 4
REC_DK = 128
REC_DV = 128
N_REC_HEADS = D_REC // REC_DV
REC_CHUNK = 32
CONV_CH = D_MODEL // 4
CONV_WIDTH = 31
D_FF = 5632
N_MOD = 9
EPS = 1e-6
GATE_FLOOR = 1e-30
LOG2_E = 1.4426950408889634
IN_COLS = 5120

COL_Q = 0
COL_K = 8
COL_V = 10
COL_RQ = 12
COL_RF_F = 16
COL_RF_B = 20
COL_RI = 24
COL_RG = 28
COL_CA = 32
COL_CB = 36

SUBLANES = 8
VMEM_LIMIT = 56 * 1024 * 1024

TM = 1024
TM_FFN = 1024
TF = 512
TN_FFN_OUT = 256
TN_IN = 1024
TN_ADA = 1024
CONV_ROWS = 256
CONV_HALO = 16
CONV_SUB = 32
LAT_Q_TILE = 1024
CTX_SEQS_PER_STEP = 2
REC_LOCAL_UNROLL = 8
REC_STATE_UNROLL = 16


def _params(*sem):
    return pltpu.CompilerParams(dimension_semantics=sem, vmem_limit_bytes=VMEM_LIMIT)


def _dot(a, b):
    return jnp.dot(a, b, preferred_element_type=F32)


def _dot_nt(a, b):
    return lax.dot_general(a, b, (((1,), (1,)), ((), ())), preferred_element_type=F32)


def _dot_tn(a, b):
    return lax.dot_general(a, b, (((0,), (0,)), ((), ())), preferred_element_type=F32)


def _rms(x, g):
    return x * lax.rsqrt(jnp.mean(x * x, axis=-1, keepdims=True) + EPS) * g


def _ada_norm(x, g, scale, shift):
    gain = g * (1.0 + scale)
    return x * lax.rsqrt(jnp.mean(x * x, axis=-1, keepdims=True) + EPS) * gain + shift


def _sigmoid(x):
    return 1.0 / (1.0 + jnp.exp(-x))


def _ada_kernel(c_ref, w_ref, b_ref, o_ref):
    c = c_ref[...]
    s = (c * _sigmoid(c)).astype(BF16)
    o_ref[...] = _dot(s, w_ref[...].astype(BF16)) + b_ref[...]


def _modulation(cond8, w_ada, b_ada):
    ncol = N_MOD * D_MODEL
    return pl.pallas_call(
        _ada_kernel,
        grid=(DEPTH, ncol // TN_ADA),
        in_specs=[
            pl.BlockSpec((SUBLANES, D_MODEL), lambda l, j: (0, 0)),
            pl.BlockSpec((None, D_MODEL, TN_ADA), lambda l, j: (l, 0, j)),
            pl.BlockSpec((None, 1, TN_ADA), lambda l, j: (l, 0, j)),
        ],
        out_specs=pl.BlockSpec((None, SUBLANES, TN_ADA), lambda l, j: (l, 0, j)),
        out_shape=jax.ShapeDtypeStruct((DEPTH, SUBLANES, ncol), F32),
        compiler_params=_params("parallel", "parallel"),
        name="modulation",
    )(cond8, w_ada, b_ada.reshape(DEPTH, 1, ncol))


def _mod_spec(chunk, rows_per_cond, tm=TM):
    return pl.BlockSpec((None, None, 1, D_MODEL), lambda i, j: ((i * tm) // rows_per_cond, chunk, 0, 0))


def _ffn_kernel(nf, x_ref, sh_ref, sc_ref, g_ref, ng_ref, wa_ref, wb_ref, wo_ref, o_ref, h_scr, act_scr):
    j = pl.program_id(1)

    @pl.when(j == 0)
    def _():
        h = _ada_norm(x_ref[...], ng_ref[...], sc_ref[...], sh_ref[...])
        h_scr[...] = h.astype(BF16)

    @pl.when(j < nf)
    def _():
        h = h_scr[...]
        a = _dot(h, wa_ref[...])
        b = _dot(h, wb_ref[...])
        act_scr[:, pl.ds(pl.multiple_of(j * TF, TF), TF)] = (a * _sigmoid(a) * b).astype(BF16)

    @pl.when(j >= nf)
    def _():
        cols = pl.ds(pl.multiple_of((j - nf) * TN_FFN_OUT, TN_FFN_OUT), TN_FFN_OUT)
        o_ref[...] = x_ref[:, cols] + 0.5 * g_ref[...] * _dot(act_scr[...], wo_ref[...])


def _ffn(x, mod, first_chunk, norm_g, w_in, w_out, layer, which, rows_per_cond):
    n = x.shape[0]
    nf = D_FF // TF
    nout = D_MODEL // TN_FFN_OUT
    hid = lambda j: jnp.minimum(j, nf - 1)
    out = lambda j: jnp.maximum(j - nf, 0)
    return pl.pallas_call(
        functools.partial(_ffn_kernel, nf),
        grid=(n // TM_FFN, nf + nout),
        in_specs=[
            pl.BlockSpec((TM_FFN, D_MODEL), lambda i, j: (i, 0)),
            _mod_spec(first_chunk, rows_per_cond, TM_FFN),
            _mod_spec(first_chunk + 1, rows_per_cond, TM_FFN),
            pl.BlockSpec((None, None, 1, TN_FFN_OUT),
                         lambda i, j: ((i * TM_FFN) // rows_per_cond, first_chunk + 2, 0, out(j))),
            pl.BlockSpec((1, D_MODEL), lambda i, j: (0, 0)),
            pl.BlockSpec((None, None, D_MODEL, TF), lambda i, j: (layer, which, 0, hid(j))),
            pl.BlockSpec((None, None, D_MODEL, TF), lambda i, j: (layer, which, 0, hid(j) + nf)),
            pl.BlockSpec((None, None, D_FF, TN_FFN_OUT), lambda i, j: (layer, which, 0, out(j))),
        ],
        out_specs=pl.BlockSpec((TM_FFN, TN_FFN_OUT), lambda i, j: (i, out(j))),
        out_shape=jax.ShapeDtypeStruct((n, D_MODEL), F32),
        scratch_shapes=[pltpu.VMEM((TM_FFN, D_MODEL), BF16), pltpu.VMEM((TM_FFN, D_FF), BF16)],
        compiler_params=_params("parallel", "arbitrary"),
        name="ffn",
    )(x, mod, mod, mod, norm_g.reshape(1, D_MODEL), w_in, w_in, w_out)


def _inproj_kernel(x_ref, sh_ref, sc_ref, ng_ref, w_ref, o_ref, h_scr):
    @pl.when(pl.program_id(1) == 0)
    def _():
        h = _ada_norm(x_ref[...], ng_ref[...], sc_ref[...], sh_ref[...])
        h_scr[...] = h.astype(BF16)

    o_ref[...] = _dot(h_scr[...], w_ref[...])


def _inproj(x, mod, norm_g, w_in, layer, rows_per_cond):
    n = x.shape[0]
    return pl.pallas_call(
        _inproj_kernel,
        grid=(n // TM, IN_COLS // TN_IN),
        in_specs=[
            pl.BlockSpec((TM, D_MODEL), lambda i, j: (i, 0)),
            _mod_spec(3, rows_per_cond),
            _mod_spec(4, rows_per_cond),
            pl.BlockSpec((1, D_MODEL), lambda i, j: (0, 0)),
            pl.BlockSpec((None, D_MODEL, TN_IN), lambda i, j: (layer, 0, j)),
        ],
        out_specs=pl.BlockSpec((TM, TN_IN), lambda i, j: (i, j)),
        out_shape=jax.ShapeDtypeStruct((n, IN_COLS), F32),
        scratch_shapes=[pltpu.VMEM((TM, D_MODEL), BF16)],
        compiler_params=_params("parallel", "arbitrary"),
        name="inproj",
    )(x, mod, mod, norm_g.reshape(1, D_MODEL), w_in)


QK_SCALE_LOG2 = ATTN_SCALE * LOG2_E


def _with_ones(v):
    return jnp.concatenate([v, jnp.ones_like(v)], axis=-1)


def _softmax_sink(scores, values, sink):
    sink2 = sink * LOG2_E
    m = sink2
    for s in scores:
        m = jnp.maximum(jnp.max(s, axis=-1, keepdims=True), m)
    acc = None
    for s, v in zip(scores, values):
        pv = _dot(jnp.exp2(s - m).astype(BF16), v)
        acc = pv if acc is None else acc + pv
    return acc[:, :HEAD_DIM] / (acc[:, HEAD_DIM:] + jnp.exp2(sink2 - m))


def _ctx_attn_kernel(sink_ref, q_ref, k_ref, v_ref, qg_ref, kg_ref, o_ref, kc_ref, vc_ref):
    kv = pl.program_id(1)
    nseq, seq = kc_ref.shape[0], kc_ref.shape[1]
    heads = [slice(g * HEAD_DIM, (g + 1) * HEAD_DIM) for g in range(Q_PER_KV)]
    keys, values = [], []
    for i in range(nseq):
        rows = slice(i * seq, (i + 1) * seq)
        kn = _rms(k_ref[rows, :], kg_ref[...])
        v = v_ref[rows, :]
        kc_ref[i] = kn
        vc_ref[i] = v
        keys.append(kn.astype(BF16))
        values.append(_with_ones(v.astype(BF16)))
    logits = [[_dot_nt((_rms(q_ref[i * seq:(i + 1) * seq, cols], qg_ref[...]) * QK_SCALE_LOG2).astype(BF16), keys[i])
               for cols in heads] for i in range(nseq)]
    for i in range(nseq):
        for g, cols in enumerate(heads):
            o = _softmax_sink([logits[i][g]], [values[i]], sink_ref[kv, g])
            o_ref[i * seq:(i + 1) * seq, cols] = o.astype(BF16)


def _ctx_attention(z, sink, q_g, k_g, batch, seq):
    qw = Q_PER_KV * HEAD_DIM
    per = CTX_SEQS_PER_STEP
    cache_shape = jax.ShapeDtypeStruct((batch, N_KV_HEADS, seq, HEAD_DIM), F32)
    cache_spec = pl.BlockSpec((per, None, seq, HEAD_DIM), lambda b, kv: (b, kv, 0, 0))
    return pl.pallas_call(
        _ctx_attn_kernel,
        grid=(batch // per, N_KV_HEADS),
        in_specs=[
            pl.BlockSpec(memory_space=pltpu.SMEM),
            pl.BlockSpec((per * seq, qw), lambda b, kv: (b, kv)),
            pl.BlockSpec((per * seq, HEAD_DIM), lambda b, kv: (b, COL_K + kv)),
            pl.BlockSpec((per * seq, HEAD_DIM), lambda b, kv: (b, COL_V + kv)),
            pl.BlockSpec((1, HEAD_DIM), lambda b, kv: (0, 0)),
            pl.BlockSpec((1, HEAD_DIM), lambda b, kv: (0, 0)),
        ],
        out_specs=[pl.BlockSpec((per * seq, qw), lambda b, kv: (b, kv)), cache_spec, cache_spec],
        out_shape=[jax.ShapeDtypeStruct((batch * seq, D_ATTN), BF16), cache_shape, cache_shape],
        compiler_params=_params("parallel", "parallel"),
        name="ctx_attention",
    )(sink, z, z, z, q_g.reshape(1, HEAD_DIM), k_g.reshape(1, HEAD_DIM))


def _rope(x, cos, sin_a, sin_b):
    quarter = HEAD_DIM // 4
    up = pltpu.roll(x, HEAD_DIM - quarter, 1)
    down = pltpu.roll(x, quarter, 1)
    return x * cos + up * sin_a + down * sin_b


def _lat_attn_kernel(sink_ref, q_ref, k_ref, v_ref, kc_ref, vc_ref, qg_ref, kg_ref,
                     cos_ref, sa_ref, sb_ref, cosq_ref, saq_ref, sbq_ref, o_ref, k_scr, v_scr, kc_scr, vc_scr):
    kv = pl.program_id(1)
    qt = pl.program_id(2)
    seq = k_ref.shape[0]
    span = 3 * ATTN_BLOCK

    @pl.when(qt == 0)
    def _():
        kn = _rms(k_ref[...], kg_ref[...])
        k_scr[...] = _rope(kn, cos_ref[...], sa_ref[...], sb_ref[...]).astype(BF16)
        v_scr[...] = _with_ones(v_ref[...].astype(BF16))
        kc_scr[...] = kc_ref[...].astype(BF16)
        vc_scr[...] = _with_ones(vc_ref[...].astype(BF16))

    kcb = kc_scr[...]
    vcb = vc_scr[...]
    nblk = LAT_Q_TILE // ATTN_BLOCK
    heads = [slice(g * HEAD_DIM, (g + 1) * HEAD_DIM) for g in range(Q_PER_KV)]
    stacked = Q_PER_KV * ATTN_BLOCK

    head_of_row = lax.broadcasted_iota(jnp.int32, (stacked, 1), 0) // ATTN_BLOCK
    sink_col = jnp.zeros((stacked, 1), F32)
    for g in range(Q_PER_KV):
        sink_col = jnp.where(head_of_row == g, sink_ref[kv, g], sink_col)

    def block_rows(blk):
        return slice(blk * ATTN_BLOCK, (blk + 1) * ATTN_BLOCK)

    def logits_of(blk):
        rows = block_rows(blk)
        q0 = (qt * nblk + blk) * ATTN_BLOCK
        start = pl.multiple_of(jnp.clip(q0 - ATTN_BLOCK, 0, seq - span), ATTN_BLOCK)
        qpos = q0 + lax.broadcasted_iota(jnp.int32, (stacked, span), 0) % ATTN_BLOCK
        kpos = start + lax.broadcasted_iota(jnp.int32, (stacked, span), 1)
        valid = jnp.abs(qpos - kpos) <= WINDOW
        cos, sa, sb = cosq_ref[rows, :], saq_ref[rows, :], sbq_ref[rows, :]
        qn = jnp.concatenate([_rope(_rms(q_ref[rows, cols], qg_ref[...]), cos, sa, sb) for cols in heads], axis=0)
        qn = (qn * QK_SCALE_LOG2).astype(BF16)
        s_loc = jnp.where(valid, _dot_nt(qn, k_scr[pl.ds(start, span), :]), MASK_VALUE)
        return s_loc, _dot_nt(qn, kcb), start

    pending = logits_of(0)
    for blk in range(nblk):
        upcoming = logits_of(blk + 1) if blk + 1 < nblk else None
        s_loc, s_ctx, start = pending
        o = _softmax_sink([s_loc, s_ctx], [v_scr[pl.ds(start, span), :], vcb], sink_col)
        for g, cols in enumerate(heads):
            o_ref[block_rows(blk), cols] = o[g * ATTN_BLOCK:(g + 1) * ATTN_BLOCK, :].astype(BF16)
        pending = upcoming


def _rope_tables(seq):
    quarter = HEAD_DIM // 4
    pos = np.arange(seq)
    inv_freq = ROPE_BASE ** (-np.arange(quarter, dtype=np.float32) / quarter)
    inv_freq = jnp.asarray(inv_freq, F32)
    zero = jnp.zeros((seq, quarter), F32)

    def trig(p):
        ang = jnp.asarray(p, F32)[:, None] * inv_freq
        return jnp.cos(ang), jnp.sin(ang)

    cr, sr = trig(pos // GRID_W)
    cc, sc = trig(pos % GRID_W)
    cos = jnp.concatenate([cr, cr, cc, cc], axis=-1)
    sin_a = jnp.concatenate([-sr, zero, -sc, zero], axis=-1)
    sin_b = jnp.concatenate([zero, sr, zero, sc], axis=-1)
    return cos, sin_a, sin_b


def _lat_attention(z, sink, q_g, k_g, k_ctx, v_ctx, tables, batch, seq):
    qw = Q_PER_KV * HEAD_DIM
    nqt = seq // LAT_Q_TILE
    past = k_ctx.shape[2]
    cos, sin_a, sin_b = tables
    full_tab = pl.BlockSpec((seq, HEAD_DIM), lambda b, kv, qt: (0, 0))
    tile_tab = pl.BlockSpec((LAT_Q_TILE, HEAD_DIM), lambda b, kv, qt: (qt, 0))
    ctx_spec = pl.BlockSpec((None, None, past, HEAD_DIM), lambda b, kv, qt: (b, kv, 0, 0))
    gain = pl.BlockSpec((1, HEAD_DIM), lambda b, kv, qt: (0, 0))
    return pl.pallas_call(
        _lat_attn_kernel,
        grid=(batch, N_KV_HEADS, nqt),
        in_specs=[
            pl.BlockSpec(memory_space=pltpu.SMEM),
            pl.BlockSpec((LAT_Q_TILE, qw), lambda b, kv, qt: (b * nqt + qt, kv)),
            pl.BlockSpec((seq, HEAD_DIM), lambda b, kv, qt: (b, COL_K + kv)),
            pl.BlockSpec((seq, HEAD_DIM), lambda b, kv, qt: (b, COL_V + kv)),
            ctx_spec, ctx_spec, gain, gain,
            full_tab, full_tab, full_tab, tile_tab, tile_tab, tile_tab,
        ],
        out_specs=pl.BlockSpec((LAT_Q_TILE, qw), lambda b, kv, qt: (b * nqt + qt, kv)),
        out_shape=jax.ShapeDtypeStruct((batch * seq, D_ATTN), BF16),
        scratch_shapes=[pltpu.VMEM((seq, HEAD_DIM), BF16), pltpu.VMEM((seq, 2 * HEAD_DIM), BF16),
                        pltpu.VMEM((past, HEAD_DIM), BF16), pltpu.VMEM((past, 2 * HEAD_DIM), BF16)],
        compiler_params=_params("parallel", "parallel", "arbitrary"),
        name="lat_attention",
    )(sink, z, z, z, k_ctx, v_ctx, q_g.reshape(1, HEAD_DIM), k_g.reshape(1, HEAD_DIM),
      cos, sin_a, sin_b, cos, sin_a, sin_b)


def _split3(x):
    hi = x.astype(BF16)
    r1 = x - hi.astype(F32)
    mid = r1.astype(BF16)
    lo = (r1 - mid.astype(F32)).astype(BF16)
    return hi, mid, lo


def _rec_gates(forward, zf, lb):
    c = REC_CHUNK
    t = jnp.exp(-jnp.abs(zf))
    r = 1.0 / (1.0 + t)
    tr = t * r
    nonneg = zf >= 0
    sig_pos = jnp.where(nonneg, r, tr)
    sig_neg = jnp.where(nonneg, tr, r)
    one_m_lb = 1.0 - lb
    log_f = jnp.log(jnp.maximum(lb + one_m_lb * sig_pos, GATE_FLOOR))
    k = one_m_lb * sig_neg
    ri = lax.broadcasted_iota(jnp.int32, (c, c), 0)
    ci = lax.broadcasted_iota(jnp.int32, (c, c), 1)
    tri = jnp.where((ci <= ri) if forward else (ci >= ri), 1.0, 0.0).astype(BF16)
    hi, mid, lo = _split3(log_f)
    cum2 = (_dot(tri, hi) + _dot(tri, mid) + _dot(tri, lo)) * LOG2_E
    return cum2 - jnp.log2(k), cum2


def _rec_same_block(forward, q, key_row, cum2):
    sub_row = lax.broadcasted_iota(jnp.int32, (SUBLANES, REC_DK), 0)
    pieces = []
    for b in range(REC_CHUNK // SUBLANES):
        blk = slice(b * SUBLANES, (b + 1) * SUBLANES)
        cum_b, q_b = cum2[blk, :], q[blk, :]
        for sl in range(SUBLANES):
            s = b * SUBLANES + sl
            keep = (sub_row >= sl) if forward else (sub_row <= sl)
            pieces.append(jnp.where(keep, jnp.exp2(cum_b - key_row(s)) * q_b, 0.0))
    return _dot(jnp.concatenate(pieces, axis=0).astype(BF16), jnp.ones((REC_DK, REC_DV), BF16))


def _rec_cross_block(forward, q, key2, v, cum2):
    c = REC_CHUNK
    q_parts, k_parts, v_parts, segments = [], [], [], []
    for tb in range(c // SUBLANES):
        blk = slice(tb * SUBLANES, (tb + 1) * SUBLANES)
        src, ref_row = (slice(0, blk.start), blk.start - 1) if forward else (slice(blk.stop, c), blk.stop)
        if src.stop == src.start:
            q_parts.append(jnp.zeros((SUBLANES, REC_DK), F32))
            continue
        ref = cum2[ref_row:ref_row + 1, :]
        q_parts.append(q[blk, :] * jnp.exp2(cum2[blk, :] - ref))
        k_parts.append(jnp.exp2(ref - key2[src, :]))
        v_parts.append(v[src, :])
        segments.append((tb, src.stop - src.start))
    a = _dot_nt(jnp.concatenate(q_parts, axis=0).astype(BF16), jnp.concatenate(k_parts, axis=0).astype(BF16))
    return a, segments, jnp.concatenate(v_parts, axis=0).astype(BF16)


def _rec_cross_apply(a, segments, v_all):
    c, ncol = a.shape
    row_blk = lax.broadcasted_iota(jnp.int32, (c, ncol), 0) // SUBLANES
    col = lax.broadcasted_iota(jnp.int32, (c, ncol), 1)
    col_blk = jnp.full((c, ncol), -1, jnp.int32)
    start = 0
    for tb, width in segments:
        col_blk = jnp.where((col >= start) & (col < start + width), tb, col_blk)
        start += width
    return _dot(jnp.where(row_blk == col_blk, a, 0.0).astype(BF16), v_all)


def _rec_same_apply(lane_sums, v_row):
    o_blocks = []
    for b in range(REC_CHUNK // SUBLANES):
        acc = None
        for sl in range(SUBLANES):
            s = b * SUBLANES + sl
            term = lane_sums[s * SUBLANES:(s + 1) * SUBLANES, :] * v_row(s)
            acc = term if acc is None else acc + term
        o_blocks.append(acc)
    return jnp.concatenate(o_blocks, axis=0)


def _rec_kernel(layer, has_init, *refs):
    if has_init:
        (q_ref, ff_ref, fb_ref, v_ref, g_ref, lg_ref, ng_ref, s0_ref, o_ref,
         o_scr, qd_scr, kd_scr, gl_scr, oi_scr, key_scr) = refs
        s_out_ref = None
    else:
        (q_ref, ff_ref, fb_ref, v_ref, g_ref, lg_ref, ng_ref, o_ref, s_out_ref,
         o_scr, qd_scr, kd_scr, gl_scr, oi_scr, key_scr) = refs
    seq = q_ref.shape[0]
    nchunk = seq // REC_CHUNK
    c = REC_CHUNK

    logits = lg_ref[...]
    e = jnp.exp(logits - jnp.max(logits, axis=0, keepdims=True))
    p = e / jnp.sum(e, axis=0, keepdims=True)
    lb = jnp.zeros_like(p[0])
    for i in range(1, layer + 1):
        lb = lb + p[i]

    def rows_of(ci):
        return pl.ds(pl.multiple_of(ci * c, c), c)

    z_refs = (ff_ref, fb_ref)

    def local_body(it, carry):
        chunks = [it * REC_LOCAL_UNROLL + u for u in range(REC_LOCAL_UNROLL)]
        qs = [q_ref[rows_of(ci), :] for ci in chunks]
        vs = [v_ref[rows_of(ci), :] for ci in chunks]
        items = [(u, d) for u in range(REC_LOCAL_UNROLL) for d in range(2)]
        gates = [_rec_gates(d == 0, z_refs[d][rows_of(chunks[u]), :], lb[d]) for u, d in items]
        for idx, (key2, _) in enumerate(gates):
            key_scr[idx] = key2
        key_rows = [lambda s, idx=idx: key_scr[idx, s:s + 1, :] for idx in range(len(items))]
        v_rows = [lambda s, ci=ci: v_ref[pl.ds(ci * c + s, 1), :] for ci in chunks]
        same = [_rec_same_block(d == 0, qs[u], key_rows[idx], cum2)
                for idx, ((u, d), (_, cum2)) in enumerate(zip(items, gates))]
        cross = [_rec_cross_block(d == 0, qs[u], key2, vs[u], cum2) for (u, d), (key2, cum2) in zip(items, gates)]
        totals = [None] * REC_LOCAL_UNROLL
        for (u, d), (key2, cum2), lane_sums, (a, segments, v_all) in zip(items, gates, same, cross):
            rows = rows_of(chunks[u])
            last = cum2[c - 1:c, :] if d == 0 else cum2[0:1, :]
            qd_scr[d, rows, :] = (qs[u] * jnp.exp2(cum2)).astype(BF16)
            kd_scr[d, rows, :] = jnp.exp2(last - key2).astype(BF16)
            gl_scr[d, chunks[u]] = jnp.broadcast_to(jnp.exp2(last), (SUBLANES, REC_DK))
            o_local = _rec_same_apply(lane_sums, v_rows[u]) + _rec_cross_apply(a, segments, v_all)
            totals[u] = o_local if totals[u] is None else totals[u] + o_local
        for u, ci in enumerate(chunks):
            o_scr[rows_of(ci), :] = totals[u]
        return carry

    lax.fori_loop(0, nchunk // REC_LOCAL_UNROLL, local_body, 0)

    if has_init:
        init = (s0_ref[0].T, s0_ref[1].T)
    else:
        init = (jnp.zeros((REC_DV, REC_DK), F32),) * 2

    state_unroll = min(REC_STATE_UNROLL, nchunk)

    def state_body(it, states):
        states = list(states)
        steps = [it * state_unroll + u for u in range(state_unroll)]
        order = [(d, step if d == 0 else nchunk - 1 - step) for step in steps for d in range(2)]
        updates = [_dot_tn(v_ref[rows_of(ci), :].astype(BF16), kd_scr[d, rows_of(ci), :]) for d, ci in order]
        for (d, ci), update in zip(order, updates):
            st = states[d]
            oi_scr[d, rows_of(ci), :] = _dot_nt(qd_scr[d, rows_of(ci), :], st.astype(BF16))
            decayed = (st.reshape(REC_DV // SUBLANES, SUBLANES, REC_DK) * gl_scr[d, ci]).reshape(REC_DV, REC_DK)
            states[d] = decayed + update
        return tuple(states)

    final = lax.fori_loop(0, nchunk // state_unroll, state_body, init)
    if not has_init:
        s_out_ref[0] = final[0].T
        s_out_ref[1] = final[1].T

    def out_body(ci, carry):
        rows = rows_of(ci)
        gate = g_ref[rows, :]
        o = o_scr[rows, :] + oi_scr[0, rows, :] + oi_scr[1, rows, :]
        o_ref[rows, :] = (_rms(o, ng_ref[...]) * (gate * _sigmoid(gate))).astype(BF16)
        return carry

    lax.fori_loop(0, nchunk, out_body, 0, unroll=4)


def _recurrence(z, layer, lb_logits, norm_g, s0, batch, seq):
    has_init = s0 is not None
    col = lambda base: pl.BlockSpec((seq, REC_DK), lambda b, h: (b, base + h))
    state_spec = pl.BlockSpec((None, 2, None, REC_DK, REC_DV), lambda b, h: (b, 0, h, 0, 0))
    in_specs = [col(COL_RQ), col(COL_RF_F), col(COL_RF_B), col(COL_RI), col(COL_RG),
                pl.BlockSpec((DEPTH, 2, None, 1, REC_DK), lambda b, h: (0, 0, h, 0, 0)),
                pl.BlockSpec((None, 1, REC_DV), lambda b, h: (h, 0, 0))]
    args = [z, z, z, z, z, lb_logits, norm_g]
    out_specs = [pl.BlockSpec((seq, REC_DV), lambda b, h: (b, h))]
    out_shape = [jax.ShapeDtypeStruct((batch * seq, D_REC), BF16)]
    if has_init:
        in_specs.append(state_spec)
        args.append(s0)
    else:
        out_specs.append(state_spec)
        out_shape.append(jax.ShapeDtypeStruct((batch, 2, N_REC_HEADS, REC_DK, REC_DV), F32))
    return pl.pallas_call(
        functools.partial(_rec_kernel, layer, has_init),
        grid=(batch, N_REC_HEADS),
        in_specs=in_specs,
        out_specs=out_specs,
        out_shape=out_shape,
        scratch_shapes=[pltpu.VMEM((seq, REC_DV), F32),
                        pltpu.VMEM((2, seq, REC_DK), BF16),
                        pltpu.VMEM((2, seq, REC_DK), BF16),
                        pltpu.VMEM((2, seq // REC_CHUNK, SUBLANES, REC_DK), F32),
                        pltpu.VMEM((2, seq, REC_DV), F32),
                        pltpu.VMEM((2 * REC_LOCAL_UNROLL, REC_CHUNK, REC_DK), F32)],
        compiler_params=_params("parallel", "parallel"),
        name="hgrn2",
    )(*args)


def _conv_kernel(tiles_per_seq, a_ref, b_ref, at_ref, bt_ref, ab_ref, bb_ref, w_ref, bias_ref, lg_ref, lb_ref,
                 o_ref, h_scr, w_scr):
    i = pl.program_id(0)
    tile_in_seq = i % tiles_per_seq
    has_top = jnp.where(tile_in_seq > 0, 1.0, 0.0)
    has_bot = jnp.where(tile_in_seq < tiles_per_seq - 1, 1.0, 0.0)
    rows = a_ref.shape[0]
    half = CONV_WIDTH // 2
    shifted_rows = rows + 2 * CONV_HALO - SUBLANES

    def glu(a, b):
        return a * _sigmoid(b)

    h_scr[0, 0:CONV_HALO, :] = glu(at_ref[...], bt_ref[...]) * has_top
    h_scr[0, CONV_HALO:CONV_HALO + rows, :] = glu(a_ref[...], b_ref[...])
    h_scr[0, CONV_HALO + rows:, :] = glu(ab_ref[...], bb_ref[...]) * has_bot
    for r in range(1, SUBLANES):
        h_scr[r, 0:shifted_rows, :] = h_scr[0, r:r + shifted_rows, :]
    for j in range(CONV_WIDTH):
        w_scr[j] = jnp.broadcast_to(w_ref[j:j + 1, :], (SUBLANES, CONV_CH))

    for r0 in range(0, rows, CONV_SUB):
        acc = None
        for j in range(CONV_WIDTH):
            lo = r0 + CONV_HALO - half + j
            shift = lo % SUBLANES
            weight = w_scr[j]
            window = h_scr[shift, lo - shift:lo - shift + CONV_SUB, :]
            term = (window.reshape(CONV_SUB // SUBLANES, SUBLANES, CONV_CH) * weight).reshape(CONV_SUB, CONV_CH)
            acc = term if acc is None else acc + term
        y = acc + bias_ref[...]
        mu = jnp.mean(y, axis=-1, keepdims=True)
        yc = y - mu
        var = jnp.mean(yc * yc, axis=-1, keepdims=True)
        yn = yc * lax.rsqrt(var + EPS) * lg_ref[...] + lb_ref[...]
        o_ref[r0:r0 + CONV_SUB, :] = (yn * _sigmoid(yn)).astype(BF16)


def _conv(z, w, bias, ln_g, ln_b, seq):
    n = z.shape[0]
    tiles_per_seq = seq // CONV_ROWS
    ntiles = n // CONV_ROWS
    cw = CONV_CH // HEAD_DIM
    halo_per_tile = CONV_ROWS // CONV_HALO
    nhalo = n // CONV_HALO
    ca, cb = COL_CA // cw, COL_CB // cw
    mid = lambda c: pl.BlockSpec((CONV_ROWS, CONV_CH), lambda i: (i, c))
    top = lambda c: pl.BlockSpec((CONV_HALO, CONV_CH), lambda i: (jnp.maximum(i * halo_per_tile - 1, 0), c))
    bot = lambda c: pl.BlockSpec((CONV_HALO, CONV_CH), lambda i: (jnp.minimum((i + 1) * halo_per_tile, nhalo - 1), c))
    vec = pl.BlockSpec((1, CONV_CH), lambda i: (0, 0))
    return pl.pallas_call(
        functools.partial(_conv_kernel, tiles_per_seq),
        grid=(ntiles,),
        in_specs=[mid(ca), mid(cb), top(ca), top(cb), bot(ca), bot(cb),
                  pl.BlockSpec((CONV_WIDTH, CONV_CH), lambda i: (0, 0)), vec, vec, vec],
        out_specs=pl.BlockSpec((CONV_ROWS, CONV_CH), lambda i: (i, 0)),
        out_shape=jax.ShapeDtypeStruct((n, CONV_CH), BF16),
        scratch_shapes=[pltpu.VMEM((SUBLANES, CONV_ROWS + 2 * CONV_HALO, CONV_CH), F32),
                        pltpu.VMEM((CONV_WIDTH, SUBLANES, CONV_CH), F32)],
        compiler_params=_params("parallel"),
        name="conv",
    )(z, z, z, z, z, z, w, bias.reshape(1, CONV_CH), ln_g.reshape(1, CONV_CH), ln_b.reshape(1, CONV_CH))


def _outproj_kernel(x_ref, g_ref, a_ref, r_ref, c_ref, wa_ref, wr_ref, wc_ref, o_ref):
    mix = _dot(a_ref[...], wa_ref[...]) + _dot(r_ref[...], wr_ref[...]) + _dot(c_ref[...], wc_ref[...])
    o_ref[...] = x_ref[...] + g_ref[...] * mix


def _outproj(x, mod, attn, rec, conv, w_out, layer, rows_per_cond):
    n = x.shape[0]
    tn = D_MODEL // 2
    row = lambda width: pl.BlockSpec((TM, width), lambda i, j: (i, 0))
    return pl.pallas_call(
        _outproj_kernel,
        grid=(n // TM, D_MODEL // tn),
        in_specs=[
            pl.BlockSpec((TM, tn), lambda i, j: (i, j)),
            pl.BlockSpec((None, None, 1, tn), lambda i, j: ((i * TM) // rows_per_cond, 5, 0, j)),
            row(D_ATTN), row(D_REC), row(CONV_CH),
            pl.BlockSpec((None, D_ATTN, tn), lambda i, j: (layer, 0, j)),
            pl.BlockSpec((None, D_REC, tn), lambda i, j: (layer, D_ATTN // D_REC, j)),
            pl.BlockSpec((None, CONV_CH, tn), lambda i, j: (layer, (D_ATTN + D_REC) // CONV_CH, j)),
        ],
        out_specs=pl.BlockSpec((TM, tn), lambda i, j: (i, j)),
        out_shape=jax.ShapeDtypeStruct((n, D_MODEL), F32),
        compiler_params=_params("parallel", "parallel"),
        name="outproj",
    )(x, mod, attn, rec, conv, w_out, w_out, w_out)


def _trunk_layer(x, mod, l, P, mixers, rows_per_cond):
    x = _ffn(x, mod, 0, P['norm_g'][l, 0], P['w_ffn_in'], P['w_ffn_out'], l, 0, rows_per_cond)
    z = _inproj(x, mod, P['norm_g'][l, 1], P['w_in'], l, rows_per_cond)
    attn, rec, conv, extras = mixers(z)
    x = _outproj(x, mod, attn, rec, conv, P['w_out'], l, rows_per_cond)
    x = _ffn(x, mod, 6, P['norm_g'][l, 2], P['w_ffn_in'], P['w_ffn_out'], l, 1, rows_per_cond)
    return x, extras


def kernel(x_prompt, x_sample, cache_k, cache_v, state_rec, c, c_ctx, w_ada, b_ada, norm_g, w_ffn_in, w_ffn_out,
           w_in, w_out, q_norm_g, k_norm_g, attn_sink, rec_lb_logits, rec_norm_g, conv_w, conv_b, conv_ln_g, conv_ln_b):
    batch, seq, _ = x_prompt.shape
    dec_batch, dec_seq, _ = x_sample.shape
    assert seq % TM == 0 or TM % seq == 0
    assert dec_seq % TM == 0 and dec_seq % LAT_Q_TILE == 0 and seq % CONV_ROWS == 0 and dec_seq % CONV_ROWS == 0

    P = {'norm_g': norm_g, 'w_ffn_in': w_ffn_in.astype(BF16), 'w_ffn_out': w_ffn_out.astype(BF16),
         'w_in': w_in.astype(BF16), 'w_out': w_out.astype(BF16)}

    lb_logits = rec_lb_logits.reshape(DEPTH, 2, N_REC_HEADS, 1, REC_DK)

    cond8 = jnp.zeros((SUBLANES, D_MODEL), F32).at[0].set(c_ctx).at[1:1 + dec_batch].set(c)
    mod = _modulation(cond8, w_ada, b_ada).reshape(DEPTH, SUBLANES, N_MOD, 1, D_MODEL)
    tables = _rope_tables(dec_seq)
    sinks = attn_sink.reshape(DEPTH, N_KV_HEADS, Q_PER_KV)
    rec_g = rec_norm_g.reshape(DEPTH, N_REC_HEADS, 1, REC_DV)

    def shared_mixers(z, l, seq_len):
        return _conv(z, conv_w[l], conv_b[l], conv_ln_g[l], conv_ln_b[l], seq_len)

    h = x_prompt.reshape(batch * seq, D_MODEL)
    ks, vs, ss = [], [], []
    for l in range(DEPTH):
        def ctx_mixers(z, l=l):
            attn, k_l, v_l = _ctx_attention(z, sinks[l], q_norm_g[l], k_norm_g[l], batch, seq)
            rec, s_l = _recurrence(z, l, lb_logits, rec_g[l], None, batch, seq)
            return attn, rec, shared_mixers(z, l, seq), (k_l, v_l, s_l)

        h, (k_l, v_l, s_l) = _trunk_layer(h, mod[l, 0:1], l, P, ctx_mixers, batch * seq)
        ks.append(k_l)
        vs.append(v_l)
        ss.append(s_l)
    y_prompt = h.reshape(batch, seq, D_MODEL)

    h = x_sample.reshape(dec_batch * dec_seq, D_MODEL)
    for l in range(DEPTH):
        def lat_mixers(z, l=l):
            attn = _lat_attention(z, sinks[l], q_norm_g[l], k_norm_g[l], cache_k[:, l], cache_v[:, l], tables,
                                  dec_batch, dec_seq)
            rec = _recurrence(z, l, lb_logits, rec_g[l], state_rec[:, l], dec_batch, dec_seq)[0]
            return attn, rec, shared_mixers(z, l, dec_seq), None

        h, _ = _trunk_layer(h, mod[l, 1:1 + dec_batch], l, P, lat_mixers, dec_seq)
    y_sample = h.reshape(dec_batch, dec_seq, D_MODEL)

    return (y_prompt, y_sample, jnp.stack(ks, axis=1), jnp.stack(vs, axis=1), jnp.stack(ss, axis=1))
```

```python
import functools

import jax
import jax.numpy as jnp
import numpy as np
from jax import lax
from jax.experimental import pallas as pl
from jax.experimental.pallas import tpu as pltpu

F32 = jnp.float32
BF16 = jnp.bfloat16

D_MODEL = 2048
DEPTH = 2
GRID_W = 64
HEAD_DIM = 128
D_ATTN = D_MODEL // 2
N_Q_HEADS = D_ATTN // HEAD_DIM
N_KV_HEADS = 2
Q_PER_KV = N_Q_HEADS // N_KV_HEADS
WINDOW = 128
ATTN_BLOCK = 128
ATTN_SCALE = HEAD_DIM ** -0.5
ROPE_BASE = 10000.0
MASK_VALUE = -1e30
D_REC = D_MODEL // 4
REC_DK = 128
REC_DV = 128
N_REC_HEADS = D_REC // REC_DV
REC_CHUNK = 32
CONV_CH = D_MODEL // 4
CONV_WIDTH = 31
D_FF = 5632
N_MOD = 9
EPS = 1e-6
GATE_FLOOR = 1e-30
LOG2_E = 1.4426950408889634
IN_COLS = 5120

COL_Q = 0
COL_K = 8
COL_V = 10
COL_RQ = 12
COL_RF_F = 16
COL_RF_B = 20
COL_RI = 24
COL_RG = 28
COL_CA = 32
COL_CB = 36

SUBLANES = 8
VMEM_LIMIT = 56 * 1024 * 1024

TM = 1024
TM_FFN = 1024
TF = 512
TN_FFN_OUT = 256
TN_IN = 1024
TN_ADA = 1024
CONV_TILES_PER_BLOCK = 8
CONV_ROWS = TM_FFN // CONV_TILES_PER_BLOCK
CONV_HALO = 16
CONV_SUB = 32
LAT_Q_TILE = 1024
CTX_SEQS_PER_STEP = 2
REC_LOCAL_UNROLL = 8
REC_STATE_UNROLL = 16


def _params(*sem):
    return pltpu.CompilerParams(dimension_semantics=sem, vmem_limit_bytes=VMEM_LIMIT)


def _dot(a, b):
    return jnp.dot(a, b, preferred_element_type=F32)


def _dot_nt(a, b):
    return lax.dot_general(a, b, (((1,), (1,)), ((), ())), preferred_element_type=F32)


def _dot_tn(a, b):
    return lax.dot_general(a, b, (((0,), (0,)), ((), ())), preferred_element_type=F32)


def _rms(x, g):
    return x * lax.rsqrt(jnp.mean(x * x, axis=-1, keepdims=True) + EPS) * g


def _ada_norm(x, g, scale, shift):
    gain = g * (1.0 + scale)
    return x * lax.rsqrt(jnp.mean(x * x, axis=-1, keepdims=True) + EPS) * gain + shift


def _sigmoid(x):
    return 1.0 / (1.0 + jnp.exp(-x))


def _ada_kernel(c_ref, w_ref, b_ref, o_ref):
    c = c_ref[...]
    s = (c * _sigmoid(c)).astype(BF16)
    o_ref[...] = _dot(s, w_ref[...].astype(BF16)) + b_ref[...]


def _modulation(cond8, w_ada, b_ada):
    ncol = N_MOD * D_MODEL
    return pl.pallas_call(
        _ada_kernel,
        grid=(DEPTH, ncol // TN_ADA),
        in_specs=[
            pl.BlockSpec((SUBLANES, D_MODEL), lambda l, j: (0, 0)),
            pl.BlockSpec((None, D_MODEL, TN_ADA), lambda l, j: (l, 0, j)),
            pl.BlockSpec((None, 1, TN_ADA), lambda l, j: (l, 0, j)),
        ],
        out_specs=pl.BlockSpec((None, SUBLANES, TN_ADA), lambda l, j: (l, 0, j)),
        out_shape=jax.ShapeDtypeStruct((DEPTH, SUBLANES, ncol), F32),
        compiler_params=_params("parallel", "parallel"),
        name="modulation",
    )(cond8, w_ada, b_ada.reshape(DEPTH, 1, ncol))


def _mod_spec(chunk, rows_per_cond, tm=TM):
    return pl.BlockSpec((None, None, 1, D_MODEL), lambda i, j: ((i * tm) // rows_per_cond, chunk, 0, 0))


def _ffn_kernel(nf, conv_tiles_per_seq, x_ref, sh_ref, sc_ref, g_ref, ng_ref, wa_ref, wb_ref, wo_ref, *rest):
    with_conv = conv_tiles_per_seq is not None
    if with_conv:
        conv_in, (o_ref, conv_o_ref, h_scr, act_scr, conv_h_scr, conv_w_scr) = rest[:10], rest[10:]
    else:
        o_ref, h_scr, act_scr = rest
    i = pl.program_id(0)
    j = pl.program_id(1)

    @pl.when(j == 0)
    def _():
        h = _ada_norm(x_ref[...], ng_ref[...], sc_ref[...], sh_ref[...])
        h_scr[...] = h.astype(BF16)

    def hidden_tile():
        h = h_scr[...]
        a = _dot(h, wa_ref[...])
        b = _dot(h, wb_ref[...])
        act_scr[:, pl.ds(pl.multiple_of(j * TF, TF), TF)] = (a * _sigmoid(a) * b).astype(BF16)

    if with_conv:
        @pl.when(j < CONV_TILES_PER_BLOCK)
        def _():
            tile = i * CONV_TILES_PER_BLOCK + j
            _conv_tile(tile % conv_tiles_per_seq, conv_tiles_per_seq, *conv_in, conv_o_ref, conv_h_scr, conv_w_scr)
            hidden_tile()

        @pl.when((j >= CONV_TILES_PER_BLOCK) & (j < nf))
        def _():
            hidden_tile()
    else:
        @pl.when(j < nf)
        def _():
            hidden_tile()

    @pl.when(j >= nf)
    def _():
        cols = pl.ds(pl.multiple_of((j - nf) * TN_FFN_OUT, TN_FFN_OUT), TN_FFN_OUT)
        o_ref[...] = x_ref[:, cols] + 0.5 * g_ref[...] * _dot(act_scr[...], wo_ref[...])


def _ffn(x, mod, first_chunk, norm_g, w_in, w_out, layer, which, rows_per_cond, conv_job=None):
    n = x.shape[0]
    nf = D_FF // TF
    nout = D_MODEL // TN_FFN_OUT
    hid = lambda j: jnp.minimum(j, nf - 1)
    out = lambda j: jnp.maximum(j - nf, 0)
    in_specs = [
        pl.BlockSpec((TM_FFN, D_MODEL), lambda i, j: (i, 0), pipeline_mode=pl.Buffered(1)),
        _mod_spec(first_chunk, rows_per_cond, TM_FFN),
        _mod_spec(first_chunk + 1, rows_per_cond, TM_FFN),
        pl.BlockSpec((None, None, 1, TN_FFN_OUT),
                     lambda i, j: ((i * TM_FFN) // rows_per_cond, first_chunk + 2, 0, out(j))),
        pl.BlockSpec((1, D_MODEL), lambda i, j: (0, 0)),
        pl.BlockSpec((None, None, D_MODEL, TF), lambda i, j: (layer, which, 0, hid(j))),
        pl.BlockSpec((None, None, D_MODEL, TF), lambda i, j: (layer, which, 0, hid(j) + nf)),
        pl.BlockSpec((None, None, D_FF, TN_FFN_OUT), lambda i, j: (layer, which, 0, out(j))),
    ]
    args = [x, mod, mod, mod, norm_g.reshape(1, D_MODEL), w_in, w_in, w_out]
    out_specs = [pl.BlockSpec((TM_FFN, TN_FFN_OUT), lambda i, j: (i, out(j)))]
    out_shape = [jax.ShapeDtypeStruct((n, D_MODEL), F32)]
    scratch = [pltpu.VMEM((TM_FFN, D_MODEL), BF16), pltpu.VMEM((TM_FFN, D_FF), BF16)]
    conv_tiles_per_seq = None
    if conv_job is not None:
        z, w, bias, ln_g, ln_b, seq = conv_job
        assert z.shape[0] == n and nf >= CONV_TILES_PER_BLOCK and seq % CONV_ROWS == 0
        conv_tiles_per_seq = seq // CONV_ROWS
        cw = CONV_CH // HEAD_DIM
        halo_per_tile = CONV_ROWS // CONV_HALO
        nhalo = n // CONV_HALO
        ca, cb = COL_CA // cw, COL_CB // cw
        tile = lambda i, j: i * CONV_TILES_PER_BLOCK + jnp.minimum(j, CONV_TILES_PER_BLOCK - 1)
        mid = lambda c: pl.BlockSpec((CONV_ROWS, CONV_CH), lambda i, j: (tile(i, j), c))
        top = lambda c: pl.BlockSpec(
            (CONV_HALO, CONV_CH), lambda i, j: (jnp.maximum(tile(i, j) * halo_per_tile - 1, 0), c))
        bot = lambda c: pl.BlockSpec(
            (CONV_HALO, CONV_CH), lambda i, j: (jnp.minimum((tile(i, j) + 1) * halo_per_tile, nhalo - 1), c))
        vec = pl.BlockSpec((1, CONV_CH), lambda i, j: (0, 0))
        in_specs += [mid(ca), mid(cb), top(ca), top(cb), bot(ca), bot(cb),
                     pl.BlockSpec((CONV_WIDTH, CONV_CH), lambda i, j: (0, 0)), vec, vec, vec]
        args += [z, z, z, z, z, z, w, bias.reshape(1, CONV_CH), ln_g.reshape(1, CONV_CH), ln_b.reshape(1, CONV_CH)]
        out_specs.append(pl.BlockSpec((CONV_ROWS, CONV_CH), lambda i, j: (tile(i, j), 0)))
        out_shape.append(jax.ShapeDtypeStruct((n, CONV_CH), BF16))
        scratch += [pltpu.VMEM((SUBLANES, CONV_ROWS + 2 * CONV_HALO, CONV_CH), F32),
                    pltpu.VMEM((CONV_WIDTH, SUBLANES, CONV_CH), F32)]
    outs = pl.pallas_call(
        functools.partial(_ffn_kernel, nf, conv_tiles_per_seq),
        grid=(n // TM_FFN, nf + nout),
        in_specs=in_specs,
        out_specs=out_specs,
        out_shape=out_shape,
        scratch_shapes=scratch,
        compiler_params=_params("parallel", "arbitrary"),
        name="ffn" if conv_job is None else "ffn_conv",
    )(*args)
    return outs[0] if conv_job is None else tuple(outs)


def _inproj_kernel(x_ref, sh_ref, sc_ref, ng_ref, w_ref, o_ref, h_scr):
    @pl.when(pl.program_id(1) == 0)
    def _():
        h = _ada_norm(x_ref[...], ng_ref[...], sc_ref[...], sh_ref[...])
        h_scr[...] = h.astype(BF16)

    o_ref[...] = _dot(h_scr[...], w_ref[...])


def _inproj(x, mod, norm_g, w_in, layer, rows_per_cond):
    n = x.shape[0]
    return pl.pallas_call(
        _inproj_kernel,
        grid=(n // TM, IN_COLS // TN_IN),
        in_specs=[
            pl.BlockSpec((TM, D_MODEL), lambda i, j: (i, 0)),
            _mod_spec(3, rows_per_cond),
            _mod_spec(4, rows_per_cond),
            pl.BlockSpec((1, D_MODEL), lambda i, j: (0, 0)),
            pl.BlockSpec((None, D_MODEL, TN_IN), lambda i, j: (layer, 0, j)),
        ],
        out_specs=pl.BlockSpec((TM, TN_IN), lambda i, j: (i, j)),
        out_shape=jax.ShapeDtypeStruct((n, IN_COLS), F32),
        scratch_shapes=[pltpu.VMEM((TM, D_MODEL), BF16)],
        compiler_params=_params("parallel", "arbitrary"),
        name="inproj",
    )(x, mod, mod, norm_g.reshape(1, D_MODEL), w_in)


QK_SCALE_LOG2 = ATTN_SCALE * LOG2_E


def _with_ones(v):
    return jnp.concatenate([v, jnp.ones_like(v)], axis=-1)


def _softmax_sink(scores, values, sink):
    sink2 = sink * LOG2_E
    m = sink2
    for s in scores:
        m = jnp.maximum(jnp.max(s, axis=-1, keepdims=True), m)
    acc = None
    for s, v in zip(scores, values):
        pv = _dot(jnp.exp2(s - m).astype(BF16), v)
        acc = pv if acc is None else acc + pv
    return acc[:, :HEAD_DIM] / (acc[:, HEAD_DIM:] + jnp.exp2(sink2 - m))


def _ctx_attn_kernel(sink_ref, q_ref, k_ref, v_ref, qg_ref, kg_ref, o_ref, kc_ref, vc_ref):
    kv = pl.program_id(1)
    nseq, seq = kc_ref.shape[0], kc_ref.shape[1]
    heads = [slice(g * HEAD_DIM, (g + 1) * HEAD_DIM) for g in range(Q_PER_KV)]
    keys, values = [], []
    for i in range(nseq):
        rows = slice(i * seq, (i + 1) * seq)
        kn = _rms(k_ref[rows, :], kg_ref[...])
        v = v_ref[rows, :]
        kc_ref[i] = kn
        vc_ref[i] = v
        keys.append(kn.astype(BF16))
        values.append(_with_ones(v.astype(BF16)))
    logits = [[_dot_nt((_rms(q_ref[i * seq:(i + 1) * seq, cols], qg_ref[...]) * QK_SCALE_LOG2).astype(BF16), keys[i])
               for cols in heads] for i in range(nseq)]
    for i in range(nseq):
        for g, cols in enumerate(heads):
            o = _softmax_sink([logits[i][g]], [values[i]], sink_ref[kv, g])
            o_ref[i * seq:(i + 1) * seq, cols] = o.astype(BF16)


def _ctx_attention(z, sink, q_g, k_g, batch, seq):
    qw = Q_PER_KV * HEAD_DIM
    per = CTX_SEQS_PER_STEP
    cache_shape = jax.ShapeDtypeStruct((batch, N_KV_HEADS, seq, HEAD_DIM), F32)
    cache_spec = pl.BlockSpec((per, None, seq, HEAD_DIM), lambda b, kv: (b, kv, 0, 0))
    return pl.pallas_call(
        _ctx_attn_kernel,
        grid=(batch // per, N_KV_HEADS),
        in_specs=[
            pl.BlockSpec(memory_space=pltpu.SMEM),
            pl.BlockSpec((per * seq, qw), lambda b, kv: (b, kv)),
            pl.BlockSpec((per * seq, HEAD_DIM), lambda b, kv: (b, COL_K + kv)),
            pl.BlockSpec((per * seq, HEAD_DIM), lambda b, kv: (b, COL_V + kv)),
            pl.BlockSpec((1, HEAD_DIM), lambda b, kv: (0, 0)),
            pl.BlockSpec((1, HEAD_DIM), lambda b, kv: (0, 0)),
        ],
        out_specs=[pl.BlockSpec((per * seq, qw), lambda b, kv: (b, kv)), cache_spec, cache_spec],
        out_shape=[jax.ShapeDtypeStruct((batch * seq, D_ATTN), BF16), cache_shape, cache_shape],
        compiler_params=_params("parallel", "parallel"),
        name="ctx_attention",
    )(sink, z, z, z, q_g.reshape(1, HEAD_DIM), k_g.reshape(1, HEAD_DIM))


def _rope(x, cos, sin_a, sin_b):
    quarter = HEAD_DIM // 4
    up = pltpu.roll(x, HEAD_DIM - quarter, 1)
    down = pltpu.roll(x, quarter, 1)
    return x * cos + up * sin_a + down * sin_b


def _lat_attn_kernel(sink_ref, q_ref, k_ref, v_ref, kc_ref, vc_ref, qg_ref, kg_ref,
                     cos_ref, sa_ref, sb_ref, cosq_ref, saq_ref, sbq_ref, o_ref, k_scr, v_scr, kc_scr, vc_scr):
    kv = pl.program_id(1)
    qt = pl.program_id(2)
    seq = k_ref.shape[0]
    span = 3 * ATTN_BLOCK

    @pl.when(qt == 0)
    def _():
        kn = _rms(k_ref[...], kg_ref[...])
        k_scr[...] = _rope(kn, cos_ref[...], sa_ref[...], sb_ref[...]).astype(BF16)
        v_scr[...] = _with_ones(v_ref[...].astype(BF16))
        kc_scr[...] = kc_ref[...].astype(BF16)
        vc_scr[...] = _with_ones(vc_ref[...].astype(BF16))

    kcb = kc_scr[...]
    vcb = vc_scr[...]
    nblk = LAT_Q_TILE // ATTN_BLOCK
    heads = [slice(g * HEAD_DIM, (g + 1) * HEAD_DIM) for g in range(Q_PER_KV)]
    stacked = Q_PER_KV * ATTN_BLOCK

    head_of_row = lax.broadcasted_iota(jnp.int32, (stacked, 1), 0) // ATTN_BLOCK
    sink_col = jnp.zeros((stacked, 1), F32)
    for g in range(Q_PER_KV):
        sink_col = jnp.where(head_of_row == g, sink_ref[kv, g], sink_col)

    def block_rows(blk):
        return slice(blk * ATTN_BLOCK, (blk + 1) * ATTN_BLOCK)

    def logits_of(blk):
        rows = block_rows(blk)
        q0 = (qt * nblk + blk) * ATTN_BLOCK
        start = pl.multiple_of(jnp.clip(q0 - ATTN_BLOCK, 0, seq - span), ATTN_BLOCK)
        qpos = q0 + lax.broadcasted_iota(jnp.int32, (stacked, span), 0) % ATTN_BLOCK
        kpos = start + lax.broadcasted_iota(jnp.int32, (stacked, span), 1)
        valid = jnp.abs(qpos - kpos) <= WINDOW
        cos, sa, sb = cosq_ref[rows, :], saq_ref[rows, :], sbq_ref[rows, :]
        qn = jnp.concatenate([_rope(_rms(q_ref[rows, cols], qg_ref[...]), cos, sa, sb) for cols in heads], axis=0)
        qn = (qn * QK_SCALE_LOG2).astype(BF16)
        s_loc = jnp.where(valid, _dot_nt(qn, k_scr[pl.ds(start, span), :]), MASK_VALUE)
        return s_loc, _dot_nt(qn, kcb), start

    pending = logits_of(0)
    for blk in range(nblk):
        upcoming = logits_of(blk + 1) if blk + 1 < nblk else None
        s_loc, s_ctx, start = pending
        o = _softmax_sink([s_loc, s_ctx], [v_scr[pl.ds(start, span), :], vcb], sink_col)
        for g, cols in enumerate(heads):
            o_ref[block_rows(blk), cols] = o[g * ATTN_BLOCK:(g + 1) * ATTN_BLOCK, :].astype(BF16)
        pending = upcoming


def _rope_tables(seq):
    quarter = HEAD_DIM // 4
    pos = np.arange(seq)
    inv_freq = ROPE_BASE ** (-np.arange(quarter, dtype=np.float32) / quarter)
    inv_freq = jnp.asarray(inv_freq, F32)
    zero = jnp.zeros((seq, quarter), F32)

    def trig(p):
        ang = jnp.asarray(p, F32)[:, None] * inv_freq
        return jnp.cos(ang), jnp.sin(ang)

    cr, sr = trig(pos // GRID_W)
    cc, sc = trig(pos % GRID_W)
    cos = jnp.concatenate([cr, cr, cc, cc], axis=-1)
    sin_a = jnp.concatenate([-sr, zero, -sc, zero], axis=-1)
    sin_b = jnp.concatenate([zero, sr, zero, sc], axis=-1)
    return cos, sin_a, sin_b


def _lat_attention(z, sink, q_g, k_g, k_ctx, v_ctx, tables, batch, seq):
    qw = Q_PER_KV * HEAD_DIM
    nqt = seq // LAT_Q_TILE
    past = k_ctx.shape[2]
    cos, sin_a, sin_b = tables
    full_tab = pl.BlockSpec((seq, HEAD_DIM), lambda b, kv, qt: (0, 0))
    tile_tab = pl.BlockSpec((LAT_Q_TILE, HEAD_DIM), lambda b, kv, qt: (qt, 0))
    ctx_spec = pl.BlockSpec((None, None, past, HEAD_DIM), lambda b, kv, qt: (b, kv, 0, 0))
    gain = pl.BlockSpec((1, HEAD_DIM), lambda b, kv, qt: (0, 0))
    return pl.pallas_call(
        _lat_attn_kernel,
        grid=(batch, N_KV_HEADS, nqt),
        in_specs=[
            pl.BlockSpec(memory_space=pltpu.SMEM),
            pl.BlockSpec((LAT_Q_TILE, qw), lambda b, kv, qt: (b * nqt + qt, kv)),
            pl.BlockSpec((seq, HEAD_DIM), lambda b, kv, qt: (b, COL_K + kv)),
            pl.BlockSpec((seq, HEAD_DIM), lambda b, kv, qt: (b, COL_V + kv)),
            ctx_spec, ctx_spec, gain, gain,
            full_tab, full_tab, full_tab, tile_tab, tile_tab, tile_tab,
        ],
        out_specs=pl.BlockSpec((LAT_Q_TILE, qw), lambda b, kv, qt: (b * nqt + qt, kv)),
        out_shape=jax.ShapeDtypeStruct((batch * seq, D_ATTN), BF16),
        scratch_shapes=[pltpu.VMEM((seq, HEAD_DIM), BF16), pltpu.VMEM((seq, 2 * HEAD_DIM), BF16),
                        pltpu.VMEM((past, HEAD_DIM), BF16), pltpu.VMEM((past, 2 * HEAD_DIM), BF16)],
        compiler_params=_params("parallel", "parallel", "arbitrary"),
        name="lat_attention",
    )(sink, z, z, z, k_ctx, v_ctx, q_g.reshape(1, HEAD_DIM), k_g.reshape(1, HEAD_DIM),
      cos, sin_a, sin_b, cos, sin_a, sin_b)


def _split3(x):
    hi = x.astype(BF16)
    r1 = x - hi.astype(F32)
    mid = r1.astype(BF16)
    lo = (r1 - mid.astype(F32)).astype(BF16)
    return hi, mid, lo


def _rec_gates(forward, zf, lb):
    c = REC_CHUNK
    t = jnp.exp(-jnp.abs(zf))
    r = 1.0 / (1.0 + t)
    tr = t * r
    nonneg = zf >= 0
    sig_pos = jnp.where(nonneg, r, tr)
    sig_neg = jnp.where(nonneg, tr, r)
    one_m_lb = 1.0 - lb
    log_f = jnp.log(jnp.maximum(lb + one_m_lb * sig_pos, GATE_FLOOR))
    k = one_m_lb * sig_neg
    ri = lax.broadcasted_iota(jnp.int32, (c, c), 0)
    ci = lax.broadcasted_iota(jnp.int32, (c, c), 1)
    tri = jnp.where((ci <= ri) if forward else (ci >= ri), 1.0, 0.0).astype(BF16)
    hi, mid, lo = _split3(log_f)
    cum2 = (_dot(tri, hi) + _dot(tri, mid) + _dot(tri, lo)) * LOG2_E
    return cum2 - jnp.log2(k), cum2


def _rec_same_block(forward, q, key_row, cum2):
    sub_row = lax.broadcasted_iota(jnp.int32, (SUBLANES, REC_DK), 0)
    pieces = []
    for b in range(REC_CHUNK // SUBLANES):
        blk = slice(b * SUBLANES, (b + 1) * SUBLANES)
        cum_b, q_b = cum2[blk, :], q[blk, :]
        for sl in range(SUBLANES):
            s = b * SUBLANES + sl
            keep = (sub_row >= sl) if forward else (sub_row <= sl)
            pieces.append(jnp.where(keep, jnp.exp2(cum_b - key_row(s)) * q_b, 0.0))
    return _dot(jnp.concatenate(pieces, axis=0).astype(BF16), jnp.ones((REC_DK, REC_DV), BF16))


def _rec_cross_block(forward, q, key2, v, cum2):
    c = REC_CHUNK
    q_parts, k_parts, v_parts, segments = [], [], [], []
    for tb in range(c // SUBLANES):
        blk = slice(tb * SUBLANES, (tb + 1) * SUBLANES)
        src, ref_row = (slice(0, blk.start), blk.start - 1) if forward else (slice(blk.stop, c), blk.stop)
        if src.stop == src.start:
            q_parts.append(jnp.zeros((SUBLANES, REC_DK), F32))
            continue
        ref = cum2[ref_row:ref_row + 1, :]
        q_parts.append(q[blk, :] * jnp.exp2(cum2[blk, :] - ref))
        k_parts.append(jnp.exp2(ref - key2[src, :]))
        v_parts.append(v[src, :])
        segments.append((tb, src.stop - src.start))
    a = _dot_nt(jnp.concatenate(q_parts, axis=0).astype(BF16), jnp.concatenate(k_parts, axis=0).astype(BF16))
    return a, segments, jnp.concatenate(v_parts, axis=0).astype(BF16)


def _rec_cross_apply(a, segments, v_all):
    c, ncol = a.shape
    row_blk = lax.broadcasted_iota(jnp.int32, (c, ncol), 0) // SUBLANES
    col = lax.broadcasted_iota(jnp.int32, (c, ncol), 1)
    col_blk = jnp.full((c, ncol), -1, jnp.int32)
    start = 0
    for tb, width in segments:
        col_blk = jnp.where((col >= start) & (col < start + width), tb, col_blk)
        start += width
    return _dot(jnp.where(row_blk == col_blk, a, 0.0).astype(BF16), v_all)


def _rec_same_apply(lane_sums, v_row):
    o_blocks = []
    for b in range(REC_CHUNK // SUBLANES):
        acc = None
        for sl in range(SUBLANES):
            s = b * SUBLANES + sl
            term = lane_sums[s * SUBLANES:(s + 1) * SUBLANES, :] * v_row(s)
            acc = term if acc is None else acc + term
        o_blocks.append(acc)
    return jnp.concatenate(o_blocks, axis=0)


def _rec_kernel(layer, has_init, *refs):
    if has_init:
        (q_ref, ff_ref, fb_ref, v_ref, g_ref, lg_ref, ng_ref, s0_ref, o_ref,
         o_scr, qd_scr, kd_scr, gl_scr, oi_scr, key_scr) = refs
        s_out_ref = None
    else:
        (q_ref, ff_ref, fb_ref, v_ref, g_ref, lg_ref, ng_ref, o_ref, s_out_ref,
         o_scr, qd_scr, kd_scr, gl_scr, oi_scr, key_scr) = refs
    seq = q_ref.shape[0]
    nchunk = seq // REC_CHUNK
    c = REC_CHUNK

    logits = lg_ref[...]
    e = jnp.exp(logits - jnp.max(logits, axis=0, keepdims=True))
    p = e / jnp.sum(e, axis=0, keepdims=True)
    lb = jnp.zeros_like(p[0])
    for i in range(1, layer + 1):
        lb = lb + p[i]

    def rows_of(ci):
        return pl.ds(pl.multiple_of(ci * c, c), c)

    z_refs = (ff_ref, fb_ref)

    def local_body(it, carry):
        chunks = [it * REC_LOCAL_UNROLL + u for u in range(REC_LOCAL_UNROLL)]
        qs = [q_ref[rows_of(ci), :] for ci in chunks]
        vs = [v_ref[rows_of(ci), :] for ci in chunks]
        items = [(u, d) for u in range(REC_LOCAL_UNROLL) for d in range(2)]
        gates = [_rec_gates(d == 0, z_refs[d][rows_of(chunks[u]), :], lb[d]) for u, d in items]
        for idx, (key2, _) in enumerate(gates):
            key_scr[idx] = key2
        key_rows = [lambda s, idx=idx: key_scr[idx, s:s + 1, :] for idx in range(len(items))]
        v_rows = [lambda s, ci=ci: v_ref[pl.ds(ci * c + s, 1), :] for ci in chunks]
        same = [_rec_same_block(d == 0, qs[u], key_rows[idx], cum2)
                for idx, ((u, d), (_, cum2)) in enumerate(zip(items, gates))]
        cross = [_rec_cross_block(d == 0, qs[u], key2, vs[u], cum2) for (u, d), (key2, cum2) in zip(items, gates)]
        totals = [None] * REC_LOCAL_UNROLL
        for (u, d), (key2, cum2), lane_sums, (a, segments, v_all) in zip(items, gates, same, cross):
            rows = rows_of(chunks[u])
            last = cum2[c - 1:c, :] if d == 0 else cum2[0:1, :]
            qd_scr[d, rows, :] = (qs[u] * jnp.exp2(cum2)).astype(BF16)
            kd_scr[d, rows, :] = jnp.exp2(last - key2).astype(BF16)
            gl_scr[d, chunks[u]] = jnp.broadcast_to(jnp.exp2(last), (SUBLANES, REC_DK))
            o_local = _rec_same_apply(lane_sums, v_rows[u]) + _rec_cross_apply(a, segments, v_all)
            totals[u] = o_local if totals[u] is None else totals[u] + o_local
        for u, ci in enumerate(chunks):
            o_scr[rows_of(ci), :] = totals[u]
        return carry

    lax.fori_loop(0, nchunk // REC_LOCAL_UNROLL, local_body, 0)

    if has_init:
        init = (s0_ref[0].T, s0_ref[1].T)
    else:
        init = (jnp.zeros((REC_DV, REC_DK), F32),) * 2

    state_unroll = min(REC_STATE_UNROLL, nchunk)

    def state_body(it, states):
        states = list(states)
        steps = [it * state_unroll + u for u in range(state_unroll)]
        order = [(d, step if d == 0 else nchunk - 1 - step) for step in steps for d in range(2)]
        updates = [_dot_tn(v_ref[rows_of(ci), :].astype(BF16), kd_scr[d, rows_of(ci), :]) for d, ci in order]
        for (d, ci), update in zip(order, updates):
            st = states[d]
            oi_scr[d, rows_of(ci), :] = _dot_nt(qd_scr[d, rows_of(ci), :], st.astype(BF16))
            decayed = (st.reshape(REC_DV // SUBLANES, SUBLANES, REC_DK) * gl_scr[d, ci]).reshape(REC_DV, REC_DK)
            states[d] = decayed + update
        return tuple(states)

    final = lax.fori_loop(0, nchunk // state_unroll, state_body, init)
    if not has_init:
        s_out_ref[0] = final[0].T
        s_out_ref[1] = final[1].T

    def out_body(ci, carry):
        rows = rows_of(ci)
        gate = g_ref[rows, :]
        o = o_scr[rows, :] + oi_scr[0, rows, :] + oi_scr[1, rows, :]
        o_ref[rows, :] = (_rms(o, ng_ref[...]) * (gate * _sigmoid(gate))).astype(BF16)
        return carry

    lax.fori_loop(0, nchunk, out_body, 0, unroll=4)


def _recurrence(z, layer, lb_logits, norm_g, s0, batch, seq):
    has_init = s0 is not None
    col = lambda base: pl.BlockSpec((seq, REC_DK), lambda b, h: (b, base + h))
    state_spec = pl.BlockSpec((None, 2, None, REC_DK, REC_DV), lambda b, h: (b, 0, h, 0, 0))
    in_specs = [col(COL_RQ), col(COL_RF_F), col(COL_RF_B), col(COL_RI), col(COL_RG),
                pl.BlockSpec((DEPTH, 2, None, 1, REC_DK), lambda b, h: (0, 0, h, 0, 0)),
                pl.BlockSpec((None, 1, REC_DV), lambda b, h: (h, 0, 0))]
    args = [z, z, z, z, z, lb_logits, norm_g]
    out_specs = [pl.BlockSpec((seq, REC_DV), lambda b, h: (b, h))]
    out_shape = [jax.ShapeDtypeStruct((batch * seq, D_REC), BF16)]
    if has_init:
        in_specs.append(state_spec)
        args.append(s0)
    else:
        out_specs.append(state_spec)
        out_shape.append(jax.ShapeDtypeStruct((batch, 2, N_REC_HEADS, REC_DK, REC_DV), F32))
    return pl.pallas_call(
        functools.partial(_rec_kernel, layer, has_init),
        grid=(batch, N_REC_HEADS),
        in_specs=in_specs,
        out_specs=out_specs,
        out_shape=out_shape,
        scratch_shapes=[pltpu.VMEM((seq, REC_DV), F32),
                        pltpu.VMEM((2, seq, REC_DK), BF16),
                        pltpu.VMEM((2, seq, REC_DK), BF16),
                        pltpu.VMEM((2, seq // REC_CHUNK, SUBLANES, REC_DK), F32),
                        pltpu.VMEM((2, seq, REC_DV), F32),
                        pltpu.VMEM((2 * REC_LOCAL_UNROLL, REC_CHUNK, REC_DK), F32)],
        compiler_params=_params("parallel", "parallel"),
        name="hgrn2",
    )(*args)


def _conv_tile(tile_in_seq, tiles_per_seq, a_ref, b_ref, at_ref, bt_ref, ab_ref, bb_ref, w_ref, bias_ref, lg_ref,
               lb_ref, o_ref, h_scr, w_scr):
    has_top = jnp.where(tile_in_seq > 0, 1.0, 0.0)
    has_bot = jnp.where(tile_in_seq < tiles_per_seq - 1, 1.0, 0.0)
    rows = a_ref.shape[0]
    half = CONV_WIDTH // 2
    shifted_rows = rows + 2 * CONV_HALO - SUBLANES

    def glu(a, b):
        return a * _sigmoid(b)

    h_scr[0, 0:CONV_HALO, :] = glu(at_ref[...], bt_ref[...]) * has_top
    h_scr[0, CONV_HALO:CONV_HALO + rows, :] = glu(a_ref[...], b_ref[...])
    h_scr[0, CONV_HALO + rows:, :] = glu(ab_ref[...], bb_ref[...]) * has_bot
    for r in range(1, SUBLANES):
        h_scr[r, 0:shifted_rows, :] = h_scr[0, r:r + shifted_rows, :]
    for j in range(CONV_WIDTH):
        w_scr[j] = jnp.broadcast_to(w_ref[j:j + 1, :], (SUBLANES, CONV_CH))

    for r0 in range(0, rows, CONV_SUB):
        acc = None
        for j in range(CONV_WIDTH):
            lo = r0 + CONV_HALO - half + j
            shift = lo % SUBLANES
            weight = w_scr[j]
            window = h_scr[shift, lo - shift:lo - shift + CONV_SUB, :]
            term = (window.reshape(CONV_SUB // SUBLANES, SUBLANES, CONV_CH) * weight).reshape(CONV_SUB, CONV_CH)
            acc = term if acc is None else acc + term
        y = acc + bias_ref[...]
        mu = jnp.mean(y, axis=-1, keepdims=True)
        yc = y - mu
        var = jnp.mean(yc * yc, axis=-1, keepdims=True)
        yn = yc * lax.rsqrt(var + EPS) * lg_ref[...] + lb_ref[...]
        o_ref[r0:r0 + CONV_SUB, :] = (yn * _sigmoid(yn)).astype(BF16)


def _outproj_kernel(x_ref, g_ref, a_ref, r_ref, c_ref, wa_ref, wr_ref, wc_ref, o_ref):
    mix = _dot(a_ref[...], wa_ref[...]) + _dot(r_ref[...], wr_ref[...]) + _dot(c_ref[...], wc_ref[...])
    o_ref[...] = x_ref[...] + g_ref[...] * mix


def _outproj(x, mod, attn, rec, conv, w_out, layer, rows_per_cond):
    n = x.shape[0]
    tn = D_MODEL // 2
    row = lambda width: pl.BlockSpec((TM, width), lambda i, j: (i, 0))
    return pl.pallas_call(
        _outproj_kernel,
        grid=(n // TM, D_MODEL // tn),
        in_specs=[
            pl.BlockSpec((TM, tn), lambda i, j: (i, j)),
            pl.BlockSpec((None, None, 1, tn), lambda i, j: ((i * TM) // rows_per_cond, 5, 0, j)),
            row(D_ATTN), row(D_REC), row(CONV_CH),
            pl.BlockSpec((None, D_ATTN, tn), lambda i, j: (layer, 0, j)),
            pl.BlockSpec((None, D_REC, tn), lambda i, j: (layer, D_ATTN // D_REC, j)),
            pl.BlockSpec((None, CONV_CH, tn), lambda i, j: (layer, (D_ATTN + D_REC) // CONV_CH, j)),
        ],
        out_specs=pl.BlockSpec((TM, tn), lambda i, j: (i, j)),
        out_shape=jax.ShapeDtypeStruct((n, D_MODEL), F32),
        compiler_params=_params("parallel", "parallel"),
        name="outproj",
    )(x, mod, attn, rec, conv, w_out, w_out, w_out)


def kernel(x_prompt, x_sample, cache_k, cache_v, state_rec, c, c_ctx, w_ada, b_ada, norm_g, w_ffn_in, w_ffn_out,
           w_in, w_out, q_norm_g, k_norm_g, attn_sink, rec_lb_logits, rec_norm_g, conv_w, conv_b, conv_ln_g, conv_ln_b):
    batch, seq, _ = x_prompt.shape
    dec_batch, dec_seq, _ = x_sample.shape
    assert seq % TM == 0 or TM % seq == 0
    assert dec_seq % TM == 0 and dec_seq % LAT_Q_TILE == 0 and seq % CONV_ROWS == 0 and dec_seq % CONV_ROWS == 0

    wb_ffn_in, wb_ffn_out = w_ffn_in.astype(BF16), w_ffn_out.astype(BF16)
    wb_in, wb_out = w_in.astype(BF16), w_out.astype(BF16)

    lb_logits = rec_lb_logits.reshape(DEPTH, 2, N_REC_HEADS, 1, REC_DK)

    cond8 = jnp.zeros((SUBLANES, D_MODEL), F32).at[0].set(c_ctx).at[1:1 + dec_batch].set(c)
    mod = _modulation(cond8, w_ada, b_ada).reshape(DEPTH, SUBLANES, N_MOD, 1, D_MODEL)
    tables = _rope_tables(dec_seq)
    sinks = attn_sink.reshape(DEPTH, N_KV_HEADS, Q_PER_KV)
    rec_g = rec_norm_g.reshape(DEPTH, N_REC_HEADS, 1, REC_DV)

    n_ctx = batch * seq

    def ffn(x, mod_rows, l, which, rows_per_cond, conv_job=None):
        return _ffn(x, mod_rows, 6 * which, norm_g[l, 2 * which], wb_ffn_in, wb_ffn_out, l, which, rows_per_cond,
                    conv_job)

    hc = x_prompt.reshape(n_ctx, D_MODEL)
    hl = x_sample.reshape(dec_batch * dec_seq, D_MODEL)
    ks, vs, ss = [], [], []
    for l in range(DEPTH):
        mod_c, mod_l = mod[l, 0:1], mod[l, 1:1 + dec_batch]
        conv_params = (conv_w[l], conv_b[l], conv_ln_g[l], conv_ln_b[l])

        hc = ffn(hc, mod_c, l, 0, n_ctx)
        zc = _inproj(hc, mod_c, norm_g[l, 1], wb_in, l, n_ctx)
        hl, conv_c = ffn(hl, mod_l, l, 0, dec_seq, conv_job=(zc, *conv_params, seq))
        attn_c, k_l, v_l = _ctx_attention(zc, sinks[l], q_norm_g[l], k_norm_g[l], batch, seq)
        rec_c, s_l = _recurrence(zc, l, lb_logits, rec_g[l], None, batch, seq)
        hc = _outproj(hc, mod_c, attn_c, rec_c, conv_c, wb_out, l, n_ctx)
        ks.append(k_l)
        vs.append(v_l)
        ss.append(s_l)

        zl = _inproj(hl, mod_l, norm_g[l, 1], wb_in, l, dec_seq)
        hc, conv_l = ffn(hc, mod_c, l, 1, n_ctx, conv_job=(zl, *conv_params, dec_seq))
        attn_l = _lat_attention(zl, sinks[l], q_norm_g[l], k_norm_g[l], cache_k[:, l], cache_v[:, l], tables,
                                dec_batch, dec_seq)
        rec_l = _recurrence(zl, l, lb_logits, rec_g[l], state_rec[:, l], dec_batch, dec_seq)[0]
        hl = _outproj(hl, mod_l, attn_l, rec_l, conv_l, wb_out, l, dec_seq)
        hl = ffn(hl, mod_l, l, 1, dec_seq)

    y_prompt = hc.reshape(batch, seq, D_MODEL)
    y_sample = hl.reshape(dec_batch, dec_seq, D_MODEL)
    return (y_prompt, y_sample, jnp.stack(ks, axis=1), jnp.stack(vs, axis=1), jnp.stack(ss, axis=1))
```

```python
import functools

import jax
import jax.numpy as jnp
import numpy as np
from jax import lax
from jax.experimental import pallas as pl
from jax.experimental.pallas import tpu as pltpu

F32 = jnp.float32
BF16 = jnp.bfloat16

D_MODEL = 2048
DEPTH = 2
GRID_W = 64
HEAD_DIM = 128
D_ATTN = D_MODEL // 2
N_Q_HEADS = D_ATTN // HEAD_DIM
N_KV_HEADS = 2
Q_PER_KV = N_Q_HEADS // N_KV_HEADS
WINDOW = 128
ATTN_BLOCK = 128
ATTN_SCALE = HEAD_DIM ** -0.5
ROPE_BASE = 10000.0
MASK_VALUE = -1e30
D_REC = D_MODEL // 4
REC_DK = 128
REC_DV = 128
N_REC_HEADS = D_REC // REC_DV
REC_CHUNK = 32
CONV_CH = D_MODEL // 4
CONV_WIDTH = 31
D_FF = 5632
N_MOD = 9
EPS = 1e-6
GATE_FLOOR = 1e-30
LOG2_E = 1.4426950408889634
IN_COLS = 5120

COL_Q = 0
COL_K = 8
COL_V = 10
COL_RQ = 12
COL_RF_F = 16
COL_RF_B = 20
COL_RI = 24
COL_RG = 28
COL_CA = 32
COL_CB = 36

SUBLANES = 8
VMEM_LIMIT = 56 * 1024 * 1024

TM = 1024
TM_FFN = 1024
TF = 512
TN_FFN_OUT = 256
TN_IN = 1024
TN_ADA = 1024
CONV_HALO = 16
CONV_SUB = 32
LAT_Q_TILE = 1024
CTX_SEQS_PER_STEP = 2
REC_LOCAL_UNROLL = 8
REC_STATE_UNROLL = 16


def _params(*sem):
    return pltpu.CompilerParams(dimension_semantics=sem, vmem_limit_bytes=VMEM_LIMIT)


def _dot(a, b):
    return jnp.dot(a, b, preferred_element_type=F32)


def _dot_nt(a, b):
    return lax.dot_general(a, b, (((1,), (1,)), ((), ())), preferred_element_type=F32)


def _dot_tn(a, b):
    return lax.dot_general(a, b, (((0,), (0,)), ((), ())), preferred_element_type=F32)


def _rms(x, g):
    return x * lax.rsqrt(jnp.mean(x * x, axis=-1, keepdims=True) + EPS) * g


def _ada_norm(x, g, scale, shift):
    gain = g * (1.0 + scale)
    return x * lax.rsqrt(jnp.mean(x * x, axis=-1, keepdims=True) + EPS) * gain + shift


def _sigmoid(x):
    return 1.0 / (1.0 + jnp.exp(-x))


def _ada_kernel(c_ref, w_ref, b_ref, o_ref):
    c = c_ref[...]
    s = (c * _sigmoid(c)).astype(BF16)
    o_ref[...] = _dot(s, w_ref[...].astype(BF16)) + b_ref[...]


def _modulation(cond8, w_ada, b_ada):
    ncol = N_MOD * D_MODEL
    return pl.pallas_call(
        _ada_kernel,
        grid=(DEPTH, ncol // TN_ADA),
        in_specs=[
            pl.BlockSpec((SUBLANES, D_MODEL), lambda l, j: (0, 0)),
            pl.BlockSpec((None, D_MODEL, TN_ADA), lambda l, j: (l, 0, j)),
            pl.BlockSpec((None, 1, TN_ADA), lambda l, j: (l, 0, j)),
        ],
        out_specs=pl.BlockSpec((None, SUBLANES, TN_ADA), lambda l, j: (l, 0, j)),
        out_shape=jax.ShapeDtypeStruct((DEPTH, SUBLANES, ncol), F32),
        compiler_params=_params("parallel", "parallel"),
        name="modulation",
    )(cond8, w_ada, b_ada.reshape(DEPTH, 1, ncol))


def _mod_spec(chunk, rows_per_cond, tm=TM):
    return pl.BlockSpec((None, None, 1, D_MODEL), lambda i, j: ((i * tm) // rows_per_cond, chunk, 0, 0))


def _ffn_kernel(nf, x_ref, sh_ref, sc_ref, g_ref, ng_ref, wa_ref, wb_ref, wo_ref, o_ref, h_scr, act_scr):
    j = pl.program_id(1)

    @pl.when(j == 0)
    def _():
        h = _ada_norm(x_ref[...], ng_ref[...], sc_ref[...], sh_ref[...])
        h_scr[...] = h.astype(BF16)

    @pl.when(j < nf)
    def _():
        h = h_scr[...]
        a = _dot(h, wa_ref[...])
        b = _dot(h, wb_ref[...])
        act_scr[:, pl.ds(pl.multiple_of(j * TF, TF), TF)] = (a * _sigmoid(a) * b).astype(BF16)

    @pl.when(j >= nf)
    def _():
        cols = pl.ds(pl.multiple_of((j - nf) * TN_FFN_OUT, TN_FFN_OUT), TN_FFN_OUT)
        o_ref[...] = x_ref[:, cols] + 0.5 * g_ref[...] * _dot(act_scr[...], wo_ref[...])


def _ffn(x, mod, first_chunk, norm_g, w_in, w_out, layer, which, rows_per_cond):
    n = x.shape[0]
    nf = D_FF // TF
    nout = D_MODEL // TN_FFN_OUT
    hid = lambda j: jnp.minimum(j, nf - 1)
    out = lambda j: jnp.maximum(j - nf, 0)
    return pl.pallas_call(
        functools.partial(_ffn_kernel, nf),
        grid=(n // TM_FFN, nf + nout),
        in_specs=[
            pl.BlockSpec((TM_FFN, D_MODEL), lambda i, j: (i, 0)),
            _mod_spec(first_chunk, rows_per_cond, TM_FFN),
            _mod_spec(first_chunk + 1, rows_per_cond, TM_FFN),
            pl.BlockSpec((None, None, 1, TN_FFN_OUT),
                         lambda i, j: ((i * TM_FFN) // rows_per_cond, first_chunk + 2, 0, out(j))),
            pl.BlockSpec((1, D_MODEL), lambda i, j: (0, 0)),
            pl.BlockSpec((None, None, D_MODEL, TF), lambda i, j: (layer, which, 0, hid(j))),
            pl.BlockSpec((None, None, D_MODEL, TF), lambda i, j: (layer, which, 0, hid(j) + nf)),
            pl.BlockSpec((None, None, D_FF, TN_FFN_OUT), lambda i, j: (layer, which, 0, out(j))),
        ],
        out_specs=pl.BlockSpec((TM_FFN, TN_FFN_OUT), lambda i, j: (i, out(j))),
        out_shape=jax.ShapeDtypeStruct((n, D_MODEL), F32),
        scratch_shapes=[pltpu.VMEM((TM_FFN, D_MODEL), BF16), pltpu.VMEM((TM_FFN, D_FF), BF16)],
        compiler_params=_params("parallel", "arbitrary"),
        name="ffn",
    )(x, mod, mod, mod, norm_g.reshape(1, D_MODEL), w_in, w_in, w_out)


def _inproj_kernel(x_ref, sh_ref, sc_ref, ng_ref, w_ref, o_ref, h_scr):
    @pl.when(pl.program_id(1) == 0)
    def _():
        h = _ada_norm(x_ref[...], ng_ref[...], sc_ref[...], sh_ref[...])
        h_scr[...] = h.astype(BF16)

    o_ref[...] = _dot(h_scr[...], w_ref[...])


def _inproj(x, mod, norm_g, w_in, layer, rows_per_cond):
    n = x.shape[0]
    return pl.pallas_call(
        _inproj_kernel,
        grid=(n // TM, IN_COLS // TN_IN),
        in_specs=[
            pl.BlockSpec((TM, D_MODEL), lambda i, j: (i, 0)),
            _mod_spec(3, rows_per_cond),
            _mod_spec(4, rows_per_cond),
            pl.BlockSpec((1, D_MODEL), lambda i, j: (0, 0)),
            pl.BlockSpec((None, D_MODEL, TN_IN), lambda i, j: (layer, 0, j)),
        ],
        out_specs=pl.BlockSpec((TM, TN_IN), lambda i, j: (i, j)),
        out_shape=jax.ShapeDtypeStruct((n, IN_COLS), F32),
        scratch_shapes=[pltpu.VMEM((TM, D_MODEL), BF16)],
        compiler_params=_params("parallel", "arbitrary"),
        name="inproj",
    )(x, mod, mod, norm_g.reshape(1, D_MODEL), w_in)


QK_SCALE_LOG2 = ATTN_SCALE * LOG2_E


def _with_ones(v):
    return jnp.concatenate([v, jnp.ones_like(v)], axis=-1)


def _softmax_sink(scores, values, sink):
    sink2 = sink * LOG2_E
    m = sink2
    for s in scores:
        m = jnp.maximum(jnp.max(s, axis=-1, keepdims=True), m)
    acc = None
    for s, v in zip(scores, values):
        pv = _dot(jnp.exp2(s - m).astype(BF16), v)
        acc = pv if acc is None else acc + pv
    return acc[:, :HEAD_DIM] / (acc[:, HEAD_DIM:] + jnp.exp2(sink2 - m))


def _ctx_attn_kernel(sink_ref, q_ref, k_ref, v_ref, qg_ref, kg_ref, o_ref, kc_ref, vc_ref):
    kv = pl.program_id(1)
    nseq, seq = kc_ref.shape[0], kc_ref.shape[1]
    heads = [slice(g * HEAD_DIM, (g + 1) * HEAD_DIM) for g in range(Q_PER_KV)]
    keys, values = [], []
    for i in range(nseq):
        rows = slice(i * seq, (i + 1) * seq)
        kn = _rms(k_ref[rows, :], kg_ref[...])
        v = v_ref[rows, :]
        kc_ref[i] = kn
        vc_ref[i] = v
        keys.append(kn.astype(BF16))
        values.append(_with_ones(v.astype(BF16)))
    logits = [[_dot_nt((_rms(q_ref[i * seq:(i + 1) * seq, cols], qg_ref[...]) * QK_SCALE_LOG2).astype(BF16), keys[i])
               for cols in heads] for i in range(nseq)]
    for i in range(nseq):
        for g, cols in enumerate(heads):
            o = _softmax_sink([logits[i][g]], [values[i]], sink_ref[kv, g])
            o_ref[i * seq:(i + 1) * seq, cols] = o.astype(BF16)


def _ctx_attention(z, sink, q_g, k_g, batch, seq):
    qw = Q_PER_KV * HEAD_DIM
    per = CTX_SEQS_PER_STEP
    cache_shape = jax.ShapeDtypeStruct((batch, N_KV_HEADS, seq, HEAD_DIM), F32)
    cache_spec = pl.BlockSpec((per, None, seq, HEAD_DIM), lambda b, kv: (b, kv, 0, 0))
    return pl.pallas_call(
        _ctx_attn_kernel,
        grid=(batch // per, N_KV_HEADS),
        in_specs=[
            pl.BlockSpec(memory_space=pltpu.SMEM),
            pl.BlockSpec((per * seq, qw), lambda b, kv: (b, kv)),
            pl.BlockSpec((per * seq, HEAD_DIM), lambda b, kv: (b, COL_K + kv)),
            pl.BlockSpec((per * seq, HEAD_DIM), lambda b, kv: (b, COL_V + kv)),
            pl.BlockSpec((1, HEAD_DIM), lambda b, kv: (0, 0)),
            pl.BlockSpec((1, HEAD_DIM), lambda b, kv: (0, 0)),
        ],
        out_specs=[pl.BlockSpec((per * seq, qw), lambda b, kv: (b, kv)), cache_spec, cache_spec],
        out_shape=[jax.ShapeDtypeStruct((batch * seq, D_ATTN), BF16), cache_shape, cache_shape],
        compiler_params=_params("parallel", "parallel"),
        name="ctx_attention",
    )(sink, z, z, z, q_g.reshape(1, HEAD_DIM), k_g.reshape(1, HEAD_DIM))


def _rope(x, cos, sin_a, sin_b):
    quarter = HEAD_DIM // 4
    up = pltpu.roll(x, HEAD_DIM - quarter, 1)
    down = pltpu.roll(x, quarter, 1)
    return x * cos + up * sin_a + down * sin_b


def _lat_attn_kernel(sink_ref, q_ref, k_ref, v_ref, kc_ref, vc_ref, qg_ref, kg_ref,
                     cos_ref, sa_ref, sb_ref, cosq_ref, saq_ref, sbq_ref, o_ref, k_scr, v_scr, kc_scr, vc_scr):
    kv = pl.program_id(1)
    qt = pl.program_id(2)
    seq = k_ref.shape[0]
    span = 3 * ATTN_BLOCK

    @pl.when(qt == 0)
    def _():
        kn = _rms(k_ref[...], kg_ref[...])
        k_scr[...] = _rope(kn, cos_ref[...], sa_ref[...], sb_ref[...]).astype(BF16)
        v_scr[...] = _with_ones(v_ref[...].astype(BF16))
        kc_scr[...] = kc_ref[...].astype(BF16)
        vc_scr[...] = _with_ones(vc_ref[...].astype(BF16))

    kcb = kc_scr[...]
    vcb = vc_scr[...]
    nblk = LAT_Q_TILE // ATTN_BLOCK
    heads = [slice(g * HEAD_DIM, (g + 1) * HEAD_DIM) for g in range(Q_PER_KV)]
    stacked = Q_PER_KV * ATTN_BLOCK

    head_of_row = lax.broadcasted_iota(jnp.int32, (stacked, 1), 0) // ATTN_BLOCK
    sink_col = jnp.zeros((stacked, 1), F32)
    for g in range(Q_PER_KV):
        sink_col = jnp.where(head_of_row == g, sink_ref[kv, g], sink_col)

    def block_rows(blk):
        return slice(blk * ATTN_BLOCK, (blk + 1) * ATTN_BLOCK)

    def logits_of(blk):
        rows = block_rows(blk)
        q0 = (qt * nblk + blk) * ATTN_BLOCK
        start = pl.multiple_of(jnp.clip(q0 - ATTN_BLOCK, 0, seq - span), ATTN_BLOCK)
        qpos = q0 + lax.broadcasted_iota(jnp.int32, (stacked, span), 0) % ATTN_BLOCK
        kpos = start + lax.broadcasted_iota(jnp.int32, (stacked, span), 1)
        valid = jnp.abs(qpos - kpos) <= WINDOW
        cos, sa, sb = cosq_ref[rows, :], saq_ref[rows, :], sbq_ref[rows, :]
        qn = jnp.concatenate([_rope(_rms(q_ref[rows, cols], qg_ref[...]), cos, sa, sb) for cols in heads], axis=0)
        qn = (qn * QK_SCALE_LOG2).astype(BF16)
        s_loc = jnp.where(valid, _dot_nt(qn, k_scr[pl.ds(start, span), :]), MASK_VALUE)
        return s_loc, _dot_nt(qn, kcb), start

    pending = logits_of(0)
    for blk in range(nblk):
        upcoming = logits_of(blk + 1) if blk + 1 < nblk else None
        s_loc, s_ctx, start = pending
        o = _softmax_sink([s_loc, s_ctx], [v_scr[pl.ds(start, span), :], vcb], sink_col)
        for g, cols in enumerate(heads):
            o_ref[block_rows(blk), cols] = o[g * ATTN_BLOCK:(g + 1) * ATTN_BLOCK, :].astype(BF16)
        pending = upcoming


def _rope_tables(seq):
    quarter = HEAD_DIM // 4
    pos = np.arange(seq)
    inv_freq = ROPE_BASE ** (-np.arange(quarter, dtype=np.float32) / quarter)
    inv_freq = jnp.asarray(inv_freq, F32)
    zero = jnp.zeros((seq, quarter), F32)

    def trig(p):
        ang = jnp.asarray(p, F32)[:, None] * inv_freq
        return jnp.cos(ang), jnp.sin(ang)

    cr, sr = trig(pos // GRID_W)
    cc, sc = trig(pos % GRID_W)
    cos = jnp.concatenate([cr, cr, cc, cc], axis=-1)
    sin_a = jnp.concatenate([-sr, zero, -sc, zero], axis=-1)
    sin_b = jnp.concatenate([zero, sr, zero, sc], axis=-1)
    return cos, sin_a, sin_b


def _lat_attention(z, sink, q_g, k_g, k_ctx, v_ctx, tables, batch, seq):
    qw = Q_PER_KV * HEAD_DIM
    nqt = seq // LAT_Q_TILE
    past = k_ctx.shape[2]
    cos, sin_a, sin_b = tables
    full_tab = pl.BlockSpec((seq, HEAD_DIM), lambda b, kv, qt: (0, 0))
    tile_tab = pl.BlockSpec((LAT_Q_TILE, HEAD_DIM), lambda b, kv, qt: (qt, 0))
    ctx_spec = pl.BlockSpec((None, None, past, HEAD_DIM), lambda b, kv, qt: (b, kv, 0, 0))
    gain = pl.BlockSpec((1, HEAD_DIM), lambda b, kv, qt: (0, 0))
    return pl.pallas_call(
        _lat_attn_kernel,
        grid=(batch, N_KV_HEADS, nqt),
        in_specs=[
            pl.BlockSpec(memory_space=pltpu.SMEM),
            pl.BlockSpec((LAT_Q_TILE, qw), lambda b, kv, qt: (b * nqt + qt, kv)),
            pl.BlockSpec((seq, HEAD_DIM), lambda b, kv, qt: (b, COL_K + kv)),
            pl.BlockSpec((seq, HEAD_DIM), lambda b, kv, qt: (b, COL_V + kv)),
            ctx_spec, ctx_spec, gain, gain,
            full_tab, full_tab, full_tab, tile_tab, tile_tab, tile_tab,
        ],
        out_specs=pl.BlockSpec((LAT_Q_TILE, qw), lambda b, kv, qt: (b * nqt + qt, kv)),
        out_shape=jax.ShapeDtypeStruct((batch * seq, D_ATTN), BF16),
        scratch_shapes=[pltpu.VMEM((seq, HEAD_DIM), BF16), pltpu.VMEM((seq, 2 * HEAD_DIM), BF16),
                        pltpu.VMEM((past, HEAD_DIM), BF16), pltpu.VMEM((past, 2 * HEAD_DIM), BF16)],
        compiler_params=_params("parallel", "parallel", "arbitrary"),
        name="lat_attention",
    )(sink, z, z, z, k_ctx, v_ctx, q_g.reshape(1, HEAD_DIM), k_g.reshape(1, HEAD_DIM),
      cos, sin_a, sin_b, cos, sin_a, sin_b)


def _split3(x):
    hi = x.astype(BF16)
    r1 = x - hi.astype(F32)
    mid = r1.astype(BF16)
    lo = (r1 - mid.astype(F32)).astype(BF16)
    return hi, mid, lo


def _rec_gates(forward, zf, lb):
    c = REC_CHUNK
    t = jnp.exp(-jnp.abs(zf))
    r = 1.0 / (1.0 + t)
    tr = t * r
    nonneg = zf >= 0
    sig_pos = jnp.where(nonneg, r, tr)
    sig_neg = jnp.where(nonneg, tr, r)
    one_m_lb = 1.0 - lb
    log_f = jnp.log(jnp.maximum(lb + one_m_lb * sig_pos, GATE_FLOOR))
    k = one_m_lb * sig_neg
    ri = lax.broadcasted_iota(jnp.int32, (c, c), 0)
    ci = lax.broadcasted_iota(jnp.int32, (c, c), 1)
    tri = jnp.where((ci <= ri) if forward else (ci >= ri), 1.0, 0.0).astype(BF16)
    hi, mid, lo = _split3(log_f)
    cum2 = (_dot(tri, hi) + _dot(tri, mid) + _dot(tri, lo)) * LOG2_E
    return cum2 - jnp.log2(k), cum2


def _rec_same_block(forward, q, key_row, cum2):
    sub_row = lax.broadcasted_iota(jnp.int32, (SUBLANES, REC_DK), 0)
    pieces = []
    for b in range(REC_CHUNK // SUBLANES):
        blk = slice(b * SUBLANES, (b + 1) * SUBLANES)
        cum_b, q_b = cum2[blk, :], q[blk, :]
        for sl in range(SUBLANES):
            s = b * SUBLANES + sl
            keep = (sub_row >= sl) if forward else (sub_row <= sl)
            pieces.append(jnp.where(keep, jnp.exp2(cum_b - key_row(s)) * q_b, 0.0))
    return _dot(jnp.concatenate(pieces, axis=0).astype(BF16), jnp.ones((REC_DK, REC_DV), BF16))


def _rec_cross_block(forward, q, key2, v, cum2):
    c = REC_CHUNK
    q_parts, k_parts, v_parts, segments = [], [], [], []
    for tb in range(c // SUBLANES):
        blk = slice(tb * SUBLANES, (tb + 1) * SUBLANES)
        src, ref_row = (slice(0, blk.start), blk.start - 1) if forward else (slice(blk.stop, c), blk.stop)
        if src.stop == src.start:
            q_parts.append(jnp.zeros((SUBLANES, REC_DK), F32))
            continue
        ref = cum2[ref_row:ref_row + 1, :]
        q_parts.append(q[blk, :] * jnp.exp2(cum2[blk, :] - ref))
        k_parts.append(jnp.exp2(ref - key2[src, :]))
        v_parts.append(v[src, :])
        segments.append((tb, src.stop - src.start))
    a = _dot_nt(jnp.concatenate(q_parts, axis=0).astype(BF16), jnp.concatenate(k_parts, axis=0).astype(BF16))
    return a, segments, jnp.concatenate(v_parts, axis=0).astype(BF16)


def _rec_cross_apply(a, segments, v_all):
    c, ncol = a.shape
    row_blk = lax.broadcasted_iota(jnp.int32, (c, ncol), 0) // SUBLANES
    col = lax.broadcasted_iota(jnp.int32, (c, ncol), 1)
    col_blk = jnp.full((c, ncol), -1, jnp.int32)
    start = 0
    for tb, width in segments:
        col_blk = jnp.where((col >= start) & (col < start + width), tb, col_blk)
        start += width
    return _dot(jnp.where(row_blk == col_blk, a, 0.0).astype(BF16), v_all)


def _rec_same_apply(lane_sums, v_row):
    o_blocks = []
    for b in range(REC_CHUNK // SUBLANES):
        acc = None
        for sl in range(SUBLANES):
            s = b * SUBLANES + sl
            term = lane_sums[s * SUBLANES:(s + 1) * SUBLANES, :] * v_row(s)
            acc = term if acc is None else acc + term
        o_blocks.append(acc)
    return jnp.concatenate(o_blocks, axis=0)


def _rec_kernel(layer, has_init, *refs):
    q_ref, ff_ref, fb_ref, v_ref, g_ref, lg_ref, ng_ref = refs[:7]
    refs = refs[7:]
    if has_init:
        s0_ref, refs = refs[0], refs[1:]
    conv_in, refs = refs[:10], refs[10:]
    o_ref, conv_o_ref = refs[:2]
    refs = refs[2:]
    if not has_init:
        s_out_ref, refs = refs[0], refs[1:]
    o_scr, qd_scr, kd_scr, gl_scr, oi_scr, key_scr, glu_scr, conv_h_scr, conv_w_scr = refs
    seq = q_ref.shape[0]
    nchunk = seq // REC_CHUNK
    c = REC_CHUNK

    logits = lg_ref[...]
    e = jnp.exp(logits - jnp.max(logits, axis=0, keepdims=True))
    p = e / jnp.sum(e, axis=0, keepdims=True)
    lb = jnp.zeros_like(p[0])
    for i in range(1, layer + 1):
        lb = lb + p[i]

    def rows_of(ci):
        return pl.ds(pl.multiple_of(ci * c, c), c)

    z_refs = (ff_ref, fb_ref)

    def local_body(it, carry):
        chunks = [it * REC_LOCAL_UNROLL + u for u in range(REC_LOCAL_UNROLL)]
        qs = [q_ref[rows_of(ci), :] for ci in chunks]
        vs = [v_ref[rows_of(ci), :] for ci in chunks]
        items = [(u, d) for u in range(REC_LOCAL_UNROLL) for d in range(2)]
        gates = [_rec_gates(d == 0, z_refs[d][rows_of(chunks[u]), :], lb[d]) for u, d in items]
        for idx, (key2, _) in enumerate(gates):
            key_scr[idx] = key2
        key_rows = [lambda s, idx=idx: key_scr[idx, s:s + 1, :] for idx in range(len(items))]
        v_rows = [lambda s, ci=ci: v_ref[pl.ds(ci * c + s, 1), :] for ci in chunks]
        same = [_rec_same_block(d == 0, qs[u], key_rows[idx], cum2)
                for idx, ((u, d), (_, cum2)) in enumerate(zip(items, gates))]
        cross = [_rec_cross_block(d == 0, qs[u], key2, vs[u], cum2) for (u, d), (key2, cum2) in zip(items, gates)]
        totals = [None] * REC_LOCAL_UNROLL
        for (u, d), (key2, cum2), lane_sums, (a, segments, v_all) in zip(items, gates, same, cross):
            rows = rows_of(chunks[u])
            last = cum2[c - 1:c, :] if d == 0 else cum2[0:1, :]
            qd_scr[d, rows, :] = (qs[u] * jnp.exp2(cum2)).astype(BF16)
            kd_scr[d, rows, :] = jnp.exp2(last - key2).astype(BF16)
            gl_scr[d, chunks[u]] = jnp.broadcast_to(jnp.exp2(last), (SUBLANES, REC_DK))
            o_local = _rec_same_apply(lane_sums, v_rows[u]) + _rec_cross_apply(a, segments, v_all)
            totals[u] = o_local if totals[u] is None else totals[u] + o_local
        for u, ci in enumerate(chunks):
            o_scr[rows_of(ci), :] = totals[u]
        return carry

    lax.fori_loop(0, nchunk // REC_LOCAL_UNROLL, local_body, 0)

    if has_init:
        init = (s0_ref[0].T, s0_ref[1].T)
    else:
        init = (jnp.zeros((REC_DV, REC_DK), F32),) * 2

    state_unroll = min(REC_STATE_UNROLL, nchunk)
    conv_rows = conv_o_ref.shape[0] // (nchunk // state_unroll)
    _conv_stage(pl.program_id(1), N_REC_HEADS, *conv_in[:7], glu_scr, conv_w_scr)

    def state_body(it, states):
        offset = pl.multiple_of(it * conv_rows, conv_rows)
        conv_h_scr[0] = glu_scr[pl.ds(offset, conv_rows + 2 * CONV_HALO), :]

        def store(r0, val):
            conv_o_ref[pl.ds(pl.multiple_of(offset + r0, CONV_SUB), CONV_SUB), :] = val

        _conv_rows(conv_rows, conv_h_scr, conv_w_scr, *conv_in[7:], store)
        states = list(states)
        steps = [it * state_unroll + u for u in range(state_unroll)]
        order = [(d, step if d == 0 else nchunk - 1 - step) for step in steps for d in range(2)]
        updates = [_dot_tn(v_ref[rows_of(ci), :].astype(BF16), kd_scr[d, rows_of(ci), :]) for d, ci in order]
        for (d, ci), update in zip(order, updates):
            st = states[d]
            oi_scr[d, rows_of(ci), :] = _dot_nt(qd_scr[d, rows_of(ci), :], st.astype(BF16))
            decayed = (st.reshape(REC_DV // SUBLANES, SUBLANES, REC_DK) * gl_scr[d, ci]).reshape(REC_DV, REC_DK)
            states[d] = decayed + update
        return tuple(states)

    final = lax.fori_loop(0, nchunk // state_unroll, state_body, init)
    if not has_init:
        s_out_ref[0] = final[0].T
        s_out_ref[1] = final[1].T

    def out_body(ci, carry):
        rows = rows_of(ci)
        gate = g_ref[rows, :]
        o = o_scr[rows, :] + oi_scr[0, rows, :] + oi_scr[1, rows, :]
        o_ref[rows, :] = (_rms(o, ng_ref[...]) * (gate * _sigmoid(gate))).astype(BF16)
        return carry

    lax.fori_loop(0, nchunk, out_body, 0, unroll=4)


def _recurrence(z, layer, lb_logits, norm_g, s0, conv_params, batch, seq):
    has_init = s0 is not None
    n = batch * seq
    col = lambda base: pl.BlockSpec((seq, REC_DK), lambda b, h: (b, base + h))
    state_spec = pl.BlockSpec((None, 2, None, REC_DK, REC_DV), lambda b, h: (b, 0, h, 0, 0))
    in_specs = [col(COL_RQ), col(COL_RF_F), col(COL_RF_B), col(COL_RI), col(COL_RG),
                pl.BlockSpec((DEPTH, 2, None, 1, REC_DK), lambda b, h: (0, 0, h, 0, 0)),
                pl.BlockSpec((None, 1, REC_DV), lambda b, h: (h, 0, 0))]
    args = [z, z, z, z, z, lb_logits, norm_g]
    if has_init:
        in_specs.append(state_spec)
        args.append(s0)

    conv_step = seq // N_REC_HEADS
    state_trips = (seq // REC_CHUNK) // min(REC_STATE_UNROLL, seq // REC_CHUNK)
    conv_rows = conv_step // state_trips
    assert conv_step % CONV_HALO == 0 and conv_rows % CONV_SUB == 0
    cw = CONV_CH // HEAD_DIM
    halo_per_step = conv_step // CONV_HALO
    nhalo = n // CONV_HALO
    part = lambda b, h: b * N_REC_HEADS + h
    mid_mode = dict(pipeline_mode=pl.Buffered(1)) if conv_step * CONV_CH * 4 >= (1 << 20) else {}
    mid = lambda cc: pl.BlockSpec((conv_step, CONV_CH), lambda b, h: (part(b, h), cc), **mid_mode)
    top = lambda cc: pl.BlockSpec(
        (CONV_HALO, CONV_CH), lambda b, h: (jnp.maximum(part(b, h) * halo_per_step - 1, 0), cc))
    bot = lambda cc: pl.BlockSpec(
        (CONV_HALO, CONV_CH), lambda b, h: (jnp.minimum((part(b, h) + 1) * halo_per_step, nhalo - 1), cc))
    vec = pl.BlockSpec((1, CONV_CH), lambda b, h: (0, 0))
    ca, cb = COL_CA // cw, COL_CB // cw
    w, bias, ln_g, ln_b = conv_params
    in_specs += [mid(ca), mid(cb), top(ca), top(cb), bot(ca), bot(cb),
                 pl.BlockSpec((CONV_WIDTH, CONV_CH), lambda b, h: (0, 0)), vec, vec, vec]
    args += [z, z, z, z, z, z, w, bias.reshape(1, CONV_CH), ln_g.reshape(1, CONV_CH), ln_b.reshape(1, CONV_CH)]

    out_specs = [pl.BlockSpec((seq, REC_DV), lambda b, h: (b, h)),
                 pl.BlockSpec((conv_step, CONV_CH), lambda b, h: (part(b, h), 0))]
    out_shape = [jax.ShapeDtypeStruct((n, D_REC), BF16), jax.ShapeDtypeStruct((n, CONV_CH), BF16)]
    if not has_init:
        out_specs.append(state_spec)
        out_shape.append(jax.ShapeDtypeStruct((batch, 2, N_REC_HEADS, REC_DK, REC_DV), F32))
    return pl.pallas_call(
        functools.partial(_rec_kernel, layer, has_init),
        grid=(batch, N_REC_HEADS),
        in_specs=in_specs,
        out_specs=out_specs,
        out_shape=out_shape,
        scratch_shapes=[pltpu.VMEM((seq, REC_DV), F32),
                        pltpu.VMEM((2, seq, REC_DK), BF16),
                        pltpu.VMEM((2, seq, REC_DK), BF16),
                        pltpu.VMEM((2, seq // REC_CHUNK, SUBLANES, REC_DK), F32),
                        pltpu.VMEM((2, seq, REC_DV), F32),
                        pltpu.VMEM((2 * REC_LOCAL_UNROLL, REC_CHUNK, REC_DK), F32),
                        pltpu.VMEM((conv_step + 2 * CONV_HALO, CONV_CH), F32),
                        pltpu.VMEM((SUBLANES, conv_rows + 2 * CONV_HALO, CONV_CH), F32),
                        pltpu.VMEM((CONV_WIDTH, SUBLANES, CONV_CH), F32)],
        compiler_params=_params("parallel", "parallel"),
        name="hgrn2_conv",
    )(*args)


def _conv_stage(part, nparts, a_ref, b_ref, at_ref, bt_ref, ab_ref, bb_ref, w_ref, glu_scr, w_scr):
    has_top = jnp.where(part > 0, 1.0, 0.0)
    has_bot = jnp.where(part < nparts - 1, 1.0, 0.0)
    rows = a_ref.shape[0]

    def glu(a, b):
        return a * _sigmoid(b)

    glu_scr[0:CONV_HALO, :] = glu(at_ref[...], bt_ref[...]) * has_top
    glu_scr[CONV_HALO:CONV_HALO + rows, :] = glu(a_ref[...], b_ref[...])
    glu_scr[CONV_HALO + rows:, :] = glu(ab_ref[...], bb_ref[...]) * has_bot
    for j in range(CONV_WIDTH):
        w_scr[j] = jnp.broadcast_to(w_ref[j:j + 1, :], (SUBLANES, CONV_CH))


def _conv_rows(rows, h_scr, w_scr, bias_ref, lg_ref, lb_ref, store):
    half = CONV_WIDTH // 2
    shifted_rows = rows + 2 * CONV_HALO - SUBLANES
    for r in range(1, SUBLANES):
        h_scr[r, 0:shifted_rows, :] = h_scr[0, r:r + shifted_rows, :]
    for r0 in range(0, rows, CONV_SUB):
        acc = None
        for j in range(CONV_WIDTH):
            lo = r0 + CONV_HALO - half + j
            shift = lo % SUBLANES
            weight = w_scr[j]
            window = h_scr[shift, lo - shift:lo - shift + CONV_SUB, :]
            term = (window.reshape(CONV_SUB // SUBLANES, SUBLANES, CONV_CH) * weight).reshape(CONV_SUB, CONV_CH)
            acc = term if acc is None else acc + term
        y = acc + bias_ref[...]
        mu = jnp.mean(y, axis=-1, keepdims=True)
        yc = y - mu
        var = jnp.mean(yc * yc, axis=-1, keepdims=True)
        yn = yc * lax.rsqrt(var + EPS) * lg_ref[...] + lb_ref[...]
        store(r0, (yn * _sigmoid(yn)).astype(BF16))


def _outproj_kernel(x_ref, g_ref, a_ref, r_ref, c_ref, wa_ref, wr_ref, wc_ref, o_ref):
    mix = _dot(a_ref[...], wa_ref[...]) + _dot(r_ref[...], wr_ref[...]) + _dot(c_ref[...], wc_ref[...])
    o_ref[...] = x_ref[...] + g_ref[...] * mix


def _outproj(x, mod, attn, rec, conv, w_out, layer, rows_per_cond):
    n = x.shape[0]
    tn = D_MODEL // 2
    row = lambda width: pl.BlockSpec((TM, width), lambda i, j: (i, 0))
    return pl.pallas_call(
        _outproj_kernel,
        grid=(n // TM, D_MODEL // tn),
        in_specs=[
            pl.BlockSpec((TM, tn), lambda i, j: (i, j)),
            pl.BlockSpec((None, None, 1, tn), lambda i, j: ((i * TM) // rows_per_cond, 5, 0, j)),
            row(D_ATTN), row(D_REC), row(CONV_CH),
            pl.BlockSpec((None, D_ATTN, tn), lambda i, j: (layer, 0, j)),
            pl.BlockSpec((None, D_REC, tn), lambda i, j: (layer, D_ATTN // D_REC, j)),
            pl.BlockSpec((None, CONV_CH, tn), lambda i, j: (layer, (D_ATTN + D_REC) // CONV_CH, j)),
        ],
        out_specs=pl.BlockSpec((TM, tn), lambda i, j: (i, j)),
        out_shape=jax.ShapeDtypeStruct((n, D_MODEL), F32),
        compiler_params=_params("parallel", "parallel"),
        name="outproj",
    )(x, mod, attn, rec, conv, w_out, w_out, w_out)


def _trunk_layer(x, mod, l, P, mixers, rows_per_cond):
    x = _ffn(x, mod, 0, P['norm_g'][l, 0], P['w_ffn_in'], P['w_ffn_out'], l, 0, rows_per_cond)
    z = _inproj(x, mod, P['norm_g'][l, 1], P['w_in'], l, rows_per_cond)
    attn, rec, conv, extras = mixers(z)
    x = _outproj(x, mod, attn, rec, conv, P['w_out'], l, rows_per_cond)
    x = _ffn(x, mod, 6, P['norm_g'][l, 2], P['w_ffn_in'], P['w_ffn_out'], l, 1, rows_per_cond)
    return x, extras


def kernel(x_prompt, x_sample, cache_k, cache_v, state_rec, c, c_ctx, w_ada, b_ada, norm_g, w_ffn_in, w_ffn_out,
           w_in, w_out, q_norm_g, k_norm_g, attn_sink, rec_lb_logits, rec_norm_g, conv_w, conv_b, conv_ln_g, conv_ln_b):
    batch, seq, _ = x_prompt.shape
    dec_batch, dec_seq, _ = x_sample.shape
    assert seq % TM == 0 or TM % seq == 0
    assert dec_seq % TM == 0 and dec_seq % LAT_Q_TILE == 0

    P = {'norm_g': norm_g, 'w_ffn_in': w_ffn_in.astype(BF16), 'w_ffn_out': w_ffn_out.astype(BF16),
         'w_in': w_in.astype(BF16), 'w_out': w_out.astype(BF16)}

    lb_logits = rec_lb_logits.reshape(DEPTH, 2, N_REC_HEADS, 1, REC_DK)

    cond8 = jnp.zeros((SUBLANES, D_MODEL), F32).at[0].set(c_ctx).at[1:1 + dec_batch].set(c)
    mod = _modulation(cond8, w_ada, b_ada).reshape(DEPTH, SUBLANES, N_MOD, 1, D_MODEL)
    tables = _rope_tables(dec_seq)
    sinks = attn_sink.reshape(DEPTH, N_KV_HEADS, Q_PER_KV)
    rec_g = rec_norm_g.reshape(DEPTH, N_REC_HEADS, 1, REC_DV)

    conv_params = lambda l: (conv_w[l], conv_b[l], conv_ln_g[l], conv_ln_b[l])

    h = x_prompt.reshape(batch * seq, D_MODEL)
    ks, vs, ss = [], [], []
    for l in range(DEPTH):
        def ctx_mixers(z, l=l):
            attn, k_l, v_l = _ctx_attention(z, sinks[l], q_norm_g[l], k_norm_g[l], batch, seq)
            rec, conv, s_l = _recurrence(z, l, lb_logits, rec_g[l], None, conv_params(l), batch, seq)
            return attn, rec, conv, (k_l, v_l, s_l)

        h, (k_l, v_l, s_l) = _trunk_layer(h, mod[l, 0:1], l, P, ctx_mixers, batch * seq)
        ks.append(k_l)
        vs.append(v_l)
        ss.append(s_l)
    y_prompt = h.reshape(batch, seq, D_MODEL)

    h = x_sample.reshape(dec_batch * dec_seq, D_MODEL)
    for l in range(DEPTH):
        def lat_mixers(z, l=l):
            attn = _lat_attention(z, sinks[l], q_norm_g[l], k_norm_g[l], cache_k[:, l], cache_v[:, l], tables,
                                  dec_batch, dec_seq)
            rec, conv = _recurrence(z, l, lb_logits, rec_g[l], state_rec[:, l], conv_params(l), dec_batch, dec_seq)
            return attn, rec, conv, None

        h, _ = _trunk_layer(h, mod[l, 1:1 + dec_batch], l, P, lat_mixers, dec_seq)
    y_sample = h.reshape(dec_batch, dec_seq, D_MODEL)

    return (y_prompt, y_sample, jnp.stack(ks, axis=1), jnp.stack(vs, axis=1), jnp.stack(ss, axis=1))
```

```python
import functools

import jax
import jax.numpy as jnp
import numpy as np
from jax import lax
from jax.experimental import pallas as pl
from jax.experimental.pallas import tpu as pltpu

F32 = jnp.float32
BF16 = jnp.bfloat16

D_MODEL = 2048
DEPTH = 2
GRID_W = 64
HEAD_DIM = 128
D_ATTN = D_MODEL // 2
N_Q_HEADS = D_ATTN // HEAD_DIM
N_KV_HEADS = 2
Q_PER_KV = N_Q_HEADS // N_KV_HEADS
WINDOW = 128
ATTN_BLOCK = 128
ATTN_SCALE = HEAD_DIM ** -0.5
ROPE_BASE = 10000.0
MASK_VALUE = -1e30
D_REC = D_MODEL // 4
REC_DK = 128
REC_DV = 128
N_REC_HEADS = D_REC // REC_DV
REC_CHUNK = 32
CONV_CH = D_MODEL // 4
CONV_WIDTH = 31
D_FF = 5632
N_MOD = 9
EPS = 1e-6
GATE_FLOOR = 1e-30
LOG2_E = 1.4426950408889634
IN_COLS = 5120

COL_Q = 0
COL_K = 8
COL_V = 10
COL_RQ = 12
COL_RF_F = 16
COL_RF_B = 20
COL_RI = 24
COL_RG = 28
COL_CA = 32
COL_CB = 36

SUBLANES = 8
V7X_VMEM_BYTES = 64 * 1024 * 1024
VMEM_LIMIT = V7X_VMEM_BYTES * 7 // 8
SINGLE_BUFFER_BYTES = 1024 * 1024

TM = 1024
TM_FFN = 1024
TF = 512
TN_FFN_OUT = 256
TN_IN = 1024
TN_ADA = 1024
CONV_HALO = 16
CONV_SUB = 32
LAT_Q_TILE = 1024
CTX_SEQS_PER_STEP = 4
REC_LOCAL_UNROLL = 8
REC_STATE_UNROLL = 16


def _params(*sem):
    return pltpu.CompilerParams(dimension_semantics=sem, vmem_limit_bytes=VMEM_LIMIT)


def _dot(a, b):
    return jnp.dot(a, b, preferred_element_type=F32)


def _dot_nt(a, b):
    return lax.dot_general(a, b, (((1,), (1,)), ((), ())), preferred_element_type=F32)


def _dot_tn(a, b):
    return lax.dot_general(a, b, (((0,), (0,)), ((), ())), preferred_element_type=F32)


def _rms(x, g):
    return x * lax.rsqrt(jnp.mean(x * x, axis=-1, keepdims=True) + EPS) * g


def _ada_norm(x, g, scale, shift):
    gain = g * (1.0 + scale)
    return x * lax.rsqrt(jnp.mean(x * x, axis=-1, keepdims=True) + EPS) * gain + shift


def _sigmoid(x):
    return 1.0 / (1.0 + jnp.exp(-x))


def _ada_kernel(c_ref, w_ref, b_ref, o_ref):
    c = c_ref[...]
    s = (c * _sigmoid(c)).astype(BF16)
    o_ref[...] = _dot(s, w_ref[...].astype(BF16)) + b_ref[...]


def _modulation(cond8, w_ada, b_ada):
    ncol = N_MOD * D_MODEL
    return pl.pallas_call(
        _ada_kernel,
        grid=(DEPTH, ncol // TN_ADA),
        in_specs=[
            pl.BlockSpec((SUBLANES, D_MODEL), lambda l, j: (0, 0)),
            pl.BlockSpec((None, D_MODEL, TN_ADA), lambda l, j: (l, 0, j)),
            pl.BlockSpec((None, 1, TN_ADA), lambda l, j: (l, 0, j)),
        ],
        out_specs=pl.BlockSpec((None, SUBLANES, TN_ADA), lambda l, j: (l, 0, j)),
        out_shape=jax.ShapeDtypeStruct((DEPTH, SUBLANES, ncol), F32),
        compiler_params=_params("parallel", "parallel"),
        name="modulation",
    )(cond8, w_ada, b_ada.reshape(DEPTH, 1, ncol))


def _mod_spec(chunk, rows_per_cond, tm=TM):
    return pl.BlockSpec((None, None, 1, D_MODEL), lambda i, j: ((i * tm) // rows_per_cond, chunk, 0, 0))


def _ffn_kernel(nf, x_ref, sh_ref, sc_ref, g_ref, ng_ref, wa_ref, wb_ref, wo_ref, o_ref, h_scr, act_scr):
    j = pl.program_id(1)

    @pl.when(j == 0)
    def _():
        h = _ada_norm(x_ref[...], ng_ref[...], sc_ref[...], sh_ref[...])
        h_scr[...] = h.astype(BF16)

    @pl.when(j < nf)
    def _():
        h = h_scr[...]
        a = _dot(h, wa_ref[...])
        b = _dot(h, wb_ref[...])
        act_scr[:, pl.ds(pl.multiple_of(j * TF, TF), TF)] = (a * _sigmoid(a) * b).astype(BF16)

    @pl.when(j >= nf)
    def _():
        cols = pl.ds(pl.multiple_of((j - nf) * TN_FFN_OUT, TN_FFN_OUT), TN_FFN_OUT)
        o_ref[...] = x_ref[:, cols] + 0.5 * g_ref[...] * _dot(act_scr[...], wo_ref[...])


def _ffn(x, mod, first_chunk, norm_g, w_in, w_out, layer, which, rows_per_cond):
    n = x.shape[0]
    nf = D_FF // TF
    nout = D_MODEL // TN_FFN_OUT
    hid = lambda j: jnp.minimum(j, nf - 1)
    out = lambda j: jnp.maximum(j - nf, 0)
    return pl.pallas_call(
        functools.partial(_ffn_kernel, nf),
        grid=(n // TM_FFN, nf + nout),
        in_specs=[
            pl.BlockSpec((TM_FFN, D_MODEL), lambda i, j: (i, 0)),
            _mod_spec(first_chunk, rows_per_cond, TM_FFN),
            _mod_spec(first_chunk + 1, rows_per_cond, TM_FFN),
            pl.BlockSpec((None, None, 1, TN_FFN_OUT),
                         lambda i, j: ((i * TM_FFN) // rows_per_cond, first_chunk + 2, 0, out(j))),
            pl.BlockSpec((1, D_MODEL), lambda i, j: (0, 0)),
            pl.BlockSpec((None, None, D_MODEL, TF), lambda i, j: (layer, which, 0, hid(j))),
            pl.BlockSpec((None, None, D_MODEL, TF), lambda i, j: (layer, which, 0, hid(j) + nf)),
            pl.BlockSpec((None, None, D_FF, TN_FFN_OUT), lambda i, j: (layer, which, 0, out(j))),
        ],
        out_specs=pl.BlockSpec((TM_FFN, TN_FFN_OUT), lambda i, j: (i, out(j))),
        out_shape=jax.ShapeDtypeStruct((n, D_MODEL), F32),
        scratch_shapes=[pltpu.VMEM((TM_FFN, D_MODEL), BF16), pltpu.VMEM((TM_FFN, D_FF), BF16)],
        compiler_params=_params("parallel", "arbitrary"),
        name="ffn",
    )(x, mod, mod, mod, norm_g.reshape(1, D_MODEL), w_in, w_in, w_out)


def _inproj_kernel(x_ref, sh_ref, sc_ref, ng_ref, w_ref, o_ref, h_scr):
    @pl.when(pl.program_id(1) == 0)
    def _():
        h = _ada_norm(x_ref[...], ng_ref[...], sc_ref[...], sh_ref[...])
        h_scr[...] = h.astype(BF16)

    o_ref[...] = _dot(h_scr[...], w_ref[...])


def _inproj(x, mod, norm_g, w_in, layer, rows_per_cond):
    n = x.shape[0]
    return pl.pallas_call(
        _inproj_kernel,
        grid=(n // TM, IN_COLS // TN_IN),
        in_specs=[
            pl.BlockSpec((TM, D_MODEL), lambda i, j: (i, 0)),
            _mod_spec(3, rows_per_cond),
            _mod_spec(4, rows_per_cond),
            pl.BlockSpec((1, D_MODEL), lambda i, j: (0, 0)),
            pl.BlockSpec((None, D_MODEL, TN_IN), lambda i, j: (layer, 0, j)),
        ],
        out_specs=pl.BlockSpec((TM, TN_IN), lambda i, j: (i, j)),
        out_shape=jax.ShapeDtypeStruct((n, IN_COLS), F32),
        scratch_shapes=[pltpu.VMEM((TM, D_MODEL), BF16)],
        compiler_params=_params("parallel", "arbitrary"),
        name="inproj",
    )(x, mod, mod, norm_g.reshape(1, D_MODEL), w_in)


QK_SCALE_LOG2 = ATTN_SCALE * LOG2_E


def _with_ones(v):
    return jnp.concatenate([v, jnp.ones_like(v)], axis=-1)


def _softmax_sink(scores, values, sink):
    sink2 = sink * LOG2_E
    m = sink2
    for s in scores:
        m = jnp.maximum(jnp.max(s, axis=-1, keepdims=True), m)
    acc = None
    for s, v in zip(scores, values):
        pv = _dot(jnp.exp2(s - m).astype(BF16), v)
        acc = pv if acc is None else acc + pv
    return acc[:, :HEAD_DIM] / (acc[:, HEAD_DIM:] + jnp.exp2(sink2 - m))


def _ctx_attn_kernel(sink_ref, q_ref, k_ref, v_ref, qg_ref, kg_ref, o_ref, kc_ref, vc_ref):
    kv = pl.program_id(1)
    nseq, seq = kc_ref.shape[0], kc_ref.shape[1]
    heads = [slice(g * HEAD_DIM, (g + 1) * HEAD_DIM) for g in range(Q_PER_KV)]
    keys, values = [], []
    for i in range(nseq):
        rows = slice(i * seq, (i + 1) * seq)
        kn = _rms(k_ref[rows, :], kg_ref[...])
        v = v_ref[rows, :]
        kc_ref[i] = kn
        vc_ref[i] = v
        keys.append(kn.astype(BF16))
        values.append(_with_ones(v.astype(BF16)))
    logits = [[_dot_nt((_rms(q_ref[i * seq:(i + 1) * seq, cols], qg_ref[...]) * QK_SCALE_LOG2).astype(BF16), keys[i])
               for cols in heads] for i in range(nseq)]
    for i in range(nseq):
        for g, cols in enumerate(heads):
            o = _softmax_sink([logits[i][g]], [values[i]], sink_ref[kv, g])
            o_ref[i * seq:(i + 1) * seq, cols] = o.astype(BF16)


def _ctx_attention(z, sink, q_g, k_g, batch, seq):
    qw = Q_PER_KV * HEAD_DIM
    per = CTX_SEQS_PER_STEP
    cache_shape = jax.ShapeDtypeStruct((batch, N_KV_HEADS, seq, HEAD_DIM), F32)
    cache_spec = pl.BlockSpec((per, None, seq, HEAD_DIM), lambda b, kv: (b, kv, 0, 0))
    return pl.pallas_call(
        _ctx_attn_kernel,
        grid=(batch // per, N_KV_HEADS),
        in_specs=[
            pl.BlockSpec(memory_space=pltpu.SMEM),
            pl.BlockSpec((per * seq, qw), lambda b, kv: (b, kv)),
            pl.BlockSpec((per * seq, HEAD_DIM), lambda b, kv: (b, COL_K + kv)),
            pl.BlockSpec((per * seq, HEAD_DIM), lambda b, kv: (b, COL_V + kv)),
            pl.BlockSpec((1, HEAD_DIM), lambda b, kv: (0, 0)),
            pl.BlockSpec((1, HEAD_DIM), lambda b, kv: (0, 0)),
        ],
        out_specs=[pl.BlockSpec((per * seq, qw), lambda b, kv: (b, kv)), cache_spec, cache_spec],
        out_shape=[jax.ShapeDtypeStruct((batch * seq, D_ATTN), BF16), cache_shape, cache_shape],
        compiler_params=_params("parallel", "parallel"),
        name="ctx_attention",
    )(sink, z, z, z, q_g.reshape(1, HEAD_DIM), k_g.reshape(1, HEAD_DIM))


def _rope(x, cos, sin_a, sin_b):
    quarter = HEAD_DIM // 4
    up = pltpu.roll(x, HEAD_DIM - quarter, 1)
    down = pltpu.roll(x, quarter, 1)
    return x * cos + up * sin_a + down * sin_b


def _lat_attn_kernel(sink_ref, q_ref, k_ref, v_ref, kc_ref, vc_ref, qg_ref, kg_ref,
                     cos_ref, sa_ref, sb_ref, cosq_ref, saq_ref, sbq_ref, o_ref, k_scr, v_scr, kc_scr, vc_scr):
    kv = pl.program_id(1)
    qt = pl.program_id(2)
    seq = k_ref.shape[0]
    span = 3 * ATTN_BLOCK

    @pl.when(qt == 0)
    def _():
        kn = _rms(k_ref[...], kg_ref[...])
        k_scr[...] = _rope(kn, cos_ref[...], sa_ref[...], sb_ref[...]).astype(BF16)
        v_scr[...] = _with_ones(v_ref[...].astype(BF16))
        kc_scr[...] = kc_ref[...].astype(BF16)
        vc_scr[...] = _with_ones(vc_ref[...].astype(BF16))

    kcb = kc_scr[...]
    vcb = vc_scr[...]
    nblk = LAT_Q_TILE // ATTN_BLOCK
    heads = [slice(g * HEAD_DIM, (g + 1) * HEAD_DIM) for g in range(Q_PER_KV)]
    stacked = Q_PER_KV * ATTN_BLOCK

    head_of_row = lax.broadcasted_iota(jnp.int32, (stacked, 1), 0) // ATTN_BLOCK
    sink_col = jnp.zeros((stacked, 1), F32)
    for g in range(Q_PER_KV):
        sink_col = jnp.where(head_of_row == g, sink_ref[kv, g], sink_col)

    def block_rows(blk):
        return slice(blk * ATTN_BLOCK, (blk + 1) * ATTN_BLOCK)

    def logits_of(blk):
        rows = block_rows(blk)
        q0 = (qt * nblk + blk) * ATTN_BLOCK
        start = pl.multiple_of(jnp.clip(q0 - ATTN_BLOCK, 0, seq - span), ATTN_BLOCK)
        qpos = q0 + lax.broadcasted_iota(jnp.int32, (stacked, span), 0) % ATTN_BLOCK
        kpos = start + lax.broadcasted_iota(jnp.int32, (stacked, span), 1)
        valid = jnp.abs(qpos - kpos) <= WINDOW
        cos, sa, sb = cosq_ref[rows, :], saq_ref[rows, :], sbq_ref[rows, :]
        qn = jnp.concatenate([_rope(_rms(q_ref[rows, cols], qg_ref[...]), cos, sa, sb) for cols in heads], axis=0)
        qn = (qn * QK_SCALE_LOG2).astype(BF16)
        s_loc = jnp.where(valid, _dot_nt(qn, k_scr[pl.ds(start, span), :]), MASK_VALUE)
        return s_loc, _dot_nt(qn, kcb), start

    pending = logits_of(0)
    for blk in range(nblk):
        upcoming = logits_of(blk + 1) if blk + 1 < nblk else None
        s_loc, s_ctx, start = pending
        o = _softmax_sink([s_loc, s_ctx], [v_scr[pl.ds(start, span), :], vcb], sink_col)
        for g, cols in enumerate(heads):
            o_ref[block_rows(blk), cols] = o[g * ATTN_BLOCK:(g + 1) * ATTN_BLOCK, :].astype(BF16)
        pending = upcoming


def _rope_tables(seq):
    quarter = HEAD_DIM // 4
    pos = np.arange(seq)
    inv_freq = ROPE_BASE ** (-np.arange(quarter, dtype=np.float32) / quarter)
    inv_freq = jnp.asarray(inv_freq, F32)
    zero = jnp.zeros((seq, quarter), F32)

    def trig(p):
        ang = jnp.asarray(p, F32)[:, None] * inv_freq
        return jnp.cos(ang), jnp.sin(ang)

    cr, sr = trig(pos // GRID_W)
    cc, sc = trig(pos % GRID_W)
    cos = jnp.concatenate([cr, cr, cc, cc], axis=-1)
    sin_a = jnp.concatenate([-sr, zero, -sc, zero], axis=-1)
    sin_b = jnp.concatenate([zero, sr, zero, sc], axis=-1)
    return cos, sin_a, sin_b


def _lat_attention(z, sink, q_g, k_g, k_ctx, v_ctx, tables, batch, seq):
    qw = Q_PER_KV * HEAD_DIM
    nqt = seq // LAT_Q_TILE
    past = k_ctx.shape[2]
    cos, sin_a, sin_b = tables
    full_tab = pl.BlockSpec((seq, HEAD_DIM), lambda b, kv, qt: (0, 0))
    tile_tab = pl.BlockSpec((LAT_Q_TILE, HEAD_DIM), lambda b, kv, qt: (qt, 0))
    ctx_spec = pl.BlockSpec((None, None, past, HEAD_DIM), lambda b, kv, qt: (b, kv, 0, 0))
    gain = pl.BlockSpec((1, HEAD_DIM), lambda b, kv, qt: (0, 0))
    return pl.pallas_call(
        _lat_attn_kernel,
        grid=(batch, N_KV_HEADS, nqt),
        in_specs=[
            pl.BlockSpec(memory_space=pltpu.SMEM),
            pl.BlockSpec((LAT_Q_TILE, qw), lambda b, kv, qt: (b * nqt + qt, kv)),
            pl.BlockSpec((seq, HEAD_DIM), lambda b, kv, qt: (b, COL_K + kv)),
            pl.BlockSpec((seq, HEAD_DIM), lambda b, kv, qt: (b, COL_V + kv)),
            ctx_spec, ctx_spec, gain, gain,
            full_tab, full_tab, full_tab, tile_tab, tile_tab, tile_tab,
        ],
        out_specs=pl.BlockSpec((LAT_Q_TILE, qw), lambda b, kv, qt: (b * nqt + qt, kv)),
        out_shape=jax.ShapeDtypeStruct((batch * seq, D_ATTN), BF16),
        scratch_shapes=[pltpu.VMEM((seq, HEAD_DIM), BF16), pltpu.VMEM((seq, 2 * HEAD_DIM), BF16),
                        pltpu.VMEM((past, HEAD_DIM), BF16), pltpu.VMEM((past, 2 * HEAD_DIM), BF16)],
        compiler_params=_params("parallel", "parallel", "arbitrary"),
        name="lat_attention",
    )(sink, z, z, z, k_ctx, v_ctx, q_g.reshape(1, HEAD_DIM), k_g.reshape(1, HEAD_DIM),
      cos, sin_a, sin_b, cos, sin_a, sin_b)


def _split3(x):
    hi = x.astype(BF16)
    r1 = x - hi.astype(F32)
    mid = r1.astype(BF16)
    lo = (r1 - mid.astype(F32)).astype(BF16)
    return hi, mid, lo


def _rec_gates(forward, zf, lb):
    c = REC_CHUNK
    t = jnp.exp(-jnp.abs(zf))
    r = 1.0 / (1.0 + t)
    tr = t * r
    nonneg = zf >= 0
    sig_pos = jnp.where(nonneg, r, tr)
    sig_neg = jnp.where(nonneg, tr, r)
    one_m_lb = 1.0 - lb
    log_f = jnp.log(jnp.maximum(lb + one_m_lb * sig_pos, GATE_FLOOR))
    k = one_m_lb * sig_neg
    ri = lax.broadcasted_iota(jnp.int32, (c, c), 0)
    ci = lax.broadcasted_iota(jnp.int32, (c, c), 1)
    tri = jnp.where((ci <= ri) if forward else (ci >= ri), 1.0, 0.0).astype(BF16)
    hi, mid, lo = _split3(log_f)
    cum2 = (_dot(tri, hi) + _dot(tri, mid) + _dot(tri, lo)) * LOG2_E
    return cum2 - jnp.log2(k), cum2


def _rec_same_block(forward, q, key_row, cum2):
    sub_row = lax.broadcasted_iota(jnp.int32, (SUBLANES, REC_DK), 0)
    pieces = []
    for b in range(REC_CHUNK // SUBLANES):
        blk = slice(b * SUBLANES, (b + 1) * SUBLANES)
        cum_b, q_b = cum2[blk, :], q[blk, :]
        for sl in range(SUBLANES):
            s = b * SUBLANES + sl
            keep = (sub_row >= sl) if forward else (sub_row <= sl)
            pieces.append(jnp.where(keep, jnp.exp2(cum_b - key_row(s)) * q_b, 0.0))
    return _dot(jnp.concatenate(pieces, axis=0).astype(BF16), jnp.ones((REC_DK, REC_DV), BF16))


def _rec_cross_block(forward, q, key2, v, cum2):
    c = REC_CHUNK
    q_parts, k_parts, v_parts, segments = [], [], [], []
    for tb in range(c // SUBLANES):
        blk = slice(tb * SUBLANES, (tb + 1) * SUBLANES)
        src, ref_row = (slice(0, blk.start), blk.start - 1) if forward else (slice(blk.stop, c), blk.stop)
        if src.stop == src.start:
            q_parts.append(jnp.zeros((SUBLANES, REC_DK), F32))
            continue
        ref = cum2[ref_row:ref_row + 1, :]
        q_parts.append(q[blk, :] * jnp.exp2(cum2[blk, :] - ref))
        k_parts.append(jnp.exp2(ref - key2[src, :]))
        v_parts.append(v[src, :])
        segments.append((tb, src.stop - src.start))
    a = _dot_nt(jnp.concatenate(q_parts, axis=0).astype(BF16), jnp.concatenate(k_parts, axis=0).astype(BF16))
    return a, segments, jnp.concatenate(v_parts, axis=0).astype(BF16)


def _rec_cross_apply(a, segments, v_all):
    c, ncol = a.shape
    row_blk = lax.broadcasted_iota(jnp.int32, (c, ncol), 0) // SUBLANES
    col = lax.broadcasted_iota(jnp.int32, (c, ncol), 1)
    col_blk = jnp.full((c, ncol), -1, jnp.int32)
    start = 0
    for tb, width in segments:
        col_blk = jnp.where((col >= start) & (col < start + width), tb, col_blk)
        start += width
    return _dot(jnp.where(row_blk == col_blk, a, 0.0).astype(BF16), v_all)


def _rec_same_apply(lane_sums, v_row):
    o_blocks = []
    for b in range(REC_CHUNK // SUBLANES):
        acc = None
        for sl in range(SUBLANES):
            s = b * SUBLANES + sl
            term = lane_sums[s * SUBLANES:(s + 1) * SUBLANES, :] * v_row(s)
            acc = term if acc is None else acc + term
        o_blocks.append(acc)
    return jnp.concatenate(o_blocks, axis=0)


def _rec_kernel(layer, has_init, *refs):
    q_ref, ff_ref, fb_ref, v_ref, g_ref, lg_ref, ng_ref = refs[:7]
    refs = refs[7:]
    if has_init:
        s0_ref, refs = refs[0], refs[1:]
    conv_in, refs = refs[:10], refs[10:]
    o_ref, conv_o_ref = refs[:2]
    refs = refs[2:]
    if not has_init:
        s_out_ref, refs = refs[0], refs[1:]
    o_scr, qd_scr, kd_scr, gl_scr, oi_scr, key_scr, glu_scr, conv_h_scr, conv_w_scr = refs
    seq = q_ref.shape[0]
    nchunk = seq // REC_CHUNK
    c = REC_CHUNK

    logits = lg_ref[...]
    e = jnp.exp(logits - jnp.max(logits, axis=0, keepdims=True))
    p = e / jnp.sum(e, axis=0, keepdims=True)
    lb = jnp.zeros_like(p[0])
    for i in range(1, layer + 1):
        lb = lb + p[i]

    def rows_of(ci):
        return pl.ds(pl.multiple_of(ci * c, c), c)

    z_refs = (ff_ref, fb_ref)

    def local_body(it, carry):
        chunks = [it * REC_LOCAL_UNROLL + u for u in range(REC_LOCAL_UNROLL)]
        qs = [q_ref[rows_of(ci), :] for ci in chunks]
        vs = [v_ref[rows_of(ci), :] for ci in chunks]
        items = [(u, d) for u in range(REC_LOCAL_UNROLL) for d in range(2)]
        gates = [_rec_gates(d == 0, z_refs[d][rows_of(chunks[u]), :], lb[d]) for u, d in items]
        for idx, (key2, _) in enumerate(gates):
            key_scr[idx] = key2
        key_rows = [lambda s, idx=idx: key_scr[idx, s:s + 1, :] for idx in range(len(items))]
        v_rows = [lambda s, ci=ci: v_ref[pl.ds(ci * c + s, 1), :] for ci in chunks]
        same = [_rec_same_block(d == 0, qs[u], key_rows[idx], cum2)
                for idx, ((u, d), (_, cum2)) in enumerate(zip(items, gates))]
        cross = [_rec_cross_block(d == 0, qs[u], key2, vs[u], cum2) for (u, d), (key2, cum2) in zip(items, gates)]
        totals = [None] * REC_LOCAL_UNROLL
        for (u, d), (key2, cum2), lane_sums, (a, segments, v_all) in zip(items, gates, same, cross):
            rows = rows_of(chunks[u])
            last = cum2[c - 1:c, :] if d == 0 else cum2[0:1, :]
            qd_scr[d, rows, :] = (qs[u] * jnp.exp2(cum2)).astype(BF16)
            kd_scr[d, rows, :] = jnp.exp2(last - key2).astype(BF16)
            gl_scr[d, chunks[u]] = jnp.broadcast_to(jnp.exp2(last), (SUBLANES, REC_DK))
            o_local = _rec_same_apply(lane_sums, v_rows[u]) + _rec_cross_apply(a, segments, v_all)
            totals[u] = o_local if totals[u] is None else totals[u] + o_local
        for u, ci in enumerate(chunks):
            o_scr[rows_of(ci), :] = totals[u]
        return carry

    lax.fori_loop(0, nchunk // REC_LOCAL_UNROLL, local_body, 0)

    if has_init:
        init = (s0_ref[0].T, s0_ref[1].T)
    else:
        init = (jnp.zeros((REC_DV, REC_DK), F32),) * 2

    state_unroll = min(REC_STATE_UNROLL, nchunk)
    conv_rows = conv_o_ref.shape[0] // (nchunk // state_unroll)
    _conv_stage(pl.program_id(1), N_REC_HEADS, *conv_in[:7], glu_scr, conv_w_scr)

    def state_body(it, states):
        offset = pl.multiple_of(it * conv_rows, conv_rows)
        conv_h_scr[0] = glu_scr[pl.ds(offset, conv_rows + 2 * CONV_HALO), :]

        def store(r0, val):
            conv_o_ref[pl.ds(pl.multiple_of(offset + r0, CONV_SUB), CONV_SUB), :] = val

        _conv_rows(conv_rows, conv_h_scr, conv_w_scr, *conv_in[7:], store)
        states = list(states)
        steps = [it * state_unroll + u for u in range(state_unroll)]
        order = [(d, step if d == 0 else nchunk - 1 - step) for step in steps for d in range(2)]
        updates = [_dot_tn(v_ref[rows_of(ci), :].astype(BF16), kd_scr[d, rows_of(ci), :]) for d, ci in order]
        for (d, ci), update in zip(order, updates):
            st = states[d]
            oi_scr[d, rows_of(ci), :] = _dot_nt(qd_scr[d, rows_of(ci), :], st.astype(BF16))
            decayed = (st.reshape(REC_DV // SUBLANES, SUBLANES, REC_DK) * gl_scr[d, ci]).reshape(REC_DV, REC_DK)
            states[d] = decayed + update
        return tuple(states)

    final = lax.fori_loop(0, nchunk // state_unroll, state_body, init)
    if not has_init:
        s_out_ref[0] = final[0].T
        s_out_ref[1] = final[1].T

    def out_body(ci, carry):
        rows = rows_of(ci)
        gate = g_ref[rows, :]
        o = o_scr[rows, :] + oi_scr[0, rows, :] + oi_scr[1, rows, :]
        o_ref[rows, :] = (_rms(o, ng_ref[...]) * (gate * _sigmoid(gate))).astype(BF16)
        return carry

    lax.fori_loop(0, nchunk, out_body, 0, unroll=4)


def _recurrence(z, layer, lb_logits, norm_g, s0, conv_params, batch, seq):
    has_init = s0 is not None
    n = batch * seq
    col = lambda base: pl.BlockSpec((seq, REC_DK), lambda b, h: (b, base + h))
    state_spec = pl.BlockSpec((None, 2, None, REC_DK, REC_DV), lambda b, h: (b, 0, h, 0, 0))
    in_specs = [col(COL_RQ), col(COL_RF_F), col(COL_RF_B), col(COL_RI), col(COL_RG),
                pl.BlockSpec((DEPTH, 2, None, 1, REC_DK), lambda b, h: (0, 0, h, 0, 0)),
                pl.BlockSpec((None, 1, REC_DV), lambda b, h: (h, 0, 0))]
    args = [z, z, z, z, z, lb_logits, norm_g]
    if has_init:
        in_specs.append(state_spec)
        args.append(s0)

    conv_step = seq // N_REC_HEADS
    state_trips = (seq // REC_CHUNK) // min(REC_STATE_UNROLL, seq // REC_CHUNK)
    conv_rows = conv_step // state_trips
    assert conv_step % CONV_HALO == 0 and conv_rows % CONV_SUB == 0
    cw = CONV_CH // HEAD_DIM
    halo_per_step = conv_step // CONV_HALO
    nhalo = n // CONV_HALO
    part = lambda b, h: b * N_REC_HEADS + h
    big_block = conv_step * CONV_CH * jnp.dtype(F32).itemsize >= SINGLE_BUFFER_BYTES
    mid_mode = dict(pipeline_mode=pl.Buffered(1)) if big_block else {}
    mid = lambda cc: pl.BlockSpec((conv_step, CONV_CH), lambda b, h: (part(b, h), cc), **mid_mode)
    top = lambda cc: pl.BlockSpec(
        (CONV_HALO, CONV_CH), lambda b, h: (jnp.maximum(part(b, h) * halo_per_step - 1, 0), cc))
    bot = lambda cc: pl.BlockSpec(
        (CONV_HALO, CONV_CH), lambda b, h: (jnp.minimum((part(b, h) + 1) * halo_per_step, nhalo - 1), cc))
    vec = pl.BlockSpec((1, CONV_CH), lambda b, h: (0, 0))
    ca, cb = COL_CA // cw, COL_CB // cw
    w, bias, ln_g, ln_b = conv_params
    in_specs += [mid(ca), mid(cb), top(ca), top(cb), bot(ca), bot(cb),
                 pl.BlockSpec((CONV_WIDTH, CONV_CH), lambda b, h: (0, 0)), vec, vec, vec]
    args += [z, z, z, z, z, z, w, bias.reshape(1, CONV_CH), ln_g.reshape(1, CONV_CH), ln_b.reshape(1, CONV_CH)]

    out_specs = [pl.BlockSpec((seq, REC_DV), lambda b, h: (b, h)),
                 pl.BlockSpec((conv_step, CONV_CH), lambda b, h: (part(b, h), 0))]
    out_shape = [jax.ShapeDtypeStruct((n, D_REC), BF16), jax.ShapeDtypeStruct((n, CONV_CH), BF16)]
    if not has_init:
        out_specs.append(state_spec)
        out_shape.append(jax.ShapeDtypeStruct((batch, 2, N_REC_HEADS, REC_DK, REC_DV), F32))
    return pl.pallas_call(
        functools.partial(_rec_kernel, layer, has_init),
        grid=(batch, N_REC_HEADS),
        in_specs=in_specs,
        out_specs=out_specs,
        out_shape=out_shape,
        scratch_shapes=[pltpu.VMEM((seq, REC_DV), F32),
                        pltpu.VMEM((2, seq, REC_DK), BF16),
                        pltpu.VMEM((2, seq, REC_DK), BF16),
                        pltpu.VMEM((2, seq // REC_CHUNK, SUBLANES, REC_DK), F32),
                        pltpu.VMEM((2, seq, REC_DV), F32),
                        pltpu.VMEM((2 * REC_LOCAL_UNROLL, REC_CHUNK, REC_DK), F32),
                        pltpu.VMEM((conv_step + 2 * CONV_HALO, CONV_CH), F32),
                        pltpu.VMEM((SUBLANES, conv_rows + 2 * CONV_HALO, CONV_CH), F32),
                        pltpu.VMEM((CONV_WIDTH, SUBLANES, CONV_CH), F32)],
        compiler_params=_params("parallel", "parallel"),
        name="hgrn2_conv",
    )(*args)


def _conv_stage(part, nparts, a_ref, b_ref, at_ref, bt_ref, ab_ref, bb_ref, w_ref, glu_scr, w_scr):
    has_top = jnp.where(part > 0, 1.0, 0.0)
    has_bot = jnp.where(part < nparts - 1, 1.0, 0.0)
    rows = a_ref.shape[0]

    def glu(a, b):
        return a * _sigmoid(b)

    glu_scr[0:CONV_HALO, :] = glu(at_ref[...], bt_ref[...]) * has_top
    glu_scr[CONV_HALO:CONV_HALO + rows, :] = glu(a_ref[...], b_ref[...])
    glu_scr[CONV_HALO + rows:, :] = glu(ab_ref[...], bb_ref[...]) * has_bot
    for j in range(CONV_WIDTH):
        w_scr[j] = jnp.broadcast_to(w_ref[j:j + 1, :], (SUBLANES, CONV_CH))


def _conv_rows(rows, h_scr, w_scr, bias_ref, lg_ref, lb_ref, store):
    half = CONV_WIDTH // 2
    shifted_rows = rows + 2 * CONV_HALO - SUBLANES
    for r in range(1, SUBLANES):
        h_scr[r, 0:shifted_rows, :] = h_scr[0, r:r + shifted_rows, :]
    for r0 in range(0, rows, CONV_SUB):
        acc = None
        for j in range(CONV_WIDTH):
            lo = r0 + CONV_HALO - half + j
            shift = lo % SUBLANES
            weight = w_scr[j]
            window = h_scr[shift, lo - shift:lo - shift + CONV_SUB, :]
            term = (window.reshape(CONV_SUB // SUBLANES, SUBLANES, CONV_CH) * weight).reshape(CONV_SUB, CONV_CH)
            acc = term if acc is None else acc + term
        y = acc + bias_ref[...]
        mu = jnp.mean(y, axis=-1, keepdims=True)
        yc = y - mu
        var = jnp.mean(yc * yc, axis=-1, keepdims=True)
        yn = yc * lax.rsqrt(var + EPS) * lg_ref[...] + lb_ref[...]
        store(r0, (yn * _sigmoid(yn)).astype(BF16))


def _outproj_kernel(x_ref, g_ref, a_ref, r_ref, c_ref, wa_ref, wr_ref, wc_ref, o_ref):
    mix = _dot(a_ref[...], wa_ref[...]) + _dot(r_ref[...], wr_ref[...]) + _dot(c_ref[...], wc_ref[...])
    o_ref[...] = x_ref[...] + g_ref[...] * mix


def _outproj(x, mod, attn, rec, conv, w_out, layer, rows_per_cond):
    n = x.shape[0]
    tn = D_MODEL // 2
    row = lambda width: pl.BlockSpec((TM, width), lambda i, j: (i, 0))
    return pl.pallas_call(
        _outproj_kernel,
        grid=(n // TM, D_MODEL // tn),
        in_specs=[
            pl.BlockSpec((TM, tn), lambda i, j: (i, j)),
            pl.BlockSpec((None, None, 1, tn), lambda i, j: ((i * TM) // rows_per_cond, 5, 0, j)),
            row(D_ATTN), row(D_REC), row(CONV_CH),
            pl.BlockSpec((None, D_ATTN, tn), lambda i, j: (layer, 0, j)),
            pl.BlockSpec((None, D_REC, tn), lambda i, j: (layer, D_ATTN // D_REC, j)),
            pl.BlockSpec((None, CONV_CH, tn), lambda i, j: (layer, (D_ATTN + D_REC) // CONV_CH, j)),
        ],
        out_specs=pl.BlockSpec((TM, tn), lambda i, j: (i, j)),
        out_shape=jax.ShapeDtypeStruct((n, D_MODEL), F32),
        compiler_params=_params("parallel", "parallel"),
        name="outproj",
    )(x, mod, attn, rec, conv, w_out, w_out, w_out)


def _trunk_layer(x, mod, l, P, mixers, rows_per_cond):
    x = _ffn(x, mod, 0, P['norm_g'][l, 0], P['w_ffn_in'], P['w_ffn_out'], l, 0, rows_per_cond)
    z = _inproj(x, mod, P['norm_g'][l, 1], P['w_in'], l, rows_per_cond)
    attn, rec, conv, extras = mixers(z)
    x = _outproj(x, mod, attn, rec, conv, P['w_out'], l, rows_per_cond)
    x = _ffn(x, mod, 6, P['norm_g'][l, 2], P['w_ffn_in'], P['w_ffn_out'], l, 1, rows_per_cond)
    return x, extras


def kernel(x_prompt, x_sample, cache_k, cache_v, state_rec, c, c_ctx, w_ada, b_ada, norm_g, w_ffn_in, w_ffn_out,
           w_in, w_out, q_norm_g, k_norm_g, attn_sink, rec_lb_logits, rec_norm_g, conv_w, conv_b, conv_ln_g, conv_ln_b):
    batch, seq, _ = x_prompt.shape
    dec_batch, dec_seq, _ = x_sample.shape
    assert seq % TM == 0 or TM % seq == 0
    assert dec_seq % TM == 0 and dec_seq % LAT_Q_TILE == 0

    P = {'norm_g': norm_g, 'w_ffn_in': w_ffn_in.astype(BF16), 'w_ffn_out': w_ffn_out.astype(BF16),
         'w_in': w_in.astype(BF16), 'w_out': w_out.astype(BF16)}

    lb_logits = rec_lb_logits.reshape(DEPTH, 2, N_REC_HEADS, 1, REC_DK)

    cond8 = jnp.zeros((SUBLANES, D_MODEL), F32).at[0].set(c_ctx).at[1:1 + dec_batch].set(c)
    mod = _modulation(cond8, w_ada, b_ada).reshape(DEPTH, SUBLANES, N_MOD, 1, D_MODEL)
    tables = _rope_tables(dec_seq)
    sinks = attn_sink.reshape(DEPTH, N_KV_HEADS, Q_PER_KV)
    rec_g = rec_norm_g.reshape(DEPTH, N_REC_HEADS, 1, REC_DV)

    conv_params = lambda l: (conv_w[l], conv_b[l], conv_ln_g[l], conv_ln_b[l])

    h = x_prompt.reshape(batch * seq, D_MODEL)
    ks, vs, ss = [], [], []
    for l in range(DEPTH):
        def ctx_mixers(z, l=l):
            attn, k_l, v_l = _ctx_attention(z, sinks[l], q_norm_g[l], k_norm_g[l], batch, seq)
            rec, conv, s_l = _recurrence(z, l, lb_logits, rec_g[l], None, conv_params(l), batch, seq)
            return attn, rec, conv, (k_l, v_l, s_l)

        h, (k_l, v_l, s_l) = _trunk_layer(h, mod[l, 0:1], l, P, ctx_mixers, batch * seq)
        ks.append(k_l)
        vs.append(v_l)
        ss.append(s_l)
    y_prompt = h.reshape(batch, seq, D_MODEL)

    h = x_sample.reshape(dec_batch * dec_seq, D_MODEL)
    for l in range(DEPTH):
        def lat_mixers(z, l=l):
            attn = _lat_attention(z, sinks[l], q_norm_g[l], k_norm_g[l], cache_k[:, l], cache_v[:, l], tables,
                                  dec_batch, dec_seq)
            rec, conv = _recurrence(z, l, lb_logits, rec_g[l], state_rec[:, l], conv_params(l), dec_batch, dec_seq)
            return attn, rec, conv, None

        h, _ = _trunk_layer(h, mod[l, 1:1 + dec_batch], l, P, lat_mixers, dec_seq)
    y_sample = h.reshape(dec_batch, dec_seq, D_MODEL)

    return (y_prompt, y_sample, jnp.stack(ks, axis=1), jnp.stack(vs, axis=1), jnp.stack(ss, axis=1))
```

```python
import functools

import jax
import jax.numpy as jnp
import numpy as np
from jax import lax
from jax.experimental import pallas as pl
from jax.experimental.pallas import tpu as pltpu

F32 = jnp.float32
BF16 = jnp.bfloat16

D_MODEL = 2048
DEPTH = 2
GRID_W = 64
HEAD_DIM = 128
D_ATTN = D_MODEL // 2
N_Q_HEADS = D_ATTN // HEAD_DIM
N_KV_HEADS = 2
Q_PER_KV = N_Q_HEADS // N_KV_HEADS
WINDOW = 128
ATTN_BLOCK = 128
ATTN_SCALE = HEAD_DIM ** -0.5
ROPE_BASE = 10000.0
MASK_VALUE = -1e30
D_REC = D_MODEL // 4
REC_DK = 128
REC_DV = 128
N_REC_HEADS = D_REC // REC_DV
REC_CHUNK = 32
CONV_CH = D_MODEL // 4
CONV_WIDTH = 31
D_FF = 5632
N_MOD = 9
EPS = 1e-6
GATE_FLOOR = 1e-30
LOG2_E = 1.4426950408889634
IN_COLS = 5120

COL_Q = 0
COL_K = 8
COL_V = 10
COL_RQ = 12
COL_RF_F = 16
COL_RF_B = 20
COL_RI = 24
COL_RG = 28
COL_CA = 32
COL_CB = 36

SUBLANES = 8
V7X_VMEM_BYTES = 64 * 1024 * 1024
VMEM_LIMIT = V7X_VMEM_BYTES * 7 // 8
SINGLE_BUFFER_BYTES = 1024 * 1024

TM = 1024
TM_FFN = 1024
TF = 512
TN_FFN_OUT = 256
TN_IN = 1024
TN_ADA = 1024
CONV_HALO = 16
CONV_SUB = 32
LAT_Q_TILE = 1024
CTX_SEQS_PER_STEP = 4
REC_LOCAL_UNROLL = 8
REC_STATE_UNROLL = 16


def _params(*sem):
    return pltpu.CompilerParams(dimension_semantics=sem, vmem_limit_bytes=VMEM_LIMIT)


def _dot(a, b):
    return jnp.dot(a, b, preferred_element_type=F32)


def _dot_nt(a, b):
    return lax.dot_general(a, b, (((1,), (1,)), ((), ())), preferred_element_type=F32)


def _dot_tn(a, b):
    return lax.dot_general(a, b, (((0,), (0,)), ((), ())), preferred_element_type=F32)


def _rms(x, g):
    return x * lax.rsqrt(jnp.mean(x * x, axis=-1, keepdims=True) + EPS) * g


def _ada_norm(x, g, scale, shift):
    gain = g * (1.0 + scale)
    return x * lax.rsqrt(jnp.mean(x * x, axis=-1, keepdims=True) + EPS) * gain + shift


def _sigmoid(x):
    return 1.0 / (1.0 + jnp.exp(-x))


def _ada_kernel(c_ref, w_ref, b_ref, o_ref):
    c = c_ref[...]
    s = (c * _sigmoid(c)).astype(BF16)
    o_ref[...] = _dot(s, w_ref[...].astype(BF16)) + b_ref[...]


def _modulation(cond8, w_ada, b_ada):
    ncol = N_MOD * D_MODEL
    return pl.pallas_call(
        _ada_kernel,
        grid=(DEPTH, ncol // TN_ADA),
        in_specs=[
            pl.BlockSpec((SUBLANES, D_MODEL), lambda l, j: (0, 0)),
            pl.BlockSpec((None, D_MODEL, TN_ADA), lambda l, j: (l, 0, j)),
            pl.BlockSpec((None, 1, TN_ADA), lambda l, j: (l, 0, j)),
        ],
        out_specs=pl.BlockSpec((None, SUBLANES, TN_ADA), lambda l, j: (l, 0, j)),
        out_shape=jax.ShapeDtypeStruct((DEPTH, SUBLANES, ncol), F32),
        compiler_params=_params("parallel", "parallel"),
        name="modulation",
    )(cond8, w_ada, b_ada.reshape(DEPTH, 1, ncol))


def _mod_spec(chunk, rows_per_cond, tm=TM):
    return pl.BlockSpec((None, None, 1, D_MODEL), lambda i, j: ((i * tm) // rows_per_cond, chunk, 0, 0))


def _ffn_kernel(nf, x_ref, sh_ref, sc_ref, g_ref, ng_ref, wa_ref, wb_ref, wo_ref, o_ref, h_scr, act_scr):
    j = pl.program_id(1)

    @pl.when(j == 0)
    def _():
        h = _ada_norm(x_ref[...], ng_ref[...], sc_ref[...], sh_ref[...])
        h_scr[...] = h.astype(BF16)

    @pl.when(j < nf)
    def _():
        h = h_scr[...]
        a = _dot(h, wa_ref[...])
        b = _dot(h, wb_ref[...])
        act_scr[:, pl.ds(pl.multiple_of(j * TF, TF), TF)] = (a * _sigmoid(a) * b).astype(BF16)

    @pl.when(j >= nf)
    def _():
        cols = pl.ds(pl.multiple_of((j - nf) * TN_FFN_OUT, TN_FFN_OUT), TN_FFN_OUT)
        o_ref[...] = x_ref[:, cols] + 0.5 * g_ref[...] * _dot(act_scr[...], wo_ref[...])


def _ffn(x, mod, first_chunk, norm_g, w_in, w_out, layer, which, rows_per_cond):
    n = x.shape[0]
    nf = D_FF // TF
    nout = D_MODEL // TN_FFN_OUT
    hid = lambda j: jnp.minimum(j, nf - 1)
    out = lambda j: jnp.maximum(j - nf, 0)
    return pl.pallas_call(
        functools.partial(_ffn_kernel, nf),
        grid=(n // TM_FFN, nf + nout),
        in_specs=[
            pl.BlockSpec((TM_FFN, D_MODEL), lambda i, j: (i, 0)),
            _mod_spec(first_chunk, rows_per_cond, TM_FFN),
            _mod_spec(first_chunk + 1, rows_per_cond, TM_FFN),
            pl.BlockSpec((None, None, 1, TN_FFN_OUT),
                         lambda i, j: ((i * TM_FFN) // rows_per_cond, first_chunk + 2, 0, out(j))),
            pl.BlockSpec((1, D_MODEL), lambda i, j: (0, 0)),
            pl.BlockSpec((None, None, D_MODEL, TF), lambda i, j: (layer, which, 0, hid(j))),
            pl.BlockSpec((None, None, D_MODEL, TF), lambda i, j: (layer, which, 0, hid(j) + nf)),
            pl.BlockSpec((None, None, D_FF, TN_FFN_OUT), lambda i, j: (layer, which, 0, out(j))),
        ],
        out_specs=pl.BlockSpec((TM_FFN, TN_FFN_OUT), lambda i, j: (i, out(j))),
        out_shape=jax.ShapeDtypeStruct((n, D_MODEL), F32),
        scratch_shapes=[pltpu.VMEM((TM_FFN, D_MODEL), BF16), pltpu.VMEM((TM_FFN, D_FF), BF16)],
        compiler_params=_params("parallel", "arbitrary"),
        name="ffn",
    )(x, mod, mod, mod, norm_g.reshape(1, D_MODEL), w_in, w_in, w_out)


def _inproj_kernel(x_ref, sh_ref, sc_ref, ng_ref, w_ref, o_ref, h_scr):
    @pl.when(pl.program_id(1) == 0)
    def _():
        h = _ada_norm(x_ref[...], ng_ref[...], sc_ref[...], sh_ref[...])
        h_scr[...] = h.astype(BF16)

    o_ref[...] = _dot(h_scr[...], w_ref[...])


def _inproj(x, mod, norm_g, w_in, layer, rows_per_cond):
    n = x.shape[0]
    return pl.pallas_call(
        _inproj_kernel,
        grid=(n // TM, IN_COLS // TN_IN),
        in_specs=[
            pl.BlockSpec((TM, D_MODEL), lambda i, j: (i, 0)),
            _mod_spec(3, rows_per_cond),
            _mod_spec(4, rows_per_cond),
            pl.BlockSpec((1, D_MODEL), lambda i, j: (0, 0)),
            pl.BlockSpec((None, D_MODEL, TN_IN), lambda i, j: (layer, 0, j)),
        ],
        out_specs=pl.BlockSpec((TM, TN_IN), lambda i, j: (i, j)),
        out_shape=jax.ShapeDtypeStruct((n, IN_COLS), F32),
        scratch_shapes=[pltpu.VMEM((TM, D_MODEL), BF16)],
        compiler_params=_params("parallel", "arbitrary"),
        name="inproj",
    )(x, mod, mod, norm_g.reshape(1, D_MODEL), w_in)


QK_SCALE_LOG2 = ATTN_SCALE * LOG2_E


def _with_ones(v):
    return jnp.concatenate([v, jnp.ones_like(v)], axis=-1)


def _softmax_sink(scores, values, sink):
    sink2 = sink * LOG2_E
    m = sink2
    for s in scores:
        m = jnp.maximum(jnp.max(s, axis=-1, keepdims=True), m)
    acc = None
    for s, v in zip(scores, values):
        pv = _dot(jnp.exp2(s - m).astype(BF16), v)
        acc = pv if acc is None else acc + pv
    return acc[:, :HEAD_DIM] / (acc[:, HEAD_DIM:] + jnp.exp2(sink2 - m))


def _ctx_attn_kernel(sink_ref, q_ref, k_ref, v_ref, qg_ref, kg_ref, o_ref, kc_ref, vc_ref):
    kv = pl.program_id(1)
    nseq, seq = kc_ref.shape[0], kc_ref.shape[1]
    heads = [slice(g * HEAD_DIM, (g + 1) * HEAD_DIM) for g in range(Q_PER_KV)]
    keys, values = [], []
    for i in range(nseq):
        rows = slice(i * seq, (i + 1) * seq)
        kn = _rms(k_ref[rows, :], kg_ref[...])
        v = v_ref[rows, :]
        kc_ref[i] = kn
        vc_ref[i] = v
        keys.append(kn.astype(BF16))
        values.append(_with_ones(v.astype(BF16)))
    logits = [[_dot_nt((_rms(q_ref[i * seq:(i + 1) * seq, cols], qg_ref[...]) * QK_SCALE_LOG2).astype(BF16), keys[i])
               for cols in heads] for i in range(nseq)]
    for i in range(nseq):
        for g, cols in enumerate(heads):
            o = _softmax_sink([logits[i][g]], [values[i]], sink_ref[kv, g])
            o_ref[i * seq:(i + 1) * seq, cols] = o.astype(BF16)


def _ctx_attention(z, sink, q_g, k_g, batch, seq):
    qw = Q_PER_KV * HEAD_DIM
    per = CTX_SEQS_PER_STEP
    cache_shape = jax.ShapeDtypeStruct((batch, N_KV_HEADS, seq, HEAD_DIM), F32)
    cache_spec = pl.BlockSpec((per, None, seq, HEAD_DIM), lambda b, kv: (b, kv, 0, 0))
    return pl.pallas_call(
        _ctx_attn_kernel,
        grid=(batch // per, N_KV_HEADS),
        in_specs=[
            pl.BlockSpec(memory_space=pltpu.SMEM),
            pl.BlockSpec((per * seq, qw), lambda b, kv: (b, kv)),
            pl.BlockSpec((per * seq, HEAD_DIM), lambda b, kv: (b, COL_K + kv)),
            pl.BlockSpec((per * seq, HEAD_DIM), lambda b, kv: (b, COL_V + kv)),
            pl.BlockSpec((1, HEAD_DIM), lambda b, kv: (0, 0)),
            pl.BlockSpec((1, HEAD_DIM), lambda b, kv: (0, 0)),
        ],
        out_specs=[pl.BlockSpec((per * seq, qw), lambda b, kv: (b, kv)), cache_spec, cache_spec],
        out_shape=[jax.ShapeDtypeStruct((batch * seq, D_ATTN), BF16), cache_shape, cache_shape],
        compiler_params=_params("parallel", "parallel"),
        name="ctx_attention",
    )(sink, z, z, z, q_g.reshape(1, HEAD_DIM), k_g.reshape(1, HEAD_DIM))


def _rope(x, cos, sin_a, sin_b):
    quarter = HEAD_DIM // 4
    up = pltpu.roll(x, HEAD_DIM - quarter, 1)
    down = pltpu.roll(x, quarter, 1)
    return x * cos + up * sin_a + down * sin_b


def _lat_attn_kernel(sink_ref, q_ref, k_ref, v_ref, kc_ref, vc_ref, qg_ref, kg_ref,
                     cos_ref, sa_ref, sb_ref, cosq_ref, saq_ref, sbq_ref, o_ref, k_scr, v_scr, kc_scr, vc_scr):
    kv = pl.program_id(1)
    qt = pl.program_id(2)
    seq = k_ref.shape[0]
    span = 3 * ATTN_BLOCK

    @pl.when(qt == 0)
    def _():
        kn = _rms(k_ref[...], kg_ref[...])
        k_scr[...] = _rope(kn, cos_ref[...], sa_ref[...], sb_ref[...]).astype(BF16)
        v_scr[...] = _with_ones(v_ref[...].astype(BF16))
        kc_scr[...] = kc_ref[...].astype(BF16)
        vc_scr[...] = _with_ones(vc_ref[...].astype(BF16))

    kcb = kc_scr[...]
    vcb = vc_scr[...]
    nblk = LAT_Q_TILE // ATTN_BLOCK
    heads = [slice(g * HEAD_DIM, (g + 1) * HEAD_DIM) for g in range(Q_PER_KV)]
    stacked = Q_PER_KV * ATTN_BLOCK

    head_of_row = lax.broadcasted_iota(jnp.int32, (stacked, 1), 0) // ATTN_BLOCK
    sink_col = jnp.zeros((stacked, 1), F32)
    for g in range(Q_PER_KV):
        sink_col = jnp.where(head_of_row == g, sink_ref[kv, g], sink_col)

    def block_rows(blk):
        return slice(blk * ATTN_BLOCK, (blk + 1) * ATTN_BLOCK)

    def logits_of(blk):
        rows = block_rows(blk)
        q0 = (qt * nblk + blk) * ATTN_BLOCK
        start = pl.multiple_of(jnp.clip(q0 - ATTN_BLOCK, 0, seq - span), ATTN_BLOCK)
        qpos = q0 + lax.broadcasted_iota(jnp.int32, (stacked, span), 0) % ATTN_BLOCK
        kpos = start + lax.broadcasted_iota(jnp.int32, (stacked, span), 1)
        valid = jnp.abs(qpos - kpos) <= WINDOW
        cos, sa, sb = cosq_ref[rows, :], saq_ref[rows, :], sbq_ref[rows, :]
        qn = jnp.concatenate([_rope(_rms(q_ref[rows, cols], qg_ref[...]), cos, sa, sb) for cols in heads], axis=0)
        qn = (qn * QK_SCALE_LOG2).astype(BF16)
        s_loc = jnp.where(valid, _dot_nt(qn, k_scr[pl.ds(start, span), :]), MASK_VALUE)
        return s_loc, _dot_nt(qn, kcb), start

    pending = logits_of(0)
    for blk in range(nblk):
        upcoming = logits_of(blk + 1) if blk + 1 < nblk else None
        s_loc, s_ctx, start = pending
        o = _softmax_sink([s_loc, s_ctx], [v_scr[pl.ds(start, span), :], vcb], sink_col)
        for g, cols in enumerate(heads):
            o_ref[block_rows(blk), cols] = o[g * ATTN_BLOCK:(g + 1) * ATTN_BLOCK, :].astype(BF16)
        pending = upcoming


def _rope_tables(seq):
    quarter = HEAD_DIM // 4
    pos = np.arange(seq)
    inv_freq = ROPE_BASE ** (-np.arange(quarter, dtype=np.float32) / quarter)
    inv_freq = jnp.asarray(inv_freq, F32)
    zero = jnp.zeros((seq, quarter), F32)

    def trig(p):
        ang = jnp.asarray(p, F32)[:, None] * inv_freq
        return jnp.cos(ang), jnp.sin(ang)

    cr, sr = trig(pos // GRID_W)
    cc, sc = trig(pos % GRID_W)
    cos = jnp.concatenate([cr, cr, cc, cc], axis=-1)
    sin_a = jnp.concatenate([-sr, zero, -sc, zero], axis=-1)
    sin_b = jnp.concatenate([zero, sr, zero, sc], axis=-1)
    return cos, sin_a, sin_b


def _lat_attention(z, sink, q_g, k_g, k_ctx, v_ctx, tables, batch, seq):
    qw = Q_PER_KV * HEAD_DIM
    nqt = seq // LAT_Q_TILE
    past = k_ctx.shape[2]
    cos, sin_a, sin_b = tables
    full_tab = pl.BlockSpec((seq, HEAD_DIM), lambda b, kv, qt: (0, 0))
    tile_tab = pl.BlockSpec((LAT_Q_TILE, HEAD_DIM), lambda b, kv, qt: (qt, 0))
    ctx_spec = pl.BlockSpec((None, None, past, HEAD_DIM), lambda b, kv, qt: (b, kv, 0, 0))
    gain = pl.BlockSpec((1, HEAD_DIM), lambda b, kv, qt: (0, 0))
    return pl.pallas_call(
        _lat_attn_kernel,
        grid=(batch, N_KV_HEADS, nqt),
        in_specs=[
            pl.BlockSpec(memory_space=pltpu.SMEM),
            pl.BlockSpec((LAT_Q_TILE, qw), lambda b, kv, qt: (b * nqt + qt, kv)),
            pl.BlockSpec((seq, HEAD_DIM), lambda b, kv, qt: (b, COL_K + kv)),
            pl.BlockSpec((seq, HEAD_DIM), lambda b, kv, qt: (b, COL_V + kv)),
            ctx_spec, ctx_spec, gain, gain,
            full_tab, full_tab, full_tab, tile_tab, tile_tab, tile_tab,
        ],
        out_specs=pl.BlockSpec((LAT_Q_TILE, qw), lambda b, kv, qt: (b * nqt + qt, kv)),
        out_shape=jax.ShapeDtypeStruct((batch * seq, D_ATTN), BF16),
        scratch_shapes=[pltpu.VMEM((seq, HEAD_DIM), BF16), pltpu.VMEM((seq, 2 * HEAD_DIM), BF16),
                        pltpu.VMEM((past, HEAD_DIM), BF16), pltpu.VMEM((past, 2 * HEAD_DIM), BF16)],
        compiler_params=_params("parallel", "parallel", "arbitrary"),
        name="lat_attention",
    )(sink, z, z, z, k_ctx, v_ctx, q_g.reshape(1, HEAD_DIM), k_g.reshape(1, HEAD_DIM),
      cos, sin_a, sin_b, cos, sin_a, sin_b)


def _split3(x):
    hi = x.astype(BF16)
    r1 = x - hi.astype(F32)
    mid = r1.astype(BF16)
    lo = (r1 - mid.astype(F32)).astype(BF16)
    return hi, mid, lo


def _rec_gates(forward, zf, lb):
    c = REC_CHUNK
    t = jnp.exp(-jnp.abs(zf))
    r = 1.0 / (1.0 + t)
    tr = t * r
    nonneg = zf >= 0
    sig_pos = jnp.where(nonneg, r, tr)
    sig_neg = jnp.where(nonneg, tr, r)
    one_m_lb = 1.0 - lb
    log_f = jnp.log(jnp.maximum(lb + one_m_lb * sig_pos, GATE_FLOOR))
    k = one_m_lb * sig_neg
    ri = lax.broadcasted_iota(jnp.int32, (c, c), 0)
    ci = lax.broadcasted_iota(jnp.int32, (c, c), 1)
    tri = jnp.where((ci <= ri) if forward else (ci >= ri), 1.0, 0.0).astype(BF16)
    hi, mid, lo = _split3(log_f)
    cum2 = (_dot(tri, hi) + _dot(tri, mid) + _dot(tri, lo)) * LOG2_E
    return cum2 - jnp.log2(k), cum2


def _rec_same_block(forward, q, key_row, cum2):
    sub_row = lax.broadcasted_iota(jnp.int32, (SUBLANES, REC_DK), 0)
    pieces = []
    for b in range(REC_CHUNK // SUBLANES):
        blk = slice(b * SUBLANES, (b + 1) * SUBLANES)
        cum_b, q_b = cum2[blk, :], q[blk, :]
        for sl in range(SUBLANES):
            s = b * SUBLANES + sl
            keep = (sub_row >= sl) if forward else (sub_row <= sl)
            pieces.append(jnp.where(keep, jnp.exp2(cum_b - key_row(s)) * q_b, 0.0))
    return _dot(jnp.concatenate(pieces, axis=0).astype(BF16), jnp.ones((REC_DK, REC_DV), BF16))


def _rec_cross_block(forward, q, key2, v, cum2):
    c = REC_CHUNK
    q_parts, k_parts, v_parts, segments = [], [], [], []
    for tb in range(c // SUBLANES):
        blk = slice(tb * SUBLANES, (tb + 1) * SUBLANES)
        src, ref_row = (slice(0, blk.start), blk.start - 1) if forward else (slice(blk.stop, c), blk.stop)
        if src.stop == src.start:
            q_parts.append(jnp.zeros((SUBLANES, REC_DK), F32))
            continue
        ref = cum2[ref_row:ref_row + 1, :]
        q_parts.append(q[blk, :] * jnp.exp2(cum2[blk, :] - ref))
        k_parts.append(jnp.exp2(ref - key2[src, :]))
        v_parts.append(v[src, :])
        segments.append((tb, src.stop - src.start))
    a = _dot_nt(jnp.concatenate(q_parts, axis=0).astype(BF16), jnp.concatenate(k_parts, axis=0).astype(BF16))
    return a, segments, jnp.concatenate(v_parts, axis=0).astype(BF16)


def _rec_cross_apply(a, segments, v_all):
    c, ncol = a.shape
    row_blk = lax.broadcasted_iota(jnp.int32, (c, ncol), 0) // SUBLANES
    col = lax.broadcasted_iota(jnp.int32, (c, ncol), 1)
    col_blk = jnp.full((c, ncol), -1, jnp.int32)
    start = 0
    for tb, width in segments:
        col_blk = jnp.where((col >= start) & (col < start + width), tb, col_blk)
        start += width
    return _dot(jnp.where(row_blk == col_blk, a, 0.0).astype(BF16), v_all)


def _rec_same_apply(lane_sums, v_row):
    o_blocks = []
    for b in range(REC_CHUNK // SUBLANES):
        acc = None
        for sl in range(SUBLANES):
            s = b * SUBLANES + sl
            term = lane_sums[s * SUBLANES:(s + 1) * SUBLANES, :] * v_row(s)
            acc = term if acc is None else acc + term
        o_blocks.append(acc)
    return jnp.concatenate(o_blocks, axis=0)


def _rec_kernel(layer, has_init, *refs):
    q_ref, ff_ref, fb_ref, v_ref, g_ref, lg_ref, ng_ref = refs[:7]
    refs = refs[7:]
    if has_init:
        s0_ref, refs = refs[0], refs[1:]
    conv_in, refs = refs[:10], refs[10:]
    o_ref, conv_o_ref = refs[:2]
    refs = refs[2:]
    if not has_init:
        s_out_ref, refs = refs[0], refs[1:]
    o_scr, qd_scr, kd_scr, gl_scr, oi_scr, key_scr, glu_scr, conv_h_scr, conv_w_scr = refs
    seq = q_ref.shape[0]
    nchunk = seq // REC_CHUNK
    c = REC_CHUNK

    logits = lg_ref[...]
    e = jnp.exp(logits - jnp.max(logits, axis=0, keepdims=True))
    p = e / jnp.sum(e, axis=0, keepdims=True)
    lb = jnp.zeros_like(p[0])
    for i in range(1, layer + 1):
        lb = lb + p[i]

    def rows_of(ci):
        return pl.ds(pl.multiple_of(ci * c, c), c)

    z_refs = (ff_ref, fb_ref)

    def local_body(it, carry):
        chunks = [it * REC_LOCAL_UNROLL + u for u in range(REC_LOCAL_UNROLL)]
        qs = [q_ref[rows_of(ci), :] for ci in chunks]
        vs = [v_ref[rows_of(ci), :] for ci in chunks]
        items = [(u, d) for u in range(REC_LOCAL_UNROLL) for d in range(2)]
        gates = [_rec_gates(d == 0, z_refs[d][rows_of(chunks[u]), :], lb[d]) for u, d in items]
        for idx, (key2, _) in enumerate(gates):
            key_scr[idx] = key2
        key_rows = [lambda s, idx=idx: key_scr[idx, s:s + 1, :] for idx in range(len(items))]
        v_rows = [lambda s, ci=ci: v_ref[pl.ds(ci * c + s, 1), :] for ci in chunks]
        same = [_rec_same_block(d == 0, qs[u], key_rows[idx], cum2)
                for idx, ((u, d), (_, cum2)) in enumerate(zip(items, gates))]
        cross = [_rec_cross_block(d == 0, qs[u], key2, vs[u], cum2) for (u, d), (key2, cum2) in zip(items, gates)]
        totals = [None] * REC_LOCAL_UNROLL
        for (u, d), (key2, cum2), lane_sums, (a, segments, v_all) in zip(items, gates, same, cross):
            rows = rows_of(chunks[u])
            last = cum2[c - 1:c, :] if d == 0 else cum2[0:1, :]
            qd_scr[d, rows, :] = (qs[u] * jnp.exp2(cum2)).astype(BF16)
            kd_scr[d, rows, :] = jnp.exp2(last - key2).astype(BF16)
            gl_scr[d, chunks[u]] = jnp.broadcast_to(jnp.exp2(last), (SUBLANES, REC_DK))
            o_local = _rec_same_apply(lane_sums, v_rows[u]) + _rec_cross_apply(a, segments, v_all)
            totals[u] = o_local if totals[u] is None else totals[u] + o_local
        for u, ci in enumerate(chunks):
            o_scr[rows_of(ci), :] = totals[u]
        return carry

    lax.fori_loop(0, nchunk // REC_LOCAL_UNROLL, local_body, 0)

    if has_init:
        init = (s0_ref[0].T, s0_ref[1].T)
    else:
        init = (jnp.zeros((REC_DV, REC_DK), F32),) * 2

    state_unroll = min(REC_STATE_UNROLL, nchunk)
    conv_rows = conv_o_ref.shape[0] // (nchunk // state_unroll)
    _conv_stage(pl.program_id(1), N_REC_HEADS, *conv_in[:7], glu_scr, conv_w_scr)

    def state_body(it, states):
        offset = pl.multiple_of(it * conv_rows, conv_rows)
        conv_h_scr[0] = glu_scr[pl.ds(offset, conv_rows + 2 * CONV_HALO), :]

        def store(r0, val):
            conv_o_ref[pl.ds(pl.multiple_of(offset + r0, CONV_SUB), CONV_SUB), :] = val

        _conv_rows(conv_rows, conv_h_scr, conv_w_scr, *conv_in[7:], store)
        states = list(states)
        steps = [it * state_unroll + u for u in range(state_unroll)]
        order = [(d, step if d == 0 else nchunk - 1 - step) for step in steps for d in range(2)]
        updates = [_dot_tn(v_ref[rows_of(ci), :].astype(BF16), kd_scr[d, rows_of(ci), :]) for d, ci in order]
        for (d, ci), update in zip(order, updates):
            st = states[d]
            oi_scr[d, rows_of(ci), :] = _dot_nt(qd_scr[d, rows_of(ci), :], st.astype(BF16))
            decayed = (st.reshape(REC_DV // SUBLANES, SUBLANES, REC_DK) * gl_scr[d, ci]).reshape(REC_DV, REC_DK)
            states[d] = decayed + update
        return tuple(states)

    final = lax.fori_loop(0, nchunk // state_unroll, state_body, init)
    if not has_init:
        s_out_ref[0] = final[0].T
        s_out_ref[1] = final[1].T

    def out_body(ci, carry):
        rows = rows_of(ci)
        gate = g_ref[rows, :]
        o = o_scr[rows, :] + oi_scr[0, rows, :] + oi_scr[1, rows, :]
        o_ref[rows, :] = (_rms(o, ng_ref[...]) * (gate * _sigmoid(gate))).astype(BF16)
        return carry

    lax.fori_loop(0, nchunk, out_body, 0, unroll=8)


def _recurrence(z, layer, lb_logits, norm_g, s0, conv_params, batch, seq):
    has_init = s0 is not None
    n = batch * seq
    col = lambda base: pl.BlockSpec((seq, REC_DK), lambda b, h: (b, base + h))
    state_spec = pl.BlockSpec((None, 2, None, REC_DK, REC_DV), lambda b, h: (b, 0, h, 0, 0))
    in_specs = [col(COL_RQ), col(COL_RF_F), col(COL_RF_B), col(COL_RI), col(COL_RG),
                pl.BlockSpec((DEPTH, 2, None, 1, REC_DK), lambda b, h: (0, 0, h, 0, 0)),
                pl.BlockSpec((None, 1, REC_DV), lambda b, h: (h, 0, 0))]
    args = [z, z, z, z, z, lb_logits, norm_g]
    if has_init:
        in_specs.append(state_spec)
        args.append(s0)

    conv_step = seq // N_REC_HEADS
    state_trips = (seq // REC_CHUNK) // min(REC_STATE_UNROLL, seq // REC_CHUNK)
    conv_rows = conv_step // state_trips
    assert conv_step % CONV_HALO == 0 and conv_rows % CONV_SUB == 0
    cw = CONV_CH // HEAD_DIM
    halo_per_step = conv_step // CONV_HALO
    nhalo = n // CONV_HALO
    part = lambda b, h: b * N_REC_HEADS + h
    big_block = conv_step * CONV_CH * jnp.dtype(F32).itemsize >= SINGLE_BUFFER_BYTES
    mid_mode = dict(pipeline_mode=pl.Buffered(1)) if big_block else {}
    mid = lambda cc: pl.BlockSpec((conv_step, CONV_CH), lambda b, h: (part(b, h), cc), **mid_mode)
    top = lambda cc: pl.BlockSpec(
        (CONV_HALO, CONV_CH), lambda b, h: (jnp.maximum(part(b, h) * halo_per_step - 1, 0), cc))
    bot = lambda cc: pl.BlockSpec(
        (CONV_HALO, CONV_CH), lambda b, h: (jnp.minimum((part(b, h) + 1) * halo_per_step, nhalo - 1), cc))
    vec = pl.BlockSpec((1, CONV_CH), lambda b, h: (0, 0))
    ca, cb = COL_CA // cw, COL_CB // cw
    w, bias, ln_g, ln_b = conv_params
    in_specs += [mid(ca), mid(cb), top(ca), top(cb), bot(ca), bot(cb),
                 pl.BlockSpec((CONV_WIDTH, CONV_CH), lambda b, h: (0, 0)), vec, vec, vec]
    args += [z, z, z, z, z, z, w, bias.reshape(1, CONV_CH), ln_g.reshape(1, CONV_CH), ln_b.reshape(1, CONV_CH)]

    out_specs = [pl.BlockSpec((seq, REC_DV), lambda b, h: (b, h)),
                 pl.BlockSpec((conv_step, CONV_CH), lambda b, h: (part(b, h), 0))]
    out_shape = [jax.ShapeDtypeStruct((n, D_REC), BF16), jax.ShapeDtypeStruct((n, CONV_CH), BF16)]
    if not has_init:
        out_specs.append(state_spec)
        out_shape.append(jax.ShapeDtypeStruct((batch, 2, N_REC_HEADS, REC_DK, REC_DV), F32))
    return pl.pallas_call(
        functools.partial(_rec_kernel, layer, has_init),
        grid=(batch, N_REC_HEADS),
        in_specs=in_specs,
        out_specs=out_specs,
        out_shape=out_shape,
        scratch_shapes=[pltpu.VMEM((seq, REC_DV), F32),
                        pltpu.VMEM((2, seq, REC_DK), BF16),
                        pltpu.VMEM((2, seq, REC_DK), BF16),
                        pltpu.VMEM((2, seq // REC_CHUNK, SUBLANES, REC_DK), F32),
                        pltpu.VMEM((2, seq, REC_DV), F32),
                        pltpu.VMEM((2 * REC_LOCAL_UNROLL, REC_CHUNK, REC_DK), F32),
                        pltpu.VMEM((conv_step + 2 * CONV_HALO, CONV_CH), F32),
                        pltpu.VMEM((SUBLANES, conv_rows + 2 * CONV_HALO, CONV_CH), F32),
                        pltpu.VMEM((CONV_WIDTH, SUBLANES, CONV_CH), F32)],
        compiler_params=_params("parallel", "parallel"),
        name="hgrn2_conv",
    )(*args)


def _conv_stage(part, nparts, a_ref, b_ref, at_ref, bt_ref, ab_ref, bb_ref, w_ref, glu_scr, w_scr):
    has_top = jnp.where(part > 0, 1.0, 0.0)
    has_bot = jnp.where(part < nparts - 1, 1.0, 0.0)
    rows = a_ref.shape[0]

    def glu(a, b):
        return a * _sigmoid(b)

    glu_scr[0:CONV_HALO, :] = glu(at_ref[...], bt_ref[...]) * has_top
    glu_scr[CONV_HALO:CONV_HALO + rows, :] = glu(a_ref[...], b_ref[...])
    glu_scr[CONV_HALO + rows:, :] = glu(ab_ref[...], bb_ref[...]) * has_bot
    for j in range(CONV_WIDTH):
        w_scr[j] = jnp.broadcast_to(w_ref[j:j + 1, :], (SUBLANES, CONV_CH))


def _conv_rows(rows, h_scr, w_scr, bias_ref, lg_ref, lb_ref, store):
    half = CONV_WIDTH // 2
    shifted_rows = rows + 2 * CONV_HALO - SUBLANES
    for r in range(1, SUBLANES):
        h_scr[r, 0:shifted_rows, :] = h_scr[0, r:r + shifted_rows, :]
    for r0 in range(0, rows, CONV_SUB):
        acc = None
        for j in range(CONV_WIDTH):
            lo = r0 + CONV_HALO - half + j
            shift = lo % SUBLANES
            weight = w_scr[j]
            window = h_scr[shift, lo - shift:lo - shift + CONV_SUB, :]
            term = (window.reshape(CONV_SUB // SUBLANES, SUBLANES, CONV_CH) * weight).reshape(CONV_SUB, CONV_CH)
            acc = term if acc is None else acc + term
        y = acc + bias_ref[...]
        mu = jnp.mean(y, axis=-1, keepdims=True)
        yc = y - mu
        var = jnp.mean(yc * yc, axis=-1, keepdims=True)
        yn = yc * lax.rsqrt(var + EPS) * lg_ref[...] + lb_ref[...]
        store(r0, (yn * _sigmoid(yn)).astype(BF16))


def _outproj_kernel(x_ref, g_ref, a_ref, r_ref, c_ref, wa_ref, wr_ref, wc_ref, o_ref):
    mix = _dot(a_ref[...], wa_ref[...]) + _dot(r_ref[...], wr_ref[...]) + _dot(c_ref[...], wc_ref[...])
    o_ref[...] = x_ref[...] + g_ref[...] * mix


def _outproj(x, mod, attn, rec, conv, w_out, layer, rows_per_cond):
    n = x.shape[0]
    tn = D_MODEL // 2
    row = lambda width: pl.BlockSpec((TM, width), lambda i, j: (i, 0))
    return pl.pallas_call(
        _outproj_kernel,
        grid=(n // TM, D_MODEL // tn),
        in_specs=[
            pl.BlockSpec((TM, tn), lambda i, j: (i, j)),
            pl.BlockSpec((None, None, 1, tn), lambda i, j: ((i * TM) // rows_per_cond, 5, 0, j)),
            row(D_ATTN), row(D_REC), row(CONV_CH),
            pl.BlockSpec((None, D_ATTN, tn), lambda i, j: (layer, 0, j)),
            pl.BlockSpec((None, D_REC, tn), lambda i, j: (layer, D_ATTN // D_REC, j)),
            pl.BlockSpec((None, CONV_CH, tn), lambda i, j: (layer, (D_ATTN + D_REC) // CONV_CH, j)),
        ],
        out_specs=pl.BlockSpec((TM, tn), lambda i, j: (i, j)),
        out_shape=jax.ShapeDtypeStruct((n, D_MODEL), F32),
        compiler_params=_params("parallel", "parallel"),
        name="outproj",
    )(x, mod, attn, rec, conv, w_out, w_out, w_out)


def _trunk_layer(x, mod, l, P, mixers, rows_per_cond):
    x = _ffn(x, mod, 0, P['norm_g'][l, 0], P['w_ffn_in'], P['w_ffn_out'], l, 0, rows_per_cond)
    z = _inproj(x, mod, P['norm_g'][l, 1], P['w_in'], l, rows_per_cond)
    attn, rec, conv, extras = mixers(z)
    x = _outproj(x, mod, attn, rec, conv, P['w_out'], l, rows_per_cond)
    x = _ffn(x, mod, 6, P['norm_g'][l, 2], P['w_ffn_in'], P['w_ffn_out'], l, 1, rows_per_cond)
    return x, extras


def kernel(x_prompt, x_sample, cache_k, cache_v, state_rec, c, c_ctx, w_ada, b_ada, norm_g, w_ffn_in, w_ffn_out,
           w_in, w_out, q_norm_g, k_norm_g, attn_sink, rec_lb_logits, rec_norm_g, conv_w, conv_b, conv_ln_g, conv_ln_b):
    batch, seq, _ = x_prompt.shape
    dec_batch, dec_seq, _ = x_sample.shape
    assert seq % TM == 0 or TM % seq == 0
    assert dec_seq % TM == 0 and dec_seq % LAT_Q_TILE == 0

    P = {'norm_g': norm_g, 'w_ffn_in': w_ffn_in.astype(BF16), 'w_ffn_out': w_ffn_out.astype(BF16),
         'w_in': w_in.astype(BF16), 'w_out': w_out.astype(BF16)}

    lb_logits = rec_lb_logits.reshape(DEPTH, 2, N_REC_HEADS, 1, REC_DK)

    cond8 = jnp.zeros((SUBLANES, D_MODEL), F32).at[0].set(c_ctx).at[1:1 + dec_batch].set(c)
    mod = _modulation(cond8, w_ada, b_ada).reshape(DEPTH, SUBLANES, N_MOD, 1, D_MODEL)
    tables = _rope_tables(dec_seq)
    sinks = attn_sink.reshape(DEPTH, N_KV_HEADS, Q_PER_KV)
    rec_g = rec_norm_g.reshape(DEPTH, N_REC_HEADS, 1, REC_DV)

    conv_params = lambda l: (conv_w[l], conv_b[l], conv_ln_g[l], conv_ln_b[l])

    h = x_prompt.reshape(batch * seq, D_MODEL)
    ks, vs, ss = [], [], []
    for l in range(DEPTH):
        def ctx_mixers(z, l=l):
            attn, k_l, v_l = _ctx_attention(z, sinks[l], q_norm_g[l], k_norm_g[l], batch, seq)
            rec, conv, s_l = _recurrence(z, l, lb_logits, rec_g[l], None, conv_params(l), batch, seq)
            return attn, rec, conv, (k_l, v_l, s_l)

        h, (k_l, v_l, s_l) = _trunk_layer(h, mod[l, 0:1], l, P, ctx_mixers, batch * seq)
        ks.append(k_l)
        vs.append(v_l)
        ss.append(s_l)
    y_prompt = h.reshape(batch, seq, D_MODEL)

    h = x_sample.reshape(dec_batch * dec_seq, D_MODEL)
    for l in range(DEPTH):
        def lat_mixers(z, l=l):
            attn = _lat_attention(z, sinks[l], q_norm_g[l], k_norm_g[l], cache_k[:, l], cache_v[:, l], tables,
                                  dec_batch, dec_seq)
            rec, conv = _recurrence(z, l, lb_logits, rec_g[l], state_rec[:, l], conv_params(l), dec_batch, dec_seq)
            return attn, rec, conv, None

        h, _ = _trunk_layer(h, mod[l, 1:1 + dec_batch], l, P, lat_mixers, dec_seq)
    y_sample = h.reshape(dec_batch, dec_seq, D_MODEL)

    return (y_prompt, y_sample, jnp.stack(ks, axis=1), jnp.stack(vs, axis=1), jnp.stack(ss, axis=1))
```

```python
import functools

import jax
import jax.numpy as jnp
import numpy as np
from jax import lax
from jax.experimental import pallas as pl
from jax.experimental.pallas import tpu as pltpu

F32 = jnp.float32
BF16 = jnp.bfloat16

D_MODEL = 2048
DEPTH = 2
GRID_W = 64
HEAD_DIM = 128
D_ATTN = D_MODEL // 2
N_Q_HEADS = D_ATTN // HEAD_DIM
N_KV_HEADS = 2
Q_PER_KV = N_Q_HEADS // N_KV_HEADS
WINDOW = 128
ATTN_BLOCK = 128
ATTN_SCALE = HEAD_DIM ** -0.5
ROPE_BASE = 10000.0
MASK_VALUE = -1e30
D_REC = D_MODEL // 4
REC_DK = 128
REC_DV = 128
N_REC_HEADS = D_REC // REC_DV
REC_CHUNK = 32
CONV_CH = D_MODEL // 4
CONV_WIDTH = 31
D_FF = 5632
N_MOD = 9
EPS = 1e-6
GATE_FLOOR = 1e-30
LOG2_E = 1.4426950408889634
IN_COLS = 5120

COL_Q = 0
COL_K = 8
COL_V = 10
COL_RQ = 12
COL_RF_F = 16
COL_RF_B = 20
COL_RI = 24
COL_RG = 28
COL_CA = 32
COL_CB = 36

SUBLANES = 8
V7X_VMEM_BYTES = 64 * 1024 * 1024
VMEM_LIMIT = V7X_VMEM_BYTES * 7 // 8
SINGLE_BUFFER_BYTES = 1024 * 1024

TM = 1024
TM_FFN = 1024
TF = 512
TN_FFN_OUT = 256
TN_IN = 1024
TN_ADA = 1024
CONV_HALO = 16
CONV_SUB = 32
LAT_Q_TILE = 1024
CTX_SEQS_PER_STEP = 4
REC_LOCAL_UNROLL = 8
REC_STATE_UNROLL = 16


def _params(*sem):
    return pltpu.CompilerParams(dimension_semantics=sem, vmem_limit_bytes=VMEM_LIMIT)


def _dot(a, b):
    return jnp.dot(a, b, preferred_element_type=F32)


def _dot_nt(a, b):
    return lax.dot_general(a, b, (((1,), (1,)), ((), ())), preferred_element_type=F32)


def _dot_tn(a, b):
    return lax.dot_general(a, b, (((0,), (0,)), ((), ())), preferred_element_type=F32)


def _rms(x, g):
    return x * lax.rsqrt(jnp.mean(x * x, axis=-1, keepdims=True) + EPS) * g


def _ada_norm(x, g, scale, shift):
    gain = g * (1.0 + scale)
    return x * lax.rsqrt(jnp.mean(x * x, axis=-1, keepdims=True) + EPS) * gain + shift


def _sigmoid(x):
    return 1.0 / (1.0 + jnp.exp(-x))


def _ada_kernel(c_ref, w_ref, b_ref, o_ref):
    c = c_ref[...]
    s = (c * _sigmoid(c)).astype(BF16)
    o_ref[...] = _dot(s, w_ref[...].astype(BF16)) + b_ref[...]


def _modulation(cond8, w_ada, b_ada):
    ncol = N_MOD * D_MODEL
    return pl.pallas_call(
        _ada_kernel,
        grid=(DEPTH, ncol // TN_ADA),
        in_specs=[
            pl.BlockSpec((SUBLANES, D_MODEL), lambda l, j: (0, 0)),
            pl.BlockSpec((None, D_MODEL, TN_ADA), lambda l, j: (l, 0, j)),
            pl.BlockSpec((None, 1, TN_ADA), lambda l, j: (l, 0, j)),
        ],
        out_specs=pl.BlockSpec((None, SUBLANES, TN_ADA), lambda l, j: (l, 0, j)),
        out_shape=jax.ShapeDtypeStruct((DEPTH, SUBLANES, ncol), F32),
        compiler_params=_params("parallel", "parallel"),
        name="modulation",
    )(cond8, w_ada, b_ada.reshape(DEPTH, 1, ncol))


def _mod_spec(chunk, rows_per_cond, tm=TM):
    return pl.BlockSpec((None, None, 1, D_MODEL), lambda i, j: ((i * tm) // rows_per_cond, chunk, 0, 0))


def _ffn_kernel(nf, x_ref, sh_ref, sc_ref, g_ref, ng_ref, wa_ref, wb_ref, wo_ref, o_ref, h_scr, act_scr):
    j = pl.program_id(1)

    @pl.when(j == 0)
    def _():
        h = _ada_norm(x_ref[...], ng_ref[...], sc_ref[...], sh_ref[...])
        h_scr[...] = h.astype(BF16)

    @pl.when(j < nf)
    def _():
        h = h_scr[...]
        a = _dot(h, wa_ref[...])
        b = _dot(h, wb_ref[...])
        act_scr[:, pl.ds(pl.multiple_of(j * TF, TF), TF)] = (a * _sigmoid(a) * b).astype(BF16)

    @pl.when(j >= nf)
    def _():
        cols = pl.ds(pl.multiple_of((j - nf) * TN_FFN_OUT, TN_FFN_OUT), TN_FFN_OUT)
        o_ref[...] = x_ref[:, cols] + 0.5 * g_ref[...] * _dot(act_scr[...], wo_ref[...])


def _ffn(x, mod, first_chunk, norm_g, w_in, w_out, layer, which, rows_per_cond):
    n = x.shape[0]
    nf = D_FF // TF
    nout = D_MODEL // TN_FFN_OUT
    hid = lambda j: jnp.minimum(j, nf - 1)
    out = lambda j: jnp.maximum(j - nf, 0)
    return pl.pallas_call(
        functools.partial(_ffn_kernel, nf),
        grid=(n // TM_FFN, nf + nout),
        in_specs=[
            pl.BlockSpec((TM_FFN, D_MODEL), lambda i, j: (i, 0)),
            _mod_spec(first_chunk, rows_per_cond, TM_FFN),
            _mod_spec(first_chunk + 1, rows_per_cond, TM_FFN),
            pl.BlockSpec((None, None, 1, TN_FFN_OUT),
                         lambda i, j: ((i * TM_FFN) // rows_per_cond, first_chunk + 2, 0, out(j))),
            pl.BlockSpec((1, D_MODEL), lambda i, j: (0, 0)),
            pl.BlockSpec((None, None, D_MODEL, TF), lambda i, j: (layer, which, 0, hid(j))),
            pl.BlockSpec((None, None, D_MODEL, TF), lambda i, j: (layer, which, 0, hid(j) + nf)),
            pl.BlockSpec((None, None, D_FF, TN_FFN_OUT), lambda i, j: (layer, which, 0, out(j))),
        ],
        out_specs=pl.BlockSpec((TM_FFN, TN_FFN_OUT), lambda i, j: (i, out(j))),
        out_shape=jax.ShapeDtypeStruct((n, D_MODEL), F32),
        scratch_shapes=[pltpu.VMEM((TM_FFN, D_MODEL), BF16), pltpu.VMEM((TM_FFN, D_FF), BF16)],
        compiler_params=_params("parallel", "arbitrary"),
        name="ffn",
    )(x, mod, mod, mod, norm_g.reshape(1, D_MODEL), w_in, w_in, w_out)


def _inproj_kernel(x_ref, sh_ref, sc_ref, ng_ref, w_ref, o_ref, h_scr):
    @pl.when(pl.program_id(1) == 0)
    def _():
        h = _ada_norm(x_ref[...], ng_ref[...], sc_ref[...], sh_ref[...])
        h_scr[...] = h.astype(BF16)

    o_ref[...] = _dot(h_scr[...], w_ref[...])


def _inproj(x, mod, norm_g, w_in, layer, rows_per_cond):
    n = x.shape[0]
    return pl.pallas_call(
        _inproj_kernel,
        grid=(n // TM, IN_COLS // TN_IN),
        in_specs=[
            pl.BlockSpec((TM, D_MODEL), lambda i, j: (i, 0)),
            _mod_spec(3, rows_per_cond),
            _mod_spec(4, rows_per_cond),
            pl.BlockSpec((1, D_MODEL), lambda i, j: (0, 0)),
            pl.BlockSpec((None, D_MODEL, TN_IN), lambda i, j: (layer, 0, j)),
        ],
        out_specs=pl.BlockSpec((TM, TN_IN), lambda i, j: (i, j)),
        out_shape=jax.ShapeDtypeStruct((n, IN_COLS), F32),
        scratch_shapes=[pltpu.VMEM((TM, D_MODEL), BF16)],
        compiler_params=_params("parallel", "arbitrary"),
        name="inproj",
    )(x, mod, mod, norm_g.reshape(1, D_MODEL), w_in)


QK_SCALE_LOG2 = ATTN_SCALE * LOG2_E


def _with_ones(v):
    return jnp.concatenate([v, jnp.ones_like(v)], axis=-1)


def _softmax_sink(scores, values, sink):
    sink2 = sink * LOG2_E
    m = sink2
    for s in scores:
        m = jnp.maximum(jnp.max(s, axis=-1, keepdims=True), m)
    acc = None
    for s, v in zip(scores, values):
        pv = _dot(jnp.exp2(s - m).astype(BF16), v)
        acc = pv if acc is None else acc + pv
    return acc[:, :HEAD_DIM] / (acc[:, HEAD_DIM:] + jnp.exp2(sink2 - m))


def _put_layer(ref, first_layer, lead, tail, value):
    if first_layer:
        ref[lead + (0,) + tail] = value
        for other in range(1, DEPTH):
            ref[lead + (other,) + tail] = jnp.zeros_like(value)
    else:
        ref[lead + tail] = value


def _ctx_attn_kernel(first_layer, sink_ref, q_ref, k_ref, v_ref, qg_ref, kg_ref, *rest):
    o_ref, kc_ref, vc_ref = rest[-3:]
    kv = pl.program_id(1)
    nseq, seq = kc_ref.shape[0], kc_ref.shape[-2]
    heads = [slice(g * HEAD_DIM, (g + 1) * HEAD_DIM) for g in range(Q_PER_KV)]
    keys, values = [], []
    for i in range(nseq):
        rows = slice(i * seq, (i + 1) * seq)
        kn = _rms(k_ref[rows, :], kg_ref[...])
        v = v_ref[rows, :]
        _put_layer(kc_ref, first_layer, (i,), (), kn)
        _put_layer(vc_ref, first_layer, (i,), (), v)
        keys.append(kn.astype(BF16))
        values.append(_with_ones(v.astype(BF16)))
    logits = [[_dot_nt((_rms(q_ref[i * seq:(i + 1) * seq, cols], qg_ref[...]) * QK_SCALE_LOG2).astype(BF16), keys[i])
               for cols in heads] for i in range(nseq)]
    for i in range(nseq):
        for g, cols in enumerate(heads):
            o = _softmax_sink([logits[i][g]], [values[i]], sink_ref[kv, g])
            o_ref[i * seq:(i + 1) * seq, cols] = o.astype(BF16)


def _ctx_attention(z, sink, q_g, k_g, batch, seq, layer, caches):
    qw = Q_PER_KV * HEAD_DIM
    per = CTX_SEQS_PER_STEP
    first = caches is None
    cache_shape = jax.ShapeDtypeStruct((batch, DEPTH, N_KV_HEADS, seq, HEAD_DIM), F32)
    if first:
        cache_spec = pl.BlockSpec((per, DEPTH, None, seq, HEAD_DIM), lambda b, kv: (b, 0, kv, 0, 0))
    else:
        cache_spec = pl.BlockSpec((per, None, None, seq, HEAD_DIM), lambda b, kv: (b, layer, kv, 0, 0))
    in_specs = [
        pl.BlockSpec(memory_space=pltpu.SMEM),
        pl.BlockSpec((per * seq, qw), lambda b, kv: (b, kv)),
        pl.BlockSpec((per * seq, HEAD_DIM), lambda b, kv: (b, COL_K + kv)),
        pl.BlockSpec((per * seq, HEAD_DIM), lambda b, kv: (b, COL_V + kv)),
        pl.BlockSpec((1, HEAD_DIM), lambda b, kv: (0, 0)),
        pl.BlockSpec((1, HEAD_DIM), lambda b, kv: (0, 0)),
    ]
    args = [sink, z, z, z, q_g.reshape(1, HEAD_DIM), k_g.reshape(1, HEAD_DIM)]
    aliases = {}
    if not first:
        aliases = {len(args): 1, len(args) + 1: 2}
        in_specs += [pl.BlockSpec(memory_space=pl.ANY)] * 2
        args += list(caches)
    return pl.pallas_call(
        functools.partial(_ctx_attn_kernel, first),
        grid=(batch // per, N_KV_HEADS),
        in_specs=in_specs,
        out_specs=[pl.BlockSpec((per * seq, qw), lambda b, kv: (b, kv)), cache_spec, cache_spec],
        out_shape=[jax.ShapeDtypeStruct((batch * seq, D_ATTN), BF16), cache_shape, cache_shape],
        input_output_aliases=aliases,
        compiler_params=_params("parallel", "parallel"),
        name="ctx_attention",
    )(*args)


def _rope(x, cos, sin_a, sin_b):
    quarter = HEAD_DIM // 4
    up = pltpu.roll(x, HEAD_DIM - quarter, 1)
    down = pltpu.roll(x, quarter, 1)
    return x * cos + up * sin_a + down * sin_b


def _lat_attn_kernel(sink_ref, q_ref, k_ref, v_ref, kc_ref, vc_ref, qg_ref, kg_ref,
                     cos_ref, sa_ref, sb_ref, cosq_ref, saq_ref, sbq_ref, o_ref, k_scr, v_scr, kc_scr, vc_scr):
    kv = pl.program_id(1)
    qt = pl.program_id(2)
    seq = k_ref.shape[0]
    span = 3 * ATTN_BLOCK

    @pl.when(qt == 0)
    def _():
        kn = _rms(k_ref[...], kg_ref[...])
        k_scr[...] = _rope(kn, cos_ref[...], sa_ref[...], sb_ref[...]).astype(BF16)
        v_scr[...] = _with_ones(v_ref[...].astype(BF16))
        kc_scr[...] = kc_ref[...].astype(BF16)
        vc_scr[...] = _with_ones(vc_ref[...].astype(BF16))

    kcb = kc_scr[...]
    vcb = vc_scr[...]
    nblk = LAT_Q_TILE // ATTN_BLOCK
    heads = [slice(g * HEAD_DIM, (g + 1) * HEAD_DIM) for g in range(Q_PER_KV)]
    stacked = Q_PER_KV * ATTN_BLOCK

    head_of_row = lax.broadcasted_iota(jnp.int32, (stacked, 1), 0) // ATTN_BLOCK
    sink_col = jnp.zeros((stacked, 1), F32)
    for g in range(Q_PER_KV):
        sink_col = jnp.where(head_of_row == g, sink_ref[kv, g], sink_col)

    def block_rows(blk):
        return slice(blk * ATTN_BLOCK, (blk + 1) * ATTN_BLOCK)

    def logits_of(blk):
        rows = block_rows(blk)
        q0 = (qt * nblk + blk) * ATTN_BLOCK
        start = pl.multiple_of(jnp.clip(q0 - ATTN_BLOCK, 0, seq - span), ATTN_BLOCK)
        qpos = q0 + lax.broadcasted_iota(jnp.int32, (stacked, span), 0) % ATTN_BLOCK
        kpos = start + lax.broadcasted_iota(jnp.int32, (stacked, span), 1)
        valid = jnp.abs(qpos - kpos) <= WINDOW
        cos, sa, sb = cosq_ref[rows, :], saq_ref[rows, :], sbq_ref[rows, :]
        qn = jnp.concatenate([_rope(_rms(q_ref[rows, cols], qg_ref[...]), cos, sa, sb) for cols in heads], axis=0)
        qn = (qn * QK_SCALE_LOG2).astype(BF16)
        s_loc = jnp.where(valid, _dot_nt(qn, k_scr[pl.ds(start, span), :]), MASK_VALUE)
        return s_loc, _dot_nt(qn, kcb), start

    pending = logits_of(0)
    for blk in range(nblk):
        upcoming = logits_of(blk + 1) if blk + 1 < nblk else None
        s_loc, s_ctx, start = pending
        o = _softmax_sink([s_loc, s_ctx], [v_scr[pl.ds(start, span), :], vcb], sink_col)
        for g, cols in enumerate(heads):
            o_ref[block_rows(blk), cols] = o[g * ATTN_BLOCK:(g + 1) * ATTN_BLOCK, :].astype(BF16)
        pending = upcoming


def _rope_tables(seq):
    quarter = HEAD_DIM // 4
    pos = np.arange(seq)
    inv_freq = ROPE_BASE ** (-np.arange(quarter, dtype=np.float32) / quarter)
    inv_freq = jnp.asarray(inv_freq, F32)
    zero = jnp.zeros((seq, quarter), F32)

    def trig(p):
        ang = jnp.asarray(p, F32)[:, None] * inv_freq
        return jnp.cos(ang), jnp.sin(ang)

    cr, sr = trig(pos // GRID_W)
    cc, sc = trig(pos % GRID_W)
    cos = jnp.concatenate([cr, cr, cc, cc], axis=-1)
    sin_a = jnp.concatenate([-sr, zero, -sc, zero], axis=-1)
    sin_b = jnp.concatenate([zero, sr, zero, sc], axis=-1)
    return cos, sin_a, sin_b


def _lat_attention(z, sink, q_g, k_g, k_ctx, v_ctx, tables, batch, seq):
    qw = Q_PER_KV * HEAD_DIM
    nqt = seq // LAT_Q_TILE
    past = k_ctx.shape[2]
    cos, sin_a, sin_b = tables
    full_tab = pl.BlockSpec((seq, HEAD_DIM), lambda b, kv, qt: (0, 0))
    tile_tab = pl.BlockSpec((LAT_Q_TILE, HEAD_DIM), lambda b, kv, qt: (qt, 0))
    ctx_spec = pl.BlockSpec((None, None, past, HEAD_DIM), lambda b, kv, qt: (b, kv, 0, 0))
    gain = pl.BlockSpec((1, HEAD_DIM), lambda b, kv, qt: (0, 0))
    return pl.pallas_call(
        _lat_attn_kernel,
        grid=(batch, N_KV_HEADS, nqt),
        in_specs=[
            pl.BlockSpec(memory_space=pltpu.SMEM),
            pl.BlockSpec((LAT_Q_TILE, qw), lambda b, kv, qt: (b * nqt + qt, kv)),
            pl.BlockSpec((seq, HEAD_DIM), lambda b, kv, qt: (b, COL_K + kv)),
            pl.BlockSpec((seq, HEAD_DIM), lambda b, kv, qt: (b, COL_V + kv)),
            ctx_spec, ctx_spec, gain, gain,
            full_tab, full_tab, full_tab, tile_tab, tile_tab, tile_tab,
        ],
        out_specs=pl.BlockSpec((LAT_Q_TILE, qw), lambda b, kv, qt: (b * nqt + qt, kv)),
        out_shape=jax.ShapeDtypeStruct((batch * seq, D_ATTN), BF16),
        scratch_shapes=[pltpu.VMEM((seq, HEAD_DIM), BF16), pltpu.VMEM((seq, 2 * HEAD_DIM), BF16),
                        pltpu.VMEM((past, HEAD_DIM), BF16), pltpu.VMEM((past, 2 * HEAD_DIM), BF16)],
        compiler_params=_params("parallel", "parallel", "arbitrary"),
        name="lat_attention",
    )(sink, z, z, z, k_ctx, v_ctx, q_g.reshape(1, HEAD_DIM), k_g.reshape(1, HEAD_DIM),
      cos, sin_a, sin_b, cos, sin_a, sin_b)


def _split3(x):
    hi = x.astype(BF16)
    r1 = x - hi.astype(F32)
    mid = r1.astype(BF16)
    lo = (r1 - mid.astype(F32)).astype(BF16)
    return hi, mid, lo


def _rec_gates(forward, zf, lb):
    c = REC_CHUNK
    t = jnp.exp(-jnp.abs(zf))
    r = 1.0 / (1.0 + t)
    tr = t * r
    nonneg = zf >= 0
    sig_pos = jnp.where(nonneg, r, tr)
    sig_neg = jnp.where(nonneg, tr, r)
    one_m_lb = 1.0 - lb
    log_f = jnp.log(jnp.maximum(lb + one_m_lb * sig_pos, GATE_FLOOR))
    k = one_m_lb * sig_neg
    ri = lax.broadcasted_iota(jnp.int32, (c, c), 0)
    ci = lax.broadcasted_iota(jnp.int32, (c, c), 1)
    tri = jnp.where((ci <= ri) if forward else (ci >= ri), 1.0, 0.0).astype(BF16)
    hi, mid, lo = _split3(log_f)
    cum2 = (_dot(tri, hi) + _dot(tri, mid) + _dot(tri, lo)) * LOG2_E
    return cum2 - jnp.log2(k), cum2


def _rec_same_block(forward, q, key_row, cum2):
    sub_row = lax.broadcasted_iota(jnp.int32, (SUBLANES, REC_DK), 0)
    pieces = []
    for b in range(REC_CHUNK // SUBLANES):
        blk = slice(b * SUBLANES, (b + 1) * SUBLANES)
        cum_b, q_b = cum2[blk, :], q[blk, :]
        for sl in range(SUBLANES):
            s = b * SUBLANES + sl
            keep = (sub_row >= sl) if forward else (sub_row <= sl)
            pieces.append(jnp.where(keep, jnp.exp2(cum_b - key_row(s)) * q_b, 0.0))
    return _dot(jnp.concatenate(pieces, axis=0).astype(BF16), jnp.ones((REC_DK, REC_DV), BF16))


def _rec_cross_block(forward, q, key2, v, cum2):
    c = REC_CHUNK
    q_parts, k_parts, v_parts, segments = [], [], [], []
    for tb in range(c // SUBLANES):
        blk = slice(tb * SUBLANES, (tb + 1) * SUBLANES)
        src, ref_row = (slice(0, blk.start), blk.start - 1) if forward else (slice(blk.stop, c), blk.stop)
        if src.stop == src.start:
            q_parts.append(jnp.zeros((SUBLANES, REC_DK), F32))
            continue
        ref = cum2[ref_row:ref_row + 1, :]
        q_parts.append(q[blk, :] * jnp.exp2(cum2[blk, :] - ref))
        k_parts.append(jnp.exp2(ref - key2[src, :]))
        v_parts.append(v[src, :])
        segments.append((tb, src.stop - src.start))
    a = _dot_nt(jnp.concatenate(q_parts, axis=0).astype(BF16), jnp.concatenate(k_parts, axis=0).astype(BF16))
    return a, segments, jnp.concatenate(v_parts, axis=0).astype(BF16)


def _rec_cross_apply(a, segments, v_all):
    c, ncol = a.shape
    row_blk = lax.broadcasted_iota(jnp.int32, (c, ncol), 0) // SUBLANES
    col = lax.broadcasted_iota(jnp.int32, (c, ncol), 1)
    col_blk = jnp.full((c, ncol), -1, jnp.int32)
    start = 0
    for tb, width in segments:
        col_blk = jnp.where((col >= start) & (col < start + width), tb, col_blk)
        start += width
    return _dot(jnp.where(row_blk == col_blk, a, 0.0).astype(BF16), v_all)


def _rec_same_apply(lane_sums, v_row):
    o_blocks = []
    for b in range(REC_CHUNK // SUBLANES):
        acc = None
        for sl in range(SUBLANES):
            s = b * SUBLANES + sl
            term = lane_sums[s * SUBLANES:(s + 1) * SUBLANES, :] * v_row(s)
            acc = term if acc is None else acc + term
        o_blocks.append(acc)
    return jnp.concatenate(o_blocks, axis=0)


def _rec_kernel(layer, has_init, *refs):
    q_ref, ff_ref, fb_ref, v_ref, g_ref, lg_ref, ng_ref = refs[:7]
    refs = refs[7:]
    if has_init:
        s0_ref, refs = refs[0], refs[1:]
    conv_in, refs = refs[:10], refs[10:]
    if not has_init and layer > 0:
        refs = refs[1:]
    o_ref, conv_o_ref = refs[:2]
    refs = refs[2:]
    if not has_init:
        s_out_ref, refs = refs[0], refs[1:]
    o_scr, qd_scr, kd_scr, gl_scr, oi_scr, key_scr, glu_scr, conv_h_scr, conv_w_scr = refs
    seq = q_ref.shape[0]
    nchunk = seq // REC_CHUNK
    c = REC_CHUNK

    logits = lg_ref[...]
    e = jnp.exp(logits - jnp.max(logits, axis=0, keepdims=True))
    p = e / jnp.sum(e, axis=0, keepdims=True)
    lb = jnp.zeros_like(p[0])
    for i in range(1, layer + 1):
        lb = lb + p[i]

    def rows_of(ci):
        return pl.ds(pl.multiple_of(ci * c, c), c)

    z_refs = (ff_ref, fb_ref)

    def local_body(it, carry):
        chunks = [it * REC_LOCAL_UNROLL + u for u in range(REC_LOCAL_UNROLL)]
        qs = [q_ref[rows_of(ci), :] for ci in chunks]
        vs = [v_ref[rows_of(ci), :] for ci in chunks]
        items = [(u, d) for u in range(REC_LOCAL_UNROLL) for d in range(2)]
        gates = [_rec_gates(d == 0, z_refs[d][rows_of(chunks[u]), :], lb[d]) for u, d in items]
        for idx, (key2, _) in enumerate(gates):
            key_scr[idx] = key2
        key_rows = [lambda s, idx=idx: key_scr[idx, s:s + 1, :] for idx in range(len(items))]
        v_rows = [lambda s, ci=ci: v_ref[pl.ds(ci * c + s, 1), :] for ci in chunks]
        same = [_rec_same_block(d == 0, qs[u], key_rows[idx], cum2)
                for idx, ((u, d), (_, cum2)) in enumerate(zip(items, gates))]
        cross = [_rec_cross_block(d == 0, qs[u], key2, vs[u], cum2) for (u, d), (key2, cum2) in zip(items, gates)]
        totals = [None] * REC_LOCAL_UNROLL
        for (u, d), (key2, cum2), lane_sums, (a, segments, v_all) in zip(items, gates, same, cross):
            rows = rows_of(chunks[u])
            last = cum2[c - 1:c, :] if d == 0 else cum2[0:1, :]
            qd_scr[d, rows, :] = (qs[u] * jnp.exp2(cum2)).astype(BF16)
            kd_scr[d, rows, :] = jnp.exp2(last - key2).astype(BF16)
            gl_scr[d, chunks[u]] = jnp.broadcast_to(jnp.exp2(last), (SUBLANES, REC_DK))
            o_local = _rec_same_apply(lane_sums, v_rows[u]) + _rec_cross_apply(a, segments, v_all)
            totals[u] = o_local if totals[u] is None else totals[u] + o_local
        for u, ci in enumerate(chunks):
            o_scr[rows_of(ci), :] = totals[u]
        return carry

    lax.fori_loop(0, nchunk // REC_LOCAL_UNROLL, local_body, 0)

    if has_init:
        init = (s0_ref[0].T, s0_ref[1].T)
    else:
        init = (jnp.zeros((REC_DV, REC_DK), F32),) * 2

    state_unroll = min(REC_STATE_UNROLL, nchunk)
    conv_rows = conv_o_ref.shape[0] // (nchunk // state_unroll)
    _conv_stage(pl.program_id(1), N_REC_HEADS, *conv_in[:7], glu_scr, conv_w_scr)

    def state_body(it, states):
        offset = pl.multiple_of(it * conv_rows, conv_rows)
        conv_h_scr[0] = glu_scr[pl.ds(offset, conv_rows + 2 * CONV_HALO), :]

        def store(r0, val):
            conv_o_ref[pl.ds(pl.multiple_of(offset + r0, CONV_SUB), CONV_SUB), :] = val

        _conv_rows(conv_rows, conv_h_scr, conv_w_scr, *conv_in[7:], store)
        states = list(states)
        steps = [it * state_unroll + u for u in range(state_unroll)]
        order = [(d, step if d == 0 else nchunk - 1 - step) for step in steps for d in range(2)]
        updates = [_dot_tn(v_ref[rows_of(ci), :].astype(BF16), kd_scr[d, rows_of(ci), :]) for d, ci in order]
        for (d, ci), update in zip(order, updates):
            st = states[d]
            oi_scr[d, rows_of(ci), :] = _dot_nt(qd_scr[d, rows_of(ci), :], st.astype(BF16))
            decayed = (st.reshape(REC_DV // SUBLANES, SUBLANES, REC_DK) * gl_scr[d, ci]).reshape(REC_DV, REC_DK)
            states[d] = decayed + update
        return tuple(states)

    final = lax.fori_loop(0, nchunk // state_unroll, state_body, init)
    if not has_init:
        for d in range(2):
            _put_layer(s_out_ref, layer == 0, (), (d,), final[d].T)

    def out_body(ci, carry):
        rows = rows_of(ci)
        gate = g_ref[rows, :]
        o = o_scr[rows, :] + oi_scr[0, rows, :] + oi_scr[1, rows, :]
        o_ref[rows, :] = (_rms(o, ng_ref[...]) * (gate * _sigmoid(gate))).astype(BF16)
        return carry

    lax.fori_loop(0, nchunk, out_body, 0, unroll=8)


def _recurrence(z, layer, lb_logits, norm_g, s0, conv_params, batch, seq, states=None):
    has_init = s0 is not None
    assert has_init or (states is None) == (layer == 0)
    n = batch * seq
    col = lambda base: pl.BlockSpec((seq, REC_DK), lambda b, h: (b, base + h))
    state_spec = pl.BlockSpec((None, 2, None, REC_DK, REC_DV), lambda b, h: (b, 0, h, 0, 0))
    in_specs = [col(COL_RQ), col(COL_RF_F), col(COL_RF_B), col(COL_RI), col(COL_RG),
                pl.BlockSpec((DEPTH, 2, None, 1, REC_DK), lambda b, h: (0, 0, h, 0, 0)),
                pl.BlockSpec((None, 1, REC_DV), lambda b, h: (h, 0, 0))]
    args = [z, z, z, z, z, lb_logits, norm_g]
    if has_init:
        in_specs.append(state_spec)
        args.append(s0)

    conv_step = seq // N_REC_HEADS
    state_trips = (seq // REC_CHUNK) // min(REC_STATE_UNROLL, seq // REC_CHUNK)
    conv_rows = conv_step // state_trips
    assert conv_step % CONV_HALO == 0 and conv_rows % CONV_SUB == 0
    cw = CONV_CH // HEAD_DIM
    halo_per_step = conv_step // CONV_HALO
    nhalo = n // CONV_HALO
    part = lambda b, h: b * N_REC_HEADS + h
    big_block = conv_step * CONV_CH * jnp.dtype(F32).itemsize >= SINGLE_BUFFER_BYTES
    mid_mode = dict(pipeline_mode=pl.Buffered(1)) if big_block else {}
    mid = lambda cc: pl.BlockSpec((conv_step, CONV_CH), lambda b, h: (part(b, h), cc), **mid_mode)
    top = lambda cc: pl.BlockSpec(
        (CONV_HALO, CONV_CH), lambda b, h: (jnp.maximum(part(b, h) * halo_per_step - 1, 0), cc))
    bot = lambda cc: pl.BlockSpec(
        (CONV_HALO, CONV_CH), lambda b, h: (jnp.minimum((part(b, h) + 1) * halo_per_step, nhalo - 1), cc))
    vec = pl.BlockSpec((1, CONV_CH), lambda b, h: (0, 0))
    ca, cb = COL_CA // cw, COL_CB // cw
    w, bias, ln_g, ln_b = conv_params
    in_specs += [mid(ca), mid(cb), top(ca), top(cb), bot(ca), bot(cb),
                 pl.BlockSpec((CONV_WIDTH, CONV_CH), lambda b, h: (0, 0)), vec, vec, vec]
    args += [z, z, z, z, z, z, w, bias.reshape(1, CONV_CH), ln_g.reshape(1, CONV_CH), ln_b.reshape(1, CONV_CH)]

    out_specs = [pl.BlockSpec((seq, REC_DV), lambda b, h: (b, h)),
                 pl.BlockSpec((conv_step, CONV_CH), lambda b, h: (part(b, h), 0))]
    out_shape = [jax.ShapeDtypeStruct((n, D_REC), BF16), jax.ShapeDtypeStruct((n, CONV_CH), BF16)]
    aliases = {}
    if not has_init:
        if layer == 0:
            out_specs.append(pl.BlockSpec((None, DEPTH, 2, None, REC_DK, REC_DV), lambda b, h: (b, 0, 0, h, 0, 0)))
        else:
            out_specs.append(pl.BlockSpec((None, None, 2, None, REC_DK, REC_DV),
                                          lambda b, h: (b, layer, 0, h, 0, 0)))
            aliases = {len(args): 2}
            in_specs.append(pl.BlockSpec(memory_space=pl.ANY))
            args.append(states)
        out_shape.append(jax.ShapeDtypeStruct((batch, DEPTH, 2, N_REC_HEADS, REC_DK, REC_DV), F32))
    return pl.pallas_call(
        functools.partial(_rec_kernel, layer, has_init),
        grid=(batch, N_REC_HEADS),
        in_specs=in_specs,
        out_specs=out_specs,
        out_shape=out_shape,
        input_output_aliases=aliases,
        scratch_shapes=[pltpu.VMEM((seq, REC_DV), F32),
                        pltpu.VMEM((2, seq, REC_DK), BF16),
                        pltpu.VMEM((2, seq, REC_DK), BF16),
                        pltpu.VMEM((2, seq // REC_CHUNK, SUBLANES, REC_DK), F32),
                        pltpu.VMEM((2, seq, REC_DV), F32),
                        pltpu.VMEM((2 * REC_LOCAL_UNROLL, REC_CHUNK, REC_DK), F32),
                        pltpu.VMEM((conv_step + 2 * CONV_HALO, CONV_CH), F32),
                        pltpu.VMEM((SUBLANES, conv_rows + 2 * CONV_HALO, CONV_CH), F32),
                        pltpu.VMEM((CONV_WIDTH, SUBLANES, CONV_CH), F32)],
        compiler_params=_params("parallel", "parallel"),
        name="hgrn2_conv",
    )(*args)


def _conv_stage(part, nparts, a_ref, b_ref, at_ref, bt_ref, ab_ref, bb_ref, w_ref, glu_scr, w_scr):
    has_top = jnp.where(part > 0, 1.0, 0.0)
    has_bot = jnp.where(part < nparts - 1, 1.0, 0.0)
    rows = a_ref.shape[0]

    def glu(a, b):
        return a * _sigmoid(b)

    glu_scr[0:CONV_HALO, :] = glu(at_ref[...], bt_ref[...]) * has_top
    glu_scr[CONV_HALO:CONV_HALO + rows, :] = glu(a_ref[...], b_ref[...])
    glu_scr[CONV_HALO + rows:, :] = glu(ab_ref[...], bb_ref[...]) * has_bot
    for j in range(CONV_WIDTH):
        w_scr[j] = jnp.broadcast_to(w_ref[j:j + 1, :], (SUBLANES, CONV_CH))


def _conv_rows(rows, h_scr, w_scr, bias_ref, lg_ref, lb_ref, store):
    half = CONV_WIDTH // 2
    shifted_rows = rows + 2 * CONV_HALO - SUBLANES
    for r in range(1, SUBLANES):
        h_scr[r, 0:shifted_rows, :] = h_scr[0, r:r + shifted_rows, :]
    for r0 in range(0, rows, CONV_SUB):
        acc = None
        for j in range(CONV_WIDTH):
            lo = r0 + CONV_HALO - half + j
            shift = lo % SUBLANES
            weight = w_scr[j]
            window = h_scr[shift, lo - shift:lo - shift + CONV_SUB, :]
            term = (window.reshape(CONV_SUB // SUBLANES, SUBLANES, CONV_CH) * weight).reshape(CONV_SUB, CONV_CH)
            acc = term if acc is None else acc + term
        y = acc + bias_ref[...]
        mu = jnp.mean(y, axis=-1, keepdims=True)
        yc = y - mu
        var = jnp.mean(yc * yc, axis=-1, keepdims=True)
        yn = yc * lax.rsqrt(var + EPS) * lg_ref[...] + lb_ref[...]
        store(r0, (yn * _sigmoid(yn)).astype(BF16))


def _outproj_kernel(x_ref, g_ref, a_ref, r_ref, c_ref, wa_ref, wr_ref, wc_ref, o_ref):
    mix = _dot(a_ref[...], wa_ref[...]) + _dot(r_ref[...], wr_ref[...]) + _dot(c_ref[...], wc_ref[...])
    o_ref[...] = x_ref[...] + g_ref[...] * mix


def _outproj(x, mod, attn, rec, conv, w_out, layer, rows_per_cond):
    n = x.shape[0]
    tn = D_MODEL // 2
    row = lambda width: pl.BlockSpec((TM, width), lambda i, j: (i, 0))
    return pl.pallas_call(
        _outproj_kernel,
        grid=(n // TM, D_MODEL // tn),
        in_specs=[
            pl.BlockSpec((TM, tn), lambda i, j: (i, j)),
            pl.BlockSpec((None, None, 1, tn), lambda i, j: ((i * TM) // rows_per_cond, 5, 0, j)),
            row(D_ATTN), row(D_REC), row(CONV_CH),
            pl.BlockSpec((None, D_ATTN, tn), lambda i, j: (layer, 0, j)),
            pl.BlockSpec((None, D_REC, tn), lambda i, j: (layer, D_ATTN // D_REC, j)),
            pl.BlockSpec((None, CONV_CH, tn), lambda i, j: (layer, (D_ATTN + D_REC) // CONV_CH, j)),
        ],
        out_specs=pl.BlockSpec((TM, tn), lambda i, j: (i, j)),
        out_shape=jax.ShapeDtypeStruct((n, D_MODEL), F32),
        compiler_params=_params("parallel", "parallel"),
        name="outproj",
    )(x, mod, attn, rec, conv, w_out, w_out, w_out)


def _trunk_layer(x, mod, l, P, mixers, rows_per_cond):
    x = _ffn(x, mod, 0, P['norm_g'][l, 0], P['w_ffn_in'], P['w_ffn_out'], l, 0, rows_per_cond)
    z = _inproj(x, mod, P['norm_g'][l, 1], P['w_in'], l, rows_per_cond)
    attn, rec, conv, extras = mixers(z)
    x = _outproj(x, mod, attn, rec, conv, P['w_out'], l, rows_per_cond)
    x = _ffn(x, mod, 6, P['norm_g'][l, 2], P['w_ffn_in'], P['w_ffn_out'], l, 1, rows_per_cond)
    return x, extras


def kernel(x_prompt, x_sample, cache_k, cache_v, state_rec, c, c_ctx, w_ada, b_ada, norm_g, w_ffn_in, w_ffn_out,
           w_in, w_out, q_norm_g, k_norm_g, attn_sink, rec_lb_logits, rec_norm_g, conv_w, conv_b, conv_ln_g, conv_ln_b):
    batch, seq, _ = x_prompt.shape
    dec_batch, dec_seq, _ = x_sample.shape
    assert seq % TM == 0 or TM % seq == 0
    assert dec_seq % TM == 0 and dec_seq % LAT_Q_TILE == 0

    P = {'norm_g': norm_g, 'w_ffn_in': w_ffn_in.astype(BF16), 'w_ffn_out': w_ffn_out.astype(BF16),
         'w_in': w_in.astype(BF16), 'w_out': w_out.astype(BF16)}

    lb_logits = rec_lb_logits.reshape(DEPTH, 2, N_REC_HEADS, 1, REC_DK)

    cond8 = jnp.zeros((SUBLANES, D_MODEL), F32).at[0].set(c_ctx).at[1:1 + dec_batch].set(c)
    mod = _modulation(cond8, w_ada, b_ada).reshape(DEPTH, SUBLANES, N_MOD, 1, D_MODEL)
    tables = _rope_tables(dec_seq)
    sinks = attn_sink.reshape(DEPTH, N_KV_HEADS, Q_PER_KV)
    rec_g = rec_norm_g.reshape(DEPTH, N_REC_HEADS, 1, REC_DV)

    conv_params = lambda l: (conv_w[l], conv_b[l], conv_ln_g[l], conv_ln_b[l])

    h = x_prompt.reshape(batch * seq, D_MODEL)
    caches, states = None, None
    for l in range(DEPTH):
        def ctx_mixers(z, l=l, caches=caches, states=states):
            attn, new_k, new_v = _ctx_attention(z, sinks[l], q_norm_g[l], k_norm_g[l], batch, seq, l, caches)
            rec, conv, new_s = _recurrence(z, l, lb_logits, rec_g[l], None, conv_params(l), batch, seq, states)
            return attn, rec, conv, ((new_k, new_v), new_s)

        h, (caches, states) = _trunk_layer(h, mod[l, 0:1], l, P, ctx_mixers, batch * seq)
    y_prompt = h.reshape(batch, seq, D_MODEL)

    h = x_sample.reshape(dec_batch * dec_seq, D_MODEL)
    for l in range(DEPTH):
        def lat_mixers(z, l=l):
            attn = _lat_attention(z, sinks[l], q_norm_g[l], k_norm_g[l], cache_k[:, l], cache_v[:, l], tables,
                                  dec_batch, dec_seq)
            rec, conv = _recurrence(z, l, lb_logits, rec_g[l], state_rec[:, l], conv_params(l), dec_batch, dec_seq)
            return attn, rec, conv, None

        h, _ = _trunk_layer(h, mod[l, 1:1 + dec_batch], l, P, lat_mixers, dec_seq)
    y_sample = h.reshape(dec_batch, dec_seq, D_MODEL)

    return (y_prompt, y_sample, caches[0], caches[1], states)
```

```python
import functools

import jax
import jax.numpy as jnp
import numpy as np
from jax import lax
from jax.experimental import pallas as pl
from jax.experimental.pallas import tpu as pltpu

F32 = jnp.float32
BF16 = jnp.bfloat16

D_MODEL = 2048
DEPTH = 2
GRID_W = 64
HEAD_DIM = 128
D_ATTN = D_MODEL // 2
N_Q_HEADS = D_ATTN // HEAD_DIM
N_KV_HEADS = 2
Q_PER_KV = N_Q_HEADS // N_KV_HEADS
WINDOW = 128
ATTN_BLOCK = 128
ATTN_SCALE = HEAD_DIM ** -0.5
ROPE_BASE = 10000.0
MASK_VALUE = -1e30
D_REC = D_MODEL // 4
REC_DK = 128
REC_DV = 128
N_REC_HEADS = D_REC // REC_DV
REC_CHUNK = 32
CONV_CH = D_MODEL // 4
CONV_WIDTH = 31
D_FF = 5632
N_MOD = 9
EPS = 1e-6
GATE_FLOOR = 1e-30
LOG2_E = 1.4426950408889634
IN_COLS = 5120

COL_Q = 0
COL_K = 8
COL_V = 10
COL_RQ = 12
COL_RF_F = 16
COL_RF_B = 20
COL_RI = 24
COL_RG = 28
COL_CA = 32
COL_CB = 36

SUBLANES = 8
V7X_VMEM_BYTES = 64 * 1024 * 1024
VMEM_LIMIT = V7X_VMEM_BYTES * 7 // 8
SINGLE_BUFFER_BYTES = 1024 * 1024

TM = 1024
TM_FFN = 1024
TF = 512
TN_FFN_OUT = 256
TN_IN = 1024
TN_ADA = 1024
CONV_HALO = 16
CONV_SUB = 32
LAT_Q_TILE = 1024
CTX_SEQS_PER_STEP = 4
REC_LOCAL_UNROLL = 8
REC_STATE_UNROLL = 16


def _params(*sem):
    return pltpu.CompilerParams(dimension_semantics=sem, vmem_limit_bytes=VMEM_LIMIT)


def _dot(a, b):
    return jnp.dot(a, b, preferred_element_type=F32)


def _dot_nt(a, b):
    return lax.dot_general(a, b, (((1,), (1,)), ((), ())), preferred_element_type=F32)


def _dot_tn(a, b):
    return lax.dot_general(a, b, (((0,), (0,)), ((), ())), preferred_element_type=F32)


def _rms(x, g):
    return x * lax.rsqrt(jnp.mean(x * x, axis=-1, keepdims=True) + EPS) * g


def _ada_norm(x, g, scale, shift):
    gain = g * (1.0 + scale)
    return x * lax.rsqrt(jnp.mean(x * x, axis=-1, keepdims=True) + EPS) * gain + shift


def _sigmoid(x):
    return 1.0 / (1.0 + jnp.exp(-x))


def _ada_kernel(c_ref, w_ref, b_ref, o_ref):
    c = c_ref[...]
    s = (c * _sigmoid(c)).astype(BF16)
    o_ref[...] = _dot(s, w_ref[...].astype(BF16)) + b_ref[...]


def _modulation(cond8, w_ada, b_ada):
    ncol = N_MOD * D_MODEL
    return pl.pallas_call(
        _ada_kernel,
        grid=(DEPTH, ncol // TN_ADA),
        in_specs=[
            pl.BlockSpec((SUBLANES, D_MODEL), lambda l, j: (0, 0)),
            pl.BlockSpec((None, D_MODEL, TN_ADA), lambda l, j: (l, 0, j)),
            pl.BlockSpec((None, 1, TN_ADA), lambda l, j: (l, 0, j)),
        ],
        out_specs=pl.BlockSpec((None, SUBLANES, TN_ADA), lambda l, j: (l, 0, j)),
        out_shape=jax.ShapeDtypeStruct((DEPTH, SUBLANES, ncol), F32),
        compiler_params=_params("parallel", "parallel"),
        name="modulation",
    )(cond8, w_ada, b_ada.reshape(DEPTH, 1, ncol))


def _mod_spec(chunk, rows_per_cond, tm=TM):
    return pl.BlockSpec((None, None, 1, D_MODEL), lambda i, j: ((i * tm) // rows_per_cond, chunk, 0, 0))


def _ffn_kernel(nf, x_ref, sh_ref, sc_ref, g_ref, ng_ref, wa_ref, wb_ref, wo_ref, o_ref, h_scr, act_scr):
    j = pl.program_id(1)

    @pl.when(j == 0)
    def _():
        h = _ada_norm(x_ref[...], ng_ref[...], sc_ref[...], sh_ref[...])
        h_scr[...] = h.astype(BF16)

    @pl.when(j < nf)
    def _():
        h = h_scr[...]
        a = _dot(h, wa_ref[...])
        b = _dot(h, wb_ref[...])
        act_scr[:, pl.ds(pl.multiple_of(j * TF, TF), TF)] = (a * _sigmoid(a) * b).astype(BF16)

    @pl.when(j >= nf)
    def _():
        cols = pl.ds(pl.multiple_of((j - nf) * TN_FFN_OUT, TN_FFN_OUT), TN_FFN_OUT)
        o_ref[...] = x_ref[:, cols] + 0.5 * g_ref[...] * _dot(act_scr[...], wo_ref[...])


def _ffn(x, mod, first_chunk, norm_g, w_in, w_out, layer, which, rows_per_cond):
    n = x.shape[0]
    nf = D_FF // TF
    nout = D_MODEL // TN_FFN_OUT
    hid = lambda j: jnp.minimum(j, nf - 1)
    out = lambda j: jnp.maximum(j - nf, 0)
    return pl.pallas_call(
        functools.partial(_ffn_kernel, nf),
        grid=(n // TM_FFN, nf + nout),
        in_specs=[
            pl.BlockSpec((TM_FFN, D_MODEL), lambda i, j: (i, 0)),
            _mod_spec(first_chunk, rows_per_cond, TM_FFN),
            _mod_spec(first_chunk + 1, rows_per_cond, TM_FFN),
            pl.BlockSpec((None, None, 1, TN_FFN_OUT),
                         lambda i, j: ((i * TM_FFN) // rows_per_cond, first_chunk + 2, 0, out(j))),
            pl.BlockSpec((1, D_MODEL), lambda i, j: (0, 0)),
            pl.BlockSpec((None, None, D_MODEL, TF), lambda i, j: (layer, which, 0, hid(j))),
            pl.BlockSpec((None, None, D_MODEL, TF), lambda i, j: (layer, which, 0, hid(j) + nf)),
            pl.BlockSpec((None, None, D_FF, TN_FFN_OUT), lambda i, j: (layer, which, 0, out(j))),
        ],
        out_specs=pl.BlockSpec((TM_FFN, TN_FFN_OUT), lambda i, j: (i, out(j))),
        out_shape=jax.ShapeDtypeStruct((n, D_MODEL), F32),
        scratch_shapes=[pltpu.VMEM((TM_FFN, D_MODEL), BF16), pltpu.VMEM((TM_FFN, D_FF), BF16)],
        compiler_params=_params("parallel", "arbitrary"),
        name="ffn",
    )(x, mod, mod, mod, norm_g.reshape(1, D_MODEL), w_in, w_in, w_out)


def _inproj_kernel(x_ref, sh_ref, sc_ref, ng_ref, w_ref, o_ref, h_scr):
    @pl.when(pl.program_id(1) == 0)
    def _():
        h = _ada_norm(x_ref[...], ng_ref[...], sc_ref[...], sh_ref[...])
        h_scr[...] = h.astype(BF16)

    o_ref[...] = _dot(h_scr[...], w_ref[...])


def _inproj(x, mod, norm_g, w_in, layer, rows_per_cond):
    n = x.shape[0]
    return pl.pallas_call(
        _inproj_kernel,
        grid=(n // TM, IN_COLS // TN_IN),
        in_specs=[
            pl.BlockSpec((TM, D_MODEL), lambda i, j: (i, 0)),
            _mod_spec(3, rows_per_cond),
            _mod_spec(4, rows_per_cond),
            pl.BlockSpec((1, D_MODEL), lambda i, j: (0, 0)),
            pl.BlockSpec((None, D_MODEL, TN_IN), lambda i, j: (layer, 0, j)),
        ],
        out_specs=pl.BlockSpec((TM, TN_IN), lambda i, j: (i, j)),
        out_shape=jax.ShapeDtypeStruct((n, IN_COLS), F32),
        scratch_shapes=[pltpu.VMEM((TM, D_MODEL), BF16)],
        compiler_params=_params("parallel", "arbitrary"),
        name="inproj",
    )(x, mod, mod, norm_g.reshape(1, D_MODEL), w_in)


QK_SCALE_LOG2 = ATTN_SCALE * LOG2_E


def _with_ones(v):
    return jnp.concatenate([v, jnp.ones_like(v)], axis=-1)


def _softmax_sink(scores, values, sink):
    sink2 = sink * LOG2_E
    m = sink2
    for s in scores:
        m = jnp.maximum(jnp.max(s, axis=-1, keepdims=True), m)
    acc = None
    for s, v in zip(scores, values):
        pv = _dot(jnp.exp2(s - m).astype(BF16), v)
        acc = pv if acc is None else acc + pv
    return acc[:, :HEAD_DIM] / (acc[:, HEAD_DIM:] + jnp.exp2(sink2 - m))


def _put_layer(ref, first_layer, lead, tail, value):
    if first_layer:
        ref[lead + (0,) + tail] = value
        for other in range(1, DEPTH):
            ref[lead + (other,) + tail] = jnp.zeros_like(value)
    else:
        ref[lead + tail] = value


def _ctx_attn_kernel(first_layer, sink_ref, q_ref, k_ref, v_ref, qg_ref, kg_ref, *rest):
    o_ref, kc_ref, vc_ref = rest[-3:]
    kv = pl.program_id(1)
    nseq, seq = kc_ref.shape[0], kc_ref.shape[-2]
    heads = [slice(g * HEAD_DIM, (g + 1) * HEAD_DIM) for g in range(Q_PER_KV)]
    keys, values = [], []
    for i in range(nseq):
        rows = slice(i * seq, (i + 1) * seq)
        kn = _rms(k_ref[rows, :], kg_ref[...])
        v = v_ref[rows, :]
        _put_layer(kc_ref, first_layer, (i,), (), kn)
        _put_layer(vc_ref, first_layer, (i,), (), v)
        keys.append(kn.astype(BF16))
        values.append(_with_ones(v.astype(BF16)))
    logits = [[_dot_nt((_rms(q_ref[i * seq:(i + 1) * seq, cols], qg_ref[...]) * QK_SCALE_LOG2).astype(BF16), keys[i])
               for cols in heads] for i in range(nseq)]
    for i in range(nseq):
        for g, cols in enumerate(heads):
            o = _softmax_sink([logits[i][g]], [values[i]], sink_ref[kv, g])
            o_ref[i * seq:(i + 1) * seq, cols] = o.astype(BF16)


def _ctx_attention(z, sink, q_g, k_g, batch, seq, layer, caches):
    qw = Q_PER_KV * HEAD_DIM
    per = CTX_SEQS_PER_STEP
    first = caches is None
    cache_shape = jax.ShapeDtypeStruct((batch, DEPTH, N_KV_HEADS, seq, HEAD_DIM), F32)
    if first:
        cache_spec = pl.BlockSpec((per, DEPTH, None, seq, HEAD_DIM), lambda b, kv: (b, 0, kv, 0, 0))
    else:
        cache_spec = pl.BlockSpec((per, None, None, seq, HEAD_DIM), lambda b, kv: (b, layer, kv, 0, 0))
    in_specs = [
        pl.BlockSpec(memory_space=pltpu.SMEM),
        pl.BlockSpec((per * seq, qw), lambda b, kv: (b, kv)),
        pl.BlockSpec((per * seq, HEAD_DIM), lambda b, kv: (b, COL_K + kv)),
        pl.BlockSpec((per * seq, HEAD_DIM), lambda b, kv: (b, COL_V + kv)),
        pl.BlockSpec((1, HEAD_DIM), lambda b, kv: (0, 0)),
        pl.BlockSpec((1, HEAD_DIM), lambda b, kv: (0, 0)),
    ]
    args = [sink, z, z, z, q_g.reshape(1, HEAD_DIM), k_g.reshape(1, HEAD_DIM)]
    aliases = {}
    if not first:
        aliases = {len(args): 1, len(args) + 1: 2}
        in_specs += [pl.BlockSpec(memory_space=pl.ANY)] * 2
        args += list(caches)
    return pl.pallas_call(
        functools.partial(_ctx_attn_kernel, first),
        grid=(batch // per, N_KV_HEADS),
        in_specs=in_specs,
        out_specs=[pl.BlockSpec((per * seq, qw), lambda b, kv: (b, kv)), cache_spec, cache_spec],
        out_shape=[jax.ShapeDtypeStruct((batch * seq, D_ATTN), BF16), cache_shape, cache_shape],
        input_output_aliases=aliases,
        compiler_params=_params("parallel", "parallel"),
        name="ctx_attention",
    )(*args)


def _rope(x, cos, sin_a, sin_b):
    quarter = HEAD_DIM // 4
    up = pltpu.roll(x, HEAD_DIM - quarter, 1)
    down = pltpu.roll(x, quarter, 1)
    return x * cos + up * sin_a + down * sin_b


def _lat_attn_kernel(sink_ref, q_ref, k_ref, v_ref, kc_ref, vc_ref, qg_ref, kg_ref,
                     cos_ref, sa_ref, sb_ref, cosq_ref, saq_ref, sbq_ref, o_ref, k_scr, v_scr, kc_scr, vc_scr):
    kv = pl.program_id(1)
    qt = pl.program_id(2)
    seq = k_ref.shape[0]
    span = 3 * ATTN_BLOCK

    @pl.when(qt == 0)
    def _():
        kn = _rms(k_ref[...], kg_ref[...])
        k_scr[...] = _rope(kn, cos_ref[...], sa_ref[...], sb_ref[...]).astype(BF16)
        v_scr[...] = _with_ones(v_ref[...].astype(BF16))
        kc_scr[...] = kc_ref[...].astype(BF16)
        vc_scr[...] = _with_ones(vc_ref[...].astype(BF16))

    kcb = kc_scr[...]
    vcb = vc_scr[...]
    nblk = LAT_Q_TILE // ATTN_BLOCK
    heads = [slice(g * HEAD_DIM, (g + 1) * HEAD_DIM) for g in range(Q_PER_KV)]
    stacked = Q_PER_KV * ATTN_BLOCK

    head_of_row = lax.broadcasted_iota(jnp.int32, (stacked, 1), 0) // ATTN_BLOCK
    sink_col = jnp.zeros((stacked, 1), F32)
    for g in range(Q_PER_KV):
        sink_col = jnp.where(head_of_row == g, sink_ref[kv, g], sink_col)

    def block_rows(blk):
        return slice(blk * ATTN_BLOCK, (blk + 1) * ATTN_BLOCK)

    def logits_of(blk):
        rows = block_rows(blk)
        q0 = (qt * nblk + blk) * ATTN_BLOCK
        start = pl.multiple_of(jnp.clip(q0 - ATTN_BLOCK, 0, seq - span), ATTN_BLOCK)
        qpos = q0 + lax.broadcasted_iota(jnp.int32, (stacked, span), 0) % ATTN_BLOCK
        kpos = start + lax.broadcasted_iota(jnp.int32, (stacked, span), 1)
        valid = jnp.abs(qpos - kpos) <= WINDOW
        cos, sa, sb = cosq_ref[rows, :], saq_ref[rows, :], sbq_ref[rows, :]
        qn = jnp.concatenate([_rope(_rms(q_ref[rows, cols], qg_ref[...]), cos, sa, sb) for cols in heads], axis=0)
        qn = (qn * QK_SCALE_LOG2).astype(BF16)
        s_loc = jnp.where(valid, _dot_nt(qn, k_scr[pl.ds(start, span), :]), MASK_VALUE)
        return s_loc, _dot_nt(qn, kcb), start

    pending = logits_of(0)
    for blk in range(nblk):
        upcoming = logits_of(blk + 1) if blk + 1 < nblk else None
        s_loc, s_ctx, start = pending
        o = _softmax_sink([s_loc, s_ctx], [v_scr[pl.ds(start, span), :], vcb], sink_col)
        for g, cols in enumerate(heads):
            o_ref[block_rows(blk), cols] = o[g * ATTN_BLOCK:(g + 1) * ATTN_BLOCK, :].astype(BF16)
        pending = upcoming


def _rope_tables(seq):
    quarter = HEAD_DIM // 4
    pos = np.arange(seq)
    inv_freq = (ROPE_BASE ** (-np.arange(quarter, dtype=np.float32) / quarter)).astype(np.float32)
    zero = np.zeros((seq, quarter), np.float32)

    def trig(p):
        ang = p.astype(np.float32)[:, None] * inv_freq
        return np.cos(ang).astype(np.float32), np.sin(ang).astype(np.float32)

    cr, sr = trig(pos // GRID_W)
    cc, sc = trig(pos % GRID_W)
    cos = np.concatenate([cr, cr, cc, cc], axis=-1)
    sin_a = np.concatenate([-sr, zero, -sc, zero], axis=-1)
    sin_b = np.concatenate([zero, sr, zero, sc], axis=-1)
    return jnp.asarray(cos), jnp.asarray(sin_a), jnp.asarray(sin_b)


def _lat_attention(z, sink, q_g, k_g, k_ctx, v_ctx, tables, batch, seq):
    qw = Q_PER_KV * HEAD_DIM
    nqt = seq // LAT_Q_TILE
    past = k_ctx.shape[2]
    cos, sin_a, sin_b = tables
    full_tab = pl.BlockSpec((seq, HEAD_DIM), lambda b, kv, qt: (0, 0))
    tile_tab = pl.BlockSpec((LAT_Q_TILE, HEAD_DIM), lambda b, kv, qt: (qt, 0))
    ctx_spec = pl.BlockSpec((None, None, past, HEAD_DIM), lambda b, kv, qt: (b, kv, 0, 0))
    gain = pl.BlockSpec((1, HEAD_DIM), lambda b, kv, qt: (0, 0))
    return pl.pallas_call(
        _lat_attn_kernel,
        grid=(batch, N_KV_HEADS, nqt),
        in_specs=[
            pl.BlockSpec(memory_space=pltpu.SMEM),
            pl.BlockSpec((LAT_Q_TILE, qw), lambda b, kv, qt: (b * nqt + qt, kv)),
            pl.BlockSpec((seq, HEAD_DIM), lambda b, kv, qt: (b, COL_K + kv)),
            pl.BlockSpec((seq, HEAD_DIM), lambda b, kv, qt: (b, COL_V + kv)),
            ctx_spec, ctx_spec, gain, gain,
            full_tab, full_tab, full_tab, tile_tab, tile_tab, tile_tab,
        ],
        out_specs=pl.BlockSpec((LAT_Q_TILE, qw), lambda b, kv, qt: (b * nqt + qt, kv)),
        out_shape=jax.ShapeDtypeStruct((batch * seq, D_ATTN), BF16),
        scratch_shapes=[pltpu.VMEM((seq, HEAD_DIM), BF16), pltpu.VMEM((seq, 2 * HEAD_DIM), BF16),
                        pltpu.VMEM((past, HEAD_DIM), BF16), pltpu.VMEM((past, 2 * HEAD_DIM), BF16)],
        compiler_params=_params("parallel", "parallel", "arbitrary"),
        name="lat_attention",
    )(sink, z, z, z, k_ctx, v_ctx, q_g.reshape(1, HEAD_DIM), k_g.reshape(1, HEAD_DIM),
      cos, sin_a, sin_b, cos, sin_a, sin_b)


def _split3(x):
    hi = x.astype(BF16)
    r1 = x - hi.astype(F32)
    mid = r1.astype(BF16)
    lo = (r1 - mid.astype(F32)).astype(BF16)
    return hi, mid, lo


def _rec_gates(forward, zf, lb):
    c = REC_CHUNK
    t = jnp.exp(-jnp.abs(zf))
    r = 1.0 / (1.0 + t)
    tr = t * r
    nonneg = zf >= 0
    sig_pos = jnp.where(nonneg, r, tr)
    sig_neg = jnp.where(nonneg, tr, r)
    one_m_lb = 1.0 - lb
    log_f = jnp.log(jnp.maximum(lb + one_m_lb * sig_pos, GATE_FLOOR))
    k = one_m_lb * sig_neg
    ri = lax.broadcasted_iota(jnp.int32, (c, c), 0)
    ci = lax.broadcasted_iota(jnp.int32, (c, c), 1)
    tri = jnp.where((ci <= ri) if forward else (ci >= ri), 1.0, 0.0).astype(BF16)
    hi, mid, lo = _split3(log_f)
    cum2 = (_dot(tri, hi) + _dot(tri, mid) + _dot(tri, lo)) * LOG2_E
    return cum2 - jnp.log2(k), cum2


def _rec_same_block(forward, q, key_row, cum2):
    sub_row = lax.broadcasted_iota(jnp.int32, (SUBLANES, REC_DK), 0)
    pieces = []
    for b in range(REC_CHUNK // SUBLANES):
        blk = slice(b * SUBLANES, (b + 1) * SUBLANES)
        cum_b, q_b = cum2[blk, :], q[blk, :]
        for sl in range(SUBLANES):
            s = b * SUBLANES + sl
            keep = (sub_row >= sl) if forward else (sub_row <= sl)
            pieces.append(jnp.where(keep, jnp.exp2(cum_b - key_row(s)) * q_b, 0.0))
    return _dot(jnp.concatenate(pieces, axis=0).astype(BF16), jnp.ones((REC_DK, REC_DV), BF16))


def _rec_cross_block(forward, q, key2, v, cum2):
    c = REC_CHUNK
    q_parts, k_parts, v_parts, segments = [], [], [], []
    for tb in range(c // SUBLANES):
        blk = slice(tb * SUBLANES, (tb + 1) * SUBLANES)
        src, ref_row = (slice(0, blk.start), blk.start - 1) if forward else (slice(blk.stop, c), blk.stop)
        if src.stop == src.start:
            q_parts.append(jnp.zeros((SUBLANES, REC_DK), F32))
            continue
        ref = cum2[ref_row:ref_row + 1, :]
        q_parts.append(q[blk, :] * jnp.exp2(cum2[blk, :] - ref))
        k_parts.append(jnp.exp2(ref - key2[src, :]))
        v_parts.append(v[src, :])
        segments.append((tb, src.stop - src.start))
    a = _dot_nt(jnp.concatenate(q_parts, axis=0).astype(BF16), jnp.concatenate(k_parts, axis=0).astype(BF16))
    return a, segments, jnp.concatenate(v_parts, axis=0).astype(BF16)


def _rec_cross_apply(a, segments, v_all):
    c, ncol = a.shape
    row_blk = lax.broadcasted_iota(jnp.int32, (c, ncol), 0) // SUBLANES
    col = lax.broadcasted_iota(jnp.int32, (c, ncol), 1)
    col_blk = jnp.full((c, ncol), -1, jnp.int32)
    start = 0
    for tb, width in segments:
        col_blk = jnp.where((col >= start) & (col < start + width), tb, col_blk)
        start += width
    return _dot(jnp.where(row_blk == col_blk, a, 0.0).astype(BF16), v_all)


def _rec_same_apply(lane_sums, v_row):
    o_blocks = []
    for b in range(REC_CHUNK // SUBLANES):
        acc = None
        for sl in range(SUBLANES):
            s = b * SUBLANES + sl
            term = lane_sums[s * SUBLANES:(s + 1) * SUBLANES, :] * v_row(s)
            acc = term if acc is None else acc + term
        o_blocks.append(acc)
    return jnp.concatenate(o_blocks, axis=0)


def _rec_kernel(layer, has_init, *refs):
    q_ref, ff_ref, fb_ref, v_ref, g_ref, lg_ref, ng_ref = refs[:7]
    refs = refs[7:]
    if has_init:
        s0_ref, refs = refs[0], refs[1:]
    conv_in, refs = refs[:10], refs[10:]
    if not has_init and layer > 0:
        refs = refs[1:]
    o_ref, conv_o_ref = refs[:2]
    refs = refs[2:]
    if not has_init:
        s_out_ref, refs = refs[0], refs[1:]
    o_scr, qd_scr, kd_scr, gl_scr, oi_scr, key_scr, glu_scr, conv_h_scr, conv_w_scr = refs
    seq = q_ref.shape[0]
    nchunk = seq // REC_CHUNK
    c = REC_CHUNK

    logits = lg_ref[...]
    e = jnp.exp(logits - jnp.max(logits, axis=0, keepdims=True))
    p = e / jnp.sum(e, axis=0, keepdims=True)
    lb = jnp.zeros_like(p[0])
    for i in range(1, layer + 1):
        lb = lb + p[i]

    def rows_of(ci):
        return pl.ds(pl.multiple_of(ci * c, c), c)

    z_refs = (ff_ref, fb_ref)

    def local_body(it, carry):
        chunks = [it * REC_LOCAL_UNROLL + u for u in range(REC_LOCAL_UNROLL)]
        qs = [q_ref[rows_of(ci), :] for ci in chunks]
        vs = [v_ref[rows_of(ci), :] for ci in chunks]
        items = [(u, d) for u in range(REC_LOCAL_UNROLL) for d in range(2)]
        gates = [_rec_gates(d == 0, z_refs[d][rows_of(chunks[u]), :], lb[d]) for u, d in items]
        for idx, (key2, _) in enumerate(gates):
            key_scr[idx] = key2
        key_rows = [lambda s, idx=idx: key_scr[idx, s:s + 1, :] for idx in range(len(items))]
        v_rows = [lambda s, ci=ci: v_ref[pl.ds(ci * c + s, 1), :] for ci in chunks]
        same = [_rec_same_block(d == 0, qs[u], key_rows[idx], cum2)
                for idx, ((u, d), (_, cum2)) in enumerate(zip(items, gates))]
        cross = [_rec_cross_block(d == 0, qs[u], key2, vs[u], cum2) for (u, d), (key2, cum2) in zip(items, gates)]
        totals = [None] * REC_LOCAL_UNROLL
        for (u, d), (key2, cum2), lane_sums, (a, segments, v_all) in zip(items, gates, same, cross):
            rows = rows_of(chunks[u])
            last = cum2[c - 1:c, :] if d == 0 else cum2[0:1, :]
            qd_scr[d, rows, :] = (qs[u] * jnp.exp2(cum2)).astype(BF16)
            kd_scr[d, rows, :] = jnp.exp2(last - key2).astype(BF16)
            gl_scr[d, chunks[u]] = jnp.broadcast_to(jnp.exp2(last), (SUBLANES, REC_DK))
            o_local = _rec_same_apply(lane_sums, v_rows[u]) + _rec_cross_apply(a, segments, v_all)
            totals[u] = o_local if totals[u] is None else totals[u] + o_local
        for u, ci in enumerate(chunks):
            o_scr[rows_of(ci), :] = totals[u]
        return carry

    lax.fori_loop(0, nchunk // REC_LOCAL_UNROLL, local_body, 0)

    if has_init:
        init = (s0_ref[0].T, s0_ref[1].T)
    else:
        init = (jnp.zeros((REC_DV, REC_DK), F32),) * 2

    state_unroll = min(REC_STATE_UNROLL, nchunk)
    conv_rows = conv_o_ref.shape[0] // (nchunk // state_unroll)
    _conv_stage(pl.program_id(1), N_REC_HEADS, *conv_in[:7], glu_scr, conv_w_scr)

    def state_body(it, states):
        offset = pl.multiple_of(it * conv_rows, conv_rows)
        conv_h_scr[0] = glu_scr[pl.ds(offset, conv_rows + 2 * CONV_HALO), :]

        def store(r0, val):
            conv_o_ref[pl.ds(pl.multiple_of(offset + r0, CONV_SUB), CONV_SUB), :] = val

        _conv_rows(conv_rows, conv_h_scr, conv_w_scr, *conv_in[7:], store)
        states = list(states)
        steps = [it * state_unroll + u for u in range(state_unroll)]
        order = [(d, step if d == 0 else nchunk - 1 - step) for step in steps for d in range(2)]
        updates = [_dot_tn(v_ref[rows_of(ci), :].astype(BF16), kd_scr[d, rows_of(ci), :]) for d, ci in order]
        for (d, ci), update in zip(order, updates):
            st = states[d]
            oi_scr[d, rows_of(ci), :] = _dot_nt(qd_scr[d, rows_of(ci), :], st.astype(BF16))
            decayed = (st.reshape(REC_DV // SUBLANES, SUBLANES, REC_DK) * gl_scr[d, ci]).reshape(REC_DV, REC_DK)
            states[d] = decayed + update
        return tuple(states)

    final = lax.fori_loop(0, nchunk // state_unroll, state_body, init)
    if not has_init:
        for d in range(2):
            _put_layer(s_out_ref, layer == 0, (), (d,), final[d].T)

    def out_body(ci, carry):
        rows = rows_of(ci)
        gate = g_ref[rows, :]
        o = o_scr[rows, :] + oi_scr[0, rows, :] + oi_scr[1, rows, :]
        o_ref[rows, :] = (_rms(o, ng_ref[...]) * (gate * _sigmoid(gate))).astype(BF16)
        return carry

    lax.fori_loop(0, nchunk, out_body, 0, unroll=8)


def _recurrence(z, layer, lb_logits, norm_g, s0, conv_params, batch, seq, states=None):
    has_init = s0 is not None
    assert has_init or (states is None) == (layer == 0)
    n = batch * seq
    col = lambda base: pl.BlockSpec((seq, REC_DK), lambda b, h: (b, base + h))
    state_spec = pl.BlockSpec((None, 2, None, REC_DK, REC_DV), lambda b, h: (b, 0, h, 0, 0))
    in_specs = [col(COL_RQ), col(COL_RF_F), col(COL_RF_B), col(COL_RI), col(COL_RG),
                pl.BlockSpec((DEPTH, 2, None, 1, REC_DK), lambda b, h: (0, 0, h, 0, 0)),
                pl.BlockSpec((None, 1, REC_DV), lambda b, h: (h, 0, 0))]
    args = [z, z, z, z, z, lb_logits, norm_g]
    if has_init:
        in_specs.append(state_spec)
        args.append(s0)

    conv_step = seq // N_REC_HEADS
    state_trips = (seq // REC_CHUNK) // min(REC_STATE_UNROLL, seq // REC_CHUNK)
    conv_rows = conv_step // state_trips
    assert conv_step % CONV_HALO == 0 and conv_rows % CONV_SUB == 0
    cw = CONV_CH // HEAD_DIM
    halo_per_step = conv_step // CONV_HALO
    nhalo = n // CONV_HALO
    part = lambda b, h: b * N_REC_HEADS + h
    big_block = conv_step * CONV_CH * jnp.dtype(F32).itemsize >= SINGLE_BUFFER_BYTES
    mid_mode = dict(pipeline_mode=pl.Buffered(1)) if big_block else {}
    mid = lambda cc: pl.BlockSpec((conv_step, CONV_CH), lambda b, h: (part(b, h), cc), **mid_mode)
    top = lambda cc: pl.BlockSpec(
        (CONV_HALO, CONV_CH), lambda b, h: (jnp.maximum(part(b, h) * halo_per_step - 1, 0), cc))
    bot = lambda cc: pl.BlockSpec(
        (CONV_HALO, CONV_CH), lambda b, h: (jnp.minimum((part(b, h) + 1) * halo_per_step, nhalo - 1), cc))
    vec = pl.BlockSpec((1, CONV_CH), lambda b, h: (0, 0))
    ca, cb = COL_CA // cw, COL_CB // cw
    w, bias, ln_g, ln_b = conv_params
    in_specs += [mid(ca), mid(cb), top(ca), top(cb), bot(ca), bot(cb),
                 pl.BlockSpec((CONV_WIDTH, CONV_CH), lambda b, h: (0, 0)), vec, vec, vec]
    args += [z, z, z, z, z, z, w, bias.reshape(1, CONV_CH), ln_g.reshape(1, CONV_CH), ln_b.reshape(1, CONV_CH)]

    out_specs = [pl.BlockSpec((seq, REC_DV), lambda b, h: (b, h)),
                 pl.BlockSpec((conv_step, CONV_CH), lambda b, h: (part(b, h), 0))]
    out_shape = [jax.ShapeDtypeStruct((n, D_REC), BF16), jax.ShapeDtypeStruct((n, CONV_CH), BF16)]
    aliases = {}
    if not has_init:
        if layer == 0:
            out_specs.append(pl.BlockSpec((None, DEPTH, 2, None, REC_DK, REC_DV), lambda b, h: (b, 0, 0, h, 0, 0)))
        else:
            out_specs.append(pl.BlockSpec((None, None, 2, None, REC_DK, REC_DV),
                                          lambda b, h: (b, layer, 0, h, 0, 0)))
            aliases = {len(args): 2}
            in_specs.append(pl.BlockSpec(memory_space=pl.ANY))
            args.append(states)
        out_shape.append(jax.ShapeDtypeStruct((batch, DEPTH, 2, N_REC_HEADS, REC_DK, REC_DV), F32))
    return pl.pallas_call(
        functools.partial(_rec_kernel, layer, has_init),
        grid=(batch, N_REC_HEADS),
        in_specs=in_specs,
        out_specs=out_specs,
        out_shape=out_shape,
        input_output_aliases=aliases,
        scratch_shapes=[pltpu.VMEM((seq, REC_DV), F32),
                        pltpu.VMEM((2, seq, REC_DK), BF16),
                        pltpu.VMEM((2, seq, REC_DK), BF16),
                        pltpu.VMEM((2, seq // REC_CHUNK, SUBLANES, REC_DK), F32),
                        pltpu.VMEM((2, seq, REC_DV), F32),
                        pltpu.VMEM((2 * REC_LOCAL_UNROLL, REC_CHUNK, REC_DK), F32),
                        pltpu.VMEM((conv_step + 2 * CONV_HALO, CONV_CH), F32),
                        pltpu.VMEM((SUBLANES, conv_rows + 2 * CONV_HALO, CONV_CH), F32),
                        pltpu.VMEM((CONV_WIDTH, SUBLANES, CONV_CH), F32)],
        compiler_params=_params("parallel", "parallel"),
        name="hgrn2_conv",
    )(*args)


def _conv_stage(part, nparts, a_ref, b_ref, at_ref, bt_ref, ab_ref, bb_ref, w_ref, glu_scr, w_scr):
    has_top = jnp.where(part > 0, 1.0, 0.0)
    has_bot = jnp.where(part < nparts - 1, 1.0, 0.0)
    rows = a_ref.shape[0]

    def glu(a, b):
        return a * _sigmoid(b)

    glu_scr[0:CONV_HALO, :] = glu(at_ref[...], bt_ref[...]) * has_top
    glu_scr[CONV_HALO:CONV_HALO + rows, :] = glu(a_ref[...], b_ref[...])
    glu_scr[CONV_HALO + rows:, :] = glu(ab_ref[...], bb_ref[...]) * has_bot
    for j in range(CONV_WIDTH):
        w_scr[j] = jnp.broadcast_to(w_ref[j:j + 1, :], (SUBLANES, CONV_CH))


def _conv_rows(rows, h_scr, w_scr, bias_ref, lg_ref, lb_ref, store):
    half = CONV_WIDTH // 2
    shifted_rows = rows + 2 * CONV_HALO - SUBLANES
    for r in range(1, SUBLANES):
        h_scr[r, 0:shifted_rows, :] = h_scr[0, r:r + shifted_rows, :]
    for r0 in range(0, rows, CONV_SUB):
        acc = None
        for j in range(CONV_WIDTH):
            lo = r0 + CONV_HALO - half + j
            shift = lo % SUBLANES
            weight = w_scr[j]
            window = h_scr[shift, lo - shift:lo - shift + CONV_SUB, :]
            term = (window.reshape(CONV_SUB // SUBLANES, SUBLANES, CONV_CH) * weight).reshape(CONV_SUB, CONV_CH)
            acc = term if acc is None else acc + term
        y = acc + bias_ref[...]
        mu = jnp.mean(y, axis=-1, keepdims=True)
        yc = y - mu
        var = jnp.mean(yc * yc, axis=-1, keepdims=True)
        yn = yc * lax.rsqrt(var + EPS) * lg_ref[...] + lb_ref[...]
        store(r0, (yn * _sigmoid(yn)).astype(BF16))


def _outproj_kernel(x_ref, g_ref, a_ref, r_ref, c_ref, wa_ref, wr_ref, wc_ref, o_ref):
    mix = _dot(a_ref[...], wa_ref[...]) + _dot(r_ref[...], wr_ref[...]) + _dot(c_ref[...], wc_ref[...])
    o_ref[...] = x_ref[...] + g_ref[...] * mix


def _outproj(x, mod, attn, rec, conv, w_out, layer, rows_per_cond):
    n = x.shape[0]
    tn = D_MODEL // 2
    row = lambda width: pl.BlockSpec((TM, width), lambda i, j: (i, 0))
    return pl.pallas_call(
        _outproj_kernel,
        grid=(n // TM, D_MODEL // tn),
        in_specs=[
            pl.BlockSpec((TM, tn), lambda i, j: (i, j)),
            pl.BlockSpec((None, None, 1, tn), lambda i, j: ((i * TM) // rows_per_cond, 5, 0, j)),
            row(D_ATTN), row(D_REC), row(CONV_CH),
            pl.BlockSpec((None, D_ATTN, tn), lambda i, j: (layer, 0, j)),
            pl.BlockSpec((None, D_REC, tn), lambda i, j: (layer, D_ATTN // D_REC, j)),
            pl.BlockSpec((None, CONV_CH, tn), lambda i, j: (layer, (D_ATTN + D_REC) // CONV_CH, j)),
        ],
        out_specs=pl.BlockSpec((TM, tn), lambda i, j: (i, j)),
        out_shape=jax.ShapeDtypeStruct((n, D_MODEL), F32),
        compiler_params=_params("parallel", "parallel"),
        name="outproj",
    )(x, mod, attn, rec, conv, w_out, w_out, w_out)


def _trunk_layer(x, mod, l, P, mixers, rows_per_cond):
    x = _ffn(x, mod, 0, P['norm_g'][l, 0], P['w_ffn_in'], P['w_ffn_out'], l, 0, rows_per_cond)
    z = _inproj(x, mod, P['norm_g'][l, 1], P['w_in'], l, rows_per_cond)
    attn, rec, conv, extras = mixers(z)
    x = _outproj(x, mod, attn, rec, conv, P['w_out'], l, rows_per_cond)
    x = _ffn(x, mod, 6, P['norm_g'][l, 2], P['w_ffn_in'], P['w_ffn_out'], l, 1, rows_per_cond)
    return x, extras


def kernel(x_prompt, x_sample, cache_k, cache_v, state_rec, c, c_ctx, w_ada, b_ada, norm_g, w_ffn_in, w_ffn_out,
           w_in, w_out, q_norm_g, k_norm_g, attn_sink, rec_lb_logits, rec_norm_g, conv_w, conv_b, conv_ln_g, conv_ln_b):
    batch, seq, _ = x_prompt.shape
    dec_batch, dec_seq, _ = x_sample.shape
    assert seq % TM == 0 or TM % seq == 0
    assert dec_seq % TM == 0 and dec_seq % LAT_Q_TILE == 0

    P = {'norm_g': norm_g, 'w_ffn_in': w_ffn_in.astype(BF16), 'w_ffn_out': w_ffn_out.astype(BF16),
         'w_in': w_in.astype(BF16), 'w_out': w_out.astype(BF16)}

    lb_logits = rec_lb_logits.reshape(DEPTH, 2, N_REC_HEADS, 1, REC_DK)

    cond8 = jnp.zeros((SUBLANES, D_MODEL), F32).at[0].set(c_ctx).at[1:1 + dec_batch].set(c)
    mod = _modulation(cond8, w_ada, b_ada).reshape(DEPTH, SUBLANES, N_MOD, 1, D_MODEL)
    tables = _rope_tables(dec_seq)
    sinks = attn_sink.reshape(DEPTH, N_KV_HEADS, Q_PER_KV)
    rec_g = rec_norm_g.reshape(DEPTH, N_REC_HEADS, 1, REC_DV)

    conv_params = lambda l: (conv_w[l], conv_b[l], conv_ln_g[l], conv_ln_b[l])

    h = x_prompt.reshape(batch * seq, D_MODEL)
    caches, states = None, None
    for l in range(DEPTH):
        def ctx_mixers(z, l=l, caches=caches, states=states):
            attn, new_k, new_v = _ctx_attention(z, sinks[l], q_norm_g[l], k_norm_g[l], batch, seq, l, caches)
            rec, conv, new_s = _recurrence(z, l, lb_logits, rec_g[l], None, conv_params(l), batch, seq, states)
            return attn, rec, conv, ((new_k, new_v), new_s)

        h, (caches, states) = _trunk_layer(h, mod[l, 0:1], l, P, ctx_mixers, batch * seq)
    y_prompt = h.reshape(batch, seq, D_MODEL)

    h = x_sample.reshape(dec_batch * dec_seq, D_MODEL)
    for l in range(DEPTH):
        def lat_mixers(z, l=l):
            attn = _lat_attention(z, sinks[l], q_norm_g[l], k_norm_g[l], cache_k[:, l], cache_v[:, l], tables,
                                  dec_batch, dec_seq)
            rec, conv = _recurrence(z, l, lb_logits, rec_g[l], state_rec[:, l], conv_params(l), dec_batch, dec_seq)
            return attn, rec, conv, None

        h, _ = _trunk_layer(h, mod[l, 1:1 + dec_batch], l, P, lat_mixers, dec_seq)
    y_sample = h.reshape(dec_batch, dec_seq, D_MODEL)

    return (y_prompt, y_sample, caches[0], caches[1], states)
```

```python
import functools

import jax
import jax.numpy as jnp
import numpy as np
from jax import lax
from jax.experimental import pallas as pl
from jax.experimental.pallas import tpu as pltpu

F32 = jnp.float32
BF16 = jnp.bfloat16

D_MODEL = 2048
DEPTH = 2
GRID_W = 64
HEAD_DIM = 128
D_ATTN = D_MODEL // 2
N_Q_HEADS = D_ATTN // HEAD_DIM
N_KV_HEADS = 2
Q_PER_KV = N_Q_HEADS // N_KV_HEADS
WINDOW = 128
ATTN_BLOCK = 128
ATTN_SCALE = HEAD_DIM ** -0.5
ROPE_BASE = 10000.0
MASK_VALUE = -1e30
D_REC = D_MODEL // 4
REC_DK = 128
REC_DV = 128
N_REC_HEADS = D_REC // REC_DV
REC_CHUNK = 32
CONV_CH = D_MODEL // 4
CONV_WIDTH = 31
D_FF = 5632
N_MOD = 9
EPS = 1e-6
GATE_FLOOR = 1e-30
LOG2_E = 1.4426950408889634
IN_COLS = 5120

COL_Q = 0
COL_K = 8
COL_V = 10
COL_RQ = 12
COL_RF_F = 16
COL_RF_B = 20
COL_RI = 24
COL_RG = 28
COL_CA = 32
COL_CB = 36

SUBLANES = 8
V7X_VMEM_BYTES = 64 * 1024 * 1024
VMEM_LIMIT = V7X_VMEM_BYTES * 7 // 8
SINGLE_BUFFER_BYTES = 1024 * 1024

TM = 1024
TM_FFN = 1024
TF = 512
TN_FFN_OUT = 256
TN_IN = 1024
TN_ADA = 1024
CONV_HALO = 16
CONV_SUB = 32
LAT_Q_TILE = 1024
CTX_SEQS_PER_STEP = 4
REC_LOCAL_UNROLL = 16
REC_STATE_UNROLL = 16


def _params(*sem):
    return pltpu.CompilerParams(dimension_semantics=sem, vmem_limit_bytes=VMEM_LIMIT)


def _dot(a, b):
    return jnp.dot(a, b, preferred_element_type=F32)


def _dot_nt(a, b):
    return lax.dot_general(a, b, (((1,), (1,)), ((), ())), preferred_element_type=F32)


def _dot_tn(a, b):
    return lax.dot_general(a, b, (((0,), (0,)), ((), ())), preferred_element_type=F32)


def _rms(x, g):
    return x * lax.rsqrt(jnp.mean(x * x, axis=-1, keepdims=True) + EPS) * g


def _ada_norm(x, g, scale, shift):
    gain = g * (1.0 + scale)
    return x * lax.rsqrt(jnp.mean(x * x, axis=-1, keepdims=True) + EPS) * gain + shift


def _sigmoid(x):
    return 1.0 / (1.0 + jnp.exp(-x))


def _ada_kernel(c_ref, w_ref, b_ref, o_ref):
    c = c_ref[...]
    s = (c * _sigmoid(c)).astype(BF16)
    o_ref[...] = _dot(s, w_ref[...].astype(BF16)) + b_ref[...]


def _modulation(cond8, w_ada, b_ada):
    ncol = N_MOD * D_MODEL
    return pl.pallas_call(
        _ada_kernel,
        grid=(DEPTH, ncol // TN_ADA),
        in_specs=[
            pl.BlockSpec((SUBLANES, D_MODEL), lambda l, j: (0, 0)),
            pl.BlockSpec((None, D_MODEL, TN_ADA), lambda l, j: (l, 0, j)),
            pl.BlockSpec((None, 1, TN_ADA), lambda l, j: (l, 0, j)),
        ],
        out_specs=pl.BlockSpec((None, SUBLANES, TN_ADA), lambda l, j: (l, 0, j)),
        out_shape=jax.ShapeDtypeStruct((DEPTH, SUBLANES, ncol), F32),
        compiler_params=_params("parallel", "parallel"),
        name="modulation",
    )(cond8, w_ada, b_ada.reshape(DEPTH, 1, ncol))


def _mod_spec(chunk, rows_per_cond, tm=TM):
    return pl.BlockSpec((None, None, 1, D_MODEL), lambda i, j: ((i * tm) // rows_per_cond, chunk, 0, 0))


def _ffn_kernel(nf, x_ref, sh_ref, sc_ref, g_ref, ng_ref, wa_ref, wb_ref, wo_ref, o_ref, h_scr, act_scr):
    j = pl.program_id(1)

    @pl.when(j == 0)
    def _():
        h = _ada_norm(x_ref[...], ng_ref[...], sc_ref[...], sh_ref[...])
        h_scr[...] = h.astype(BF16)

    @pl.when(j < nf)
    def _():
        h = h_scr[...]
        a = _dot(h, wa_ref[...])
        b = _dot(h, wb_ref[...])
        act_scr[:, pl.ds(pl.multiple_of(j * TF, TF), TF)] = (a * _sigmoid(a) * b).astype(BF16)

    @pl.when(j >= nf)
    def _():
        cols = pl.ds(pl.multiple_of((j - nf) * TN_FFN_OUT, TN_FFN_OUT), TN_FFN_OUT)
        o_ref[...] = x_ref[:, cols] + 0.5 * g_ref[...] * _dot(act_scr[...], wo_ref[...])


def _ffn(x, mod, first_chunk, norm_g, w_in, w_out, layer, which, rows_per_cond):
    n = x.shape[0]
    nf = D_FF // TF
    nout = D_MODEL // TN_FFN_OUT
    hid = lambda j: jnp.minimum(j, nf - 1)
    out = lambda j: jnp.maximum(j - nf, 0)
    return pl.pallas_call(
        functools.partial(_ffn_kernel, nf),
        grid=(n // TM_FFN, nf + nout),
        in_specs=[
            pl.BlockSpec((TM_FFN, D_MODEL), lambda i, j: (i, 0)),
            _mod_spec(first_chunk, rows_per_cond, TM_FFN),
            _mod_spec(first_chunk + 1, rows_per_cond, TM_FFN),
            pl.BlockSpec((None, None, 1, TN_FFN_OUT),
                         lambda i, j: ((i * TM_FFN) // rows_per_cond, first_chunk + 2, 0, out(j))),
            pl.BlockSpec((1, D_MODEL), lambda i, j: (0, 0)),
            pl.BlockSpec((None, None, D_MODEL, TF), lambda i, j: (layer, which, 0, hid(j))),
            pl.BlockSpec((None, None, D_MODEL, TF), lambda i, j: (layer, which, 0, hid(j) + nf)),
            pl.BlockSpec((None, None, D_FF, TN_FFN_OUT), lambda i, j: (layer, which, 0, out(j))),
        ],
        out_specs=pl.BlockSpec((TM_FFN, TN_FFN_OUT), lambda i, j: (i, out(j))),
        out_shape=jax.ShapeDtypeStruct((n, D_MODEL), F32),
        scratch_shapes=[pltpu.VMEM((TM_FFN, D_MODEL), BF16), pltpu.VMEM((TM_FFN, D_FF), BF16)],
        compiler_params=_params("parallel", "arbitrary"),
        name="ffn",
    )(x, mod, mod, mod, norm_g.reshape(1, D_MODEL), w_in, w_in, w_out)


def _inproj_kernel(x_ref, sh_ref, sc_ref, ng_ref, w_ref, o_ref, h_scr):
    @pl.when(pl.program_id(1) == 0)
    def _():
        h = _ada_norm(x_ref[...], ng_ref[...], sc_ref[...], sh_ref[...])
        h_scr[...] = h.astype(BF16)

    o_ref[...] = _dot(h_scr[...], w_ref[...])


def _inproj(x, mod, norm_g, w_in, layer, rows_per_cond):
    n = x.shape[0]
    return pl.pallas_call(
        _inproj_kernel,
        grid=(n // TM, IN_COLS // TN_IN),
        in_specs=[
            pl.BlockSpec((TM, D_MODEL), lambda i, j: (i, 0)),
            _mod_spec(3, rows_per_cond),
            _mod_spec(4, rows_per_cond),
            pl.BlockSpec((1, D_MODEL), lambda i, j: (0, 0)),
            pl.BlockSpec((None, D_MODEL, TN_IN), lambda i, j: (layer, 0, j)),
        ],
        out_specs=pl.BlockSpec((TM, TN_IN), lambda i, j: (i, j)),
        out_shape=jax.ShapeDtypeStruct((n, IN_COLS), F32),
        scratch_shapes=[pltpu.VMEM((TM, D_MODEL), BF16)],
        compiler_params=_params("parallel", "arbitrary"),
        name="inproj",
    )(x, mod, mod, norm_g.reshape(1, D_MODEL), w_in)


QK_SCALE_LOG2 = ATTN_SCALE * LOG2_E


def _with_ones(v):
    return jnp.concatenate([v, jnp.ones_like(v)], axis=-1)


def _softmax_sink(scores, values, sink):
    sink2 = sink * LOG2_E
    m = sink2
    for s in scores:
        m = jnp.maximum(jnp.max(s, axis=-1, keepdims=True), m)
    acc = None
    for s, v in zip(scores, values):
        pv = _dot(jnp.exp2(s - m).astype(BF16), v)
        acc = pv if acc is None else acc + pv
    return acc[:, :HEAD_DIM] / (acc[:, HEAD_DIM:] + jnp.exp2(sink2 - m))


def _put_layer(ref, first_layer, lead, tail, value):
    if first_layer:
        ref[lead + (0,) + tail] = value
        for other in range(1, DEPTH):
            ref[lead + (other,) + tail] = jnp.zeros_like(value)
    else:
        ref[lead + tail] = value


def _ctx_attn_kernel(first_layer, sink_ref, q_ref, k_ref, v_ref, qg_ref, kg_ref, *rest):
    o_ref, kc_ref, vc_ref = rest[-3:]
    kv = pl.program_id(1)
    nseq, seq = kc_ref.shape[0], kc_ref.shape[-2]
    heads = [slice(g * HEAD_DIM, (g + 1) * HEAD_DIM) for g in range(Q_PER_KV)]
    keys, values = [], []
    for i in range(nseq):
        rows = slice(i * seq, (i + 1) * seq)
        kn = _rms(k_ref[rows, :], kg_ref[...])
        v = v_ref[rows, :]
        _put_layer(kc_ref, first_layer, (i,), (), kn)
        _put_layer(vc_ref, first_layer, (i,), (), v)
        keys.append(kn.astype(BF16))
        values.append(_with_ones(v.astype(BF16)))
    logits = [[_dot_nt((_rms(q_ref[i * seq:(i + 1) * seq, cols], qg_ref[...]) * QK_SCALE_LOG2).astype(BF16), keys[i])
               for cols in heads] for i in range(nseq)]
    for i in range(nseq):
        for g, cols in enumerate(heads):
            o = _softmax_sink([logits[i][g]], [values[i]], sink_ref[kv, g])
            o_ref[i * seq:(i + 1) * seq, cols] = o.astype(BF16)


def _ctx_attention(z, sink, q_g, k_g, batch, seq, layer, caches):
    qw = Q_PER_KV * HEAD_DIM
    per = CTX_SEQS_PER_STEP
    first = caches is None
    cache_shape = jax.ShapeDtypeStruct((batch, DEPTH, N_KV_HEADS, seq, HEAD_DIM), F32)
    if first:
        cache_spec = pl.BlockSpec((per, DEPTH, None, seq, HEAD_DIM), lambda b, kv: (b, 0, kv, 0, 0))
    else:
        cache_spec = pl.BlockSpec((per, None, None, seq, HEAD_DIM), lambda b, kv: (b, layer, kv, 0, 0))
    in_specs = [
        pl.BlockSpec(memory_space=pltpu.SMEM),
        pl.BlockSpec((per * seq, qw), lambda b, kv: (b, kv)),
        pl.BlockSpec((per * seq, HEAD_DIM), lambda b, kv: (b, COL_K + kv)),
        pl.BlockSpec((per * seq, HEAD_DIM), lambda b, kv: (b, COL_V + kv)),
        pl.BlockSpec((1, HEAD_DIM), lambda b, kv: (0, 0)),
        pl.BlockSpec((1, HEAD_DIM), lambda b, kv: (0, 0)),
    ]
    args = [sink, z, z, z, q_g.reshape(1, HEAD_DIM), k_g.reshape(1, HEAD_DIM)]
    aliases = {}
    if not first:
        aliases = {len(args): 1, len(args) + 1: 2}
        in_specs += [pl.BlockSpec(memory_space=pl.ANY)] * 2
        args += list(caches)
    return pl.pallas_call(
        functools.partial(_ctx_attn_kernel, first),
        grid=(batch // per, N_KV_HEADS),
        in_specs=in_specs,
        out_specs=[pl.BlockSpec((per * seq, qw), lambda b, kv: (b, kv)), cache_spec, cache_spec],
        out_shape=[jax.ShapeDtypeStruct((batch * seq, D_ATTN), BF16), cache_shape, cache_shape],
        input_output_aliases=aliases,
        compiler_params=_params("parallel", "parallel"),
        name="ctx_attention",
    )(*args)


def _rope(x, cos, sin_a, sin_b):
    quarter = HEAD_DIM // 4
    up = pltpu.roll(x, HEAD_DIM - quarter, 1)
    down = pltpu.roll(x, quarter, 1)
    return x * cos + up * sin_a + down * sin_b


def _lat_attn_kernel(sink_ref, q_ref, k_ref, v_ref, kc_ref, vc_ref, qg_ref, kg_ref,
                     cos_ref, sa_ref, sb_ref, cosq_ref, saq_ref, sbq_ref, o_ref, k_scr, v_scr, kc_scr, vc_scr):
    kv = pl.program_id(1)
    qt = pl.program_id(2)
    seq = k_ref.shape[0]
    span = 3 * ATTN_BLOCK

    @pl.when(qt == 0)
    def _():
        kn = _rms(k_ref[...], kg_ref[...])
        k_scr[...] = _rope(kn, cos_ref[...], sa_ref[...], sb_ref[...]).astype(BF16)
        v_scr[...] = _with_ones(v_ref[...].astype(BF16))
        kc_scr[...] = kc_ref[...].astype(BF16)
        vc_scr[...] = _with_ones(vc_ref[...].astype(BF16))

    kcb = kc_scr[...]
    vcb = vc_scr[...]
    nblk = LAT_Q_TILE // ATTN_BLOCK
    heads = [slice(g * HEAD_DIM, (g + 1) * HEAD_DIM) for g in range(Q_PER_KV)]
    stacked = Q_PER_KV * ATTN_BLOCK

    head_of_row = lax.broadcasted_iota(jnp.int32, (stacked, 1), 0) // ATTN_BLOCK
    sink_col = jnp.zeros((stacked, 1), F32)
    for g in range(Q_PER_KV):
        sink_col = jnp.where(head_of_row == g, sink_ref[kv, g], sink_col)

    def block_rows(blk):
        return slice(blk * ATTN_BLOCK, (blk + 1) * ATTN_BLOCK)

    def logits_of(blk):
        rows = block_rows(blk)
        q0 = (qt * nblk + blk) * ATTN_BLOCK
        start = pl.multiple_of(jnp.clip(q0 - ATTN_BLOCK, 0, seq - span), ATTN_BLOCK)
        qpos = q0 + lax.broadcasted_iota(jnp.int32, (stacked, span), 0) % ATTN_BLOCK
        kpos = start + lax.broadcasted_iota(jnp.int32, (stacked, span), 1)
        valid = jnp.abs(qpos - kpos) <= WINDOW
        cos, sa, sb = cosq_ref[rows, :], saq_ref[rows, :], sbq_ref[rows, :]
        qn = jnp.concatenate([_rope(_rms(q_ref[rows, cols], qg_ref[...]), cos, sa, sb) for cols in heads], axis=0)
        qn = (qn * QK_SCALE_LOG2).astype(BF16)
        s_loc = jnp.where(valid, _dot_nt(qn, k_scr[pl.ds(start, span), :]), MASK_VALUE)
        return s_loc, _dot_nt(qn, kcb), start

    pending = logits_of(0)
    for blk in range(nblk):
        upcoming = logits_of(blk + 1) if blk + 1 < nblk else None
        s_loc, s_ctx, start = pending
        o = _softmax_sink([s_loc, s_ctx], [v_scr[pl.ds(start, span), :], vcb], sink_col)
        for g, cols in enumerate(heads):
            o_ref[block_rows(blk), cols] = o[g * ATTN_BLOCK:(g + 1) * ATTN_BLOCK, :].astype(BF16)
        pending = upcoming


def _rope_tables(seq):
    quarter = HEAD_DIM // 4
    pos = np.arange(seq)
    inv_freq = (ROPE_BASE ** (-np.arange(quarter, dtype=np.float32) / quarter)).astype(np.float32)
    zero = np.zeros((seq, quarter), np.float32)

    def trig(p):
        ang = p.astype(np.float32)[:, None] * inv_freq
        return np.cos(ang).astype(np.float32), np.sin(ang).astype(np.float32)

    cr, sr = trig(pos // GRID_W)
    cc, sc = trig(pos % GRID_W)
    cos = np.concatenate([cr, cr, cc, cc], axis=-1)
    sin_a = np.concatenate([-sr, zero, -sc, zero], axis=-1)
    sin_b = np.concatenate([zero, sr, zero, sc], axis=-1)
    return jnp.asarray(cos), jnp.asarray(sin_a), jnp.asarray(sin_b)


def _lat_attention(z, sink, q_g, k_g, k_ctx, v_ctx, tables, batch, seq):
    qw = Q_PER_KV * HEAD_DIM
    nqt = seq // LAT_Q_TILE
    past = k_ctx.shape[2]
    cos, sin_a, sin_b = tables
    full_tab = pl.BlockSpec((seq, HEAD_DIM), lambda b, kv, qt: (0, 0))
    tile_tab = pl.BlockSpec((LAT_Q_TILE, HEAD_DIM), lambda b, kv, qt: (qt, 0))
    ctx_spec = pl.BlockSpec((None, None, past, HEAD_DIM), lambda b, kv, qt: (b, kv, 0, 0))
    gain = pl.BlockSpec((1, HEAD_DIM), lambda b, kv, qt: (0, 0))
    return pl.pallas_call(
        _lat_attn_kernel,
        grid=(batch, N_KV_HEADS, nqt),
        in_specs=[
            pl.BlockSpec(memory_space=pltpu.SMEM),
            pl.BlockSpec((LAT_Q_TILE, qw), lambda b, kv, qt: (b * nqt + qt, kv)),
            pl.BlockSpec((seq, HEAD_DIM), lambda b, kv, qt: (b, COL_K + kv)),
            pl.BlockSpec((seq, HEAD_DIM), lambda b, kv, qt: (b, COL_V + kv)),
            ctx_spec, ctx_spec, gain, gain,
            full_tab, full_tab, full_tab, tile_tab, tile_tab, tile_tab,
        ],
        out_specs=pl.BlockSpec((LAT_Q_TILE, qw), lambda b, kv, qt: (b * nqt + qt, kv)),
        out_shape=jax.ShapeDtypeStruct((batch * seq, D_ATTN), BF16),
        scratch_shapes=[pltpu.VMEM((seq, HEAD_DIM), BF16), pltpu.VMEM((seq, 2 * HEAD_DIM), BF16),
                        pltpu.VMEM((past, HEAD_DIM), BF16), pltpu.VMEM((past, 2 * HEAD_DIM), BF16)],
        compiler_params=_params("parallel", "parallel", "arbitrary"),
        name="lat_attention",
    )(sink, z, z, z, k_ctx, v_ctx, q_g.reshape(1, HEAD_DIM), k_g.reshape(1, HEAD_DIM),
      cos, sin_a, sin_b, cos, sin_a, sin_b)


def _split3(x):
    hi = x.astype(BF16)
    r1 = x - hi.astype(F32)
    mid = r1.astype(BF16)
    lo = (r1 - mid.astype(F32)).astype(BF16)
    return hi, mid, lo


def _rec_gates(forward, zf, lb):
    c = REC_CHUNK
    t = jnp.exp(-jnp.abs(zf))
    r = 1.0 / (1.0 + t)
    tr = t * r
    nonneg = zf >= 0
    sig_pos = jnp.where(nonneg, r, tr)
    sig_neg = jnp.where(nonneg, tr, r)
    one_m_lb = 1.0 - lb
    log_f = jnp.log(jnp.maximum(lb + one_m_lb * sig_pos, GATE_FLOOR))
    k = one_m_lb * sig_neg
    ri = lax.broadcasted_iota(jnp.int32, (c, c), 0)
    ci = lax.broadcasted_iota(jnp.int32, (c, c), 1)
    tri = jnp.where((ci <= ri) if forward else (ci >= ri), 1.0, 0.0).astype(BF16)
    hi, mid, lo = _split3(log_f)
    cum2 = (_dot(tri, hi) + _dot(tri, mid) + _dot(tri, lo)) * LOG2_E
    return cum2 - jnp.log2(k), cum2


def _rec_same_block(forward, q, key_row, cum2):
    sub_row = lax.broadcasted_iota(jnp.int32, (SUBLANES, REC_DK), 0)
    pieces = []
    for b in range(REC_CHUNK // SUBLANES):
        blk = slice(b * SUBLANES, (b + 1) * SUBLANES)
        cum_b, q_b = cum2[blk, :], q[blk, :]
        for sl in range(SUBLANES):
            s = b * SUBLANES + sl
            keep = (sub_row >= sl) if forward else (sub_row <= sl)
            pieces.append(jnp.where(keep, jnp.exp2(cum_b - key_row(s)) * q_b, 0.0))
    return _dot(jnp.concatenate(pieces, axis=0).astype(BF16), jnp.ones((REC_DK, REC_DV), BF16))


def _rec_cross_block(forward, q, key2, v, cum2):
    c = REC_CHUNK
    q_parts, k_parts, v_parts, segments = [], [], [], []
    for tb in range(c // SUBLANES):
        blk = slice(tb * SUBLANES, (tb + 1) * SUBLANES)
        src, ref_row = (slice(0, blk.start), blk.start - 1) if forward else (slice(blk.stop, c), blk.stop)
        if src.stop == src.start:
            q_parts.append(jnp.zeros((SUBLANES, REC_DK), F32))
            continue
        ref = cum2[ref_row:ref_row + 1, :]
        q_parts.append(q[blk, :] * jnp.exp2(cum2[blk, :] - ref))
        k_parts.append(jnp.exp2(ref - key2[src, :]))
        v_parts.append(v[src, :])
        segments.append((tb, src.stop - src.start))
    a = _dot_nt(jnp.concatenate(q_parts, axis=0).astype(BF16), jnp.concatenate(k_parts, axis=0).astype(BF16))
    return a, segments, jnp.concatenate(v_parts, axis=0).astype(BF16)


def _rec_cross_apply(a, segments, v_all):
    c, ncol = a.shape
    row_blk = lax.broadcasted_iota(jnp.int32, (c, ncol), 0) // SUBLANES
    col = lax.broadcasted_iota(jnp.int32, (c, ncol), 1)
    col_blk = jnp.full((c, ncol), -1, jnp.int32)
    start = 0
    for tb, width in segments:
        col_blk = jnp.where((col >= start) & (col < start + width), tb, col_blk)
        start += width
    return _dot(jnp.where(row_blk == col_blk, a, 0.0).astype(BF16), v_all)


def _rec_same_apply(lane_sums, v_row):
    o_blocks = []
    for b in range(REC_CHUNK // SUBLANES):
        acc = None
        for sl in range(SUBLANES):
            s = b * SUBLANES + sl
            term = lane_sums[s * SUBLANES:(s + 1) * SUBLANES, :] * v_row(s)
            acc = term if acc is None else acc + term
        o_blocks.append(acc)
    return jnp.concatenate(o_blocks, axis=0)


def _rec_kernel(layer, has_init, *refs):
    q_ref, ff_ref, fb_ref, v_ref, g_ref, lg_ref, ng_ref = refs[:7]
    refs = refs[7:]
    if has_init:
        s0_ref, refs = refs[0], refs[1:]
    conv_in, refs = refs[:10], refs[10:]
    if not has_init and layer > 0:
        refs = refs[1:]
    o_ref, conv_o_ref = refs[:2]
    refs = refs[2:]
    if not has_init:
        s_out_ref, refs = refs[0], refs[1:]
    o_scr, qd_scr, kd_scr, gl_scr, oi_scr, key_scr, glu_scr, conv_h_scr, conv_w_scr = refs
    seq = q_ref.shape[0]
    nchunk = seq // REC_CHUNK
    c = REC_CHUNK

    logits = lg_ref[...]
    e = jnp.exp(logits - jnp.max(logits, axis=0, keepdims=True))
    p = e / jnp.sum(e, axis=0, keepdims=True)
    lb = jnp.zeros_like(p[0])
    for i in range(1, layer + 1):
        lb = lb + p[i]

    def rows_of(ci):
        return pl.ds(pl.multiple_of(ci * c, c), c)

    z_refs = (ff_ref, fb_ref)
    local_unroll = min(REC_LOCAL_UNROLL, nchunk)

    def local_body(it, carry):
        chunks = [it * local_unroll + u for u in range(local_unroll)]
        qs = [q_ref[rows_of(ci), :] for ci in chunks]
        vs = [v_ref[rows_of(ci), :] for ci in chunks]
        items = [(u, d) for u in range(local_unroll) for d in range(2)]
        gates = [_rec_gates(d == 0, z_refs[d][rows_of(chunks[u]), :], lb[d]) for u, d in items]
        for idx, (key2, _) in enumerate(gates):
            key_scr[idx] = key2
        key_rows = [lambda s, idx=idx: key_scr[idx, s:s + 1, :] for idx in range(len(items))]
        v_rows = [lambda s, ci=ci: v_ref[pl.ds(ci * c + s, 1), :] for ci in chunks]
        same = [_rec_same_block(d == 0, qs[u], key_rows[idx], cum2)
                for idx, ((u, d), (_, cum2)) in enumerate(zip(items, gates))]
        cross = [_rec_cross_block(d == 0, qs[u], key2, vs[u], cum2) for (u, d), (key2, cum2) in zip(items, gates)]
        totals = [None] * local_unroll
        for (u, d), (key2, cum2), lane_sums, (a, segments, v_all) in zip(items, gates, same, cross):
            rows = rows_of(chunks[u])
            last = cum2[c - 1:c, :] if d == 0 else cum2[0:1, :]
            qd_scr[d, rows, :] = (qs[u] * jnp.exp2(cum2)).astype(BF16)
            kd_scr[d, rows, :] = jnp.exp2(last - key2).astype(BF16)
            gl_scr[d, chunks[u]] = jnp.broadcast_to(jnp.exp2(last), (SUBLANES, REC_DK))
            o_local = _rec_same_apply(lane_sums, v_rows[u]) + _rec_cross_apply(a, segments, v_all)
            totals[u] = o_local if totals[u] is None else totals[u] + o_local
        for u, ci in enumerate(chunks):
            o_scr[rows_of(ci), :] = totals[u]
        return carry

    lax.fori_loop(0, nchunk // local_unroll, local_body, 0)

    if has_init:
        init = (s0_ref[0].T, s0_ref[1].T)
    else:
        init = (jnp.zeros((REC_DV, REC_DK), F32),) * 2

    state_unroll = min(REC_STATE_UNROLL, nchunk)
    conv_rows = conv_o_ref.shape[0] // (nchunk // state_unroll)
    _conv_stage(pl.program_id(1), N_REC_HEADS, *conv_in[:7], glu_scr, conv_w_scr)

    def state_body(it, states):
        offset = pl.multiple_of(it * conv_rows, conv_rows)
        conv_h_scr[0] = glu_scr[pl.ds(offset, conv_rows + 2 * CONV_HALO), :]

        def store(r0, val):
            conv_o_ref[pl.ds(pl.multiple_of(offset + r0, CONV_SUB), CONV_SUB), :] = val

        _conv_rows(conv_rows, conv_h_scr, conv_w_scr, *conv_in[7:], store)
        states = list(states)
        steps = [it * state_unroll + u for u in range(state_unroll)]
        order = [(d, step if d == 0 else nchunk - 1 - step) for step in steps for d in range(2)]
        updates = [_dot_tn(v_ref[rows_of(ci), :].astype(BF16), kd_scr[d, rows_of(ci), :]) for d, ci in order]
        for (d, ci), update in zip(order, updates):
            st = states[d]
            oi_scr[d, rows_of(ci), :] = _dot_nt(qd_scr[d, rows_of(ci), :], st.astype(BF16))
            decayed = (st.reshape(REC_DV // SUBLANES, SUBLANES, REC_DK) * gl_scr[d, ci]).reshape(REC_DV, REC_DK)
            states[d] = decayed + update
        return tuple(states)

    final = lax.fori_loop(0, nchunk // state_unroll, state_body, init)
    if not has_init:
        for d in range(2):
            _put_layer(s_out_ref, layer == 0, (), (d,), final[d].T)

    def out_body(ci, carry):
        rows = rows_of(ci)
        gate = g_ref[rows, :]
        o = o_scr[rows, :] + oi_scr[0, rows, :] + oi_scr[1, rows, :]
        o_ref[rows, :] = (_rms(o, ng_ref[...]) * (gate * _sigmoid(gate))).astype(BF16)
        return carry

    lax.fori_loop(0, nchunk, out_body, 0, unroll=8)


def _recurrence(z, layer, lb_logits, norm_g, s0, conv_params, batch, seq, states=None):
    has_init = s0 is not None
    assert has_init or (states is None) == (layer == 0)
    n = batch * seq
    col = lambda base: pl.BlockSpec((seq, REC_DK), lambda b, h: (b, base + h))
    state_spec = pl.BlockSpec((None, 2, None, REC_DK, REC_DV), lambda b, h: (b, 0, h, 0, 0))
    in_specs = [col(COL_RQ), col(COL_RF_F), col(COL_RF_B), col(COL_RI), col(COL_RG),
                pl.BlockSpec((DEPTH, 2, None, 1, REC_DK), lambda b, h: (0, 0, h, 0, 0)),
                pl.BlockSpec((None, 1, REC_DV), lambda b, h: (h, 0, 0))]
    args = [z, z, z, z, z, lb_logits, norm_g]
    if has_init:
        in_specs.append(state_spec)
        args.append(s0)

    conv_step = seq // N_REC_HEADS
    state_trips = (seq // REC_CHUNK) // min(REC_STATE_UNROLL, seq // REC_CHUNK)
    conv_rows = conv_step // state_trips
    assert conv_step % CONV_HALO == 0 and conv_rows % CONV_SUB == 0
    cw = CONV_CH // HEAD_DIM
    halo_per_step = conv_step // CONV_HALO
    nhalo = n // CONV_HALO
    part = lambda b, h: b * N_REC_HEADS + h
    big_block = conv_step * CONV_CH * jnp.dtype(F32).itemsize >= SINGLE_BUFFER_BYTES
    mid_mode = dict(pipeline_mode=pl.Buffered(1)) if big_block else {}
    mid = lambda cc: pl.BlockSpec((conv_step, CONV_CH), lambda b, h: (part(b, h), cc), **mid_mode)
    top = lambda cc: pl.BlockSpec(
        (CONV_HALO, CONV_CH), lambda b, h: (jnp.maximum(part(b, h) * halo_per_step - 1, 0), cc))
    bot = lambda cc: pl.BlockSpec(
        (CONV_HALO, CONV_CH), lambda b, h: (jnp.minimum((part(b, h) + 1) * halo_per_step, nhalo - 1), cc))
    vec = pl.BlockSpec((1, CONV_CH), lambda b, h: (0, 0))
    ca, cb = COL_CA // cw, COL_CB // cw
    w, bias, ln_g, ln_b = conv_params
    in_specs += [mid(ca), mid(cb), top(ca), top(cb), bot(ca), bot(cb),
                 pl.BlockSpec((CONV_WIDTH, CONV_CH), lambda b, h: (0, 0)), vec, vec, vec]
    args += [z, z, z, z, z, z, w, bias.reshape(1, CONV_CH), ln_g.reshape(1, CONV_CH), ln_b.reshape(1, CONV_CH)]

    out_specs = [pl.BlockSpec((seq, REC_DV), lambda b, h: (b, h)),
                 pl.BlockSpec((conv_step, CONV_CH), lambda b, h: (part(b, h), 0))]
    out_shape = [jax.ShapeDtypeStruct((n, D_REC), BF16), jax.ShapeDtypeStruct((n, CONV_CH), BF16)]
    aliases = {}
    if not has_init:
        if layer == 0:
            out_specs.append(pl.BlockSpec((None, DEPTH, 2, None, REC_DK, REC_DV), lambda b, h: (b, 0, 0, h, 0, 0)))
        else:
            out_specs.append(pl.BlockSpec((None, None, 2, None, REC_DK, REC_DV),
                                          lambda b, h: (b, layer, 0, h, 0, 0)))
            aliases = {len(args): 2}
            in_specs.append(pl.BlockSpec(memory_space=pl.ANY))
            args.append(states)
        out_shape.append(jax.ShapeDtypeStruct((batch, DEPTH, 2, N_REC_HEADS, REC_DK, REC_DV), F32))
    return pl.pallas_call(
        functools.partial(_rec_kernel, layer, has_init),
        grid=(batch, N_REC_HEADS),
        in_specs=in_specs,
        out_specs=out_specs,
        out_shape=out_shape,
        input_output_aliases=aliases,
        scratch_shapes=[pltpu.VMEM((seq, REC_DV), F32),
                        pltpu.VMEM((2, seq, REC_DK), BF16),
                        pltpu.VMEM((2, seq, REC_DK), BF16),
                        pltpu.VMEM((2, seq // REC_CHUNK, SUBLANES, REC_DK), F32),
                        pltpu.VMEM((2, seq, REC_DV), F32),
                        pltpu.VMEM((2 * min(REC_LOCAL_UNROLL, seq // REC_CHUNK), REC_CHUNK, REC_DK), F32),
                        pltpu.VMEM((conv_step + 2 * CONV_HALO, CONV_CH), F32),
                        pltpu.VMEM((SUBLANES, conv_rows + 2 * CONV_HALO, CONV_CH), F32),
                        pltpu.VMEM((CONV_WIDTH, SUBLANES, CONV_CH), F32)],
        compiler_params=_params("parallel", "parallel"),
        name="hgrn2_conv",
    )(*args)


def _conv_stage(part, nparts, a_ref, b_ref, at_ref, bt_ref, ab_ref, bb_ref, w_ref, glu_scr, w_scr):
    has_top = jnp.where(part > 0, 1.0, 0.0)
    has_bot = jnp.where(part < nparts - 1, 1.0, 0.0)
    rows = a_ref.shape[0]

    def glu(a, b):
        return a * _sigmoid(b)

    glu_scr[0:CONV_HALO, :] = glu(at_ref[...], bt_ref[...]) * has_top
    glu_scr[CONV_HALO:CONV_HALO + rows, :] = glu(a_ref[...], b_ref[...])
    glu_scr[CONV_HALO + rows:, :] = glu(ab_ref[...], bb_ref[...]) * has_bot
    for j in range(CONV_WIDTH):
        w_scr[j] = jnp.broadcast_to(w_ref[j:j + 1, :], (SUBLANES, CONV_CH))


def _conv_rows(rows, h_scr, w_scr, bias_ref, lg_ref, lb_ref, store):
    half = CONV_WIDTH // 2
    shifted_rows = rows + 2 * CONV_HALO - SUBLANES
    for r in range(1, SUBLANES):
        h_scr[r, 0:shifted_rows, :] = h_scr[0, r:r + shifted_rows, :]
    for r0 in range(0, rows, CONV_SUB):
        acc = None
        for j in range(CONV_WIDTH):
            lo = r0 + CONV_HALO - half + j
            shift = lo % SUBLANES
            weight = w_scr[j]
            window = h_scr[shift, lo - shift:lo - shift + CONV_SUB, :]
            term = (window.reshape(CONV_SUB // SUBLANES, SUBLANES, CONV_CH) * weight).reshape(CONV_SUB, CONV_CH)
            acc = term if acc is None else acc + term
        y = acc + bias_ref[...]
        mu = jnp.mean(y, axis=-1, keepdims=True)
        yc = y - mu
        var = jnp.mean(yc * yc, axis=-1, keepdims=True)
        yn = yc * lax.rsqrt(var + EPS) * lg_ref[...] + lb_ref[...]
        store(r0, (yn * _sigmoid(yn)).astype(BF16))


def _outproj_kernel(x_ref, g_ref, a_ref, r_ref, c_ref, wa_ref, wr_ref, wc_ref, o_ref):
    mix = _dot(a_ref[...], wa_ref[...]) + _dot(r_ref[...], wr_ref[...]) + _dot(c_ref[...], wc_ref[...])
    o_ref[...] = x_ref[...] + g_ref[...] * mix


def _outproj(x, mod, attn, rec, conv, w_out, layer, rows_per_cond):
    n = x.shape[0]
    tn = D_MODEL // 2
    row = lambda width: pl.BlockSpec((TM, width), lambda i, j: (i, 0))
    return pl.pallas_call(
        _outproj_kernel,
        grid=(n // TM, D_MODEL // tn),
        in_specs=[
            pl.BlockSpec((TM, tn), lambda i, j: (i, j)),
            pl.BlockSpec((None, None, 1, tn), lambda i, j: ((i * TM) // rows_per_cond, 5, 0, j)),
            row(D_ATTN), row(D_REC), row(CONV_CH),
            pl.BlockSpec((None, D_ATTN, tn), lambda i, j: (layer, 0, j)),
            pl.BlockSpec((None, D_REC, tn), lambda i, j: (layer, D_ATTN // D_REC, j)),
            pl.BlockSpec((None, CONV_CH, tn), lambda i, j: (layer, (D_ATTN + D_REC) // CONV_CH, j)),
        ],
        out_specs=pl.BlockSpec((TM, tn), lambda i, j: (i, j)),
        out_shape=jax.ShapeDtypeStruct((n, D_MODEL), F32),
        compiler_params=_params("parallel", "parallel"),
        name="outproj",
    )(x, mod, attn, rec, conv, w_out, w_out, w_out)


def _trunk_layer(x, mod, l, P, mixers, rows_per_cond):
    x = _ffn(x, mod, 0, P['norm_g'][l, 0], P['w_ffn_in'], P['w_ffn_out'], l, 0, rows_per_cond)
    z = _inproj(x, mod, P['norm_g'][l, 1], P['w_in'], l, rows_per_cond)
    attn, rec, conv, extras = mixers(z)
    x = _outproj(x, mod, attn, rec, conv, P['w_out'], l, rows_per_cond)
    x = _ffn(x, mod, 6, P['norm_g'][l, 2], P['w_ffn_in'], P['w_ffn_out'], l, 1, rows_per_cond)
    return x, extras


def kernel(x_prompt, x_sample, cache_k, cache_v, state_rec, c, c_ctx, w_ada, b_ada, norm_g, w_ffn_in, w_ffn_out,
           w_in, w_out, q_norm_g, k_norm_g, attn_sink, rec_lb_logits, rec_norm_g, conv_w, conv_b, conv_ln_g, conv_ln_b):
    batch, seq, _ = x_prompt.shape
    dec_batch, dec_seq, _ = x_sample.shape
    assert seq % TM == 0 or TM % seq == 0
    assert dec_seq % TM == 0 and dec_seq % LAT_Q_TILE == 0

    P = {'norm_g': norm_g, 'w_ffn_in': w_ffn_in.astype(BF16), 'w_ffn_out': w_ffn_out.astype(BF16),
         'w_in': w_in.astype(BF16), 'w_out': w_out.astype(BF16)}

    lb_logits = rec_lb_logits.reshape(DEPTH, 2, N_REC_HEADS, 1, REC_DK)

    cond8 = jnp.zeros((SUBLANES, D_MODEL), F32).at[0].set(c_ctx).at[1:1 + dec_batch].set(c)
    mod = _modulation(cond8, w_ada, b_ada).reshape(DEPTH, SUBLANES, N_MOD, 1, D_MODEL)
    tables = _rope_tables(dec_seq)
    sinks = attn_sink.reshape(DEPTH, N_KV_HEADS, Q_PER_KV)
    rec_g = rec_norm_g.reshape(DEPTH, N_REC_HEADS, 1, REC_DV)

    conv_params = lambda l: (conv_w[l], conv_b[l], conv_ln_g[l], conv_ln_b[l])

    h = x_prompt.reshape(batch * seq, D_MODEL)
    caches, states = None, None
    for l in range(DEPTH):
        def ctx_mixers(z, l=l, caches=caches, states=states):
            attn, new_k, new_v = _ctx_attention(z, sinks[l], q_norm_g[l], k_norm_g[l], batch, seq, l, caches)
            rec, conv, new_s = _recurrence(z, l, lb_logits, rec_g[l], None, conv_params(l), batch, seq, states)
            return attn, rec, conv, ((new_k, new_v), new_s)

        h, (caches, states) = _trunk_layer(h, mod[l, 0:1], l, P, ctx_mixers, batch * seq)
    y_prompt = h.reshape(batch, seq, D_MODEL)

    h = x_sample.reshape(dec_batch * dec_seq, D_MODEL)
    for l in range(DEPTH):
        def lat_mixers(z, l=l):
            attn = _lat_attention(z, sinks[l], q_norm_g[l], k_norm_g[l], cache_k[:, l], cache_v[:, l], tables,
                                  dec_batch, dec_seq)
            rec, conv = _recurrence(z, l, lb_logits, rec_g[l], state_rec[:, l], conv_params(l), dec_batch, dec_seq)
            return attn, rec, conv, None

        h, _ = _trunk_layer(h, mod[l, 1:1 + dec_batch], l, P, lat_mixers, dec_seq)
    y_sample = h.reshape(dec_batch, dec_seq, D_MODEL)

    return (y_prompt, y_sample, caches[0], caches[1], states)
```

```python
import functools

import jax
import jax.numpy as jnp
import numpy as np
from jax import lax
from jax.experimental import pallas as pl
from jax.experimental.pallas import tpu as pltpu

F32 = jnp.float32
BF16 = jnp.bfloat16

D_MODEL = 2048
DEPTH = 2
GRID_W = 64
HEAD_DIM = 128
D_ATTN = D_MODEL // 2
N_Q_HEADS = D_ATTN // HEAD_DIM
N_KV_HEADS = 2
Q_PER_KV = N_Q_HEADS // N_KV_HEADS
WINDOW = 128
ATTN_BLOCK = 128
ATTN_SCALE = HEAD_DIM ** -0.5
ROPE_BASE = 10000.0
MASK_VALUE = -1e30
D_REC = D_MODEL // 4
REC_DK = 128
REC_DV = 128
N_REC_HEADS = D_REC // REC_DV
REC_CHUNK = 32
CONV_CH = D_MODEL // 4
CONV_WIDTH = 31
D_FF = 5632
N_MOD = 9
EPS = 1e-6
GATE_FLOOR = 1e-30
LOG2_E = 1.4426950408889634
IN_COLS = 5120

COL_Q = 0
COL_K = 8
COL_V = 10
COL_RQ = 12
COL_RF_F = 16
COL_RF_B = 20
COL_RI = 24
COL_RG = 28
COL_CA = 32
COL_CB = 36

SUBLANES = 8
V7X_VMEM_BYTES = 64 * 1024 * 1024
VMEM_LIMIT = V7X_VMEM_BYTES * 7 // 8
SINGLE_BUFFER_BYTES = 1024 * 1024

TM = 1024
TM_FFN = 1024
TF = 512
TN_FFN_OUT = 256
TN_IN = 1024
TN_ADA = 1024
CONV_HALO = 16
CONV_SUB = 32
LAT_Q_TILE = 1024
CTX_SEQS_PER_STEP = 8
REC_LOCAL_UNROLL = 16
REC_STATE_UNROLL = 16


def _params(*sem):
    return pltpu.CompilerParams(dimension_semantics=sem, vmem_limit_bytes=VMEM_LIMIT)


def _dot(a, b):
    return jnp.dot(a, b, preferred_element_type=F32)


def _dot_nt(a, b):
    return lax.dot_general(a, b, (((1,), (1,)), ((), ())), preferred_element_type=F32)


def _dot_tn(a, b):
    return lax.dot_general(a, b, (((0,), (0,)), ((), ())), preferred_element_type=F32)


def _rms(x, g):
    return x * lax.rsqrt(jnp.mean(x * x, axis=-1, keepdims=True) + EPS) * g


def _ada_norm(x, g, scale, shift):
    gain = g * (1.0 + scale)
    return x * lax.rsqrt(jnp.mean(x * x, axis=-1, keepdims=True) + EPS) * gain + shift


def _sigmoid(x):
    return 1.0 / (1.0 + jnp.exp(-x))


def _ada_kernel(c_ref, w_ref, b_ref, o_ref):
    c = c_ref[...]
    s = (c * _sigmoid(c)).astype(BF16)
    o_ref[...] = _dot(s, w_ref[...].astype(BF16)) + b_ref[...]


def _modulation(cond8, w_ada, b_ada):
    ncol = N_MOD * D_MODEL
    return pl.pallas_call(
        _ada_kernel,
        grid=(DEPTH, ncol // TN_ADA),
        in_specs=[
            pl.BlockSpec((SUBLANES, D_MODEL), lambda l, j: (0, 0)),
            pl.BlockSpec((None, D_MODEL, TN_ADA), lambda l, j: (l, 0, j)),
            pl.BlockSpec((None, 1, TN_ADA), lambda l, j: (l, 0, j)),
        ],
        out_specs=pl.BlockSpec((None, SUBLANES, TN_ADA), lambda l, j: (l, 0, j)),
        out_shape=jax.ShapeDtypeStruct((DEPTH, SUBLANES, ncol), F32),
        compiler_params=_params("parallel", "parallel"),
        name="modulation",
    )(cond8, w_ada, b_ada.reshape(DEPTH, 1, ncol))


def _mod_spec(chunk, rows_per_cond, tm=TM):
    return pl.BlockSpec((None, None, 1, D_MODEL), lambda i, j: ((i * tm) // rows_per_cond, chunk, 0, 0))


def _ffn_kernel(nf, x_ref, sh_ref, sc_ref, g_ref, ng_ref, wa_ref, wb_ref, wo_ref, o_ref, h_scr, act_scr):
    j = pl.program_id(1)

    @pl.when(j == 0)
    def _():
        h = _ada_norm(x_ref[...], ng_ref[...], sc_ref[...], sh_ref[...])
        h_scr[...] = h.astype(BF16)

    @pl.when(j < nf)
    def _():
        h = h_scr[...]
        a = _dot(h, wa_ref[...])
        b = _dot(h, wb_ref[...])
        act_scr[:, pl.ds(pl.multiple_of(j * TF, TF), TF)] = (a * _sigmoid(a) * b).astype(BF16)

    @pl.when(j >= nf)
    def _():
        cols = pl.ds(pl.multiple_of((j - nf) * TN_FFN_OUT, TN_FFN_OUT), TN_FFN_OUT)
        o_ref[...] = x_ref[:, cols] + 0.5 * g_ref[...] * _dot(act_scr[...], wo_ref[...])


def _ffn(x, mod, first_chunk, norm_g, w_in, w_out, layer, which, rows_per_cond):
    n = x.shape[0]
    nf = D_FF // TF
    nout = D_MODEL // TN_FFN_OUT
    hid = lambda j: jnp.minimum(j, nf - 1)
    out = lambda j: jnp.maximum(j - nf, 0)
    return pl.pallas_call(
        functools.partial(_ffn_kernel, nf),
        grid=(n // TM_FFN, nf + nout),
        in_specs=[
            pl.BlockSpec((TM_FFN, D_MODEL), lambda i, j: (i, 0)),
            _mod_spec(first_chunk, rows_per_cond, TM_FFN),
            _mod_spec(first_chunk + 1, rows_per_cond, TM_FFN),
            pl.BlockSpec((None, None, 1, TN_FFN_OUT),
                         lambda i, j: ((i * TM_FFN) // rows_per_cond, first_chunk + 2, 0, out(j))),
            pl.BlockSpec((1, D_MODEL), lambda i, j: (0, 0)),
            pl.BlockSpec((None, None, D_MODEL, TF), lambda i, j: (layer, which, 0, hid(j))),
            pl.BlockSpec((None, None, D_MODEL, TF), lambda i, j: (layer, which, 0, hid(j) + nf)),
            pl.BlockSpec((None, None, D_FF, TN_FFN_OUT), lambda i, j: (layer, which, 0, out(j))),
        ],
        out_specs=pl.BlockSpec((TM_FFN, TN_FFN_OUT), lambda i, j: (i, out(j))),
        out_shape=jax.ShapeDtypeStruct((n, D_MODEL), F32),
        scratch_shapes=[pltpu.VMEM((TM_FFN, D_MODEL), BF16), pltpu.VMEM((TM_FFN, D_FF), BF16)],
        compiler_params=_params("parallel", "arbitrary"),
        name="ffn",
    )(x, mod, mod, mod, norm_g.reshape(1, D_MODEL), w_in, w_in, w_out)


def _inproj_kernel(x_ref, sh_ref, sc_ref, ng_ref, w_ref, o_ref, h_scr):
    @pl.when(pl.program_id(1) == 0)
    def _():
        h = _ada_norm(x_ref[...], ng_ref[...], sc_ref[...], sh_ref[...])
        h_scr[...] = h.astype(BF16)

    o_ref[...] = _dot(h_scr[...], w_ref[...])


def _inproj(x, mod, norm_g, w_in, layer, rows_per_cond):
    n = x.shape[0]
    return pl.pallas_call(
        _inproj_kernel,
        grid=(n // TM, IN_COLS // TN_IN),
        in_specs=[
            pl.BlockSpec((TM, D_MODEL), lambda i, j: (i, 0)),
            _mod_spec(3, rows_per_cond),
            _mod_spec(4, rows_per_cond),
            pl.BlockSpec((1, D_MODEL), lambda i, j: (0, 0)),
            pl.BlockSpec((None, D_MODEL, TN_IN), lambda i, j: (layer, 0, j)),
        ],
        out_specs=pl.BlockSpec((TM, TN_IN), lambda i, j: (i, j)),
        out_shape=jax.ShapeDtypeStruct((n, IN_COLS), F32),
        scratch_shapes=[pltpu.VMEM((TM, D_MODEL), BF16)],
        compiler_params=_params("parallel", "arbitrary"),
        name="inproj",
    )(x, mod, mod, norm_g.reshape(1, D_MODEL), w_in)


QK_SCALE_LOG2 = ATTN_SCALE * LOG2_E


def _with_ones(v):
    return jnp.concatenate([v, jnp.ones_like(v)], axis=-1)


def _softmax_sink(scores, values, sink):
    sink2 = sink * LOG2_E
    m = sink2
    for s in scores:
        m = jnp.maximum(jnp.max(s, axis=-1, keepdims=True), m)
    acc = None
    for s, v in zip(scores, values):
        pv = _dot(jnp.exp2(s - m).astype(BF16), v)
        acc = pv if acc is None else acc + pv
    return acc[:, :HEAD_DIM] / (acc[:, HEAD_DIM:] + jnp.exp2(sink2 - m))


def _put_layer(ref, first_layer, lead, tail, value):
    if first_layer:
        ref[lead + (0,) + tail] = value
        for other in range(1, DEPTH):
            ref[lead + (other,) + tail] = jnp.zeros_like(value)
    else:
        ref[lead + tail] = value


def _ctx_attn_kernel(first_layer, sink_ref, q_ref, k_ref, v_ref, qg_ref, kg_ref, *rest):
    o_ref, kc_ref, vc_ref = rest[-3:]
    kv = pl.program_id(1)
    nseq, seq = kc_ref.shape[0], kc_ref.shape[-2]
    heads = [slice(g * HEAD_DIM, (g + 1) * HEAD_DIM) for g in range(Q_PER_KV)]
    keys, values = [], []
    for i in range(nseq):
        rows = slice(i * seq, (i + 1) * seq)
        kn = _rms(k_ref[rows, :], kg_ref[...])
        v = v_ref[rows, :]
        _put_layer(kc_ref, first_layer, (i,), (), kn)
        _put_layer(vc_ref, first_layer, (i,), (), v)
        keys.append(kn.astype(BF16))
        values.append(_with_ones(v.astype(BF16)))
    logits = [[_dot_nt((_rms(q_ref[i * seq:(i + 1) * seq, cols], qg_ref[...]) * QK_SCALE_LOG2).astype(BF16), keys[i])
               for cols in heads] for i in range(nseq)]
    for i in range(nseq):
        for g, cols in enumerate(heads):
            o = _softmax_sink([logits[i][g]], [values[i]], sink_ref[kv, g])
            o_ref[i * seq:(i + 1) * seq, cols] = o.astype(BF16)


def _ctx_attention(z, sink, q_g, k_g, batch, seq, layer, caches):
    qw = Q_PER_KV * HEAD_DIM
    per = CTX_SEQS_PER_STEP
    first = caches is None
    cache_shape = jax.ShapeDtypeStruct((batch, DEPTH, N_KV_HEADS, seq, HEAD_DIM), F32)
    if first:
        cache_spec = pl.BlockSpec((per, DEPTH, None, seq, HEAD_DIM), lambda b, kv: (b, 0, kv, 0, 0))
    else:
        cache_spec = pl.BlockSpec((per, None, None, seq, HEAD_DIM), lambda b, kv: (b, layer, kv, 0, 0))
    in_specs = [
        pl.BlockSpec(memory_space=pltpu.SMEM),
        pl.BlockSpec((per * seq, qw), lambda b, kv: (b, kv)),
        pl.BlockSpec((per * seq, HEAD_DIM), lambda b, kv: (b, COL_K + kv)),
        pl.BlockSpec((per * seq, HEAD_DIM), lambda b, kv: (b, COL_V + kv)),
        pl.BlockSpec((1, HEAD_DIM), lambda b, kv: (0, 0)),
        pl.BlockSpec((1, HEAD_DIM), lambda b, kv: (0, 0)),
    ]
    args = [sink, z, z, z, q_g.reshape(1, HEAD_DIM), k_g.reshape(1, HEAD_DIM)]
    aliases = {}
    if not first:
        aliases = {len(args): 1, len(args) + 1: 2}
        in_specs += [pl.BlockSpec(memory_space=pl.ANY)] * 2
        args += list(caches)
    return pl.pallas_call(
        functools.partial(_ctx_attn_kernel, first),
        grid=(batch // per, N_KV_HEADS),
        in_specs=in_specs,
        out_specs=[pl.BlockSpec((per * seq, qw), lambda b, kv: (b, kv)), cache_spec, cache_spec],
        out_shape=[jax.ShapeDtypeStruct((batch * seq, D_ATTN), BF16), cache_shape, cache_shape],
        input_output_aliases=aliases,
        compiler_params=_params("parallel", "parallel"),
        name="ctx_attention",
    )(*args)


def _rope(x, cos, sin_a, sin_b):
    quarter = HEAD_DIM // 4
    up = pltpu.roll(x, HEAD_DIM - quarter, 1)
    down = pltpu.roll(x, quarter, 1)
    return x * cos + up * sin_a + down * sin_b


def _lat_attn_kernel(sink_ref, q_ref, k_ref, v_ref, kc_ref, vc_ref, qg_ref, kg_ref,
                     cos_ref, sa_ref, sb_ref, cosq_ref, saq_ref, sbq_ref, o_ref, k_scr, v_scr, kc_scr, vc_scr):
    kv = pl.program_id(1)
    qt = pl.program_id(2)
    seq = k_ref.shape[0]
    span = 3 * ATTN_BLOCK

    @pl.when(qt == 0)
    def _():
        kn = _rms(k_ref[...], kg_ref[...])
        k_scr[...] = _rope(kn, cos_ref[...], sa_ref[...], sb_ref[...]).astype(BF16)
        v_scr[...] = _with_ones(v_ref[...].astype(BF16))
        kc_scr[...] = kc_ref[...].astype(BF16)
        vc_scr[...] = _with_ones(vc_ref[...].astype(BF16))

    kcb = kc_scr[...]
    vcb = vc_scr[...]
    nblk = LAT_Q_TILE // ATTN_BLOCK
    heads = [slice(g * HEAD_DIM, (g + 1) * HEAD_DIM) for g in range(Q_PER_KV)]
    stacked = Q_PER_KV * ATTN_BLOCK

    head_of_row = lax.broadcasted_iota(jnp.int32, (stacked, 1), 0) // ATTN_BLOCK
    sink_col = jnp.zeros((stacked, 1), F32)
    for g in range(Q_PER_KV):
        sink_col = jnp.where(head_of_row == g, sink_ref[kv, g], sink_col)

    def block_rows(blk):
        return slice(blk * ATTN_BLOCK, (blk + 1) * ATTN_BLOCK)

    def logits_of(blk):
        rows = block_rows(blk)
        q0 = (qt * nblk + blk) * ATTN_BLOCK
        start = pl.multiple_of(jnp.clip(q0 - ATTN_BLOCK, 0, seq - span), ATTN_BLOCK)
        qpos = q0 + lax.broadcasted_iota(jnp.int32, (stacked, span), 0) % ATTN_BLOCK
        kpos = start + lax.broadcasted_iota(jnp.int32, (stacked, span), 1)
        valid = jnp.abs(qpos - kpos) <= WINDOW
        cos, sa, sb = cosq_ref[rows, :], saq_ref[rows, :], sbq_ref[rows, :]
        qn = jnp.concatenate([_rope(_rms(q_ref[rows, cols], qg_ref[...]), cos, sa, sb) for cols in heads], axis=0)
        qn = (qn * QK_SCALE_LOG2).astype(BF16)
        s_loc = jnp.where(valid, _dot_nt(qn, k_scr[pl.ds(start, span), :]), MASK_VALUE)
        return s_loc, _dot_nt(qn, kcb), start

    pending = logits_of(0)
    for blk in range(nblk):
        upcoming = logits_of(blk + 1) if blk + 1 < nblk else None
        s_loc, s_ctx, start = pending
        o = _softmax_sink([s_loc, s_ctx], [v_scr[pl.ds(start, span), :], vcb], sink_col)
        for g, cols in enumerate(heads):
            o_ref[block_rows(blk), cols] = o[g * ATTN_BLOCK:(g + 1) * ATTN_BLOCK, :].astype(BF16)
        pending = upcoming


def _rope_tables(seq):
    quarter = HEAD_DIM // 4
    pos = np.arange(seq)
    inv_freq = (ROPE_BASE ** (-np.arange(quarter, dtype=np.float32) / quarter)).astype(np.float32)
    zero = np.zeros((seq, quarter), np.float32)

    def trig(p):
        ang = p.astype(np.float32)[:, None] * inv_freq
        return np.cos(ang).astype(np.float32), np.sin(ang).astype(np.float32)

    cr, sr = trig(pos // GRID_W)
    cc, sc = trig(pos % GRID_W)
    cos = np.concatenate([cr, cr, cc, cc], axis=-1)
    sin_a = np.concatenate([-sr, zero, -sc, zero], axis=-1)
    sin_b = np.concatenate([zero, sr, zero, sc], axis=-1)
    return jnp.asarray(cos), jnp.asarray(sin_a), jnp.asarray(sin_b)


def _lat_attention(z, sink, q_g, k_g, k_ctx, v_ctx, tables, batch, seq):
    qw = Q_PER_KV * HEAD_DIM
    nqt = seq // LAT_Q_TILE
    past = k_ctx.shape[2]
    cos, sin_a, sin_b = tables
    full_tab = pl.BlockSpec((seq, HEAD_DIM), lambda b, kv, qt: (0, 0))
    tile_tab = pl.BlockSpec((LAT_Q_TILE, HEAD_DIM), lambda b, kv, qt: (qt, 0))
    ctx_spec = pl.BlockSpec((None, None, past, HEAD_DIM), lambda b, kv, qt: (b, kv, 0, 0))
    gain = pl.BlockSpec((1, HEAD_DIM), lambda b, kv, qt: (0, 0))
    return pl.pallas_call(
        _lat_attn_kernel,
        grid=(batch, N_KV_HEADS, nqt),
        in_specs=[
            pl.BlockSpec(memory_space=pltpu.SMEM),
            pl.BlockSpec((LAT_Q_TILE, qw), lambda b, kv, qt: (b * nqt + qt, kv)),
            pl.BlockSpec((seq, HEAD_DIM), lambda b, kv, qt: (b, COL_K + kv)),
            pl.BlockSpec((seq, HEAD_DIM), lambda b, kv, qt: (b, COL_V + kv)),
            ctx_spec, ctx_spec, gain, gain,
            full_tab, full_tab, full_tab, tile_tab, tile_tab, tile_tab,
        ],
        out_specs=pl.BlockSpec((LAT_Q_TILE, qw), lambda b, kv, qt: (b * nqt + qt, kv)),
        out_shape=jax.ShapeDtypeStruct((batch * seq, D_ATTN), BF16),
        scratch_shapes=[pltpu.VMEM((seq, HEAD_DIM), BF16), pltpu.VMEM((seq, 2 * HEAD_DIM), BF16),
                        pltpu.VMEM((past, HEAD_DIM), BF16), pltpu.VMEM((past, 2 * HEAD_DIM), BF16)],
        compiler_params=_params("parallel", "parallel", "arbitrary"),
        name="lat_attention",
    )(sink, z, z, z, k_ctx, v_ctx, q_g.reshape(1, HEAD_DIM), k_g.reshape(1, HEAD_DIM),
      cos, sin_a, sin_b, cos, sin_a, sin_b)


def _split3(x):
    hi = x.astype(BF16)
    r1 = x - hi.astype(F32)
    mid = r1.astype(BF16)
    lo = (r1 - mid.astype(F32)).astype(BF16)
    return hi, mid, lo


def _rec_gates(forward, zf, lb):
    c = REC_CHUNK
    t = jnp.exp(-jnp.abs(zf))
    r = 1.0 / (1.0 + t)
    tr = t * r
    nonneg = zf >= 0
    sig_pos = jnp.where(nonneg, r, tr)
    sig_neg = jnp.where(nonneg, tr, r)
    one_m_lb = 1.0 - lb
    log_f = jnp.log(jnp.maximum(lb + one_m_lb * sig_pos, GATE_FLOOR))
    k = one_m_lb * sig_neg
    ri = lax.broadcasted_iota(jnp.int32, (c, c), 0)
    ci = lax.broadcasted_iota(jnp.int32, (c, c), 1)
    tri = jnp.where((ci <= ri) if forward else (ci >= ri), 1.0, 0.0).astype(BF16)
    hi, mid, lo = _split3(log_f)
    cum2 = (_dot(tri, hi) + _dot(tri, mid) + _dot(tri, lo)) * LOG2_E
    return cum2 - jnp.log2(k), cum2


def _rec_same_block(forward, q, key_row, cum2):
    sub_row = lax.broadcasted_iota(jnp.int32, (SUBLANES, REC_DK), 0)
    pieces = []
    for b in range(REC_CHUNK // SUBLANES):
        blk = slice(b * SUBLANES, (b + 1) * SUBLANES)
        cum_b, q_b = cum2[blk, :], q[blk, :]
        for sl in range(SUBLANES):
            s = b * SUBLANES + sl
            keep = (sub_row >= sl) if forward else (sub_row <= sl)
            pieces.append(jnp.where(keep, jnp.exp2(cum_b - key_row(s)) * q_b, 0.0))
    return _dot(jnp.concatenate(pieces, axis=0).astype(BF16), jnp.ones((REC_DK, REC_DV), BF16))


def _rec_cross_block(forward, q, key2, v, cum2):
    c = REC_CHUNK
    q_parts, k_parts, v_parts, segments = [], [], [], []
    for tb in range(c // SUBLANES):
        blk = slice(tb * SUBLANES, (tb + 1) * SUBLANES)
        src, ref_row = (slice(0, blk.start), blk.start - 1) if forward else (slice(blk.stop, c), blk.stop)
        if src.stop == src.start:
            q_parts.append(jnp.zeros((SUBLANES, REC_DK), F32))
            continue
        ref = cum2[ref_row:ref_row + 1, :]
        q_parts.append(q[blk, :] * jnp.exp2(cum2[blk, :] - ref))
        k_parts.append(jnp.exp2(ref - key2[src, :]))
        v_parts.append(v[src, :])
        segments.append((tb, src.stop - src.start))
    a = _dot_nt(jnp.concatenate(q_parts, axis=0).astype(BF16), jnp.concatenate(k_parts, axis=0).astype(BF16))
    return a, segments, jnp.concatenate(v_parts, axis=0).astype(BF16)


def _rec_cross_apply(a, segments, v_all):
    c, ncol = a.shape
    row_blk = lax.broadcasted_iota(jnp.int32, (c, ncol), 0) // SUBLANES
    col = lax.broadcasted_iota(jnp.int32, (c, ncol), 1)
    col_blk = jnp.full((c, ncol), -1, jnp.int32)
    start = 0
    for tb, width in segments:
        col_blk = jnp.where((col >= start) & (col < start + width), tb, col_blk)
        start += width
    return _dot(jnp.where(row_blk == col_blk, a, 0.0).astype(BF16), v_all)


def _rec_same_apply(lane_sums, v_row):
    o_blocks = []
    for b in range(REC_CHUNK // SUBLANES):
        acc = None
        for sl in range(SUBLANES):
            s = b * SUBLANES + sl
            term = lane_sums[s * SUBLANES:(s + 1) * SUBLANES, :] * v_row(s)
            acc = term if acc is None else acc + term
        o_blocks.append(acc)
    return jnp.concatenate(o_blocks, axis=0)


def _rec_kernel(layer, has_init, *refs):
    q_ref, ff_ref, fb_ref, v_ref, g_ref, lg_ref, ng_ref = refs[:7]
    refs = refs[7:]
    if has_init:
        s0_ref, refs = refs[0], refs[1:]
    conv_in, refs = refs[:10], refs[10:]
    if not has_init and layer > 0:
        refs = refs[1:]
    o_ref, conv_o_ref = refs[:2]
    refs = refs[2:]
    if not has_init:
        s_out_ref, refs = refs[0], refs[1:]
    o_scr, qd_scr, kd_scr, gl_scr, oi_scr, key_scr, glu_scr, conv_h_scr, conv_w_scr = refs
    seq = q_ref.shape[0]
    nchunk = seq // REC_CHUNK
    c = REC_CHUNK

    logits = lg_ref[...]
    e = jnp.exp(logits - jnp.max(logits, axis=0, keepdims=True))
    p = e / jnp.sum(e, axis=0, keepdims=True)
    lb = jnp.zeros_like(p[0])
    for i in range(1, layer + 1):
        lb = lb + p[i]

    def rows_of(ci):
        return pl.ds(pl.multiple_of(ci * c, c), c)

    z_refs = (ff_ref, fb_ref)
    local_unroll = min(REC_LOCAL_UNROLL, nchunk)

    def local_body(it, carry):
        chunks = [it * local_unroll + u for u in range(local_unroll)]
        qs = [q_ref[rows_of(ci), :] for ci in chunks]
        vs = [v_ref[rows_of(ci), :] for ci in chunks]
        items = [(u, d) for u in range(local_unroll) for d in range(2)]
        gates = [_rec_gates(d == 0, z_refs[d][rows_of(chunks[u]), :], lb[d]) for u, d in items]
        for idx, (key2, _) in enumerate(gates):
            key_scr[idx] = key2
        key_rows = [lambda s, idx=idx: key_scr[idx, s:s + 1, :] for idx in range(len(items))]
        v_rows = [lambda s, ci=ci: v_ref[pl.ds(ci * c + s, 1), :] for ci in chunks]
        same = [_rec_same_block(d == 0, qs[u], key_rows[idx], cum2)
                for idx, ((u, d), (_, cum2)) in enumerate(zip(items, gates))]
        cross = [_rec_cross_block(d == 0, qs[u], key2, vs[u], cum2) for (u, d), (key2, cum2) in zip(items, gates)]
        totals = [None] * local_unroll
        for (u, d), (key2, cum2), lane_sums, (a, segments, v_all) in zip(items, gates, same, cross):
            rows = rows_of(chunks[u])
            last = cum2[c - 1:c, :] if d == 0 else cum2[0:1, :]
            qd_scr[d, rows, :] = (qs[u] * jnp.exp2(cum2)).astype(BF16)
            kd_scr[d, rows, :] = jnp.exp2(last - key2).astype(BF16)
            gl_scr[d, chunks[u]] = jnp.broadcast_to(jnp.exp2(last), (SUBLANES, REC_DK))
            o_local = _rec_same_apply(lane_sums, v_rows[u]) + _rec_cross_apply(a, segments, v_all)
            totals[u] = o_local if totals[u] is None else totals[u] + o_local
        for u, ci in enumerate(chunks):
            o_scr[rows_of(ci), :] = totals[u]
        return carry

    lax.fori_loop(0, nchunk // local_unroll, local_body, 0)

    if has_init:
        init = (s0_ref[0].T, s0_ref[1].T)
    else:
        init = (jnp.zeros((REC_DV, REC_DK), F32),) * 2

    state_unroll = min(REC_STATE_UNROLL, nchunk)
    conv_rows = conv_o_ref.shape[0] // (nchunk // state_unroll)
    _conv_stage(pl.program_id(1), N_REC_HEADS, *conv_in[:7], glu_scr, conv_w_scr)

    def state_body(it, states):
        offset = pl.multiple_of(it * conv_rows, conv_rows)
        conv_h_scr[0] = glu_scr[pl.ds(offset, conv_rows + 2 * CONV_HALO), :]

        def store(r0, val):
            conv_o_ref[pl.ds(pl.multiple_of(offset + r0, CONV_SUB), CONV_SUB), :] = val

        _conv_rows(conv_rows, conv_h_scr, conv_w_scr, *conv_in[7:], store)
        states = list(states)
        steps = [it * state_unroll + u for u in range(state_unroll)]
        order = [(d, step if d == 0 else nchunk - 1 - step) for step in steps for d in range(2)]
        updates = [_dot_tn(v_ref[rows_of(ci), :].astype(BF16), kd_scr[d, rows_of(ci), :]) for d, ci in order]
        for (d, ci), update in zip(order, updates):
            st = states[d]
            oi_scr[d, rows_of(ci), :] = _dot_nt(qd_scr[d, rows_of(ci), :], st.astype(BF16))
            decayed = (st.reshape(REC_DV // SUBLANES, SUBLANES, REC_DK) * gl_scr[d, ci]).reshape(REC_DV, REC_DK)
            states[d] = decayed + update
        return tuple(states)

    final = lax.fori_loop(0, nchunk // state_unroll, state_body, init)
    if not has_init:
        for d in range(2):
            _put_layer(s_out_ref, layer == 0, (), (d,), final[d].T)

    def out_body(ci, carry):
        rows = rows_of(ci)
        gate = g_ref[rows, :]
        o = o_scr[rows, :] + oi_scr[0, rows, :] + oi_scr[1, rows, :]
        o_ref[rows, :] = (_rms(o, ng_ref[...]) * (gate * _sigmoid(gate))).astype(BF16)
        return carry

    lax.fori_loop(0, nchunk, out_body, 0, unroll=8)


def _recurrence(z, layer, lb_logits, norm_g, s0, conv_params, batch, seq, states=None):
    has_init = s0 is not None
    assert has_init or (states is None) == (layer == 0)
    n = batch * seq
    col = lambda base: pl.BlockSpec((seq, REC_DK), lambda b, h: (b, base + h))
    state_spec = pl.BlockSpec((None, 2, None, REC_DK, REC_DV), lambda b, h: (b, 0, h, 0, 0))
    in_specs = [col(COL_RQ), col(COL_RF_F), col(COL_RF_B), col(COL_RI), col(COL_RG),
                pl.BlockSpec((DEPTH, 2, None, 1, REC_DK), lambda b, h: (0, 0, h, 0, 0)),
                pl.BlockSpec((None, 1, REC_DV), lambda b, h: (h, 0, 0))]
    args = [z, z, z, z, z, lb_logits, norm_g]
    if has_init:
        in_specs.append(state_spec)
        args.append(s0)

    conv_step = seq // N_REC_HEADS
    state_trips = (seq // REC_CHUNK) // min(REC_STATE_UNROLL, seq // REC_CHUNK)
    conv_rows = conv_step // state_trips
    assert conv_step % CONV_HALO == 0 and conv_rows % CONV_SUB == 0
    cw = CONV_CH // HEAD_DIM
    halo_per_step = conv_step // CONV_HALO
    nhalo = n // CONV_HALO
    part = lambda b, h: b * N_REC_HEADS + h
    big_block = conv_step * CONV_CH * jnp.dtype(F32).itemsize >= SINGLE_BUFFER_BYTES
    mid_mode = dict(pipeline_mode=pl.Buffered(1)) if big_block else {}
    mid = lambda cc: pl.BlockSpec((conv_step, CONV_CH), lambda b, h: (part(b, h), cc), **mid_mode)
    top = lambda cc: pl.BlockSpec(
        (CONV_HALO, CONV_CH), lambda b, h: (jnp.maximum(part(b, h) * halo_per_step - 1, 0), cc))
    bot = lambda cc: pl.BlockSpec(
        (CONV_HALO, CONV_CH), lambda b, h: (jnp.minimum((part(b, h) + 1) * halo_per_step, nhalo - 1), cc))
    vec = pl.BlockSpec((1, CONV_CH), lambda b, h: (0, 0))
    ca, cb = COL_CA // cw, COL_CB // cw
    w, bias, ln_g, ln_b = conv_params
    in_specs += [mid(ca), mid(cb), top(ca), top(cb), bot(ca), bot(cb),
                 pl.BlockSpec((CONV_WIDTH, CONV_CH), lambda b, h: (0, 0)), vec, vec, vec]
    args += [z, z, z, z, z, z, w, bias.reshape(1, CONV_CH), ln_g.reshape(1, CONV_CH), ln_b.reshape(1, CONV_CH)]

    out_specs = [pl.BlockSpec((seq, REC_DV), lambda b, h: (b, h)),
                 pl.BlockSpec((conv_step, CONV_CH), lambda b, h: (part(b, h), 0))]
    out_shape = [jax.ShapeDtypeStruct((n, D_REC), BF16), jax.ShapeDtypeStruct((n, CONV_CH), BF16)]
    aliases = {}
    if not has_init:
        if layer == 0:
            out_specs.append(pl.BlockSpec((None, DEPTH, 2, None, REC_DK, REC_DV), lambda b, h: (b, 0, 0, h, 0, 0)))
        else:
            out_specs.append(pl.BlockSpec((None, None, 2, None, REC_DK, REC_DV),
                                          lambda b, h: (b, layer, 0, h, 0, 0)))
            aliases = {len(args): 2}
            in_specs.append(pl.BlockSpec(memory_space=pl.ANY))
            args.append(states)
        out_shape.append(jax.ShapeDtypeStruct((batch, DEPTH, 2, N_REC_HEADS, REC_DK, REC_DV), F32))
    return pl.pallas_call(
        functools.partial(_rec_kernel, layer, has_init),
        grid=(batch, N_REC_HEADS),
        in_specs=in_specs,
        out_specs=out_specs,
        out_shape=out_shape,
        input_output_aliases=aliases,
        scratch_shapes=[pltpu.VMEM((seq, REC_DV), F32),
                        pltpu.VMEM((2, seq, REC_DK), BF16),
                        pltpu.VMEM((2, seq, REC_DK), BF16),
                        pltpu.VMEM((2, seq // REC_CHUNK, SUBLANES, REC_DK), F32),
                        pltpu.VMEM((2, seq, REC_DV), F32),
                        pltpu.VMEM((2 * min(REC_LOCAL_UNROLL, seq // REC_CHUNK), REC_CHUNK, REC_DK), F32),
                        pltpu.VMEM((conv_step + 2 * CONV_HALO, CONV_CH), F32),
                        pltpu.VMEM((SUBLANES, conv_rows + 2 * CONV_HALO, CONV_CH), F32),
                        pltpu.VMEM((CONV_WIDTH, SUBLANES, CONV_CH), F32)],
        compiler_params=_params("parallel", "parallel"),
        name="hgrn2_conv",
    )(*args)


def _conv_stage(part, nparts, a_ref, b_ref, at_ref, bt_ref, ab_ref, bb_ref, w_ref, glu_scr, w_scr):
    has_top = jnp.where(part > 0, 1.0, 0.0)
    has_bot = jnp.where(part < nparts - 1, 1.0, 0.0)
    rows = a_ref.shape[0]

    def glu(a, b):
        return a * _sigmoid(b)

    glu_scr[0:CONV_HALO, :] = glu(at_ref[...], bt_ref[...]) * has_top
    glu_scr[CONV_HALO:CONV_HALO + rows, :] = glu(a_ref[...], b_ref[...])
    glu_scr[CONV_HALO + rows:, :] = glu(ab_ref[...], bb_ref[...]) * has_bot
    for j in range(CONV_WIDTH):
        w_scr[j] = jnp.broadcast_to(w_ref[j:j + 1, :], (SUBLANES, CONV_CH))


def _conv_rows(rows, h_scr, w_scr, bias_ref, lg_ref, lb_ref, store):
    half = CONV_WIDTH // 2
    shifted_rows = rows + 2 * CONV_HALO - SUBLANES
    for r in range(1, SUBLANES):
        h_scr[r, 0:shifted_rows, :] = h_scr[0, r:r + shifted_rows, :]
    for r0 in range(0, rows, CONV_SUB):
        acc = None
        for j in range(CONV_WIDTH):
            lo = r0 + CONV_HALO - half + j
            shift = lo % SUBLANES
            weight = w_scr[j]
            window = h_scr[shift, lo - shift:lo - shift + CONV_SUB, :]
            term = (window.reshape(CONV_SUB // SUBLANES, SUBLANES, CONV_CH) * weight).reshape(CONV_SUB, CONV_CH)
            acc = term if acc is None else acc + term
        y = acc + bias_ref[...]
        mu = jnp.mean(y, axis=-1, keepdims=True)
        yc = y - mu
        var = jnp.mean(yc * yc, axis=-1, keepdims=True)
        yn = yc * lax.rsqrt(var + EPS) * lg_ref[...] + lb_ref[...]
        store(r0, (yn * _sigmoid(yn)).astype(BF16))


def _outproj_kernel(x_ref, g_ref, a_ref, r_ref, c_ref, wa_ref, wr_ref, wc_ref, o_ref):
    mix = _dot(a_ref[...], wa_ref[...]) + _dot(r_ref[...], wr_ref[...]) + _dot(c_ref[...], wc_ref[...])
    o_ref[...] = x_ref[...] + g_ref[...] * mix


def _outproj(x, mod, attn, rec, conv, w_out, layer, rows_per_cond):
    n = x.shape[0]
    tn = D_MODEL // 2
    row = lambda width: pl.BlockSpec((TM, width), lambda i, j: (i, 0))
    return pl.pallas_call(
        _outproj_kernel,
        grid=(n // TM, D_MODEL // tn),
        in_specs=[
            pl.BlockSpec((TM, tn), lambda i, j: (i, j)),
            pl.BlockSpec((None, None, 1, tn), lambda i, j: ((i * TM) // rows_per_cond, 5, 0, j)),
            row(D_ATTN), row(D_REC), row(CONV_CH),
            pl.BlockSpec((None, D_ATTN, tn), lambda i, j: (layer, 0, j)),
            pl.BlockSpec((None, D_REC, tn), lambda i, j: (layer, D_ATTN // D_REC, j)),
            pl.BlockSpec((None, CONV_CH, tn), lambda i, j: (layer, (D_ATTN + D_REC) // CONV_CH, j)),
        ],
        out_specs=pl.BlockSpec((TM, tn), lambda i, j: (i, j)),
        out_shape=jax.ShapeDtypeStruct((n, D_MODEL), F32),
        compiler_params=_params("parallel", "parallel"),
        name="outproj",
    )(x, mod, attn, rec, conv, w_out, w_out, w_out)


def _trunk_layer(x, mod, l, P, mixers, rows_per_cond):
    x = _ffn(x, mod, 0, P['norm_g'][l, 0], P['w_ffn_in'], P['w_ffn_out'], l, 0, rows_per_cond)
    z = _inproj(x, mod, P['norm_g'][l, 1], P['w_in'], l, rows_per_cond)
    attn, rec, conv, extras = mixers(z)
    x = _outproj(x, mod, attn, rec, conv, P['w_out'], l, rows_per_cond)
    x = _ffn(x, mod, 6, P['norm_g'][l, 2], P['w_ffn_in'], P['w_ffn_out'], l, 1, rows_per_cond)
    return x, extras


def kernel(x_prompt, x_sample, cache_k, cache_v, state_rec, c, c_ctx, w_ada, b_ada, norm_g, w_ffn_in, w_ffn_out,
           w_in, w_out, q_norm_g, k_norm_g, attn_sink, rec_lb_logits, rec_norm_g, conv_w, conv_b, conv_ln_g, conv_ln_b):
    batch, seq, _ = x_prompt.shape
    dec_batch, dec_seq, _ = x_sample.shape
    assert seq % TM == 0 or TM % seq == 0
    assert dec_seq % TM == 0 and dec_seq % LAT_Q_TILE == 0

    P = {'norm_g': norm_g, 'w_ffn_in': w_ffn_in.astype(BF16), 'w_ffn_out': w_ffn_out.astype(BF16),
         'w_in': w_in.astype(BF16), 'w_out': w_out.astype(BF16)}

    lb_logits = rec_lb_logits.reshape(DEPTH, 2, N_REC_HEADS, 1, REC_DK)

    cond8 = jnp.zeros((SUBLANES, D_MODEL), F32).at[0].set(c_ctx).at[1:1 + dec_batch].set(c)
    mod = _modulation(cond8, w_ada, b_ada).reshape(DEPTH, SUBLANES, N_MOD, 1, D_MODEL)
    tables = _rope_tables(dec_seq)
    sinks = attn_sink.reshape(DEPTH, N_KV_HEADS, Q_PER_KV)
    rec_g = rec_norm_g.reshape(DEPTH, N_REC_HEADS, 1, REC_DV)

    conv_params = lambda l: (conv_w[l], conv_b[l], conv_ln_g[l], conv_ln_b[l])

    h = x_prompt.reshape(batch * seq, D_MODEL)
    caches, states = None, None
    for l in range(DEPTH):
        def ctx_mixers(z, l=l, caches=caches, states=states):
            attn, new_k, new_v = _ctx_attention(z, sinks[l], q_norm_g[l], k_norm_g[l], batch, seq, l, caches)
            rec, conv, new_s = _recurrence(z, l, lb_logits, rec_g[l], None, conv_params(l), batch, seq, states)
            return attn, rec, conv, ((new_k, new_v), new_s)

        h, (caches, states) = _trunk_layer(h, mod[l, 0:1], l, P, ctx_mixers, batch * seq)
    y_prompt = h.reshape(batch, seq, D_MODEL)

    h = x_sample.reshape(dec_batch * dec_seq, D_MODEL)
    for l in range(DEPTH):
        def lat_mixers(z, l=l):
            attn = _lat_attention(z, sinks[l], q_norm_g[l], k_norm_g[l], cache_k[:, l], cache_v[:, l], tables,
                                  dec_batch, dec_seq)
            rec, conv = _recurrence(z, l, lb_logits, rec_g[l], state_rec[:, l], conv_params(l), dec_batch, dec_seq)
            return attn, rec, conv, None

        h, _ = _trunk_layer(h, mod[l, 1:1 + dec_batch], l, P, lat_mixers, dec_seq)
    y_sample = h.reshape(dec_batch, dec_seq, D_MODEL)

    return (y_prompt, y_sample, caches[0], caches[1], states)
```
